```python
import functools
import jax, jax.numpy as jnp
from jax import lax
import numpy as np

D_MODEL = 1024
BATCH = 2
SEQ = 8192
DEPTH = 1

GRID_W = 64
CTX_LEN = 256
CHUNK = 64
EPS = 1e-6
NEG_BIG = -1e30
ML_HEADS = 4
ML_QK = D_MODEL // 8
ML_V = D_MODEL // 4
GLA_HEADS = 4
GLA_QK = D_MODEL // 8
GLA_V = D_MODEL // 4
GLA_RANK = 16
GLA_TAU = 16.0
CONV_K = 3
N_GROUPS = 4
EXPERTS_PER_GROUP = 4
N_EXPERTS = N_GROUPS * EXPERTS_PER_GROUP
TOP_K = 2
D_EXPERT = D_MODEL // 2
MOE_BLOCK = 128
ML_QK_W = ML_HEADS * ML_QK
ML_V_W = ML_HEADS * ML_V
GLA_QK_W = GLA_HEADS * GLA_QK
GLA_V_W = GLA_HEADS * GLA_V
IN_SIZES = (ML_QK_W, ML_QK_W, ML_V_W, ML_V_W, 4 * ML_HEADS,
            GLA_QK_W, GLA_QK_W, GLA_V_W, GLA_V_W, 2 * GLA_RANK,
            D_MODEL, D_MODEL)
IN_WIDTH = sum(IN_SIZES)

kernel_name = 'hybrid_mlstm_gla_hmoe_dit'


def rms_norm(x, g):
    xf = x.astype(jnp.float32)
    xf = xf * lax.rsqrt(jnp.mean(xf * xf, axis=-1, keepdims=True) + EPS)
    return xf.astype(x.dtype) * g


def split_heads(a, n_heads):
    bn, t, w = a.shape
    return a.reshape(bn, t, n_heads, w // n_heads).transpose(0, 2, 1, 3)


def merge_heads_rms(h, g, dtype):
    bn, nh, t, dh = h.shape
    h = h.transpose(0, 2, 1, 3)
    h = h * lax.rsqrt(jnp.mean(h * h, axis=-1, keepdims=True) + EPS)
    return h.reshape(bn, t, nh * dh).astype(dtype) * g


def flip_t(a):
    return jnp.flip(a, axis=2)


def dw_conv_silu(u, w, b, rows, cols):
    bn, t, ch = u.shape
    img = u.reshape(bn, rows, cols, ch)
    out = lax.conv_general_dilated(img, w[:, :, None, :], window_strides=(1, 1), padding='SAME',
                                   dimension_numbers=('NHWC', 'HWIO', 'NHWC'), feature_group_count=ch)
    return jax.nn.silu(out.reshape(bn, t, ch) + b)


def to_chunks(a):
    bn, nh, t = a.shape[:3]
    return jnp.moveaxis(a.reshape(bn, nh, t // CHUNK, CHUNK, *a.shape[3:]), 2, 0)


def from_chunks(a):
    nc, bn, nh, l = a.shape[:4]
    return jnp.moveaxis(a, 0, 2).reshape(bn, nh, nc * l, *a.shape[4:])


def mlstm_scan(q, k, v, i_pre, f_pre, state, with_output):
    tri = jnp.arange(CHUNK)[:, None] >= jnp.arange(CHUNK)[None, :]
    log_f = jax.nn.log_sigmoid(f_pre)

    def step(carry, xs):
        c_mat, n_vec, m = carry
        qc, kc, vc, ic, fc = xs
        b = jnp.cumsum(fc, axis=-1)
        b_end = b[..., -1]
        log_w = b_end[..., None] - b + ic
        m_new = jnp.maximum(b_end + m, jnp.max(log_w, axis=-1))
        w = jnp.exp(log_w - m_new[..., None])
        decay = jnp.exp(b_end + m - m_new)
        c_new = decay[..., None, None] * c_mat + jnp.einsum('bhs,bhsd,bhsv->bhdv', w, kc, vc)
        n_new = decay[..., None] * n_vec + jnp.einsum('bhs,bhsd->bhd', w, kc)
        if not with_output:
            return (c_new, n_new, m_new), None
        log_d = jnp.where(tri, b[..., :, None] - b[..., None, :] + ic[..., None, :], -jnp.inf)
        log_inter = b + m[..., None]
        m_t = jnp.maximum(log_inter, jnp.max(log_d, axis=-1))
        s = jnp.einsum('bhtd,bhsd->bhts', qc, kc) * jnp.exp(log_d - m_t[..., None])
        w_inter = jnp.exp(log_inter - m_t)
        num = jnp.einsum('bhts,bhsv->bhtv', s, vc) + w_inter[..., None] * jnp.einsum('bhtd,bhdv->bhtv', qc, c_mat)
        den = jnp.abs(jnp.sum(s, axis=-1) + w_inter * jnp.einsum('bhtd,bhd->bht', qc, n_vec))
        h = num / jnp.maximum(den, jnp.exp(-m_t))[..., None]
        return (c_new, n_new, m_new), h

    state, hs = lax.scan(step, state, (to_chunks(q), to_chunks(k), to_chunks(v), to_chunks(i_pre), to_chunks(log_f)))
    return (from_chunks(hs) if with_output else None), state


def gla_scan(q, k, v, log_a, state, with_output):
    tri = jnp.arange(CHUNK)[:, None] >= jnp.arange(CHUNK)[None, :]

    def step(s_mat, xs):
        qc, kc, vc, ac = xs
        b = jnp.cumsum(ac, axis=2)
        b_end = b[:, :, -1]
        s_new = jnp.exp(b_end)[..., None] * s_mat + jnp.einsum('bhsd,bhsv->bhdv', kc * jnp.exp(b_end[:, :, None] - b), vc)
        if not with_output:
            return s_new, None
        rel = jnp.where(tri[:, :, None], b[:, :, :, None] - b[:, :, None], -jnp.inf)
        scores = jnp.einsum('bhtd,bhsd,bhtsd->bhts', qc, kc, jnp.exp(rel))
        o = jnp.einsum('bhts,bhsv->bhtv', scores, vc) + jnp.einsum('bhtd,bhdv->bhtv', qc * jnp.exp(b), s_mat)
        return s_new, o

    state, os_ = lax.scan(step, state, (to_chunks(q), to_chunks(k), to_chunks(v), to_chunks(log_a)))
    return (from_chunks(os_) if with_output else None), state


def empty_states(bn):
    ml = (jnp.zeros((bn, ML_HEADS, ML_QK, ML_V), jnp.float32),
          jnp.zeros((bn, ML_HEADS, ML_QK), jnp.float32),
          jnp.full((bn, ML_HEADS), NEG_BIG, jnp.float32))
    gla = jnp.zeros((bn, GLA_HEADS, GLA_QK, GLA_V), jnp.float32)
    return (ml, ml, gla, gla)


def token_mixers(h, conv_fn, init_states, with_output, w_in, ml_conv, ml_conv_b, b_mgate, ml_norm,
                 gla_w2, gla_b2, gla_norm, w_proj_m, w_proj_g, w_out):
    bn, t, _ = h.shape
    f32 = jnp.float32
    cuts = [int(s) for s in np.cumsum(IN_SIZES)[:-1]]
    (q_m, k_m, v_m, o_m, gate_m, q_g, k_g, v_g, r_g, lr_g, mg_m, mg_g) = jnp.split(h @ w_in, cuts, axis=-1)
    st_mf, st_mb, st_gf, st_gb = init_states

    qk = conv_fn(jnp.concatenate([q_m, k_m], axis=-1), ml_conv, ml_conv_b)
    q_m, k_m = jnp.split(qk, 2, axis=-1)
    qm = split_heads(q_m, ML_HEADS).astype(f32)
    km = split_heads(k_m, ML_HEADS).astype(f32) * (ML_QK ** -0.5)
    vm = split_heads(v_m, ML_HEADS).astype(f32)
    gm = (gate_m + b_mgate).astype(f32).reshape(bn, t, 2, 2, ML_HEADS).transpose(2, 3, 0, 4, 1)
    hm_f, st_mf = mlstm_scan(qm, km, vm, gm[0, 0], gm[0, 1], st_mf, with_output)
    hm_b, st_mb = mlstm_scan(flip_t(qm), flip_t(km), flip_t(vm), flip_t(gm[1, 0]), flip_t(gm[1, 1]), st_mb, with_output)

    qg = split_heads(q_g, GLA_HEADS).astype(f32) * (GLA_QK ** -0.5)
    kg = split_heads(k_g, GLA_HEADS).astype(f32)
    vg = split_heads(v_g, GLA_HEADS).astype(f32)
    lr = lr_g.astype(f32).reshape(bn, t, 2, GLA_RANK)
    log_a = jax.nn.log_sigmoid(jnp.einsum('btdr,dre->dbte', lr, gla_w2.astype(f32)) + gla_b2.astype(f32)[:, None, None, :]) / GLA_TAU
    og_f, st_gf = gla_scan(qg, kg, vg, split_heads(log_a[0], GLA_HEADS), st_gf, with_output)
    og_b, st_gb = gla_scan(flip_t(qg), flip_t(kg), flip_t(vg), flip_t(split_heads(log_a[1], GLA_HEADS)), st_gb, with_output)

    states = (st_mf, st_mb, st_gf, st_gb)
    if not with_output:
        return None, states
    y_m = merge_heads_rms(hm_f + flip_t(hm_b), ml_norm, h.dtype) * jax.nn.sigmoid(o_m)
    y_g = merge_heads_rms(og_f + flip_t(og_b), gla_norm, h.dtype) * jax.nn.silu(r_g)
    y = jax.nn.sigmoid(mg_m) * (y_m @ w_proj_m) + jax.nn.sigmoid(mg_g) * (y_g @ w_proj_g)
    return y @ w_out, states


def hier_moe(h, w_grp, b_grp, w_rexp, b_rexp, w_up, w_down):
    bn, t, d = h.shape
    n_tok = bn * t
    hf = h.reshape(n_tok, d)
    p_grp = jax.nn.softmax((hf @ w_grp + b_grp).astype(jnp.float32), axis=-1)
    grp_p, grp = lax.top_k(p_grp, 1)
    e_logits = (hf @ w_rexp + b_rexp).astype(jnp.float32).reshape(n_tok, N_GROUPS, EXPERTS_PER_GROUP)
    p_in = jax.nn.softmax(e_logits[jnp.arange(n_tok), grp[:, 0]], axis=-1)
    top_p, top_i = lax.top_k(p_in, TOP_K)
    weights = grp_p * top_p / jnp.sum(top_p, axis=-1, keepdims=True)
    expert = grp * EXPERTS_PER_GROUP + top_i

    n_assign = n_tok * TOP_K
    flat_e = expert.reshape(n_assign)
    flat_t = jnp.repeat(jnp.arange(n_tok, dtype=jnp.int32), TOP_K)
    flat_w = weights.reshape(n_assign)
    order = jnp.argsort(flat_e)
    se, st, sw = flat_e[order], flat_t[order], flat_w[order]
    counts = jnp.zeros((N_EXPERTS,), jnp.int32).at[flat_e].add(1)
    padded = (counts + MOE_BLOCK - 1) // MOE_BLOCK * MOE_BLOCK
    pad_end = jnp.cumsum(padded)
    dest = (pad_end - padded)[se] + jnp.arange(n_assign, dtype=jnp.int32) - (jnp.cumsum(counts) - counts)[se]
    n_blocks = -(-n_assign // MOE_BLOCK) + N_EXPERTS
    n_rows = n_blocks * MOE_BLOCK
    row_tok = jnp.full((n_rows,), n_tok, jnp.int32).at[dest].set(st)
    row_w = jnp.zeros((n_rows,), jnp.float32).at[dest].set(sw)
    block_e = jnp.minimum(jnp.searchsorted(pad_end, jnp.arange(n_blocks, dtype=jnp.int32) * MOE_BLOCK, side='right'), N_EXPERTS - 1)
    x_rows = jnp.concatenate([hf, jnp.zeros((1, d), h.dtype)], axis=0)[row_tok].reshape(n_blocks, MOE_BLOCK, d)

    def expert_block(args):
        xb, e = args
        gate, up = jnp.split(xb @ w_up[e], 2, axis=-1)
        return (jax.nn.silu(gate) * up) @ w_down[e]

    y_rows = lax.map(expert_block, (x_rows, block_e)).reshape(n_rows, d)
    y = jnp.zeros((n_tok + 1, d), h.dtype).at[row_tok].add(y_rows * row_w[:, None].astype(h.dtype))
    return y[:n_tok].reshape(bn, t, d)


def setup_inputs(seed: int = 0) -> dict:
    key = jax.random.key(seed)
    ks = jax.random.split(key, 26)

    def nrm(k, shape, scale):
        return jax.random.normal(k, shape, jnp.float32) * scale

    L = DEPTH
    d = D_MODEL
    inv = d ** -0.5
    gate_offset = jnp.array([0.0, 3.0], jnp.float32)[None, None, :, None]
    return {
        'x': nrm(ks[0], (BATCH, SEQ, d), 1.0),
        'c': nrm(ks[1], (BATCH, d), 1.0),
        'ctx': nrm(ks[2], (BATCH, CTX_LEN, d), 1.0),
        'c_ctx': nrm(ks[3], (d,), 1.0),
        'w_mod': nrm(ks[4], (L, d, 6 * d), 0.5 * inv),
        'b_mod': nrm(ks[5], (L, 6 * d), 0.01),
        'g_norm1': 1.0 + nrm(ks[6], (L, d), 0.1),
        'w_in': nrm(ks[7], (L, d, IN_WIDTH), inv),
        'ml_conv': nrm(ks[8], (L, CONV_K, CONV_K, 2 * ML_QK_W), 1.0 / CONV_K),
        'ml_conv_b': nrm(ks[9], (L, 2 * ML_QK_W), 0.01),
        'b_mgate': (nrm(ks[10], (L, 2, 2, ML_HEADS), 0.1) + gate_offset).reshape(L, 4 * ML_HEADS),
        'ml_norm': 1.0 + nrm(ks[11], (L, ML_V_W), 0.1),
        'gla_w2': nrm(ks[12], (L, 2, GLA_RANK, GLA_QK_W), GLA_RANK ** -0.5),
        'gla_b2': nrm(ks[13], (L, 2, GLA_QK_W), 0.01),
        'gla_norm': 1.0 + nrm(ks[14], (L, GLA_V_W), 0.1),
        'w_proj_m': nrm(ks[15], (L, ML_V_W, d), ML_V_W ** -0.5),
        'w_proj_g': nrm(ks[16], (L, GLA_V_W, d), GLA_V_W ** -0.5),
        'w_out': nrm(ks[17], (L, d, d), inv),
        'g_norm2': 1.0 + nrm(ks[18], (L, d), 0.1),
        'w_grp': nrm(ks[19], (L, d, N_GROUPS), inv),
        'b_grp': nrm(ks[20], (L, N_GROUPS), 0.01),
        'w_rexp': nrm(ks[21], (L, d, N_EXPERTS), inv),
        'b_rexp': nrm(ks[22], (L, N_EXPERTS), 0.01),
        'w_up': nrm(ks[23], (L, N_EXPERTS, d, 2 * D_EXPERT), inv),
        'w_down': nrm(ks[24], (L, N_EXPERTS, D_EXPERT, d), D_EXPERT ** -0.5),
        'g_final': 1.0 + nrm(ks[25], (d,), 0.1),
    }


def reference(x, c, ctx, c_ctx, w_mod, b_mod, g_norm1, w_in, ml_conv, ml_conv_b, b_mgate, ml_norm,
              gla_w2, gla_b2, gla_norm, w_proj_m, w_proj_g, w_out, g_norm2, w_grp, b_grp, w_rexp, b_rexp,
              w_up, w_down, g_final):
    bn, t, _ = x.shape
    rows = t // GRID_W
    conv_lat = functools.partial(dw_conv_silu, rows=rows, cols=GRID_W)
    conv_ctx = functools.partial(dw_conv_silu, rows=1, cols=ctx.shape[1])
    h_ctx = ctx
    for l in range(DEPTH):
        last = l == DEPTH - 1
        sh1, sc1, gt1, sh2, sc2, gt2 = [m[:, None, :] for m in jnp.split(jax.nn.silu(c) @ w_mod[l] + b_mod[l], 6, axis=-1)]
        sh1c, sc1c, gt1c, sh2c, sc2c, gt2c = jnp.split(jax.nn.silu(c_ctx) @ w_mod[l] + b_mod[l], 6, axis=-1)
        mix = functools.partial(token_mixers, w_in=w_in[l], ml_conv=ml_conv[l], ml_conv_b=ml_conv_b[l],
                                b_mgate=b_mgate[l], ml_norm=ml_norm[l], gla_w2=gla_w2[l], gla_b2=gla_b2[l],
                                gla_norm=gla_norm[l], w_proj_m=w_proj_m[l], w_proj_g=w_proj_g[l], w_out=w_out[l])
        ffn = functools.partial(hier_moe, w_grp=w_grp[l], b_grp=b_grp[l], w_rexp=w_rexp[l], b_rexp=b_rexp[l],
                                w_up=w_up[l], w_down=w_down[l])
        hc = rms_norm(h_ctx, g_norm1[l]) * (1.0 + sc1c) + sh1c
        ctx_out, ctx_states = mix(hc, conv_fn=conv_ctx, init_states=empty_states(bn), with_output=not last)
        hx = rms_norm(x, g_norm1[l]) * (1.0 + sc1) + sh1
        x_mix, _ = mix(hx, conv_fn=conv_lat, init_states=ctx_states, with_output=True)
        x = x + gt1 * x_mix
        x = x + gt2 * ffn(rms_norm(x, g_norm2[l]) * (1.0 + sc2) + sh2)
        if not last:
            h_ctx = h_ctx + gt1c * ctx_out
            h_ctx = h_ctx + gt2c * ffn(rms_norm(h_ctx, g_norm2[l]) * (1.0 + sc2c) + sh2c)
    return rms_norm(x, g_final)
```

```python
import functools

import jax
import jax.numpy as jnp
from jax import lax
from jax.experimental import pallas as pl
from jax.experimental.pallas import tpu as pltpu

D_MODEL = 1024
GRID_W = 64
CHUNK = 64
EPS = 1e-6
NEG_BIG = -1e30
HEADS = 4
QK = D_MODEL // 8
DV = D_MODEL // 4
QK_W = HEADS * QK
V_W = HEADS * DV
GLA_RANK = 16
GLA_TAU = 16.0
N_GROUPS = 4
EXPERTS_PER_GROUP = 4
N_EXPERTS = N_GROUPS * EXPERTS_PER_GROUP
D_EXPERT = D_MODEL // 2
IN_SIZES = (QK_W, QK_W, V_W, V_W, 4 * HEADS, QK_W, QK_W, V_W, V_W, 2 * GLA_RANK, D_MODEL, D_MODEL)

LANES = 128
MXU_DTYPE = jnp.bfloat16
F32 = jnp.float32
VMEM_LIMIT = 48 * 1024 * 1024

COL_QK_M, COL_V_M, COL_O_M, COL_QK_G, COL_V_G, COL_R_G, COL_MG_M, COL_MG_G = range(8)
MAIN_W = 8 * D_MODEL
SMALL_GATE0 = 2 * GLA_RANK
GLA_SAFE_DECAY = 80.0
ROUTE_E0 = N_GROUPS


def _dot(a, b):
    return jnp.dot(a.astype(MXU_DTYPE), b.astype(MXU_DTYPE), preferred_element_type=F32)


def _dot_nt(a, b):
    return lax.dot_general(a.astype(MXU_DTYPE), b.astype(MXU_DTYPE), (((1,), (1,)), ((), ())),
                           preferred_element_type=F32)


def _dot_tn(a, b):
    return lax.dot_general(a.astype(MXU_DTYPE), b.astype(MXU_DTYPE), (((0,), (0,)), ((), ())),
                           preferred_element_type=F32)


def _split3(x):
    hi = x.astype(MXU_DTYPE)
    r1 = x - hi.astype(F32)
    mid = r1.astype(MXU_DTYPE)
    lo = (r1 - mid.astype(F32)).astype(MXU_DTYPE)
    return hi, mid, lo


def _dot_exact_lhs(a01, x):
    hi, mid, lo = _split3(x)
    return _dot(a01, hi) + _dot(a01, mid) + _dot(a01, lo)


def _log_sigmoid(x):
    return jnp.minimum(x, 0.0) - jnp.log1p(jnp.exp(-jnp.abs(x)))


def _silu(x):
    return x * jax.nn.sigmoid(x)


def _rms(x):
    return x * lax.rsqrt(jnp.mean(x * x, axis=-1, keepdims=True) + EPS)


def _cparams(sem):
    return pltpu.CompilerParams(dimension_semantics=sem, vmem_limit_bytes=VMEM_LIMIT)


def _mod_kernel(c_ref, w_ref, b_ref, o_ref):
    o_ref[...] = _dot(_silu(c_ref[...]), w_ref[...]) + b_ref[...]


def _modulation(cc, w_mod, b_mod):
    n = w_mod.shape[1]
    tn = 512
    return pl.pallas_call(
        _mod_kernel,
        grid=(n // tn,),
        in_specs=[pl.BlockSpec((8, D_MODEL), lambda j: (0, 0)),
                  pl.BlockSpec((D_MODEL, tn), lambda j: (0, j)),
                  pl.BlockSpec((1, tn), lambda j: (0, j))],
        out_specs=pl.BlockSpec((8, tn), lambda j: (0, j)),
        out_shape=jax.ShapeDtypeStruct((8, n), F32),
        compiler_params=_cparams(("arbitrary",)),
        name="modulation",
    )(cc, w_mod, b_mod)


def _inproj_kernel(x_ref, g_ref, sc_ref, sh_ref, w_ref, ws_ref, o_ref, os_ref, xn_ref):
    @pl.when(pl.program_id(1) == 0)
    def _():
        xn = _rms(x_ref[...]) * g_ref[...] * (1.0 + sc_ref[...]) + sh_ref[...]
        xn_ref[...] = xn.astype(MXU_DTYPE)
        os_ref[...] = _dot(xn_ref[...], ws_ref[...])

    o_ref[...] = _dot(xn_ref[...], w_ref[...])


def _in_proj(x2, g, sc, sh, w_main, w_small, *, tm, rows_per_batch):
    m = x2.shape[0]
    tn = 512
    tiles_per_batch = rows_per_batch // tm
    vec = pl.BlockSpec((None, 1, D_MODEL), lambda i, j: (i // tiles_per_batch, 0, 0))
    return pl.pallas_call(
        _inproj_kernel,
        grid=(m // tm, MAIN_W // tn),
        in_specs=[pl.BlockSpec((tm, D_MODEL), lambda i, j: (i, 0)),
                  pl.BlockSpec((1, D_MODEL), lambda i, j: (0, 0)),
                  vec, vec,
                  pl.BlockSpec((D_MODEL, tn), lambda i, j: (0, j)),
                  pl.BlockSpec((D_MODEL, LANES), lambda i, j: (0, 0))],
        out_specs=[pl.BlockSpec((tm, tn), lambda i, j: (i, j)),
                   pl.BlockSpec((tm, LANES), lambda i, j: (i, 0))],
        out_shape=[jax.ShapeDtypeStruct((m, MAIN_W), F32), jax.ShapeDtypeStruct((m, LANES), F32)],
        scratch_shapes=[pltpu.VMEM((tm, D_MODEL), MXU_DTYPE)],
        compiler_params=_cparams(("parallel", "arbitrary")),
        name="in_proj",
    )(x2, g, sc, sh, w_main, w_small)


def _conv_kernel(x_ref, w_ref, b_ref, o_ref, *, rows, cols):
    scale = jnp.where(pl.program_id(1) * LANES >= QK_W, QK ** -0.5, 1.0).astype(F32)
    w = w_ref[...]
    bias = b_ref[...]
    tpos = lax.broadcasted_iota(jnp.int32, (cols, 1), 0)

    def body(r, carry):
        acc = jnp.zeros((cols, LANES), F32)
        for dr in ((-1, 0, 1) if rows > 1 else (0,)):
            rr = r + dr
            row_ok = jnp.logical_and(rr >= 0, rr < rows)
            start = pl.multiple_of(jnp.clip(rr, 0, rows - 1) * cols, cols)
            tile = x_ref[pl.ds(start, cols), :]
            for dc in (-1, 0, 1):
                if dc == 0:
                    shifted = tile
                else:
                    shifted = pltpu.roll(tile, (-dc) % cols, axis=0)
                    shifted = jnp.where(jnp.logical_and(tpos + dc >= 0, tpos + dc < cols), shifted, 0.0)
                k = (dr + 1) * 3 + (dc + 1)
                acc = acc + shifted * jnp.where(row_ok, w[k:k + 1, :], 0.0)
        o_ref[pl.ds(pl.multiple_of(r * cols, cols), cols), :] = _silu(acc + bias) * scale
        return carry

    lax.fori_loop(0, rows, body, 0)


def _conv_silu(main, conv_w, conv_b, *, batch, rows, cols):
    t = rows * cols
    nct = 2 * QK_W // LANES
    return pl.pallas_call(
        functools.partial(_conv_kernel, rows=rows, cols=cols),
        grid=(batch, nct),
        in_specs=[pl.BlockSpec((t, LANES), lambda b, c: (b, c)),
                  pl.BlockSpec((9, LANES), lambda b, c: (0, c)),
                  pl.BlockSpec((1, LANES), lambda b, c: (0, c))],
        out_specs=pl.BlockSpec((t, LANES), lambda b, c: (b, c)),
        out_shape=jax.ShapeDtypeStruct((batch * t, 2 * QK_W), F32),
        compiler_params=_cparams(("parallel", "arbitrary")),
        name="conv_silu",
    )(main, conv_w, conv_b)


def _chunk_masks(direction):
    row = lax.broadcasted_iota(jnp.int32, (CHUNK, CHUNK), 0)
    col = lax.broadcasted_iota(jnp.int32, (CHUNK, CHUNK), 1)
    seen = (row >= col) if direction == 0 else (row <= col)
    return seen, seen.astype(MXU_DTYPE)


def _scan_specs(batch, nc, col_blocks, widths):
    specs = []
    for direction in (0, 1):
        for cb, wd in zip(col_blocks, widths):
            if direction == 0:
                specs.append(pl.BlockSpec((CHUNK, wd), lambda b, c, cb=cb: (b * nc + c, cb)))
            else:
                specs.append(pl.BlockSpec((CHUNK, wd), lambda b, c, cb=cb: (b * nc + (nc - 1 - c), cb)))
    return specs


def _state_spec(shape):
    nd = len(shape)
    return pl.BlockSpec((None,) + tuple(shape[1:]), lambda b, c: (b,) + (0,) * (nd - 1))


def _mlstm_kernel(*refs, with_output):
    (qk_f, v_f, sm_f, qk_b, v_b, sm_b, bias_ref, c0_ref, n0_ref, m0_ref) = refs[:10]
    if with_output:
        hf_ref, hb_ref, c_s, n_s, m_s = refs[10:]
    else:
        c_out, n_out, m_out, c_s, n_s, m_s = refs[10:]
    step = pl.program_id(1)

    @pl.when(step == 0)
    def _():
        c_s[...] = c0_ref[...]
        n_s[...] = n0_ref[...]
        m_s[...] = m0_ref[...]

    lane = lax.broadcasted_iota(jnp.int32, (1, LANES), 1)
    gate_lane = jnp.logical_and(lane >= SMALL_GATE0, lane < SMALL_GATE0 + 4 * HEADS)
    forget_lane = jnp.logical_and(gate_lane, ((lane - SMALL_GATE0) % (2 * HEADS)) >= HEADS)
    eye = (lax.broadcasted_iota(jnp.int32, (LANES, LANES), 0)
           == lax.broadcasted_iota(jnp.int32, (LANES, LANES), 1)).astype(MXU_DTYPE)

    for direction, (qk_ref, v_ref, sm_ref) in enumerate(((qk_f, v_f, sm_f), (qk_b, v_b, sm_b))):
        seen, seen01 = _chunk_masks(direction)
        g = sm_ref[...] + bias_ref[...]
        gp = jnp.where(forget_lane, _log_sigmoid(g), g)
        bc = _dot_exact_lhs(seen01, gp)
        hi, mid, lo = _split3(gp)
        gp_t = _dot_nt(eye, hi) + _dot_nt(eye, mid) + _dot_nt(eye, lo)
        hi, mid, lo = _split3(bc)
        bc_t = _dot_nt(eye, hi) + _dot_nt(eye, mid) + _dot_nt(eye, lo)
        last = CHUNK - 1 if direction == 0 else 0
        bend_row = bc[last:last + 1, :]
        qk = qk_ref[...]
        v_all = v_ref[...]
        outs = []
        for h in range(HEADS):
            ji = SMALL_GATE0 + direction * 2 * HEADS + h
            jf = ji + HEADS
            q = qk[:, h * QK:(h + 1) * QK]
            k = qk[:, QK_W + h * QK:QK_W + (h + 1) * QK]
            v = v_all[:, h * DV:(h + 1) * DV]
            c_old = c_s[direction, h]
            n_old = n_s[direction, h]
            m_old = m_s[direction, h][:, :1]
            b_end = bend_row[:, jf:jf + 1]
            i_col = gp[:, ji:ji + 1]
            b_col = bc[:, jf:jf + 1]
            log_w = b_end - b_col + i_col
            m_new = jnp.maximum(b_end + m_old, jnp.max(log_w, axis=0, keepdims=True))
            w_col = jnp.exp(log_w - m_new)
            decay = jnp.exp(b_end + m_old - m_new)
            kw = k * w_col
            c_s[direction, h] = decay * c_old + _dot_tn(kw, v)
            n_s[direction, h] = decay * n_old + jnp.sum(kw, axis=0, keepdims=True)
            m_s[direction, h] = jnp.broadcast_to(m_new, (1, LANES))
            if with_output:
                i_row = gp_t[ji:ji + 1, :]
                b_row = bc_t[jf:jf + 1, :]
                log_d = jnp.where(seen, b_col - b_row + i_row, -jnp.inf)
                log_inter = b_col + m_old
                m_t = jnp.maximum(log_inter, jnp.max(log_d, axis=-1, keepdims=True))
                s = _dot_nt(q, k) * jnp.exp(log_d - m_t)
                w_inter = jnp.exp(log_inter - m_t)
                num = _dot(s, v) + w_inter * _dot(q, c_old)
                den = jnp.abs(jnp.sum(s, axis=-1, keepdims=True)
                              + w_inter * jnp.sum(q * n_old, axis=-1, keepdims=True))
                outs.append(num / jnp.maximum(den, jnp.exp(-m_t)))
        if with_output:
            (hf_ref if direction == 0 else hb_ref)[...] = jnp.concatenate(outs, axis=-1)

    if not with_output:
        @pl.when(step == pl.num_programs(1) - 1)
        def _():
            c_out[...] = c_s[...]
            n_out[...] = n_s[...]
            m_out[...] = m_s[...]


def _mlstm_scan(qk, main, small, gate_bias, states, *, batch, nc, with_output):
    c0, n0, m0 = states
    in_specs = _scan_specs(batch, nc, (0, COL_V_M, 0), (2 * QK_W, V_W, LANES))
    in_specs += [pl.BlockSpec((1, LANES), lambda b, c: (0, 0)),
                 _state_spec(c0.shape), _state_spec(n0.shape), _state_spec(m0.shape)]
    m_rows = qk.shape[0]
    if with_output:
        out_specs = [pl.BlockSpec((CHUNK, V_W), lambda b, c: (b * nc + c, 0)),
                     pl.BlockSpec((CHUNK, V_W), lambda b, c: (b * nc + (nc - 1 - c), 0))]
        out_shape = [jax.ShapeDtypeStruct((m_rows, V_W), F32)] * 2
    else:
        out_specs = [_state_spec(c0.shape), _state_spec(n0.shape), _state_spec(m0.shape)]
        out_shape = [jax.ShapeDtypeStruct(s.shape, F32) for s in states]
    return pl.pallas_call(
        functools.partial(_mlstm_kernel, with_output=with_output),
        grid=(batch, nc),
        in_specs=in_specs,
        out_specs=out_specs,
        out_shape=out_shape,
        scratch_shapes=[pltpu.VMEM(c0.shape[1:], F32), pltpu.VMEM(n0.shape[1:], F32),
                        pltpu.VMEM(m0.shape[1:], F32)],
        compiler_params=_cparams(("parallel", "arbitrary")),
        name="mlstm_scan_out" if with_output else "mlstm_scan_state",
    )(qk, main, small, qk, main, small, gate_bias, c0, n0, m0)


def _gla_kernel(*refs, with_output):
    (qk_f, v_f, sm_f, qk_b, v_b, sm_b, w2_ref, b2_ref, s0_ref) = refs[:9]
    if with_output:
        of_ref, ob_ref, s_s, intra_s = refs[9:]
    else:
        s_out, s_s = refs[9:]
    step = pl.program_id(1)

    @pl.when(step == 0)
    def _():
        s_s[...] = s0_ref[...]

    row_id = lax.broadcasted_iota(jnp.int32, (CHUNK, 1), 0)

    for direction, (qk_ref, v_ref, sm_ref) in enumerate(((qk_f, v_f, sm_f), (qk_b, v_b, sm_b))):
        seen, seen01 = _chunk_masks(direction)
        z = _dot(sm_ref[...], w2_ref[direction]) + b2_ref[direction]
        log_a = _log_sigmoid(z) * (1.0 / GLA_TAU)
        b_all = _dot_exact_lhs(seen01, log_a)
        last = CHUNK - 1 if direction == 0 else 0
        outs = []
        for h in range(HEADS):
            q = qk_ref[:, h * QK:(h + 1) * QK] * (QK ** -0.5)
            k = qk_ref[:, QK_W + h * QK:QK_W + (h + 1) * QK]
            v = v_ref[:, h * DV:(h + 1) * DV]
            b = b_all[:, h * QK:(h + 1) * QK]
            b_end = b[last:last + 1, :]
            st_old = s_s[direction, h]
            k_dec = k * jnp.exp(b_end - b)
            s_s[direction, h] = st_old * jnp.exp(b_end) + _dot_tn(v, k_dec)
            if with_output:
                q_dec = q * jnp.exp(b)
                safe = jnp.max(-b_end) <= GLA_SAFE_DECAY

                scores = jnp.where(seen, _dot_nt(q_dec, k * jnp.exp(-b)), 0.0)
                intra_s[...] = _dot(scores, v)

                @pl.when(jnp.logical_not(safe))
                def _():
                    def row(t, acc):
                        pick = row_id == t
                        b_t = jnp.sum(jnp.where(pick, b, 0.0), axis=0, keepdims=True)
                        q_t = jnp.sum(jnp.where(pick, q, 0.0), axis=0, keepdims=True)
                        ok = (row_id <= t) if direction == 0 else (row_id >= t)
                        e = jnp.exp(jnp.where(ok, b_t - b, -jnp.inf))
                        sc = jnp.sum(q_t * k * e, axis=-1, keepdims=True)
                        o_t = jnp.sum(sc * v, axis=0, keepdims=True)
                        return jnp.where(pick, o_t, acc)

                    intra_s[...] = lax.fori_loop(0, CHUNK, row, jnp.zeros((CHUNK, DV), F32))

                outs.append(intra_s[...] + _dot_nt(q_dec, st_old))
        if with_output:
            (of_ref if direction == 0 else ob_ref)[...] = jnp.concatenate(outs, axis=-1)

    if not with_output:
        @pl.when(step == pl.num_programs(1) - 1)
        def _():
            s_out[...] = s_s[...]


def _gla_scan(main, small, w2p, b2, s0, *, batch, nc, with_output):
    in_specs = _scan_specs(batch, nc, (COL_QK_G, COL_V_G, 0), (2 * QK_W, V_W, LANES))
    in_specs += [pl.BlockSpec(w2p.shape, lambda b, c: (0, 0, 0)),
                 pl.BlockSpec(b2.shape, lambda b, c: (0, 0, 0)),
                 _state_spec(s0.shape)]
    m_rows = main.shape[0]
    scratch = [pltpu.VMEM(s0.shape[1:], F32)]
    if with_output:
        out_specs = [pl.BlockSpec((CHUNK, V_W), lambda b, c: (b * nc + c, 0)),
                     pl.BlockSpec((CHUNK, V_W), lambda b, c: (b * nc + (nc - 1 - c), 0))]
        out_shape = [jax.ShapeDtypeStruct((m_rows, V_W), F32)] * 2
        scratch += [pltpu.VMEM((CHUNK, DV), F32)]
    else:
        out_specs = _state_spec(s0.shape)
        out_shape = jax.ShapeDtypeStruct(s0.shape, F32)
    return pl.pallas_call(
        functools.partial(_gla_kernel, with_output=with_output),
        grid=(batch, nc),
        in_specs=in_specs,
        out_specs=out_specs,
        out_shape=out_shape,
        scratch_shapes=scratch,
        compiler_params=_cparams(("parallel", "arbitrary")),
        name="gla_scan_out" if with_output else "gla_scan_state",
    )(main, main, small, main, main, small, w2p, b2, s0)


def _head_rms(a):
    return jnp.concatenate([_rms(a[:, h * DV:(h + 1) * DV]) for h in range(HEADS)], axis=-1)


def _merge_kernel(hmf, hmb, ogf, ogb, om, rg, mgm, mgg, x_ref, mln, gln, wpm, wpg, wo, gt1, g2, sc2, sh2,
                  wr, br, x1_ref, h2_ref, wt_ref):
    y_m = _head_rms(hmf[...] + hmb[...]) * mln[...] * jax.nn.sigmoid(om[...])
    y_g = _head_rms(ogf[...] + ogb[...]) * gln[...] * _silu(rg[...])
    y = jax.nn.sigmoid(mgm[...]) * _dot(y_m, wpm[...]) + jax.nn.sigmoid(mgg[...]) * _dot(y_g, wpg[...])
    x1 = x_ref[...] + gt1[...] * _dot(y, wo[...])
    x1_ref[...] = x1
    h2 = _rms(x1) * g2[...] * (1.0 + sc2[...]) + sh2[...]
    h2_ref[...] = h2.astype(MXU_DTYPE)

    hh, hm_, _ = _split3(h2)
    wh, wm_, _ = _split3(wr[...])
    lg = _dot(hh, wh) + _dot(hh, wm_) + _dot(hm_, wh) + br[...]
    lane = lax.broadcasted_iota(jnp.int32, lg.shape, 1)

    def masked_softmax(mask):
        l = jnp.where(mask, lg, -jnp.inf)
        e = jnp.exp(l - jnp.max(l, axis=-1, keepdims=True))
        return e / jnp.sum(e, axis=-1, keepdims=True)

    def top1(p, mask):
        pm = jnp.where(mask, p, -1.0)
        best = jnp.max(pm, axis=-1, keepdims=True)
        idx = jnp.min(jnp.where(jnp.logical_and(mask, pm == best), lane, LANES), axis=-1, keepdims=True)
        return best, idx

    gmask = lane < N_GROUPS
    grp_p, grp = top1(masked_softmax(gmask), gmask)
    e_lo = ROUTE_E0 + grp * EXPERTS_PER_GROUP
    emask = jnp.logical_and(lane >= e_lo, lane < e_lo + EXPERTS_PER_GROUP)
    p_in = masked_softmax(emask)
    p1, i1 = top1(p_in, emask)
    p2, i2 = top1(p_in, jnp.logical_and(emask, lane != i1))
    tot = p1 + p2
    wt_ref[...] = (jnp.where(lane == i1, grp_p * p1 / tot, 0.0)
                   + jnp.where(lane == i2, grp_p * p2 / tot, 0.0))


def _merge(hmf, hmb, ogf, ogb, main, x2, mln, gln, wpm, wpg, wo, gt1, g2, sc2, sh2, wr, br, *, tm, rows_per_batch):
    m = x2.shape[0]
    tpb = rows_per_batch // tm
    rowblk = pl.BlockSpec((tm, D_MODEL), lambda i: (i, 0))
    colblk = lambda cb: pl.BlockSpec((tm, D_MODEL), lambda i, cb=cb: (i, cb))
    vec = pl.BlockSpec((1, D_MODEL), lambda i: (0, 0))
    bvec = pl.BlockSpec((None, 1, D_MODEL), lambda i: (i // tpb, 0, 0))
    wmat = pl.BlockSpec((D_MODEL, D_MODEL), lambda i: (0, 0))
    return pl.pallas_call(
        _merge_kernel,
        grid=(m // tm,),
        in_specs=[rowblk, rowblk, rowblk, rowblk, colblk(COL_O_M), colblk(COL_R_G), colblk(COL_MG_M),
                  colblk(COL_MG_G), rowblk, vec, vec, wmat, wmat, wmat, bvec, vec, bvec, bvec,
                  pl.BlockSpec((D_MODEL, LANES), lambda i: (0, 0)), pl.BlockSpec((1, LANES), lambda i: (0, 0))],
        out_specs=[rowblk, rowblk, pl.BlockSpec((tm, LANES), lambda i: (i, 0))],
        out_shape=[jax.ShapeDtypeStruct((m, D_MODEL), F32), jax.ShapeDtypeStruct((m, D_MODEL), MXU_DTYPE),
                   jax.ShapeDtypeStruct((m, LANES), F32)],
        compiler_params=_cparams(("parallel",)),
        name="merge_route",
    )(hmf, hmb, ogf, ogb, main, main, main, main, x2, mln, gln, wpm, wpg, wo, gt1, g2, sc2, sh2, wr, br)


def _moe_kernel(h2_ref, wt_ref, wup_ref, wdn_ref, x1_ref, gt2_ref, gf_ref, o_ref, acc_ref):
    e = pl.program_id(1)

    @pl.when(e == 0)
    def _():
        acc_ref[...] = jnp.zeros_like(acc_ref)

    gu = _dot(h2_ref[...], wup_ref[...])
    hidden = _silu(gu[:, :D_EXPERT]) * gu[:, D_EXPERT:]
    y = _dot(hidden, wdn_ref[...])
    lane = lax.broadcasted_iota(jnp.int32, wt_ref.shape, 1)
    w_col = jnp.sum(jnp.where(lane == ROUTE_E0 + e, wt_ref[...], 0.0), axis=-1, keepdims=True)
    acc_ref[...] += y * w_col

    @pl.when(e == N_EXPERTS - 1)
    def _():
        o_ref[...] = _rms(x1_ref[...] + gt2_ref[...] * acc_ref[...]) * gf_ref[...]


def _moe_final(h2, wt, w_up, w_down, x1, gt2, g_final, *, tm, rows_per_batch):
    m = h2.shape[0]
    tpb = rows_per_batch // tm
    rowblk = pl.BlockSpec((tm, D_MODEL), lambda i, e: (i, 0))
    return pl.pallas_call(
        _moe_kernel,
        grid=(m // tm, N_EXPERTS),
        in_specs=[rowblk,
                  pl.BlockSpec((tm, LANES), lambda i, e: (i, 0)),
                  pl.BlockSpec((None, D_MODEL, 2 * D_EXPERT), lambda i, e: (e, 0, 0)),
                  pl.BlockSpec((None, D_EXPERT, D_MODEL), lambda i, e: (e, 0, 0)),
                  rowblk,
                  pl.BlockSpec((None, 1, D_MODEL), lambda i, e: (i // tpb, 0, 0)),
                  pl.BlockSpec((1, D_MODEL), lambda i, e: (0, 0))],
        out_specs=rowblk,
        out_shape=jax.ShapeDtypeStruct((m, D_MODEL), F32),
        scratch_shapes=[pltpu.VMEM((tm, D_MODEL), F32)],
        compiler_params=_cparams(("parallel", "arbitrary")),
        name="moe_final",
    )(h2, wt, w_up, w_down, x1, gt2, g_final)


def _empty_states(batch):
    ml = (jnp.zeros((batch, 2, HEADS, QK, DV), F32),
          jnp.zeros((batch, 2, HEADS, 1, QK), F32),
          jnp.full((batch, 2, HEADS, 1, LANES), NEG_BIG, F32))
    gla = jnp.zeros((batch, 2, HEADS, DV, QK), F32)
    return ml, gla


def kernel(x, c, ctx, c_ctx, w_mod, b_mod, g_norm1, w_in, ml_conv, ml_conv_b, b_mgate, ml_norm, gla_w2, gla_b2,
           gla_norm, w_proj_m, w_proj_g, w_out, g_norm2, w_grp, b_grp, w_rexp, b_rexp, w_up, w_down, g_final):
    batch, t, d = x.shape
    t_ctx = ctx.shape[1]
    assert d == D_MODEL and w_mod.shape[0] == 1 and w_in.shape[2] == sum(IN_SIZES)
    assert t % (GRID_W * 16) == 0 and t_ctx % CHUNK == 0 and GRID_W == CHUNK

    off = [0]
    for s in IN_SIZES:
        off.append(off[-1] + s)
    wi = w_in[0]
    w_main = jnp.concatenate([wi[:, off[0]:off[4]], wi[:, off[5]:off[9]], wi[:, off[10]:off[12]]], axis=1)
    w_small = jnp.concatenate([wi[:, off[9]:off[10]], wi[:, off[4]:off[5]],
                               jnp.zeros((d, LANES - 2 * GLA_RANK - 4 * HEADS), F32)], axis=1)
    w_main = w_main.astype(MXU_DTYPE)
    w_small = w_small.astype(MXU_DTYPE)
    gate_bias = jnp.zeros((1, LANES), F32).at[0, SMALL_GATE0:SMALL_GATE0 + 4 * HEADS].set(b_mgate[0])
    w2p = jnp.zeros((2, LANES, QK_W), F32)
    w2p = w2p.at[0, 0:GLA_RANK].set(gla_w2[0, 0]).at[1, GLA_RANK:2 * GLA_RANK].set(gla_w2[0, 1])
    b2 = gla_b2[0][:, None, :]
    conv_w = ml_conv[0].reshape(9, 2 * QK_W)
    conv_b = ml_conv_b[0][None, :]
    w_route = jnp.concatenate([w_grp[0], w_rexp[0], jnp.zeros((d, LANES - N_GROUPS - N_EXPERTS), F32)], axis=1)
    b_route = jnp.concatenate([b_grp[0], b_rexp[0], jnp.zeros((LANES - N_GROUPS - N_EXPERTS,), F32)])[None, :]

    cc = jnp.concatenate([c, c_ctx[None, :], jnp.zeros((8 - batch - 1, d), F32)], axis=0)
    mod = _modulation(cc, w_mod[0], b_mod[0][None, :])
    sh1, sc1, gt1, sh2, sc2, gt2 = [mod[:batch, i * d:(i + 1) * d][:, None, :] for i in range(6)]
    sh1c, sc1c = [jnp.broadcast_to(mod[batch:batch + 1, i * d:(i + 1) * d][:, None, :], (batch, 1, d)) for i in range(2)]
    g1 = g_norm1[0][None, :]

    nc_ctx = t_ctx // CHUNK
    main_c, small_c = _in_proj(ctx.reshape(batch * t_ctx, d), g1, sc1c, sh1c, w_main, w_small,
                               tm=t_ctx, rows_per_batch=t_ctx)
    qk_c = _conv_silu(main_c, conv_w, conv_b, batch=batch, rows=1, cols=t_ctx)
    ml0, gla0 = _empty_states(batch)
    ml_states = _mlstm_scan(qk_c, main_c, small_c, gate_bias, ml0, batch=batch, nc=nc_ctx, with_output=False)
    gla_state = _gla_scan(main_c, small_c, w2p, b2, gla0, batch=batch, nc=nc_ctx, with_output=False)

    nc = t // CHUNK
    x2 = x.reshape(batch * t, d)
    main, small = _in_proj(x2, g1, sc1, sh1, w_main, w_small, tm=1024, rows_per_batch=t)
    qk = _conv_silu(main, conv_w, conv_b, batch=batch, rows=t // GRID_W, cols=GRID_W)
    hm_f, hm_b = _mlstm_scan(qk, main, small, gate_bias, ml_states, batch=batch, nc=nc, with_output=True)
    og_f, og_b = _gla_scan(main, small, w2p, b2, gla_state, batch=batch, nc=nc, with_output=True)

    x1, h2, wt = _merge(hm_f, hm_b, og_f, og_b, main, x2, ml_norm, gla_norm,
                        w_proj_m[0].astype(MXU_DTYPE), w_proj_g[0].astype(MXU_DTYPE), w_out[0].astype(MXU_DTYPE),
                        gt1, g_norm2, sc2, sh2, w_route, b_route, tm=256, rows_per_batch=t)
    out = _moe_final(h2, wt, w_up[0].astype(MXU_DTYPE), w_down[0].astype(MXU_DTYPE), x1, gt2, g_final[None, :],
                     tm=512, rows_per_batch=t)
    return out.reshape(batch, t, d)
```

```python
import functools

import jax
import jax.numpy as jnp
from jax import lax
from jax.experimental import pallas as pl
from jax.experimental.pallas import tpu as pltpu

D_MODEL = 1024
GRID_W = 64
CHUNK = 64
EPS = 1e-6
NEG_BIG = -1e30
HEADS = 4
QK = D_MODEL // 8
DV = D_MODEL // 4
QK_W = HEADS * QK
V_W = HEADS * DV
GLA_RANK = 16
GLA_TAU = 16.0
N_GROUPS = 4
EXPERTS_PER_GROUP = 4
N_EXPERTS = N_GROUPS * EXPERTS_PER_GROUP
D_EXPERT = D_MODEL // 2
IN_SIZES = (QK_W, QK_W, V_W, V_W, 4 * HEADS, QK_W, QK_W, V_W, V_W, 2 * GLA_RANK, D_MODEL, D_MODEL)

LANES = 128
MXU_DTYPE = jnp.bfloat16
F32 = jnp.float32
VMEM_LIMIT = 48 * 1024 * 1024

COL_QK_M, COL_V_M, COL_O_M, COL_QK_G, COL_V_G, COL_R_G, COL_MG_M, COL_MG_G = range(8)
MAIN_W = 8 * D_MODEL
SMALL_GATE0 = 2 * GLA_RANK
GLA_SAFE_DECAY = 80.0
ROUTE_E0 = N_GROUPS


def _dot(a, b):
    return jnp.dot(a.astype(MXU_DTYPE), b.astype(MXU_DTYPE), preferred_element_type=F32)


def _dot_nt(a, b):
    return lax.dot_general(a.astype(MXU_DTYPE), b.astype(MXU_DTYPE), (((1,), (1,)), ((), ())),
                           preferred_element_type=F32)


def _dot_tn(a, b):
    return lax.dot_general(a.astype(MXU_DTYPE), b.astype(MXU_DTYPE), (((0,), (0,)), ((), ())),
                           preferred_element_type=F32)


def _split3(x):
    hi = x.astype(MXU_DTYPE)
    r1 = x - hi.astype(F32)
    mid = r1.astype(MXU_DTYPE)
    lo = (r1 - mid.astype(F32)).astype(MXU_DTYPE)
    return hi, mid, lo


def _dot_exact_lhs(a01, x):
    hi, mid, lo = _split3(x)
    return _dot(a01, hi) + _dot(a01, mid) + _dot(a01, lo)


def _log_sigmoid(x):
    return jnp.minimum(x, 0.0) - jnp.log1p(jnp.exp(-jnp.abs(x)))


def _silu(x):
    return x * jax.nn.sigmoid(x)


def _rms(x):
    return x * lax.rsqrt(jnp.mean(x * x, axis=-1, keepdims=True) + EPS)


def _cparams(sem):
    return pltpu.CompilerParams(dimension_semantics=sem, vmem_limit_bytes=VMEM_LIMIT)


def _mod_kernel(c_ref, w_ref, b_ref, o_ref):
    o_ref[...] = _dot(_silu(c_ref[...]), w_ref[...]) + b_ref[...]


def _modulation(cc, w_mod, b_mod):
    n = w_mod.shape[1]
    tn = 512
    return pl.pallas_call(
        _mod_kernel,
        grid=(n // tn,),
        in_specs=[pl.BlockSpec((8, D_MODEL), lambda j: (0, 0)),
                  pl.BlockSpec((D_MODEL, tn), lambda j: (0, j)),
                  pl.BlockSpec((1, tn), lambda j: (0, j))],
        out_specs=pl.BlockSpec((8, tn), lambda j: (0, j)),
        out_shape=jax.ShapeDtypeStruct((8, n), F32),
        compiler_params=_cparams(("arbitrary",)),
        name="modulation",
    )(cc, w_mod, b_mod)


def _inproj_kernel(x_ref, g_ref, sc_ref, sh_ref, w_ref, ws_ref, o_ref, os_ref, xn_ref):
    @pl.when(pl.program_id(1) == 0)
    def _():
        xn = _rms(x_ref[...]) * g_ref[...] * (1.0 + sc_ref[...]) + sh_ref[...]
        xn_ref[...] = xn.astype(MXU_DTYPE)
        os_ref[...] = _dot(xn_ref[...], ws_ref[...])

    o_ref[...] = _dot(xn_ref[...], w_ref[...])


def _in_proj(x2, g, sc, sh, w_main, w_small, *, tm, rows_per_batch):
    m = x2.shape[0]
    tn = 1024
    tiles_per_batch = rows_per_batch // tm
    vec = pl.BlockSpec((None, 1, D_MODEL), lambda i, j: (i // tiles_per_batch, 0, 0))
    return pl.pallas_call(
        _inproj_kernel,
        grid=(m // tm, MAIN_W // tn),
        in_specs=[pl.BlockSpec((tm, D_MODEL), lambda i, j: (i, 0)),
                  pl.BlockSpec((1, D_MODEL), lambda i, j: (0, 0)),
                  vec, vec,
                  pl.BlockSpec((D_MODEL, tn), lambda i, j: (0, j)),
                  pl.BlockSpec((D_MODEL, LANES), lambda i, j: (0, 0))],
        out_specs=[pl.BlockSpec((tm, tn), lambda i, j: (i, j)),
                   pl.BlockSpec((tm, LANES), lambda i, j: (i, 0))],
        out_shape=[jax.ShapeDtypeStruct((m, MAIN_W), F32), jax.ShapeDtypeStruct((m, LANES), F32)],
        scratch_shapes=[pltpu.VMEM((tm, D_MODEL), MXU_DTYPE)],
        compiler_params=_cparams(("parallel", "arbitrary")),
        name="in_proj",
    )(x2, g, sc, sh, w_main, w_small)


def _conv_kernel(x_ref, w_ref, b_ref, o_ref, *, rows, cols):
    scale = jnp.where(pl.program_id(1) * LANES >= QK_W, QK ** -0.5, 1.0).astype(F32)
    w = w_ref[...]
    bias = b_ref[...]
    tpos = lax.broadcasted_iota(jnp.int32, (cols, 1), 0)

    def body(r, carry):
        acc = jnp.zeros((cols, LANES), F32)
        for dr in ((-1, 0, 1) if rows > 1 else (0,)):
            rr = r + dr
            row_ok = jnp.logical_and(rr >= 0, rr < rows)
            start = pl.multiple_of(jnp.clip(rr, 0, rows - 1) * cols, cols)
            tile = x_ref[pl.ds(start, cols), :]
            for dc in (-1, 0, 1):
                if dc == 0:
                    shifted = tile
                else:
                    shifted = pltpu.roll(tile, (-dc) % cols, axis=0)
                    shifted = jnp.where(jnp.logical_and(tpos + dc >= 0, tpos + dc < cols), shifted, 0.0)
                k = (dr + 1) * 3 + (dc + 1)
                acc = acc + shifted * jnp.where(row_ok, w[k:k + 1, :], 0.0)
        o_ref[pl.ds(pl.multiple_of(r * cols, cols), cols), :] = _silu(acc + bias) * scale
        return carry

    lax.fori_loop(0, rows, body, 0)


def _conv_silu(main, conv_w, conv_b, *, batch, rows, cols):
    t = rows * cols
    nct = 2 * QK_W // LANES
    return pl.pallas_call(
        functools.partial(_conv_kernel, rows=rows, cols=cols),
        grid=(batch, nct),
        in_specs=[pl.BlockSpec((t, LANES), lambda b, c: (b, c)),
                  pl.BlockSpec((9, LANES), lambda b, c: (0, c)),
                  pl.BlockSpec((1, LANES), lambda b, c: (0, c))],
        out_specs=pl.BlockSpec((t, LANES), lambda b, c: (b, c)),
        out_shape=jax.ShapeDtypeStruct((batch * t, 2 * QK_W), F32),
        compiler_params=_cparams(("parallel", "arbitrary")),
        name="conv_silu",
    )(main, conv_w, conv_b)


def _chunk_masks(direction):
    row = lax.broadcasted_iota(jnp.int32, (CHUNK, CHUNK), 0)
    col = lax.broadcasted_iota(jnp.int32, (CHUNK, CHUNK), 1)
    seen = (row >= col) if direction == 0 else (row <= col)
    return seen, seen.astype(MXU_DTYPE)


def _scan_specs(batch, nc, col_blocks, widths):
    specs = []
    for direction in (0, 1):
        for cb, wd in zip(col_blocks, widths):
            if direction == 0:
                specs.append(pl.BlockSpec((batch, CHUNK, wd), lambda c, cb=cb: (0, c, cb)))
            else:
                specs.append(pl.BlockSpec((batch, CHUNK, wd), lambda c, cb=cb: (0, nc - 1 - c, cb)))
    return specs


def _scan_out_specs(batch, nc):
    return [pl.BlockSpec((batch, CHUNK, V_W), lambda c: (0, c, 0)),
            pl.BlockSpec((batch, CHUNK, V_W), lambda c: (0, nc - 1 - c, 0))]


def _whole(shape):
    nd = len(shape)
    return pl.BlockSpec(tuple(shape), lambda c: (0,) * nd)


def _mlstm_kernel(*refs, with_output, batch):
    (qk_f, v_f, sm_f, qk_b, v_b, sm_b, bias_ref, c0_ref, n0_ref, m0_ref) = refs[:10]
    if with_output:
        hf_ref, hb_ref, c_s, n_s, m_s = refs[10:]
    else:
        c_out, n_out, m_out, c_s, n_s, m_s = refs[10:]
    step = pl.program_id(0)

    @pl.when(step == 0)
    def _():
        c_s[...] = c0_ref[...]
        n_s[...] = n0_ref[...]
        m_s[...] = m0_ref[...]

    lane = lax.broadcasted_iota(jnp.int32, (1, LANES), 1)
    gate_lane = jnp.logical_and(lane >= SMALL_GATE0, lane < SMALL_GATE0 + 4 * HEADS)
    forget_lane = jnp.logical_and(gate_lane, ((lane - SMALL_GATE0) % (2 * HEADS)) >= HEADS)
    eye = (lax.broadcasted_iota(jnp.int32, (LANES, LANES), 0)
           == lax.broadcasted_iota(jnp.int32, (LANES, LANES), 1)).astype(MXU_DTYPE)

    for direction, (qk_ref, v_ref, sm_ref) in enumerate(((qk_f, v_f, sm_f), (qk_b, v_b, sm_b))):
        seen, seen01 = _chunk_masks(direction)
        last = CHUNK - 1 if direction == 0 else 0
        for bi in range(batch):
            g = sm_ref[bi] + bias_ref[...]
            gp = jnp.where(forget_lane, _log_sigmoid(g), g)
            bc = _dot_exact_lhs(seen01, gp)
            hi, mid, lo = _split3(gp)
            gp_t = _dot_nt(eye, hi) + _dot_nt(eye, mid) + _dot_nt(eye, lo)
            hi, mid, lo = _split3(bc)
            bc_t = _dot_nt(eye, hi) + _dot_nt(eye, mid) + _dot_nt(eye, lo)
            bend_row = bc[last:last + 1, :]
            outs = []
            for h in range(HEADS):
                ji = SMALL_GATE0 + direction * 2 * HEADS + h
                jf = ji + HEADS
                q = qk_ref[bi, :, h * QK:(h + 1) * QK]
                k = qk_ref[bi, :, QK_W + h * QK:QK_W + (h + 1) * QK]
                v = v_ref[bi, :, h * DV:(h + 1) * DV]
                c_old = c_s[bi, direction, h]
                n_old = n_s[bi, direction, h]
                m_old = m_s[bi, direction, h][:, :1]
                b_end = bend_row[:, jf:jf + 1]
                i_col = gp[:, ji:ji + 1]
                b_col = bc[:, jf:jf + 1]
                log_w = b_end - b_col + i_col
                m_new = jnp.maximum(b_end + m_old, jnp.max(log_w, axis=0, keepdims=True))
                w_col = jnp.exp(log_w - m_new)
                decay = jnp.exp(b_end + m_old - m_new)
                kw = k * w_col
                c_s[bi, direction, h] = decay * c_old + _dot_tn(kw, v)
                n_s[bi, direction, h] = decay * n_old + jnp.sum(kw, axis=0, keepdims=True)
                m_s[bi, direction, h] = jnp.broadcast_to(m_new, (1, LANES))
                if with_output:
                    i_row = gp_t[ji:ji + 1, :]
                    b_row = bc_t[jf:jf + 1, :]
                    log_d = jnp.where(seen, b_col - b_row + i_row, -jnp.inf)
                    log_inter = b_col + m_old
                    m_t = jnp.maximum(log_inter, jnp.max(log_d, axis=-1, keepdims=True))
                    s = _dot_nt(q, k) * jnp.exp(log_d - m_t)
                    w_inter = jnp.exp(log_inter - m_t)
                    num = _dot(s, v) + w_inter * _dot(q, c_old)
                    den = jnp.abs(jnp.sum(s, axis=-1, keepdims=True)
                                  + w_inter * jnp.sum(q * n_old, axis=-1, keepdims=True))
                    outs.append(num / jnp.maximum(den, jnp.exp(-m_t)))
            if with_output:
                (hf_ref if direction == 0 else hb_ref)[bi] = jnp.concatenate(outs, axis=-1)

    if not with_output:
        @pl.when(step == pl.num_programs(0) - 1)
        def _():
            c_out[...] = c_s[...]
            n_out[...] = n_s[...]
            m_out[...] = m_s[...]


def _mlstm_scan(qk, main, small, gate_bias, states, *, with_output):
    c0, n0, m0 = states
    batch, t, _ = qk.shape
    nc = t // CHUNK
    in_specs = _scan_specs(batch, nc, (0, COL_V_M, 0), (2 * QK_W, V_W, LANES))
    in_specs += [_whole(gate_bias.shape), _whole(c0.shape), _whole(n0.shape), _whole(m0.shape)]
    if with_output:
        out_specs = _scan_out_specs(batch, nc)
        out_shape = [jax.ShapeDtypeStruct((batch, t, V_W), F32)] * 2
    else:
        out_specs = [_whole(c0.shape), _whole(n0.shape), _whole(m0.shape)]
        out_shape = [jax.ShapeDtypeStruct(s.shape, F32) for s in states]
    return pl.pallas_call(
        functools.partial(_mlstm_kernel, with_output=with_output, batch=batch),
        grid=(nc,),
        in_specs=in_specs,
        out_specs=out_specs,
        out_shape=out_shape,
        scratch_shapes=[pltpu.VMEM(c0.shape, F32), pltpu.VMEM(n0.shape, F32), pltpu.VMEM(m0.shape, F32)],
        compiler_params=_cparams(("arbitrary",)),
        name="mlstm_scan_out" if with_output else "mlstm_scan_state",
    )(qk, main, small, qk, main, small, gate_bias, c0, n0, m0)


def _gla_exact_intra(q, k, v, b, direction):
    row_id = lax.broadcasted_iota(jnp.int32, (CHUNK, 1), 0)

    def row(t, acc):
        pick = row_id == t
        b_t = jnp.sum(jnp.where(pick, b, 0.0), axis=0, keepdims=True)
        q_t = jnp.sum(jnp.where(pick, q, 0.0), axis=0, keepdims=True)
        ok = (row_id <= t) if direction == 0 else (row_id >= t)
        e = jnp.exp(jnp.where(ok, b_t - b, -jnp.inf))
        sc = jnp.sum(q_t * k * e, axis=-1, keepdims=True)
        o_t = jnp.sum(sc * v, axis=0, keepdims=True)
        return jnp.where(pick, o_t, acc)

    return lax.fori_loop(0, CHUNK, row, jnp.zeros((CHUNK, DV), F32))


def _gla_kernel(*refs, with_output, batch):
    (qk_f, v_f, sm_f, qk_b, v_b, sm_b, w2_ref, b2_ref, s0_ref) = refs[:9]
    if with_output:
        of_ref, ob_ref, s_s, b_s, inter_s = refs[9:]
    else:
        s_out, s_s = refs[9:]
    step = pl.program_id(0)

    @pl.when(step == 0)
    def _():
        s_s[...] = s0_ref[...]

    worst_decay = []
    for direction, (qk_ref, v_ref, sm_ref) in enumerate(((qk_f, v_f, sm_f), (qk_b, v_b, sm_b))):
        seen, seen01 = _chunk_masks(direction)
        last = CHUNK - 1 if direction == 0 else 0
        for bi in range(batch):
            z = _dot(sm_ref[bi], w2_ref[direction]) + b2_ref[direction]
            log_a = _log_sigmoid(z) * (1.0 / GLA_TAU)
            b_all = _dot_exact_lhs(seen01, log_a)
            outs, inters = [], []
            for h in range(HEADS):
                q = qk_ref[bi, :, h * QK:(h + 1) * QK] * (QK ** -0.5)
                k = qk_ref[bi, :, QK_W + h * QK:QK_W + (h + 1) * QK]
                v = v_ref[bi, :, h * DV:(h + 1) * DV]
                b = b_all[:, h * QK:(h + 1) * QK]
                b_end = b[last:last + 1, :]
                st_old = s_s[bi, direction, h]
                k_dec = k * jnp.exp(b_end - b)
                s_s[bi, direction, h] = st_old * jnp.exp(b_end) + _dot_tn(v, k_dec)
                if with_output:
                    q_dec = q * jnp.exp(b)
                    inter = _dot_nt(q_dec, st_old)
                    scores = jnp.where(seen, _dot_nt(q_dec, k * jnp.exp(-b)), 0.0)
                    outs.append(_dot(scores, v) + inter)
                    inters.append(inter)
            if with_output:
                (of_ref if direction == 0 else ob_ref)[bi] = jnp.concatenate(outs, axis=-1)
                b_s[bi, direction] = b_all
                inter_s[bi, direction] = jnp.concatenate(inters, axis=-1)
                worst_decay.append(jnp.max(-b_all[last:last + 1, :]))

    if with_output:
        @pl.when(functools.reduce(jnp.maximum, worst_decay) > GLA_SAFE_DECAY)
        def _():
            for direction, (qk_ref, v_ref, o_ref) in enumerate(((qk_f, v_f, of_ref), (qk_b, v_b, ob_ref))):
                last = CHUNK - 1 if direction == 0 else 0
                for bi in range(batch):
                    for h in range(HEADS):
                        b = b_s[bi, direction, :, h * QK:(h + 1) * QK]

                        @pl.when(jnp.max(-b[last:last + 1, :]) > GLA_SAFE_DECAY)
                        def _():
                            q = qk_ref[bi, :, h * QK:(h + 1) * QK] * (QK ** -0.5)
                            k = qk_ref[bi, :, QK_W + h * QK:QK_W + (h + 1) * QK]
                            v = v_ref[bi, :, h * DV:(h + 1) * DV]
                            o_ref[bi, :, h * DV:(h + 1) * DV] = (
                                inter_s[bi, direction, :, h * DV:(h + 1) * DV]
                                + _gla_exact_intra(q, k, v, b, direction))
    else:
        @pl.when(step == pl.num_programs(0) - 1)
        def _():
            s_out[...] = s_s[...]


def _gla_scan(main, small, w2p, b2, s0, *, with_output):
    batch, t, _ = main.shape
    nc = t // CHUNK
    in_specs = _scan_specs(batch, nc, (COL_QK_G, COL_V_G, 0), (2 * QK_W, V_W, LANES))
    in_specs += [_whole(w2p.shape), _whole(b2.shape), _whole(s0.shape)]
    scratch = [pltpu.VMEM(s0.shape, F32)]
    if with_output:
        out_specs = _scan_out_specs(batch, nc)
        out_shape = [jax.ShapeDtypeStruct((batch, t, V_W), F32)] * 2
        scratch += [pltpu.VMEM((batch, 2, CHUNK, QK_W), F32), pltpu.VMEM((batch, 2, CHUNK, V_W), F32)]
    else:
        out_specs = _whole(s0.shape)
        out_shape = jax.ShapeDtypeStruct(s0.shape, F32)
    return pl.pallas_call(
        functools.partial(_gla_kernel, with_output=with_output, batch=batch),
        grid=(nc,),
        in_specs=in_specs,
        out_specs=out_specs,
        out_shape=out_shape,
        scratch_shapes=scratch,
        compiler_params=_cparams(("arbitrary",)),
        name="gla_scan_out" if with_output else "gla_scan_state",
    )(main, main, small, main, main, small, w2p, b2, s0)


def _head_rms(a):
    return jnp.concatenate([_rms(a[:, h * DV:(h + 1) * DV]) for h in range(HEADS)], axis=-1)


def _merge_kernel(hmf, hmb, ogf, ogb, om, rg, mgm, mgg, x_ref, mln, gln, wpm, wpg, wo, gt1, g2, sc2, sh2,
                  wr, br, x1_ref, h2_ref, wt_ref):
    y_m = _head_rms(hmf[...] + hmb[...]) * mln[...] * jax.nn.sigmoid(om[...])
    y_g = _head_rms(ogf[...] + ogb[...]) * gln[...] * _silu(rg[...])
    y = jax.nn.sigmoid(mgm[...]) * _dot(y_m, wpm[...]) + jax.nn.sigmoid(mgg[...]) * _dot(y_g, wpg[...])
    x1 = x_ref[...] + gt1[...] * _dot(y, wo[...])
    x1_ref[...] = x1
    h2 = _rms(x1) * g2[...] * (1.0 + sc2[...]) + sh2[...]
    h2_ref[...] = h2.astype(MXU_DTYPE)

    hh, hm_, _ = _split3(h2)
    wh, wm_, _ = _split3(wr[...])
    lg = _dot(hh, wh) + _dot(hh, wm_) + _dot(hm_, wh) + br[...]
    lane = lax.broadcasted_iota(jnp.int32, lg.shape, 1)

    def masked_softmax(mask):
        l = jnp.where(mask, lg, -jnp.inf)
        e = jnp.exp(l - jnp.max(l, axis=-1, keepdims=True))
        return e / jnp.sum(e, axis=-1, keepdims=True)

    def top1(p, mask):
        pm = jnp.where(mask, p, -1.0)
        best = jnp.max(pm, axis=-1, keepdims=True)
        idx = jnp.min(jnp.where(jnp.logical_and(mask, pm == best), lane, LANES), axis=-1, keepdims=True)
        return best, idx

    gmask = lane < N_GROUPS
    grp_p, grp = top1(masked_softmax(gmask), gmask)
    e_lo = ROUTE_E0 + grp * EXPERTS_PER_GROUP
    emask = jnp.logical_and(lane >= e_lo, lane < e_lo + EXPERTS_PER_GROUP)
    p_in = masked_softmax(emask)
    p1, i1 = top1(p_in, emask)
    p2, i2 = top1(p_in, jnp.logical_and(emask, lane != i1))
    tot = p1 + p2
    wt_ref[...] = (jnp.where(lane == i1, grp_p * p1 / tot, 0.0)
                   + jnp.where(lane == i2, grp_p * p2 / tot, 0.0))


def _merge(hmf, hmb, ogf, ogb, main, x2, mln, gln, wpm, wpg, wo, gt1, g2, sc2, sh2, wr, br, *, tm, rows_per_batch):
    m = x2.shape[0]
    tpb = rows_per_batch // tm
    rowblk = pl.BlockSpec((tm, D_MODEL), lambda i: (i, 0))
    colblk = lambda cb: pl.BlockSpec((tm, D_MODEL), lambda i, cb=cb: (i, cb))
    vec = pl.BlockSpec((1, D_MODEL), lambda i: (0, 0))
    bvec = pl.BlockSpec((None, 1, D_MODEL), lambda i: (i // tpb, 0, 0))
    wmat = pl.BlockSpec((D_MODEL, D_MODEL), lambda i: (0, 0))
    return pl.pallas_call(
        _merge_kernel,
        grid=(m // tm,),
        in_specs=[rowblk, rowblk, rowblk, rowblk, colblk(COL_O_M), colblk(COL_R_G), colblk(COL_MG_M),
                  colblk(COL_MG_G), rowblk, vec, vec, wmat, wmat, wmat, bvec, vec, bvec, bvec,
                  pl.BlockSpec((D_MODEL, LANES), lambda i: (0, 0)), pl.BlockSpec((1, LANES), lambda i: (0, 0))],
        out_specs=[rowblk, rowblk, pl.BlockSpec((tm, LANES), lambda i: (i, 0))],
        out_shape=[jax.ShapeDtypeStruct((m, D_MODEL), F32), jax.ShapeDtypeStruct((m, D_MODEL), MXU_DTYPE),
                   jax.ShapeDtypeStruct((m, LANES), F32)],
        compiler_params=_cparams(("parallel",)),
        name="merge_route",
    )(hmf, hmb, ogf, ogb, main, main, main, main, x2, mln, gln, wpm, wpg, wo, gt1, g2, sc2, sh2, wr, br)


def _moe_kernel(h2_ref, wt_ref, wup_ref, wdn_ref, x1_ref, gt2_ref, gf_ref, o_ref, acc_ref):
    e = pl.program_id(1)

    @pl.when(e == 0)
    def _():
        acc_ref[...] = jnp.zeros_like(acc_ref)

    gu = _dot(h2_ref[...], wup_ref[...])
    hidden = _silu(gu[:, :D_EXPERT]) * gu[:, D_EXPERT:]
    y = _dot(hidden, wdn_ref[...])
    lane = lax.broadcasted_iota(jnp.int32, wt_ref.shape, 1)
    w_col = jnp.sum(jnp.where(lane == ROUTE_E0 + e, wt_ref[...], 0.0), axis=-1, keepdims=True)
    acc_ref[...] += y * w_col

    @pl.when(e == N_EXPERTS - 1)
    def _():
        o_ref[...] = _rms(x1_ref[...] + gt2_ref[...] * acc_ref[...]) * gf_ref[...]


def _moe_final(h2, wt, w_up, w_down, x1, gt2, g_final, *, tm, rows_per_batch):
    m = h2.shape[0]
    tpb = rows_per_batch // tm
    rowblk = pl.BlockSpec((tm, D_MODEL), lambda i, e: (i, 0))
    return pl.pallas_call(
        _moe_kernel,
        grid=(m // tm, N_EXPERTS),
        in_specs=[rowblk,
                  pl.BlockSpec((tm, LANES), lambda i, e: (i, 0)),
                  pl.BlockSpec((None, D_MODEL, 2 * D_EXPERT), lambda i, e: (e, 0, 0)),
                  pl.BlockSpec((None, D_EXPERT, D_MODEL), lambda i, e: (e, 0, 0)),
                  rowblk,
                  pl.BlockSpec((None, 1, D_MODEL), lambda i, e: (i // tpb, 0, 0)),
                  pl.BlockSpec((1, D_MODEL), lambda i, e: (0, 0))],
        out_specs=rowblk,
        out_shape=jax.ShapeDtypeStruct((m, D_MODEL), F32),
        scratch_shapes=[pltpu.VMEM((tm, D_MODEL), F32)],
        compiler_params=_cparams(("parallel", "arbitrary")),
        name="moe_final",
    )(h2, wt, w_up, w_down, x1, gt2, g_final)


def _empty_states(batch):
    ml = (jnp.zeros((batch, 2, HEADS, QK, DV), F32),
          jnp.zeros((batch, 2, HEADS, 1, QK), F32),
          jnp.full((batch, 2, HEADS, 1, LANES), NEG_BIG, F32))
    gla = jnp.zeros((batch, 2, HEADS, DV, QK), F32)
    return ml, gla


def kernel(x, c, ctx, c_ctx, w_mod, b_mod, g_norm1, w_in, ml_conv, ml_conv_b, b_mgate, ml_norm, gla_w2, gla_b2,
           gla_norm, w_proj_m, w_proj_g, w_out, g_norm2, w_grp, b_grp, w_rexp, b_rexp, w_up, w_down, g_final):
    batch, t, d = x.shape
    t_ctx = ctx.shape[1]
    assert d == D_MODEL and w_mod.shape[0] == 1 and w_in.shape[2] == sum(IN_SIZES)
    assert t % (GRID_W * 16) == 0 and t_ctx % CHUNK == 0 and GRID_W == CHUNK

    off = [0]
    for s in IN_SIZES:
        off.append(off[-1] + s)
    wi = w_in[0]
    w_main = jnp.concatenate([wi[:, off[0]:off[4]], wi[:, off[5]:off[9]], wi[:, off[10]:off[12]]], axis=1)
    w_small = jnp.concatenate([wi[:, off[9]:off[10]], wi[:, off[4]:off[5]],
                               jnp.zeros((d, LANES - 2 * GLA_RANK - 4 * HEADS), F32)], axis=1)
    w_main = w_main.astype(MXU_DTYPE)
    w_small = w_small.astype(MXU_DTYPE)
    gate_bias = jnp.zeros((1, LANES), F32).at[0, SMALL_GATE0:SMALL_GATE0 + 4 * HEADS].set(b_mgate[0])
    w2p = jnp.zeros((2, LANES, QK_W), F32)
    w2p = w2p.at[0, 0:GLA_RANK].set(gla_w2[0, 0]).at[1, GLA_RANK:2 * GLA_RANK].set(gla_w2[0, 1])
    b2 = gla_b2[0][:, None, :]
    conv_w = ml_conv[0].reshape(9, 2 * QK_W)
    conv_b = ml_conv_b[0][None, :]
    w_route = jnp.concatenate([w_grp[0], w_rexp[0], jnp.zeros((d, LANES - N_GROUPS - N_EXPERTS), F32)], axis=1)
    b_route = jnp.concatenate([b_grp[0], b_rexp[0], jnp.zeros((LANES - N_GROUPS - N_EXPERTS,), F32)])[None, :]

    cc = jnp.concatenate([c, c_ctx[None, :], jnp.zeros((8 - batch - 1, d), F32)], axis=0)
    mod = _modulation(cc, w_mod[0], b_mod[0][None, :])
    sh1, sc1, gt1, sh2, sc2, gt2 = [mod[:batch, i * d:(i + 1) * d][:, None, :] for i in range(6)]
    sh1c, sc1c = [jnp.broadcast_to(mod[batch:batch + 1, i * d:(i + 1) * d][:, None, :], (batch, 1, d)) for i in range(2)]
    g1 = g_norm1[0][None, :]

    main_c, small_c = _in_proj(ctx.reshape(batch * t_ctx, d), g1, sc1c, sh1c, w_main, w_small,
                               tm=t_ctx, rows_per_batch=t_ctx)
    qk_c = _conv_silu(main_c, conv_w, conv_b, batch=batch, rows=1, cols=t_ctx)
    ml0, gla0 = _empty_states(batch)
    main_c3, small_c3 = main_c.reshape(batch, t_ctx, MAIN_W), small_c.reshape(batch, t_ctx, LANES)
    ml_states = _mlstm_scan(qk_c.reshape(batch, t_ctx, 2 * QK_W), main_c3, small_c3, gate_bias, ml0, with_output=False)
    gla_state = _gla_scan(main_c3, small_c3, w2p, b2, gla0, with_output=False)

    x2 = x.reshape(batch * t, d)
    main, small = _in_proj(x2, g1, sc1, sh1, w_main, w_small, tm=1024, rows_per_batch=t)
    qk = _conv_silu(main, conv_w, conv_b, batch=batch, rows=t // GRID_W, cols=GRID_W)
    main3, small3 = main.reshape(batch, t, MAIN_W), small.reshape(batch, t, LANES)
    hm_f, hm_b = [a.reshape(batch * t, V_W) for a in
                  _mlstm_scan(qk.reshape(batch, t, 2 * QK_W), main3, small3, gate_bias, ml_states, with_output=True)]
    og_f, og_b = [a.reshape(batch * t, V_W) for a in _gla_scan(main3, small3, w2p, b2, gla_state, with_output=True)]

    x1, h2, wt = _merge(hm_f, hm_b, og_f, og_b, main, x2, ml_norm, gla_norm,
                        w_proj_m[0].astype(MXU_DTYPE), w_proj_g[0].astype(MXU_DTYPE), w_out[0].astype(MXU_DTYPE),
                        gt1, g_norm2, sc2, sh2, w_route, b_route, tm=256, rows_per_batch=t)
    out = _moe_final(h2, wt, w_up[0].astype(MXU_DTYPE), w_down[0].astype(MXU_DTYPE), x1, gt2, g_final[None, :],
                     tm=512, rows_per_batch=t)
    return out.reshape(batch, t, d)
```

```python
import functools

import jax
import jax.numpy as jnp
from jax import lax
from jax.experimental import pallas as pl
from jax.experimental.pallas import tpu as pltpu

D_MODEL = 1024
GRID_W = 64
CHUNK = 256
EPS = 1e-6
NEG_BIG = -1e30
HEADS = 4
QK = D_MODEL // 8
DV = D_MODEL // 4
QK_W = HEADS * QK
V_W = HEADS * DV
GLA_RANK = 16
GLA_TAU = 16.0
N_GROUPS = 4
EXPERTS_PER_GROUP = 4
N_EXPERTS = N_GROUPS * EXPERTS_PER_GROUP
D_EXPERT = D_MODEL // 2
IN_SIZES = (QK_W, QK_W, V_W, V_W, 4 * HEADS, QK_W, QK_W, V_W, V_W, 2 * GLA_RANK, D_MODEL, D_MODEL)

LANES = 128
MXU_DTYPE = jnp.bfloat16
F32 = jnp.float32
VMEM_LIMIT = 48 * 1024 * 1024

COL_QK_M, COL_V_M, COL_O_M, COL_QK_G, COL_V_G, COL_R_G, COL_MG_M, COL_MG_G = range(8)
MAIN_W = 8 * D_MODEL
SMALL_GATE0 = 2 * GLA_RANK
GLA_SAFE_DECAY = 80.0
ROUTE_E0 = N_GROUPS


def _dot(a, b):
    return jnp.dot(a.astype(MXU_DTYPE), b.astype(MXU_DTYPE), preferred_element_type=F32)


def _dot_nt(a, b):
    return lax.dot_general(a.astype(MXU_DTYPE), b.astype(MXU_DTYPE), (((1,), (1,)), ((), ())),
                           preferred_element_type=F32)


def _dot_tn(a, b):
    return lax.dot_general(a.astype(MXU_DTYPE), b.astype(MXU_DTYPE), (((0,), (0,)), ((), ())),
                           preferred_element_type=F32)


def _split3(x):
    hi = x.astype(MXU_DTYPE)
    r1 = x - hi.astype(F32)
    mid = r1.astype(MXU_DTYPE)
    lo = (r1 - mid.astype(F32)).astype(MXU_DTYPE)
    return hi, mid, lo


def _dot_exact_lhs(a01, x):
    hi, mid, lo = _split3(x)
    return _dot(a01, hi) + _dot(a01, mid) + _dot(a01, lo)


def _log_sigmoid(x):
    return jnp.minimum(x, 0.0) - jnp.log1p(jnp.exp(-jnp.abs(x)))


def _silu(x):
    return x * jax.nn.sigmoid(x)


def _rms(x):
    return x * lax.rsqrt(jnp.mean(x * x, axis=-1, keepdims=True) + EPS)


def _cparams(sem):
    return pltpu.CompilerParams(dimension_semantics=sem, vmem_limit_bytes=VMEM_LIMIT)


def _mod_kernel(c_ref, w_ref, b_ref, o_ref):
    o_ref[...] = _dot(_silu(c_ref[...]), w_ref[...]) + b_ref[...]


def _modulation(cc, w_mod, b_mod):
    n = w_mod.shape[1]
    tn = 512
    return pl.pallas_call(
        _mod_kernel,
        grid=(n // tn,),
        in_specs=[pl.BlockSpec((8, D_MODEL), lambda j: (0, 0)),
                  pl.BlockSpec((D_MODEL, tn), lambda j: (0, j)),
                  pl.BlockSpec((1, tn), lambda j: (0, j))],
        out_specs=pl.BlockSpec((8, tn), lambda j: (0, j)),
        out_shape=jax.ShapeDtypeStruct((8, n), F32),
        compiler_params=_cparams(("arbitrary",)),
        name="modulation",
    )(cc, w_mod, b_mod)


def _inproj_kernel(x_ref, g_ref, sc_ref, sh_ref, w_ref, ws_ref, o_ref, os_ref, xn_ref):
    @pl.when(pl.program_id(1) == 0)
    def _():
        xn = _rms(x_ref[...]) * g_ref[...] * (1.0 + sc_ref[...]) + sh_ref[...]
        xn_ref[...] = xn.astype(MXU_DTYPE)
        os_ref[...] = _dot(xn_ref[...], ws_ref[...])

    o_ref[...] = _dot(xn_ref[...], w_ref[...])


def _in_proj(x2, g, sc, sh, w_main, w_small, *, tm, rows_per_batch):
    m = x2.shape[0]
    tn = 1024
    tiles_per_batch = rows_per_batch // tm
    vec = pl.BlockSpec((None, 1, D_MODEL), lambda i, j: (i // tiles_per_batch, 0, 0))
    return pl.pallas_call(
        _inproj_kernel,
        grid=(m // tm, MAIN_W // tn),
        in_specs=[pl.BlockSpec((tm, D_MODEL), lambda i, j: (i, 0)),
                  pl.BlockSpec((1, D_MODEL), lambda i, j: (0, 0)),
                  vec, vec,
                  pl.BlockSpec((D_MODEL, tn), lambda i, j: (0, j)),
                  pl.BlockSpec((D_MODEL, LANES), lambda i, j: (0, 0))],
        out_specs=[pl.BlockSpec((tm, tn), lambda i, j: (i, j)),
                   pl.BlockSpec((tm, LANES), lambda i, j: (i, 0))],
        out_shape=[jax.ShapeDtypeStruct((m, MAIN_W), F32), jax.ShapeDtypeStruct((m, LANES), F32)],
        scratch_shapes=[pltpu.VMEM((tm, D_MODEL), MXU_DTYPE)],
        compiler_params=_cparams(("parallel", "arbitrary")),
        name="in_proj",
    )(x2, g, sc, sh, w_main, w_small)


def _conv_kernel(x_ref, w_ref, b_ref, o_ref, *, rows, cols):
    scale = jnp.where(pl.program_id(1) * LANES >= QK_W, QK ** -0.5, 1.0).astype(F32)
    w = w_ref[...]
    bias = b_ref[...]
    tpos = lax.broadcasted_iota(jnp.int32, (cols, 1), 0)

    def body(r, carry):
        acc = jnp.zeros((cols, LANES), F32)
        for dr in ((-1, 0, 1) if rows > 1 else (0,)):
            rr = r + dr
            row_ok = jnp.logical_and(rr >= 0, rr < rows)
            start = pl.multiple_of(jnp.clip(rr, 0, rows - 1) * cols, cols)
            tile = x_ref[pl.ds(start, cols), :]
            for dc in (-1, 0, 1):
                if dc == 0:
                    shifted = tile
                else:
                    shifted = pltpu.roll(tile, (-dc) % cols, axis=0)
                    shifted = jnp.where(jnp.logical_and(tpos + dc >= 0, tpos + dc < cols), shifted, 0.0)
                k = (dr + 1) * 3 + (dc + 1)
                acc = acc + shifted * jnp.where(row_ok, w[k:k + 1, :], 0.0)
        o_ref[pl.ds(pl.multiple_of(r * cols, cols), cols), :] = _silu(acc + bias) * scale
        return carry

    lax.fori_loop(0, rows, body, 0)


def _conv_silu(main, conv_w, conv_b, *, batch, rows, cols):
    t = rows * cols
    nct = 2 * QK_W // LANES
    return pl.pallas_call(
        functools.partial(_conv_kernel, rows=rows, cols=cols),
        grid=(batch, nct),
        in_specs=[pl.BlockSpec((t, LANES), lambda b, c: (b, c)),
                  pl.BlockSpec((9, LANES), lambda b, c: (0, c)),
                  pl.BlockSpec((1, LANES), lambda b, c: (0, c))],
        out_specs=pl.BlockSpec((t, LANES), lambda b, c: (b, c)),
        out_shape=jax.ShapeDtypeStruct((batch * t, 2 * QK_W), F32),
        compiler_params=_cparams(("parallel", "arbitrary")),
        name="conv_silu",
    )(main, conv_w, conv_b)


def _chunk_masks(direction):
    row = lax.broadcasted_iota(jnp.int32, (CHUNK, CHUNK), 0)
    col = lax.broadcasted_iota(jnp.int32, (CHUNK, CHUNK), 1)
    seen = (row >= col) if direction == 0 else (row <= col)
    return seen, seen.astype(MXU_DTYPE)


def _scan_specs(batch, nc, col_blocks, widths):
    specs = []
    for direction in (0, 1):
        for cb, wd in zip(col_blocks, widths):
            if direction == 0:
                specs.append(pl.BlockSpec((batch, CHUNK, wd), lambda c, cb=cb: (0, c, cb)))
            else:
                specs.append(pl.BlockSpec((batch, CHUNK, wd), lambda c, cb=cb: (0, nc - 1 - c, cb)))
    return specs


def _scan_out_specs(batch, nc):
    return [pl.BlockSpec((batch, CHUNK, V_W), lambda c: (0, c, 0)),
            pl.BlockSpec((batch, CHUNK, V_W), lambda c: (0, nc - 1 - c, 0))]


def _whole(shape):
    nd = len(shape)
    return pl.BlockSpec(tuple(shape), lambda c: (0,) * nd)


def _mlstm_kernel(*refs, with_output, batch):
    (qk_f, v_f, sm_f, qk_b, v_b, sm_b, bias_ref, c0_ref, n0_ref, m0_ref) = refs[:10]
    if with_output:
        hf_ref, hb_ref, c_s, n_s, m_s = refs[10:]
    else:
        c_out, n_out, m_out, c_s, n_s, m_s = refs[10:]
    step = pl.program_id(0)

    @pl.when(step == 0)
    def _():
        c_s[...] = c0_ref[...]
        n_s[...] = n0_ref[...]
        m_s[...] = m0_ref[...]

    lane = lax.broadcasted_iota(jnp.int32, (1, LANES), 1)
    gate_lane = jnp.logical_and(lane >= SMALL_GATE0, lane < SMALL_GATE0 + 4 * HEADS)
    forget_lane = jnp.logical_and(gate_lane, ((lane - SMALL_GATE0) % (2 * HEADS)) >= HEADS)
    eye = (lax.broadcasted_iota(jnp.int32, (LANES, LANES), 0)
           == lax.broadcasted_iota(jnp.int32, (LANES, LANES), 1)).astype(MXU_DTYPE)

    for direction, (qk_ref, v_ref, sm_ref) in enumerate(((qk_f, v_f, sm_f), (qk_b, v_b, sm_b))):
        seen, seen01 = _chunk_masks(direction)
        last = CHUNK - 1 if direction == 0 else 0
        for bi in range(batch):
            g = sm_ref[bi] + bias_ref[...]
            gp = jnp.where(forget_lane, _log_sigmoid(g), g)
            bc = _dot_exact_lhs(seen01, gp)
            hi, mid, lo = _split3(gp)
            gp_t = _dot_nt(eye, hi) + _dot_nt(eye, mid) + _dot_nt(eye, lo)
            hi, mid, lo = _split3(bc)
            bc_t = _dot_nt(eye, hi) + _dot_nt(eye, mid) + _dot_nt(eye, lo)
            bend_row = bc[last:last + 1, :]
            outs = []
            for h in range(HEADS):
                ji = SMALL_GATE0 + direction * 2 * HEADS + h
                jf = ji + HEADS
                q = qk_ref[bi, :, h * QK:(h + 1) * QK]
                k = qk_ref[bi, :, QK_W + h * QK:QK_W + (h + 1) * QK]
                v = v_ref[bi, :, h * DV:(h + 1) * DV]
                c_old = c_s[bi, direction, h]
                n_old = n_s[bi, direction, h]
                m_old = m_s[bi, direction, h][:, :1]
                b_end = bend_row[:, jf:jf + 1]
                i_col = gp[:, ji:ji + 1]
                b_col = bc[:, jf:jf + 1]
                log_w = b_end - b_col + i_col
                m_new = jnp.maximum(b_end + m_old, jnp.max(log_w, axis=0, keepdims=True))
                w_col = jnp.exp(log_w - m_new)
                decay = jnp.exp(b_end + m_old - m_new)
                kw = k * w_col
                c_s[bi, direction, h] = decay * c_old + _dot_tn(kw, v)
                n_s[bi, direction, h] = decay * n_old + jnp.sum(kw, axis=0, keepdims=True)
                m_s[bi, direction, h] = jnp.broadcast_to(m_new, (1, LANES))
                if with_output:
                    i_row = gp_t[ji:ji + 1, :]
                    b_row = bc_t[jf:jf + 1, :]
                    log_d = jnp.where(seen, b_col - b_row + i_row, -jnp.inf)
                    log_inter = b_col + m_old
                    m_t = jnp.maximum(log_inter, jnp.max(log_d, axis=-1, keepdims=True))
                    s = _dot_nt(q, k) * jnp.exp(log_d - m_t)
                    w_inter = jnp.exp(log_inter - m_t)
                    num = _dot(s, v) + w_inter * _dot(q, c_old)
                    den = jnp.abs(jnp.sum(s, axis=-1, keepdims=True)
                                  + w_inter * jnp.sum(q * n_old, axis=-1, keepdims=True))
                    outs.append(num / jnp.maximum(den, jnp.exp(-m_t)))
            if with_output:
                (hf_ref if direction == 0 else hb_ref)[bi] = jnp.concatenate(outs, axis=-1)

    if not with_output:
        @pl.when(step == pl.num_programs(0) - 1)
        def _():
            c_out[...] = c_s[...]
            n_out[...] = n_s[...]
            m_out[...] = m_s[...]


def _mlstm_scan(qk, main, small, gate_bias, states, *, with_output):
    c0, n0, m0 = states
    batch, t, _ = qk.shape
    nc = t // CHUNK
    in_specs = _scan_specs(batch, nc, (0, COL_V_M, 0), (2 * QK_W, V_W, LANES))
    in_specs += [_whole(gate_bias.shape), _whole(c0.shape), _whole(n0.shape), _whole(m0.shape)]
    if with_output:
        out_specs = _scan_out_specs(batch, nc)
        out_shape = [jax.ShapeDtypeStruct((batch, t, V_W), F32)] * 2
    else:
        out_specs = [_whole(c0.shape), _whole(n0.shape), _whole(m0.shape)]
        out_shape = [jax.ShapeDtypeStruct(s.shape, F32) for s in states]
    return pl.pallas_call(
        functools.partial(_mlstm_kernel, with_output=with_output, batch=batch),
        grid=(nc,),
        in_specs=in_specs,
        out_specs=out_specs,
        out_shape=out_shape,
        scratch_shapes=[pltpu.VMEM(c0.shape, F32), pltpu.VMEM(n0.shape, F32), pltpu.VMEM(m0.shape, F32)],
        compiler_params=_cparams(("arbitrary",)),
        name="mlstm_scan_out" if with_output else "mlstm_scan_state",
    )(qk, main, small, qk, main, small, gate_bias, c0, n0, m0)


def _gla_exact_intra(q, k, v, b, direction):
    row_id = lax.broadcasted_iota(jnp.int32, (CHUNK, 1), 0)

    def row(t, acc):
        pick = row_id == t
        b_t = jnp.sum(jnp.where(pick, b, 0.0), axis=0, keepdims=True)
        q_t = jnp.sum(jnp.where(pick, q, 0.0), axis=0, keepdims=True)
        ok = (row_id <= t) if direction == 0 else (row_id >= t)
        e = jnp.exp(jnp.where(ok, b_t - b, -jnp.inf))
        sc = jnp.sum(q_t * k * e, axis=-1, keepdims=True)
        o_t = jnp.sum(sc * v, axis=0, keepdims=True)
        return jnp.where(pick, o_t, acc)

    return lax.fori_loop(0, CHUNK, row, jnp.zeros((CHUNK, DV), F32))


def _gla_kernel(*refs, with_output, batch):
    (qk_f, v_f, sm_f, qk_b, v_b, sm_b, w2_ref, b2_ref, s0_ref) = refs[:9]
    if with_output:
        of_ref, ob_ref, s_s, b_s, inter_s = refs[9:]
    else:
        s_out, s_s = refs[9:]
    step = pl.program_id(0)

    @pl.when(step == 0)
    def _():
        s_s[...] = s0_ref[...]

    worst_decay = []
    for direction, (qk_ref, v_ref, sm_ref) in enumerate(((qk_f, v_f, sm_f), (qk_b, v_b, sm_b))):
        seen, seen01 = _chunk_masks(direction)
        last = CHUNK - 1 if direction == 0 else 0
        for bi in range(batch):
            z = _dot(sm_ref[bi], w2_ref[direction]) + b2_ref[direction]
            log_a = _log_sigmoid(z) * (1.0 / GLA_TAU)
            b_all = _dot_exact_lhs(seen01, log_a)
            outs, inters = [], []
            for h in range(HEADS):
                q = qk_ref[bi, :, h * QK:(h + 1) * QK] * (QK ** -0.5)
                k = qk_ref[bi, :, QK_W + h * QK:QK_W + (h + 1) * QK]
                v = v_ref[bi, :, h * DV:(h + 1) * DV]
                b = b_all[:, h * QK:(h + 1) * QK]
                b_end = b[last:last + 1, :]
                st_old = s_s[bi, direction, h]
                k_dec = k * jnp.exp(b_end - b)
                s_s[bi, direction, h] = st_old * jnp.exp(b_end) + _dot_tn(v, k_dec)
                if with_output:
                    q_dec = q * jnp.exp(b)
                    inter = _dot_nt(q_dec, st_old)
                    scores = jnp.where(seen, _dot_nt(q_dec, k * jnp.exp(-b)), 0.0)
                    outs.append(_dot(scores, v) + inter)
                    inters.append(inter)
            if with_output:
                (of_ref if direction == 0 else ob_ref)[bi] = jnp.concatenate(outs, axis=-1)
                b_s[bi, direction] = b_all
                inter_s[bi, direction] = jnp.concatenate(inters, axis=-1)
                worst_decay.append(jnp.max(-b_all[last:last + 1, :]))

    if with_output:
        @pl.when(functools.reduce(jnp.maximum, worst_decay) > GLA_SAFE_DECAY)
        def _():
            for direction, (qk_ref, v_ref, o_ref) in enumerate(((qk_f, v_f, of_ref), (qk_b, v_b, ob_ref))):
                last = CHUNK - 1 if direction == 0 else 0
                for bi in range(batch):
                    for h in range(HEADS):
                        b = b_s[bi, direction, :, h * QK:(h + 1) * QK]

                        @pl.when(jnp.max(-b[last:last + 1, :]) > GLA_SAFE_DECAY)
                        def _():
                            q = qk_ref[bi, :, h * QK:(h + 1) * QK] * (QK ** -0.5)
                            k = qk_ref[bi, :, QK_W + h * QK:QK_W + (h + 1) * QK]
                            v = v_ref[bi, :, h * DV:(h + 1) * DV]
                            o_ref[bi, :, h * DV:(h + 1) * DV] = (
                                inter_s[bi, direction, :, h * DV:(h + 1) * DV]
                                + _gla_exact_intra(q, k, v, b, direction))
    else:
        @pl.when(step == pl.num_programs(0) - 1)
        def _():
            s_out[...] = s_s[...]


def _gla_scan(main, small, w2p, b2, s0, *, with_output):
    batch, t, _ = main.shape
    nc = t // CHUNK
    in_specs = _scan_specs(batch, nc, (COL_QK_G, COL_V_G, 0), (2 * QK_W, V_W, LANES))
    in_specs += [_whole(w2p.shape), _whole(b2.shape), _whole(s0.shape)]
    scratch = [pltpu.VMEM(s0.shape, F32)]
    if with_output:
        out_specs = _scan_out_specs(batch, nc)
        out_shape = [jax.ShapeDtypeStruct((batch, t, V_W), F32)] * 2
        scratch += [pltpu.VMEM((batch, 2, CHUNK, QK_W), F32), pltpu.VMEM((batch, 2, CHUNK, V_W), F32)]
    else:
        out_specs = _whole(s0.shape)
        out_shape = jax.ShapeDtypeStruct(s0.shape, F32)
    return pl.pallas_call(
        functools.partial(_gla_kernel, with_output=with_output, batch=batch),
        grid=(nc,),
        in_specs=in_specs,
        out_specs=out_specs,
        out_shape=out_shape,
        scratch_shapes=scratch,
        compiler_params=_cparams(("arbitrary",)),
        name="gla_scan_out" if with_output else "gla_scan_state",
    )(main, main, small, main, main, small, w2p, b2, s0)


def _head_rms(a):
    return jnp.concatenate([_rms(a[:, h * DV:(h + 1) * DV]) for h in range(HEADS)], axis=-1)


def _merge_kernel(hmf, hmb, ogf, ogb, om, rg, mgm, mgg, x_ref, mln, gln, wpm, wpg, wo, gt1, g2, sc2, sh2,
                  wr, br, x1_ref, h2_ref, wt_ref):
    y_m = _head_rms(hmf[...] + hmb[...]) * mln[...] * jax.nn.sigmoid(om[...])
    y_g = _head_rms(ogf[...] + ogb[...]) * gln[...] * _silu(rg[...])
    y = jax.nn.sigmoid(mgm[...]) * _dot(y_m, wpm[...]) + jax.nn.sigmoid(mgg[...]) * _dot(y_g, wpg[...])
    x1 = x_ref[...] + gt1[...] * _dot(y, wo[...])
    x1_ref[...] = x1
    h2 = _rms(x1) * g2[...] * (1.0 + sc2[...]) + sh2[...]
    h2_ref[...] = h2.astype(MXU_DTYPE)

    hh, hm_, _ = _split3(h2)
    wh, wm_, _ = _split3(wr[...])
    lg = _dot(hh, wh) + _dot(hh, wm_) + _dot(hm_, wh) + br[...]
    lane = lax.broadcasted_iota(jnp.int32, lg.shape, 1)

    def masked_softmax(mask):
        l = jnp.where(mask, lg, -jnp.inf)
        e = jnp.exp(l - jnp.max(l, axis=-1, keepdims=True))
        return e / jnp.sum(e, axis=-1, keepdims=True)

    def top1(p, mask):
        pm = jnp.where(mask, p, -1.0)
        best = jnp.max(pm, axis=-1, keepdims=True)
        idx = jnp.min(jnp.where(jnp.logical_and(mask, pm == best), lane, LANES), axis=-1, keepdims=True)
        return best, idx

    gmask = lane < N_GROUPS
    grp_p, grp = top1(masked_softmax(gmask), gmask)
    e_lo = ROUTE_E0 + grp * EXPERTS_PER_GROUP
    emask = jnp.logical_and(lane >= e_lo, lane < e_lo + EXPERTS_PER_GROUP)
    p_in = masked_softmax(emask)
    p1, i1 = top1(p_in, emask)
    p2, i2 = top1(p_in, jnp.logical_and(emask, lane != i1))
    tot = p1 + p2
    wt_ref[...] = (jnp.where(lane == i1, grp_p * p1 / tot, 0.0)
                   + jnp.where(lane == i2, grp_p * p2 / tot, 0.0))


def _merge(hmf, hmb, ogf, ogb, main, x2, mln, gln, wpm, wpg, wo, gt1, g2, sc2, sh2, wr, br, *, tm, rows_per_batch):
    m = x2.shape[0]
    tpb = rows_per_batch // tm
    rowblk = pl.BlockSpec((tm, D_MODEL), lambda i: (i, 0))
    colblk = lambda cb: pl.BlockSpec((tm, D_MODEL), lambda i, cb=cb: (i, cb))
    vec = pl.BlockSpec((1, D_MODEL), lambda i: (0, 0))
    bvec = pl.BlockSpec((None, 1, D_MODEL), lambda i: (i // tpb, 0, 0))
    wmat = pl.BlockSpec((D_MODEL, D_MODEL), lambda i: (0, 0))
    return pl.pallas_call(
        _merge_kernel,
        grid=(m // tm,),
        in_specs=[rowblk, rowblk, rowblk, rowblk, colblk(COL_O_M), colblk(COL_R_G), colblk(COL_MG_M),
                  colblk(COL_MG_G), rowblk, vec, vec, wmat, wmat, wmat, bvec, vec, bvec, bvec,
                  pl.BlockSpec((D_MODEL, LANES), lambda i: (0, 0)), pl.BlockSpec((1, LANES), lambda i: (0, 0))],
        out_specs=[rowblk, rowblk, pl.BlockSpec((tm, LANES), lambda i: (i, 0))],
        out_shape=[jax.ShapeDtypeStruct((m, D_MODEL), F32), jax.ShapeDtypeStruct((m, D_MODEL), MXU_DTYPE),
                   jax.ShapeDtypeStruct((m, LANES), F32)],
        compiler_params=_cparams(("parallel",)),
        name="merge_route",
    )(hmf, hmb, ogf, ogb, main, main, main, main, x2, mln, gln, wpm, wpg, wo, gt1, g2, sc2, sh2, wr, br)


def _moe_kernel(h2_ref, wt_ref, wup_ref, wdn_ref, x1_ref, gt2_ref, gf_ref, o_ref, acc_ref):
    e = pl.program_id(1)

    @pl.when(e == 0)
    def _():
        acc_ref[...] = jnp.zeros_like(acc_ref)

    gu = _dot(h2_ref[...], wup_ref[...])
    hidden = _silu(gu[:, :D_EXPERT]) * gu[:, D_EXPERT:]
    y = _dot(hidden, wdn_ref[...])
    lane = lax.broadcasted_iota(jnp.int32, wt_ref.shape, 1)
    w_col = jnp.sum(jnp.where(lane == ROUTE_E0 + e, wt_ref[...], 0.0), axis=-1, keepdims=True)
    acc_ref[...] += y * w_col

    @pl.when(e == N_EXPERTS - 1)
    def _():
        o_ref[...] = _rms(x1_ref[...] + gt2_ref[...] * acc_ref[...]) * gf_ref[...]


def _moe_final(h2, wt, w_up, w_down, x1, gt2, g_final, *, tm, rows_per_batch):
    m = h2.shape[0]
    tpb = rows_per_batch // tm
    rowblk = pl.BlockSpec((tm, D_MODEL), lambda i, e: (i, 0))
    return pl.pallas_call(
        _moe_kernel,
        grid=(m // tm, N_EXPERTS),
        in_specs=[rowblk,
                  pl.BlockSpec((tm, LANES), lambda i, e: (i, 0)),
                  pl.BlockSpec((None, D_MODEL, 2 * D_EXPERT), lambda i, e: (e, 0, 0)),
                  pl.BlockSpec((None, D_EXPERT, D_MODEL), lambda i, e: (e, 0, 0)),
                  rowblk,
                  pl.BlockSpec((None, 1, D_MODEL), lambda i, e: (i // tpb, 0, 0)),
                  pl.BlockSpec((1, D_MODEL), lambda i, e: (0, 0))],
        out_specs=rowblk,
        out_shape=jax.ShapeDtypeStruct((m, D_MODEL), F32),
        scratch_shapes=[pltpu.VMEM((tm, D_MODEL), F32)],
        compiler_params=_cparams(("parallel", "arbitrary")),
        name="moe_final",
    )(h2, wt, w_up, w_down, x1, gt2, g_final)


def _empty_states(batch):
    ml = (jnp.zeros((batch, 2, HEADS, QK, DV), F32),
          jnp.zeros((batch, 2, HEADS, 1, QK), F32),
          jnp.full((batch, 2, HEADS, 1, LANES), NEG_BIG, F32))
    gla = jnp.zeros((batch, 2, HEADS, DV, QK), F32)
    return ml, gla


def kernel(x, c, ctx, c_ctx, w_mod, b_mod, g_norm1, w_in, ml_conv, ml_conv_b, b_mgate, ml_norm, gla_w2, gla_b2,
           gla_norm, w_proj_m, w_proj_g, w_out, g_norm2, w_grp, b_grp, w_rexp, b_rexp, w_up, w_down, g_final):
    batch, t, d = x.shape
    t_ctx = ctx.shape[1]
    assert d == D_MODEL and w_mod.shape[0] == 1 and w_in.shape[2] == sum(IN_SIZES)
    assert t % (GRID_W * 16) == 0 and t % CHUNK == 0 and t_ctx % CHUNK == 0

    off = [0]
    for s in IN_SIZES:
        off.append(off[-1] + s)
    wi = w_in[0]
    w_main = jnp.concatenate([wi[:, off[0]:off[4]], wi[:, off[5]:off[9]], wi[:, off[10]:off[12]]], axis=1)
    w_small = jnp.concatenate([wi[:, off[9]:off[10]], wi[:, off[4]:off[5]],
                               jnp.zeros((d, LANES - 2 * GLA_RANK - 4 * HEADS), F32)], axis=1)
    w_main = w_main.astype(MXU_DTYPE)
    w_small = w_small.astype(MXU_DTYPE)
    gate_bias = jnp.zeros((1, LANES), F32).at[0, SMALL_GATE0:SMALL_GATE0 + 4 * HEADS].set(b_mgate[0])
    w2p = jnp.zeros((2, LANES, QK_W), F32)
    w2p = w2p.at[0, 0:GLA_RANK].set(gla_w2[0, 0]).at[1, GLA_RANK:2 * GLA_RANK].set(gla_w2[0, 1])
    b2 = gla_b2[0][:, None, :]
    conv_w = ml_conv[0].reshape(9, 2 * QK_W)
    conv_b = ml_conv_b[0][None, :]
    w_route = jnp.concatenate([w_grp[0], w_rexp[0], jnp.zeros((d, LANES - N_GROUPS - N_EXPERTS), F32)], axis=1)
    b_route = jnp.concatenate([b_grp[0], b_rexp[0], jnp.zeros((LANES - N_GROUPS - N_EXPERTS,), F32)])[None, :]

    cc = jnp.concatenate([c, c_ctx[None, :], jnp.zeros((8 - batch - 1, d), F32)], axis=0)
    mod = _modulation(cc, w_mod[0], b_mod[0][None, :])
    sh1, sc1, gt1, sh2, sc2, gt2 = [mod[:batch, i * d:(i + 1) * d][:, None, :] for i in range(6)]
    sh1c, sc1c = [jnp.broadcast_to(mod[batch:batch + 1, i * d:(i + 1) * d][:, None, :], (batch, 1, d)) for i in range(2)]
    g1 = g_norm1[0][None, :]

    main_c, small_c = _in_proj(ctx.reshape(batch * t_ctx, d), g1, sc1c, sh1c, w_main, w_small,
                               tm=t_ctx, rows_per_batch=t_ctx)
    qk_c = _conv_silu(main_c, conv_w, conv_b, batch=batch, rows=1, cols=t_ctx)
    ml0, gla0 = _empty_states(batch)
    main_c3, small_c3 = main_c.reshape(batch, t_ctx, MAIN_W), small_c.reshape(batch, t_ctx, LANES)
    ml_states = _mlstm_scan(qk_c.reshape(batch, t_ctx, 2 * QK_W), main_c3, small_c3, gate_bias, ml0, with_output=False)
    gla_state = _gla_scan(main_c3, small_c3, w2p, b2, gla0, with_output=False)

    x2 = x.reshape(batch * t, d)
    main, small = _in_proj(x2, g1, sc1, sh1, w_main, w_small, tm=1024, rows_per_batch=t)
    qk = _conv_silu(main, conv_w, conv_b, batch=batch, rows=t // GRID_W, cols=GRID_W)
    main3, small3 = main.reshape(batch, t, MAIN_W), small.reshape(batch, t, LANES)
    hm_f, hm_b = [a.reshape(batch * t, V_W) for a in
                  _mlstm_scan(qk.reshape(batch, t, 2 * QK_W), main3, small3, gate_bias, ml_states, with_output=True)]
    og_f, og_b = [a.reshape(batch * t, V_W) for a in _gla_scan(main3, small3, w2p, b2, gla_state, with_output=True)]

    x1, h2, wt = _merge(hm_f, hm_b, og_f, og_b, main, x2, ml_norm, gla_norm,
                        w_proj_m[0].astype(MXU_DTYPE), w_proj_g[0].astype(MXU_DTYPE), w_out[0].astype(MXU_DTYPE),
                        gt1, g_norm2, sc2, sh2, w_route, b_route, tm=256, rows_per_batch=t)
    out = _moe_final(h2, wt, w_up[0].astype(MXU_DTYPE), w_down[0].astype(MXU_DTYPE), x1, gt2, g_final[None, :],
                     tm=512, rows_per_batch=t)
    return out.reshape(batch, t, d)
```

```python
import functools

import jax
import jax.numpy as jnp
from jax import lax
from jax.experimental import pallas as pl
from jax.experimental.pallas import tpu as pltpu

D_MODEL = 1024
GRID_W = 64
CHUNK = 256
EPS = 1e-6
NEG_BIG = -1e30
HEADS = 4
QK = D_MODEL // 8
DV = D_MODEL // 4
QK_W = HEADS * QK
V_W = HEADS * DV
GLA_RANK = 16
GLA_TAU = 16.0
N_GROUPS = 4
EXPERTS_PER_GROUP = 4
N_EXPERTS = N_GROUPS * EXPERTS_PER_GROUP
D_EXPERT = D_MODEL // 2
IN_SIZES = (QK_W, QK_W, V_W, V_W, 4 * HEADS, QK_W, QK_W, V_W, V_W, 2 * GLA_RANK, D_MODEL, D_MODEL)

LANES = 128
MXU_DTYPE = jnp.bfloat16
F32 = jnp.float32
VMEM_LIMIT = 48 * 1024 * 1024

COL_QK_M, COL_V_M, COL_O_M, COL_QK_G, COL_V_G, COL_R_G, COL_MG_M, COL_MG_G = range(8)
MAIN_W = 8 * D_MODEL
SMALL_GATE0 = 2 * GLA_RANK
GLA_SAFE_DECAY = 80.0
ROUTE_E0 = N_GROUPS


def _dot(a, b):
    return jnp.dot(a.astype(MXU_DTYPE), b.astype(MXU_DTYPE), preferred_element_type=F32)


def _dot_nt(a, b):
    return lax.dot_general(a.astype(MXU_DTYPE), b.astype(MXU_DTYPE), (((1,), (1,)), ((), ())),
                           preferred_element_type=F32)


def _dot_tn(a, b):
    return lax.dot_general(a.astype(MXU_DTYPE), b.astype(MXU_DTYPE), (((0,), (0,)), ((), ())),
                           preferred_element_type=F32)


def _split3(x):
    hi = x.astype(MXU_DTYPE)
    r1 = x - hi.astype(F32)
    mid = r1.astype(MXU_DTYPE)
    lo = (r1 - mid.astype(F32)).astype(MXU_DTYPE)
    return hi, mid, lo


def _dot_exact_lhs(a01, x):
    hi, mid, lo = _split3(x)
    return _dot(a01, hi) + _dot(a01, mid) + _dot(a01, lo)


def _log_sigmoid(x):
    return jnp.minimum(x, 0.0) - jnp.log1p(jnp.exp(-jnp.abs(x)))


def _silu(x):
    return x * jax.nn.sigmoid(x)


def _rms(x):
    return x * lax.rsqrt(jnp.mean(x * x, axis=-1, keepdims=True) + EPS)


def _cparams(sem, vmem_limit=VMEM_LIMIT):
    return pltpu.CompilerParams(dimension_semantics=sem, vmem_limit_bytes=vmem_limit)


def _mod_kernel(c_ref, w_ref, b_ref, o_ref):
    o_ref[...] = _dot(_silu(c_ref[...]), w_ref[...]) + b_ref[...]


def _modulation(cc, w_mod, b_mod):
    n = w_mod.shape[1]
    tn = 512
    return pl.pallas_call(
        _mod_kernel,
        grid=(n // tn,),
        in_specs=[pl.BlockSpec((8, D_MODEL), lambda j: (0, 0)),
                  pl.BlockSpec((D_MODEL, tn), lambda j: (0, j)),
                  pl.BlockSpec((1, tn), lambda j: (0, j))],
        out_specs=pl.BlockSpec((8, tn), lambda j: (0, j)),
        out_shape=jax.ShapeDtypeStruct((8, n), F32),
        compiler_params=_cparams(("arbitrary",)),
        name="modulation",
    )(cc, w_mod, b_mod)


def _inproj_kernel(x_ref, g_ref, sc_ref, sh_ref, w_ref, ws_ref, o_ref, os_ref, xn_ref):
    @pl.when(pl.program_id(1) == 0)
    def _():
        xn = _rms(x_ref[...]) * g_ref[...] * (1.0 + sc_ref[...]) + sh_ref[...]
        xn_ref[...] = xn.astype(MXU_DTYPE)
        os_ref[...] = _dot(xn_ref[...], ws_ref[...])

    o_ref[...] = _dot(xn_ref[...], w_ref[...])


def _in_proj(x2, g, sc, sh, w_main, w_small, *, tm, rows_per_batch):
    m = x2.shape[0]
    tn = 1024
    tiles_per_batch = rows_per_batch // tm
    vec = pl.BlockSpec((None, 1, D_MODEL), lambda i, j: (i // tiles_per_batch, 0, 0))
    return pl.pallas_call(
        _inproj_kernel,
        grid=(m // tm, MAIN_W // tn),
        in_specs=[pl.BlockSpec((tm, D_MODEL), lambda i, j: (i, 0)),
                  pl.BlockSpec((1, D_MODEL), lambda i, j: (0, 0)),
                  vec, vec,
                  pl.BlockSpec((D_MODEL, tn), lambda i, j: (0, j)),
                  pl.BlockSpec((D_MODEL, LANES), lambda i, j: (0, 0))],
        out_specs=[pl.BlockSpec((tm, tn), lambda i, j: (i, j)),
                   pl.BlockSpec((tm, LANES), lambda i, j: (i, 0))],
        out_shape=[jax.ShapeDtypeStruct((m, MAIN_W), F32), jax.ShapeDtypeStruct((m, LANES), F32)],
        scratch_shapes=[pltpu.VMEM((tm, D_MODEL), MXU_DTYPE)],
        compiler_params=_cparams(("parallel", "arbitrary")),
        name="in_proj",
    )(x2, g, sc, sh, w_main, w_small)


def _conv_kernel(x_ref, w_ref, b_ref, o_ref, *, rows, cols):
    scale = jnp.where(pl.program_id(1) * LANES >= QK_W, QK ** -0.5, 1.0).astype(F32)
    w = w_ref[...]
    bias = b_ref[...]
    tpos = lax.broadcasted_iota(jnp.int32, (cols, 1), 0)

    def body(r, carry):
        acc = jnp.zeros((cols, LANES), F32)
        for dr in ((-1, 0, 1) if rows > 1 else (0,)):
            rr = r + dr
            row_ok = jnp.logical_and(rr >= 0, rr < rows)
            start = pl.multiple_of(jnp.clip(rr, 0, rows - 1) * cols, cols)
            tile = x_ref[pl.ds(start, cols), :]
            for dc in (-1, 0, 1):
                if dc == 0:
                    shifted = tile
                else:
                    shifted = pltpu.roll(tile, (-dc) % cols, axis=0)
                    shifted = jnp.where(jnp.logical_and(tpos + dc >= 0, tpos + dc < cols), shifted, 0.0)
                k = (dr + 1) * 3 + (dc + 1)
                acc = acc + shifted * jnp.where(row_ok, w[k:k + 1, :], 0.0)
        o_ref[pl.ds(pl.multiple_of(r * cols, cols), cols), :] = _silu(acc + bias) * scale
        return carry

    lax.fori_loop(0, rows, body, 0)


def _conv_silu(main, conv_w, conv_b, *, batch, rows, cols):
    t = rows * cols
    nct = 2 * QK_W // LANES
    return pl.pallas_call(
        functools.partial(_conv_kernel, rows=rows, cols=cols),
        grid=(batch, nct),
        in_specs=[pl.BlockSpec((t, LANES), lambda b, c: (b, c)),
                  pl.BlockSpec((9, LANES), lambda b, c: (0, c)),
                  pl.BlockSpec((1, LANES), lambda b, c: (0, c))],
        out_specs=pl.BlockSpec((t, LANES), lambda b, c: (b, c)),
        out_shape=jax.ShapeDtypeStruct((batch * t, 2 * QK_W), F32),
        compiler_params=_cparams(("parallel", "arbitrary")),
        name="conv_silu",
    )(main, conv_w, conv_b)


def _chunk_masks(direction):
    row = lax.broadcasted_iota(jnp.int32, (CHUNK, CHUNK), 0)
    col = lax.broadcasted_iota(jnp.int32, (CHUNK, CHUNK), 1)
    seen = (row >= col) if direction == 0 else (row <= col)
    return seen, seen.astype(MXU_DTYPE)


def _scan_specs(batch, nc, col_blocks, widths):
    specs = []
    for direction in (0, 1):
        for cb, wd in zip(col_blocks, widths):
            if direction == 0:
                specs.append(pl.BlockSpec((batch, CHUNK, wd), lambda c, cb=cb: (0, c, cb)))
            else:
                specs.append(pl.BlockSpec((batch, CHUNK, wd), lambda c, cb=cb: (0, nc - 1 - c, cb)))
    return specs


def _scan_out_specs(batch, nc):
    return [pl.BlockSpec((batch, CHUNK, V_W), lambda c: (0, c, 0)),
            pl.BlockSpec((batch, CHUNK, V_W), lambda c: (0, nc - 1 - c, 0))]


def _whole(shape):
    nd = len(shape)
    return pl.BlockSpec(tuple(shape), lambda c: (0,) * nd)


def _mlstm_kernel(*refs, with_output, batch):
    (qk_f, v_f, sm_f, qk_b, v_b, sm_b, bias_ref, c0_ref, n0_ref, m0_ref) = refs[:10]
    if with_output:
        hf_ref, hb_ref, c_s, n_s, m_s = refs[10:]
    else:
        c_out, n_out, m_out, c_s, n_s, m_s = refs[10:]
    step = pl.program_id(0)

    @pl.when(step == 0)
    def _():
        c_s[...] = c0_ref[...]
        n_s[...] = n0_ref[...]
        m_s[...] = m0_ref[...]

    lane = lax.broadcasted_iota(jnp.int32, (1, LANES), 1)
    gate_lane = jnp.logical_and(lane >= SMALL_GATE0, lane < SMALL_GATE0 + 4 * HEADS)
    forget_lane = jnp.logical_and(gate_lane, ((lane - SMALL_GATE0) % (2 * HEADS)) >= HEADS)
    eye = (lax.broadcasted_iota(jnp.int32, (LANES, LANES), 0)
           == lax.broadcasted_iota(jnp.int32, (LANES, LANES), 1)).astype(MXU_DTYPE)

    for direction, (qk_ref, v_ref, sm_ref) in enumerate(((qk_f, v_f, sm_f), (qk_b, v_b, sm_b))):
        seen, seen01 = _chunk_masks(direction)
        last = CHUNK - 1 if direction == 0 else 0
        for bi in range(batch):
            g = sm_ref[bi] + bias_ref[...]
            gp = jnp.where(forget_lane, _log_sigmoid(g), g)
            bc = _dot_exact_lhs(seen01, gp)
            hi, mid, lo = _split3(gp)
            gp_t = _dot_nt(eye, hi) + _dot_nt(eye, mid) + _dot_nt(eye, lo)
            hi, mid, lo = _split3(bc)
            bc_t = _dot_nt(eye, hi) + _dot_nt(eye, mid) + _dot_nt(eye, lo)
            bend_row = bc[last:last + 1, :]
            outs = []
            for h in range(HEADS):
                ji = SMALL_GATE0 + direction * 2 * HEADS + h
                jf = ji + HEADS
                q = qk_ref[bi, :, h * QK:(h + 1) * QK]
                k = qk_ref[bi, :, QK_W + h * QK:QK_W + (h + 1) * QK]
                v = v_ref[bi, :, h * DV:(h + 1) * DV]
                c_old = c_s[bi, direction, h]
                n_old = n_s[bi, direction, h]
                m_old = m_s[bi, direction, h][:, :1]
                b_end = bend_row[:, jf:jf + 1]
                i_col = gp[:, ji:ji + 1]
                b_col = bc[:, jf:jf + 1]
                log_w = b_end - b_col + i_col
                m_new = jnp.maximum(b_end + m_old, jnp.max(log_w, axis=0, keepdims=True))
                w_col = jnp.exp(log_w - m_new)
                decay = jnp.exp(b_end + m_old - m_new)
                kw = k * w_col
                c_s[bi, direction, h] = decay * c_old + _dot_tn(kw, v)
                n_s[bi, direction, h] = decay * n_old + jnp.sum(kw, axis=0, keepdims=True)
                m_s[bi, direction, h] = jnp.broadcast_to(m_new, (1, LANES))
                if with_output:
                    i_row = gp_t[ji:ji + 1, :]
                    b_row = bc_t[jf:jf + 1, :]
                    log_d = jnp.where(seen, b_col - b_row + i_row, -jnp.inf)
                    log_inter = b_col + m_old
                    m_t = jnp.maximum(log_inter, jnp.max(log_d, axis=-1, keepdims=True))
                    s = _dot_nt(q, k) * jnp.exp(log_d - m_t)
                    w_inter = jnp.exp(log_inter - m_t)
                    num = _dot(s, v) + w_inter * _dot(q, c_old)
                    den = jnp.abs(jnp.sum(s, axis=-1, keepdims=True)
                                  + w_inter * jnp.sum(q * n_old, axis=-1, keepdims=True))
                    outs.append(num / jnp.maximum(den, jnp.exp(-m_t)))
            if with_output:
                (hf_ref if direction == 0 else hb_ref)[bi] = jnp.concatenate(outs, axis=-1)

    if not with_output:
        @pl.when(step == pl.num_programs(0) - 1)
        def _():
            c_out[...] = c_s[...]
            n_out[...] = n_s[...]
            m_out[...] = m_s[...]


def _mlstm_scan(qk, main, small, gate_bias, states, *, with_output):
    c0, n0, m0 = states
    batch, t, _ = qk.shape
    nc = t // CHUNK
    in_specs = _scan_specs(batch, nc, (0, COL_V_M, 0), (2 * QK_W, V_W, LANES))
    in_specs += [_whole(gate_bias.shape), _whole(c0.shape), _whole(n0.shape), _whole(m0.shape)]
    if with_output:
        out_specs = _scan_out_specs(batch, nc)
        out_shape = [jax.ShapeDtypeStruct((batch, t, V_W), F32)] * 2
    else:
        out_specs = [_whole(c0.shape), _whole(n0.shape), _whole(m0.shape)]
        out_shape = [jax.ShapeDtypeStruct(s.shape, F32) for s in states]
    return pl.pallas_call(
        functools.partial(_mlstm_kernel, with_output=with_output, batch=batch),
        grid=(nc,),
        in_specs=in_specs,
        out_specs=out_specs,
        out_shape=out_shape,
        scratch_shapes=[pltpu.VMEM(c0.shape, F32), pltpu.VMEM(n0.shape, F32), pltpu.VMEM(m0.shape, F32)],
        compiler_params=_cparams(("arbitrary",)),
        name="mlstm_scan_out" if with_output else "mlstm_scan_state",
    )(qk, main, small, qk, main, small, gate_bias, c0, n0, m0)


def _gla_exact_intra(q, k, v, b, direction):
    row_id = lax.broadcasted_iota(jnp.int32, (CHUNK, 1), 0)

    def row(t, acc):
        pick = row_id == t
        b_t = jnp.sum(jnp.where(pick, b, 0.0), axis=0, keepdims=True)
        q_t = jnp.sum(jnp.where(pick, q, 0.0), axis=0, keepdims=True)
        ok = (row_id <= t) if direction == 0 else (row_id >= t)
        e = jnp.exp(jnp.where(ok, b_t - b, -jnp.inf))
        sc = jnp.sum(q_t * k * e, axis=-1, keepdims=True)
        o_t = jnp.sum(sc * v, axis=0, keepdims=True)
        return jnp.where(pick, o_t, acc)

    return lax.fori_loop(0, CHUNK, row, jnp.zeros((CHUNK, DV), F32))


def _gla_kernel(*refs, with_output, batch):
    (qk_f, v_f, sm_f, qk_b, v_b, sm_b, w2_ref, b2_ref, s0_ref) = refs[:9]
    if with_output:
        of_ref, ob_ref, s_s, b_s, inter_s = refs[9:]
    else:
        s_out, s_s = refs[9:]
    step = pl.program_id(0)

    @pl.when(step == 0)
    def _():
        s_s[...] = s0_ref[...]

    worst_decay = []
    for direction, (qk_ref, v_ref, sm_ref) in enumerate(((qk_f, v_f, sm_f), (qk_b, v_b, sm_b))):
        seen, seen01 = _chunk_masks(direction)
        last = CHUNK - 1 if direction == 0 else 0
        for bi in range(batch):
            z = _dot(sm_ref[bi], w2_ref[direction]) + b2_ref[direction]
            log_a = _log_sigmoid(z) * (1.0 / GLA_TAU)
            b_all = _dot_exact_lhs(seen01, log_a)
            outs, inters = [], []
            for h in range(HEADS):
                q = qk_ref[bi, :, h * QK:(h + 1) * QK] * (QK ** -0.5)
                k = qk_ref[bi, :, QK_W + h * QK:QK_W + (h + 1) * QK]
                v = v_ref[bi, :, h * DV:(h + 1) * DV]
                b = b_all[:, h * QK:(h + 1) * QK]
                b_end = b[last:last + 1, :]
                st_old = s_s[bi, direction, h]
                k_dec = k * jnp.exp(b_end - b)
                s_s[bi, direction, h] = st_old * jnp.exp(b_end) + _dot_tn(v, k_dec)
                if with_output:
                    q_dec = q * jnp.exp(b)
                    inter = _dot_nt(q_dec, st_old)
                    scores = jnp.where(seen, _dot_nt(q_dec, k * jnp.exp(-b)), 0.0)
                    outs.append(_dot(scores, v) + inter)
                    inters.append(inter)
            if with_output:
                (of_ref if direction == 0 else ob_ref)[bi] = jnp.concatenate(outs, axis=-1)
                b_s[bi, direction] = b_all
                inter_s[bi, direction] = jnp.concatenate(inters, axis=-1)
                worst_decay.append(jnp.max(-b_all[last:last + 1, :]))

    if with_output:
        @pl.when(functools.reduce(jnp.maximum, worst_decay) > GLA_SAFE_DECAY)
        def _():
            for direction, (qk_ref, v_ref, o_ref) in enumerate(((qk_f, v_f, of_ref), (qk_b, v_b, ob_ref))):
                last = CHUNK - 1 if direction == 0 else 0
                for bi in range(batch):
                    for h in range(HEADS):
                        b = b_s[bi, direction, :, h * QK:(h + 1) * QK]

                        @pl.when(jnp.max(-b[last:last + 1, :]) > GLA_SAFE_DECAY)
                        def _():
                            q = qk_ref[bi, :, h * QK:(h + 1) * QK] * (QK ** -0.5)
                            k = qk_ref[bi, :, QK_W + h * QK:QK_W + (h + 1) * QK]
                            v = v_ref[bi, :, h * DV:(h + 1) * DV]
                            o_ref[bi, :, h * DV:(h + 1) * DV] = (
                                inter_s[bi, direction, :, h * DV:(h + 1) * DV]
                                + _gla_exact_intra(q, k, v, b, direction))
    else:
        @pl.when(step == pl.num_programs(0) - 1)
        def _():
            s_out[...] = s_s[...]


def _gla_scan(main, small, w2p, b2, s0, *, with_output):
    batch, t, _ = main.shape
    nc = t // CHUNK
    in_specs = _scan_specs(batch, nc, (COL_QK_G, COL_V_G, 0), (2 * QK_W, V_W, LANES))
    in_specs += [_whole(w2p.shape), _whole(b2.shape), _whole(s0.shape)]
    scratch = [pltpu.VMEM(s0.shape, F32)]
    if with_output:
        out_specs = _scan_out_specs(batch, nc)
        out_shape = [jax.ShapeDtypeStruct((batch, t, V_W), F32)] * 2
        scratch += [pltpu.VMEM((batch, 2, CHUNK, QK_W), F32), pltpu.VMEM((batch, 2, CHUNK, V_W), F32)]
    else:
        out_specs = _whole(s0.shape)
        out_shape = jax.ShapeDtypeStruct(s0.shape, F32)
    return pl.pallas_call(
        functools.partial(_gla_kernel, with_output=with_output, batch=batch),
        grid=(nc,),
        in_specs=in_specs,
        out_specs=out_specs,
        out_shape=out_shape,
        scratch_shapes=scratch,
        compiler_params=_cparams(("arbitrary",)),
        name="gla_scan_out" if with_output else "gla_scan_state",
    )(main, main, small, main, main, small, w2p, b2, s0)


def _head_rms(a):
    return jnp.concatenate([_rms(a[:, h * DV:(h + 1) * DV]) for h in range(HEADS)], axis=-1)


def _merge_kernel(hmf, hmb, ogf, ogb, om, rg, mgm, mgg, x_ref, mln, gln, wpm, wpg, wo, gt1, g2, sc2, sh2,
                  wr, br, x1_ref, h2_ref, wt_ref):
    y_m = _head_rms(hmf[...] + hmb[...]) * mln[...] * jax.nn.sigmoid(om[...])
    y_g = _head_rms(ogf[...] + ogb[...]) * gln[...] * _silu(rg[...])
    y = jax.nn.sigmoid(mgm[...]) * _dot(y_m, wpm[...]) + jax.nn.sigmoid(mgg[...]) * _dot(y_g, wpg[...])
    x1 = x_ref[...] + gt1[...] * _dot(y, wo[...])
    x1_ref[...] = x1
    h2 = _rms(x1) * g2[...] * (1.0 + sc2[...]) + sh2[...]
    h2_ref[...] = h2.astype(MXU_DTYPE)

    hh, hm_, _ = _split3(h2)
    wh, wm_, _ = _split3(wr[...])
    lg = _dot(hh, wh) + _dot(hh, wm_) + _dot(hm_, wh) + br[...]
    lane = lax.broadcasted_iota(jnp.int32, lg.shape, 1)

    def masked_softmax(mask):
        l = jnp.where(mask, lg, -jnp.inf)
        e = jnp.exp(l - jnp.max(l, axis=-1, keepdims=True))
        return e / jnp.sum(e, axis=-1, keepdims=True)

    def top1(p, mask):
        pm = jnp.where(mask, p, -1.0)
        best = jnp.max(pm, axis=-1, keepdims=True)
        idx = jnp.min(jnp.where(jnp.logical_and(mask, pm == best), lane, LANES), axis=-1, keepdims=True)
        return best, idx

    gmask = lane < N_GROUPS
    grp_p, grp = top1(masked_softmax(gmask), gmask)
    e_lo = ROUTE_E0 + grp * EXPERTS_PER_GROUP
    emask = jnp.logical_and(lane >= e_lo, lane < e_lo + EXPERTS_PER_GROUP)
    p_in = masked_softmax(emask)
    p1, i1 = top1(p_in, emask)
    p2, i2 = top1(p_in, jnp.logical_and(emask, lane != i1))
    tot = p1 + p2
    wt_ref[...] = (jnp.where(lane == i1, grp_p * p1 / tot, 0.0)
                   + jnp.where(lane == i2, grp_p * p2 / tot, 0.0)
                   + jnp.where(lane == grp, 1.0, 0.0))


def _merge(hmf, hmb, ogf, ogb, main, x2, mln, gln, wpm, wpg, wo, gt1, g2, sc2, sh2, wr, br, *, tm, rows_per_batch):
    m = x2.shape[0]
    tpb = rows_per_batch // tm
    rowblk = pl.BlockSpec((tm, D_MODEL), lambda i: (i, 0))
    colblk = lambda cb: pl.BlockSpec((tm, D_MODEL), lambda i, cb=cb: (i, cb))
    vec = pl.BlockSpec((1, D_MODEL), lambda i: (0, 0))
    bvec = pl.BlockSpec((None, 1, D_MODEL), lambda i: (i // tpb, 0, 0))
    wmat = pl.BlockSpec((D_MODEL, D_MODEL), lambda i: (0, 0))
    return pl.pallas_call(
        _merge_kernel,
        grid=(m // tm,),
        in_specs=[rowblk, rowblk, rowblk, rowblk, colblk(COL_O_M), colblk(COL_R_G), colblk(COL_MG_M),
                  colblk(COL_MG_G), rowblk, vec, vec, wmat, wmat, wmat, bvec, vec, bvec, bvec,
                  pl.BlockSpec((D_MODEL, LANES), lambda i: (0, 0)), pl.BlockSpec((1, LANES), lambda i: (0, 0))],
        out_specs=[rowblk, rowblk, pl.BlockSpec((tm, LANES), lambda i: (i, 0))],
        out_shape=[jax.ShapeDtypeStruct((m, D_MODEL), F32), jax.ShapeDtypeStruct((m, D_MODEL), MXU_DTYPE),
                   jax.ShapeDtypeStruct((m, LANES), F32)],
        compiler_params=_cparams(("parallel",)),
        name="merge_route",
    )(hmf, hmb, ogf, ogb, main, main, main, main, x2, mln, gln, wpm, wpg, wo, gt1, g2, sc2, sh2, wr, br)


MOE_BLK = 128


def _moe_kernel(h2_ref, wt_ref, wup_ref, wdn_ref, x1_ref, gt2_ref, gf_ref, o_ref,
                xs_ref, ys_ref, ws_ref, dest_ref, blk_ref):
    e = pl.program_id(1)
    tm = h2_ref.shape[0]
    n_rows = xs_ref.shape[0]

    @pl.when(e == 0)
    def _():
        r = wt_ref[...]
        lane = lax.broadcasted_iota(jnp.int32, (tm, LANES), 1)
        lane1 = lax.broadcasted_iota(jnp.int32, (1, LANES), 1)
        gm = jnp.where(lane < N_GROUPS, r, 0.0)
        earlier = (lax.broadcasted_iota(jnp.int32, (tm, tm), 1)
                   < lax.broadcasted_iota(jnp.int32, (tm, tm), 0)).astype(MXU_DTYPE)
        before = _dot(earlier, gm)
        padded = jnp.floor((jnp.sum(gm, axis=0, keepdims=True) + (MOE_BLK - 1)) * (1.0 / MOE_BLK)) * MOE_BLK
        start = jnp.zeros((1, LANES), F32)
        run = jnp.zeros((1, 1), F32)
        for g in range(N_GROUPS):
            size = jnp.sum(jnp.where(lane1 == g, padded, 0.0), axis=-1, keepdims=True)
            start = jnp.where(lane1 == g, run, start)
            blk_ref[g] = (jnp.sum(run) * (1.0 / MOE_BLK)).astype(jnp.int32)
            blk_ref[N_GROUPS + g] = (jnp.sum(size) * (1.0 / MOE_BLK)).astype(jnp.int32)
            run = run + size
        dest = jnp.sum(gm * (start + before), axis=-1, keepdims=True)
        dest_ref[...] = jnp.broadcast_to(dest, (tm, LANES))
        dest_row = dest_ref[...].T[0:1, :].astype(jnp.int32)
        perm = (lax.broadcasted_iota(jnp.int32, (n_rows, tm), 0) == dest_row).astype(MXU_DTYPE)
        xs_ref[...] = _dot(perm, h2_ref[...]).astype(MXU_DTYPE)
        hi, mid, lo = _split3(r)
        ws_ref[...] = _dot(perm, hi) + _dot(perm, mid) + _dot(perm, lo)
        ys_ref[...] = jnp.zeros_like(ys_ref)

    group = e // EXPERTS_PER_GROUP
    first_blk = blk_ref[group]
    lane_b = lax.broadcasted_iota(jnp.int32, (MOE_BLK, LANES), 1)

    def block(j, carry):
        r0 = pl.multiple_of((first_blk + j) * MOE_BLK, MOE_BLK)
        gu = _dot(xs_ref[pl.ds(r0, MOE_BLK), :], wup_ref[...])
        hidden = _silu(gu[:, :D_EXPERT]) * gu[:, D_EXPERT:]
        y = _dot(hidden, wdn_ref[...])
        w_col = jnp.sum(jnp.where(lane_b == ROUTE_E0 + e, ws_ref[pl.ds(r0, MOE_BLK), :], 0.0),
                        axis=-1, keepdims=True)
        ys_ref[pl.ds(r0, MOE_BLK), :] += y * w_col
        return carry

    lax.fori_loop(0, blk_ref[N_GROUPS + group], block, 0)

    @pl.when(e == N_EXPERTS - 1)
    def _():
        dest = dest_ref[...][:, :1].astype(jnp.int32)
        unperm = (lax.broadcasted_iota(jnp.int32, (tm, n_rows), 1) == dest).astype(MXU_DTYPE)
        ys = ys_ref[...]
        hi = ys.astype(MXU_DTYPE)
        lo = (ys - hi.astype(F32)).astype(MXU_DTYPE)
        y = _dot(unperm, hi) + _dot(unperm, lo)
        o_ref[...] = _rms(x1_ref[...] + gt2_ref[...] * y) * gf_ref[...]


def _moe_final(h2, wt, w_up, w_down, x1, gt2, g_final, *, tm, rows_per_batch):
    m = h2.shape[0]
    tpb = rows_per_batch // tm
    n_rows = tm + N_GROUPS * MOE_BLK
    rowblk = pl.BlockSpec((tm, D_MODEL), lambda i, e: (i, 0))
    return pl.pallas_call(
        _moe_kernel,
        grid=(m // tm, N_EXPERTS),
        in_specs=[rowblk,
                  pl.BlockSpec((tm, LANES), lambda i, e: (i, 0)),
                  pl.BlockSpec((None, D_MODEL, 2 * D_EXPERT), lambda i, e: (e, 0, 0)),
                  pl.BlockSpec((None, D_EXPERT, D_MODEL), lambda i, e: (e, 0, 0)),
                  rowblk,
                  pl.BlockSpec((None, 1, D_MODEL), lambda i, e: (i // tpb, 0, 0)),
                  pl.BlockSpec((1, D_MODEL), lambda i, e: (0, 0))],
        out_specs=rowblk,
        out_shape=jax.ShapeDtypeStruct((m, D_MODEL), F32),
        scratch_shapes=[pltpu.VMEM((n_rows, D_MODEL), MXU_DTYPE), pltpu.VMEM((n_rows, D_MODEL), F32),
                        pltpu.VMEM((n_rows, LANES), F32), pltpu.VMEM((tm, LANES), F32),
                        pltpu.SMEM((2 * N_GROUPS,), jnp.int32)],
        compiler_params=_cparams(("parallel", "arbitrary"), 56 * 1024 * 1024),
        name="moe_final",
    )(h2, wt, w_up, w_down, x1, gt2, g_final)


def _empty_states(batch):
    ml = (jnp.zeros((batch, 2, HEADS, QK, DV), F32),
          jnp.zeros((batch, 2, HEADS, 1, QK), F32),
          jnp.full((batch, 2, HEADS, 1, LANES), NEG_BIG, F32))
    gla = jnp.zeros((batch, 2, HEADS, DV, QK), F32)
    return ml, gla


def kernel(x, c, ctx, c_ctx, w_mod, b_mod, g_norm1, w_in, ml_conv, ml_conv_b, b_mgate, ml_norm, gla_w2, gla_b2,
           gla_norm, w_proj_m, w_proj_g, w_out, g_norm2, w_grp, b_grp, w_rexp, b_rexp, w_up, w_down, g_final):
    batch, t, d = x.shape
    t_ctx = ctx.shape[1]
    assert d == D_MODEL and w_mod.shape[0] == 1 and w_in.shape[2] == sum(IN_SIZES)
    assert t % (GRID_W * 16) == 0 and t % CHUNK == 0 and t_ctx % CHUNK == 0

    off = [0]
    for s in IN_SIZES:
        off.append(off[-1] + s)
    wi = w_in[0]
    w_main = jnp.concatenate([wi[:, off[0]:off[4]], wi[:, off[5]:off[9]], wi[:, off[10]:off[12]]], axis=1)
    w_small = jnp.concatenate([wi[:, off[9]:off[10]], wi[:, off[4]:off[5]],
                               jnp.zeros((d, LANES - 2 * GLA_RANK - 4 * HEADS), F32)], axis=1)
    w_main = w_main.astype(MXU_DTYPE)
    w_small = w_small.astype(MXU_DTYPE)
    gate_bias = jnp.zeros((1, LANES), F32).at[0, SMALL_GATE0:SMALL_GATE0 + 4 * HEADS].set(b_mgate[0])
    w2p = jnp.zeros((2, LANES, QK_W), F32)
    w2p = w2p.at[0, 0:GLA_RANK].set(gla_w2[0, 0]).at[1, GLA_RANK:2 * GLA_RANK].set(gla_w2[0, 1])
    b2 = gla_b2[0][:, None, :]
    conv_w = ml_conv[0].reshape(9, 2 * QK_W)
    conv_b = ml_conv_b[0][None, :]
    w_route = jnp.concatenate([w_grp[0], w_rexp[0], jnp.zeros((d, LANES - N_GROUPS - N_EXPERTS), F32)], axis=1)
    b_route = jnp.concatenate([b_grp[0], b_rexp[0], jnp.zeros((LANES - N_GROUPS - N_EXPERTS,), F32)])[None, :]

    cc = jnp.concatenate([c, c_ctx[None, :], jnp.zeros((8 - batch - 1, d), F32)], axis=0)
    mod = _modulation(cc, w_mod[0], b_mod[0][None, :])
    sh1, sc1, gt1, sh2, sc2, gt2 = [mod[:batch, i * d:(i + 1) * d][:, None, :] for i in range(6)]
    sh1c, sc1c = [jnp.broadcast_to(mod[batch:batch + 1, i * d:(i + 1) * d][:, None, :], (batch, 1, d)) for i in range(2)]
    g1 = g_norm1[0][None, :]

    main_c, small_c = _in_proj(ctx.reshape(batch * t_ctx, d), g1, sc1c, sh1c, w_main, w_small,
                               tm=t_ctx, rows_per_batch=t_ctx)
    qk_c = _conv_silu(main_c, conv_w, conv_b, batch=batch, rows=1, cols=t_ctx)
    ml0, gla0 = _empty_states(batch)
    main_c3, small_c3 = main_c.reshape(batch, t_ctx, MAIN_W), small_c.reshape(batch, t_ctx, LANES)
    ml_states = _mlstm_scan(qk_c.reshape(batch, t_ctx, 2 * QK_W), main_c3, small_c3, gate_bias, ml0, with_output=False)
    gla_state = _gla_scan(main_c3, small_c3, w2p, b2, gla0, with_output=False)

    x2 = x.reshape(batch * t, d)
    main, small = _in_proj(x2, g1, sc1, sh1, w_main, w_small, tm=1024, rows_per_batch=t)
    qk = _conv_silu(main, conv_w, conv_b, batch=batch, rows=t // GRID_W, cols=GRID_W)
    main3, small3 = main.reshape(batch, t, MAIN_W), small.reshape(batch, t, LANES)
    hm_f, hm_b = [a.reshape(batch * t, V_W) for a in
                  _mlstm_scan(qk.reshape(batch, t, 2 * QK_W), main3, small3, gate_bias, ml_states, with_output=True)]
    og_f, og_b = [a.reshape(batch * t, V_W) for a in _gla_scan(main3, small3, w2p, b2, gla_state, with_output=True)]

    x1, h2, wt = _merge(hm_f, hm_b, og_f, og_b, main, x2, ml_norm, gla_norm,
                        w_proj_m[0].astype(MXU_DTYPE), w_proj_g[0].astype(MXU_DTYPE), w_out[0].astype(MXU_DTYPE),
                        gt1, g_norm2, sc2, sh2, w_route, b_route, tm=256, rows_per_batch=t)
    out = _moe_final(h2, wt, w_up[0].astype(MXU_DTYPE), w_down[0].astype(MXU_DTYPE), x1, gt2, g_final[None, :],
                     tm=1024, rows_per_batch=t)
    return out.reshape(batch, t, d)
```

```python
import functools

import jax
import jax.numpy as jnp
from jax import lax
from jax.experimental import pallas as pl
from jax.experimental.pallas import tpu as pltpu

D_MODEL = 1024
GRID_W = 64
CHUNK = 256
EPS = 1e-6
NEG_BIG = -1e30
HEADS = 4
QK = D_MODEL // 8
DV = D_MODEL // 4
QK_W = HEADS * QK
V_W = HEADS * DV
GLA_RANK = 16
GLA_TAU = 16.0
N_GROUPS = 4
EXPERTS_PER_GROUP = 4
N_EXPERTS = N_GROUPS * EXPERTS_PER_GROUP
D_EXPERT = D_MODEL // 2
IN_SIZES = (QK_W, QK_W, V_W, V_W, 4 * HEADS, QK_W, QK_W, V_W, V_W, 2 * GLA_RANK, D_MODEL, D_MODEL)

LANES = 128
MXU_DTYPE = jnp.bfloat16
F32 = jnp.float32
VMEM_LIMIT = 48 * 1024 * 1024

COL_QK_M, COL_V_M, COL_O_M, COL_QK_G, COL_V_G, COL_R_G, COL_MG_M, COL_MG_G = range(8)
MAIN_W = 8 * D_MODEL
SMALL_GATE0 = 2 * GLA_RANK
GLA_SAFE_DECAY = 80.0
ROUTE_E0 = N_GROUPS


def _dot(a, b):
    return jnp.dot(a.astype(MXU_DTYPE), b.astype(MXU_DTYPE), preferred_element_type=F32)


def _dot_nt(a, b):
    return lax.dot_general(a.astype(MXU_DTYPE), b.astype(MXU_DTYPE), (((1,), (1,)), ((), ())),
                           preferred_element_type=F32)


def _dot_tn(a, b):
    return lax.dot_general(a.astype(MXU_DTYPE), b.astype(MXU_DTYPE), (((0,), (0,)), ((), ())),
                           preferred_element_type=F32)


def _split3(x):
    hi = x.astype(MXU_DTYPE)
    r1 = x - hi.astype(F32)
    mid = r1.astype(MXU_DTYPE)
    lo = (r1 - mid.astype(F32)).astype(MXU_DTYPE)
    return hi, mid, lo


def _dot_exact_lhs(a01, x):
    hi, mid, lo = _split3(x)
    return _dot(a01, hi) + _dot(a01, mid) + _dot(a01, lo)


def _log_sigmoid(x):
    return jnp.minimum(x, 0.0) - jnp.log1p(jnp.exp(-jnp.abs(x)))


def _silu(x):
    return x * jax.nn.sigmoid(x)


def _rms(x):
    return x * lax.rsqrt(jnp.mean(x * x, axis=-1, keepdims=True) + EPS)


def _cparams(sem, vmem_limit=VMEM_LIMIT):
    return pltpu.CompilerParams(dimension_semantics=sem, vmem_limit_bytes=vmem_limit)


def _mod_kernel(c_ref, w_ref, b_ref, o_ref):
    o_ref[...] = _dot(_silu(c_ref[...]), w_ref[...]) + b_ref[...]


def _modulation(cc, w_mod, b_mod):
    n = w_mod.shape[1]
    tn = 512
    return pl.pallas_call(
        _mod_kernel,
        grid=(n // tn,),
        in_specs=[pl.BlockSpec((8, D_MODEL), lambda j: (0, 0)),
                  pl.BlockSpec((D_MODEL, tn), lambda j: (0, j)),
                  pl.BlockSpec((1, tn), lambda j: (0, j))],
        out_specs=pl.BlockSpec((8, tn), lambda j: (0, j)),
        out_shape=jax.ShapeDtypeStruct((8, n), F32),
        compiler_params=_cparams(("arbitrary",)),
        name="modulation",
    )(cc, w_mod, b_mod)


def _inproj_kernel(x_ref, g_ref, sc_ref, sh_ref, w_ref, ws_ref, o_ref, os_ref, xn_ref):
    @pl.when(pl.program_id(1) == 0)
    def _():
        xn = _rms(x_ref[...]) * g_ref[...] * (1.0 + sc_ref[...]) + sh_ref[...]
        xn_ref[...] = xn.astype(MXU_DTYPE)
        os_ref[...] = _dot(xn_ref[...], ws_ref[...])

    o_ref[...] = _dot(xn_ref[...], w_ref[...]).astype(o_ref.dtype)


def _in_proj(x2, g, sc, sh, w_main, w_small, *, tm, rows_per_batch):
    m = x2.shape[0]
    tn = 1024
    tiles_per_batch = rows_per_batch // tm
    vec = pl.BlockSpec((None, 1, D_MODEL), lambda i, j: (i // tiles_per_batch, 0, 0))
    return pl.pallas_call(
        _inproj_kernel,
        grid=(m // tm, MAIN_W // tn),
        in_specs=[pl.BlockSpec((tm, D_MODEL), lambda i, j: (i, 0)),
                  pl.BlockSpec((1, D_MODEL), lambda i, j: (0, 0)),
                  vec, vec,
                  pl.BlockSpec((D_MODEL, tn), lambda i, j: (0, j)),
                  pl.BlockSpec((D_MODEL, LANES), lambda i, j: (0, 0))],
        out_specs=[pl.BlockSpec((tm, tn), lambda i, j: (i, j)),
                   pl.BlockSpec((tm, LANES), lambda i, j: (i, 0))],
        out_shape=[jax.ShapeDtypeStruct((m, MAIN_W), MXU_DTYPE), jax.ShapeDtypeStruct((m, LANES), F32)],
        scratch_shapes=[pltpu.VMEM((tm, D_MODEL), MXU_DTYPE)],
        compiler_params=_cparams(("parallel", "arbitrary")),
        name="in_proj",
    )(x2, g, sc, sh, w_main, w_small)


def _conv_kernel(x_ref, w_ref, b_ref, o_ref, *, rows, cols):
    scale = jnp.where(pl.program_id(1) * LANES >= QK_W, QK ** -0.5, 1.0).astype(F32)
    w = w_ref[...]
    bias = b_ref[...]
    tpos = lax.broadcasted_iota(jnp.int32, (cols, 1), 0)
    has_left = tpos >= 1
    has_right = tpos < cols - 1

    def row_filters(j):
        tile = x_ref[pl.ds(pl.multiple_of(j * cols, cols), cols), :].astype(F32)
        left = jnp.where(has_left, pltpu.roll(tile, 1, axis=0), 0.0)
        right = jnp.where(has_right, pltpu.roll(tile, cols - 1, axis=0), 0.0)
        return [left * w[3 * i:3 * i + 1, :] + tile * w[3 * i + 1:3 * i + 2, :] + right * w[3 * i + 2:3 * i + 3, :]
                for i in range(3)]

    def finish(j, acc):
        o_ref[pl.ds(pl.multiple_of(j * cols, cols), cols), :] = (_silu(acc + bias) * scale).astype(o_ref.dtype)

    first = row_filters(0)

    def body(j, carry):
        acc, below = carry
        h = row_filters(j)
        finish(j - 1, acc + h[2])
        return below + h[1], h[0]

    acc, _ = lax.fori_loop(1, rows, body, (first[1], first[0]))
    finish(rows - 1, acc)


def _conv_silu(main, conv_w, conv_b, *, batch, rows, cols):
    t = rows * cols
    nct = 2 * QK_W // LANES
    return pl.pallas_call(
        functools.partial(_conv_kernel, rows=rows, cols=cols),
        grid=(batch, nct),
        in_specs=[pl.BlockSpec((t, LANES), lambda b, c: (b, c)),
                  pl.BlockSpec((9, LANES), lambda b, c: (0, c)),
                  pl.BlockSpec((1, LANES), lambda b, c: (0, c))],
        out_specs=pl.BlockSpec((t, LANES), lambda b, c: (b, c)),
        out_shape=jax.ShapeDtypeStruct((batch * t, 2 * QK_W), MXU_DTYPE),
        compiler_params=_cparams(("parallel", "arbitrary")),
        name="conv_silu",
    )(main, conv_w, conv_b)


def _chunk_masks(direction):
    row = lax.broadcasted_iota(jnp.int32, (CHUNK, CHUNK), 0)
    col = lax.broadcasted_iota(jnp.int32, (CHUNK, CHUNK), 1)
    seen = (row >= col) if direction == 0 else (row <= col)
    return seen, seen.astype(MXU_DTYPE)


def _scan_specs(batch, nc, col_blocks, widths):
    specs = []
    for direction in (0, 1):
        for cb, wd in zip(col_blocks, widths):
            if direction == 0:
                specs.append(pl.BlockSpec((batch, CHUNK, wd), lambda c, cb=cb: (0, c, cb)))
            else:
                specs.append(pl.BlockSpec((batch, CHUNK, wd), lambda c, cb=cb: (0, nc - 1 - c, cb)))
    return specs


def _scan_out_specs(batch, nc):
    return [pl.BlockSpec((batch, CHUNK, V_W), lambda c: (0, c, 0)),
            pl.BlockSpec((batch, CHUNK, V_W), lambda c: (0, nc - 1 - c, 0))]


def _whole(shape):
    nd = len(shape)
    return pl.BlockSpec(tuple(shape), lambda c: (0,) * nd)


def _mlstm_kernel(*refs, with_output, batch):
    (qk_f, v_f, sm_f, qk_b, v_b, sm_b, bias_ref, c0_ref, n0_ref, m0_ref) = refs[:10]
    if with_output:
        hf_ref, hb_ref, c_s, n_s, m_s = refs[10:]
    else:
        c_out, n_out, m_out, c_s, n_s, m_s = refs[10:]
    step = pl.program_id(0)

    @pl.when(step == 0)
    def _():
        c_s[...] = c0_ref[...]
        n_s[...] = n0_ref[...]
        m_s[...] = m0_ref[...]

    lane = lax.broadcasted_iota(jnp.int32, (1, LANES), 1)
    gate_lane = jnp.logical_and(lane >= SMALL_GATE0, lane < SMALL_GATE0 + 4 * HEADS)
    forget_lane = jnp.logical_and(gate_lane, ((lane - SMALL_GATE0) % (2 * HEADS)) >= HEADS)
    eye = (lax.broadcasted_iota(jnp.int32, (LANES, LANES), 0)
           == lax.broadcasted_iota(jnp.int32, (LANES, LANES), 1)).astype(MXU_DTYPE)

    for direction, (qk_ref, v_ref, sm_ref) in enumerate(((qk_f, v_f, sm_f), (qk_b, v_b, sm_b))):
        seen, seen01 = _chunk_masks(direction)
        last = CHUNK - 1 if direction == 0 else 0
        for bi in range(batch):
            g = sm_ref[bi] + bias_ref[...]
            gp = jnp.where(forget_lane, _log_sigmoid(g), g)
            bc = _dot_exact_lhs(seen01, gp)
            hi, mid, lo = _split3(gp)
            gp_t = _dot_nt(eye, hi) + _dot_nt(eye, mid) + _dot_nt(eye, lo)
            hi, mid, lo = _split3(bc)
            bc_t = _dot_nt(eye, hi) + _dot_nt(eye, mid) + _dot_nt(eye, lo)
            bend_row = bc[last:last + 1, :]
            outs = []
            for h in range(HEADS):
                ji = SMALL_GATE0 + direction * 2 * HEADS + h
                jf = ji + HEADS
                q = qk_ref[bi, :, h * QK:(h + 1) * QK].astype(F32)
                k = qk_ref[bi, :, QK_W + h * QK:QK_W + (h + 1) * QK].astype(F32)
                v = v_ref[bi, :, h * DV:(h + 1) * DV]
                c_old = c_s[bi, direction, h]
                n_old = n_s[bi, direction, h]
                m_old = m_s[bi, direction, h][:, :1]
                b_end = bend_row[:, jf:jf + 1]
                i_col = gp[:, ji:ji + 1]
                b_col = bc[:, jf:jf + 1]
                log_w = b_end - b_col + i_col
                m_new = jnp.maximum(b_end + m_old, jnp.max(log_w, axis=0, keepdims=True))
                w_col = jnp.exp(log_w - m_new)
                decay = jnp.exp(b_end + m_old - m_new)
                kw = k * w_col
                c_s[bi, direction, h] = decay * c_old + _dot_tn(kw, v)
                n_s[bi, direction, h] = decay * n_old + jnp.sum(kw, axis=0, keepdims=True)
                m_s[bi, direction, h] = jnp.broadcast_to(m_new, (1, LANES))
                if with_output:
                    i_row = gp_t[ji:ji + 1, :]
                    b_row = bc_t[jf:jf + 1, :]
                    log_d = jnp.where(seen, b_col - b_row + i_row, -jnp.inf)
                    log_inter = b_col + m_old
                    m_t = jnp.maximum(log_inter, jnp.max(log_d, axis=-1, keepdims=True))
                    s = _dot_nt(q, k) * jnp.exp(log_d - m_t)
                    w_inter = jnp.exp(log_inter - m_t)
                    num = _dot(s, v) + w_inter * _dot(q, c_old)
                    den = jnp.abs(jnp.sum(s, axis=-1, keepdims=True)
                                  + w_inter * jnp.sum(q * n_old, axis=-1, keepdims=True))
                    outs.append(num / jnp.maximum(den, jnp.exp(-m_t)))
            if with_output:
                (hf_ref if direction == 0 else hb_ref)[bi] = jnp.concatenate(outs, axis=-1)

    if not with_output:
        @pl.when(step == pl.num_programs(0) - 1)
        def _():
            c_out[...] = c_s[...]
            n_out[...] = n_s[...]
            m_out[...] = m_s[...]


def _mlstm_scan(qk, main, small, gate_bias, states, *, with_output):
    c0, n0, m0 = states
    batch, t, _ = qk.shape
    nc = t // CHUNK
    in_specs = _scan_specs(batch, nc, (0, COL_V_M, 0), (2 * QK_W, V_W, LANES))
    in_specs += [_whole(gate_bias.shape), _whole(c0.shape), _whole(n0.shape), _whole(m0.shape)]
    if with_output:
        out_specs = _scan_out_specs(batch, nc)
        out_shape = [jax.ShapeDtypeStruct((batch, t, V_W), F32)] * 2
    else:
        out_specs = [_whole(c0.shape), _whole(n0.shape), _whole(m0.shape)]
        out_shape = [jax.ShapeDtypeStruct(s.shape, F32) for s in states]
    return pl.pallas_call(
        functools.partial(_mlstm_kernel, with_output=with_output, batch=batch),
        grid=(nc,),
        in_specs=in_specs,
        out_specs=out_specs,
        out_shape=out_shape,
        scratch_shapes=[pltpu.VMEM(c0.shape, F32), pltpu.VMEM(n0.shape, F32), pltpu.VMEM(m0.shape, F32)],
        compiler_params=_cparams(("arbitrary",)),
        name="mlstm_scan_out" if with_output else "mlstm_scan_state",
    )(qk, main, small, qk, main, small, gate_bias, c0, n0, m0)


def _gla_exact_intra(q, k, v, b, direction):
    row_id = lax.broadcasted_iota(jnp.int32, (CHUNK, 1), 0)

    def row(t, acc):
        pick = row_id == t
        b_t = jnp.sum(jnp.where(pick, b, 0.0), axis=0, keepdims=True)
        q_t = jnp.sum(jnp.where(pick, q, 0.0), axis=0, keepdims=True)
        ok = (row_id <= t) if direction == 0 else (row_id >= t)
        e = jnp.exp(jnp.where(ok, b_t - b, -jnp.inf))
        sc = jnp.sum(q_t * k * e, axis=-1, keepdims=True)
        o_t = jnp.sum(sc * v, axis=0, keepdims=True)
        return jnp.where(pick, o_t, acc)

    return lax.fori_loop(0, CHUNK, row, jnp.zeros((CHUNK, DV), F32))


def _gla_kernel(*refs, with_output, batch):
    (qk_f, v_f, sm_f, qk_b, v_b, sm_b, w2_ref, b2_ref, s0_ref) = refs[:9]
    if with_output:
        of_ref, ob_ref, s_s, b_s, inter_s = refs[9:]
    else:
        s_out, s_s = refs[9:]
    step = pl.program_id(0)

    @pl.when(step == 0)
    def _():
        s_s[...] = s0_ref[...]

    worst_decay = []
    for direction, (qk_ref, v_ref, sm_ref) in enumerate(((qk_f, v_f, sm_f), (qk_b, v_b, sm_b))):
        seen, seen01 = _chunk_masks(direction)
        last = CHUNK - 1 if direction == 0 else 0
        for bi in range(batch):
            z = _dot(sm_ref[bi], w2_ref[direction]) + b2_ref[direction]
            log_a = _log_sigmoid(z) * (1.0 / GLA_TAU)
            b_all = _dot_exact_lhs(seen01, log_a)
            outs, inters = [], []
            for h in range(HEADS):
                q = qk_ref[bi, :, h * QK:(h + 1) * QK].astype(F32) * (QK ** -0.5)
                k = qk_ref[bi, :, QK_W + h * QK:QK_W + (h + 1) * QK].astype(F32)
                v = v_ref[bi, :, h * DV:(h + 1) * DV]
                b = b_all[:, h * QK:(h + 1) * QK]
                b_end = b[last:last + 1, :]
                st_old = s_s[bi, direction, h]
                k_dec = k * jnp.exp(b_end - b)
                s_s[bi, direction, h] = st_old * jnp.exp(b_end) + _dot_tn(v, k_dec)
                if with_output:
                    q_dec = q * jnp.exp(b)
                    inter = _dot_nt(q_dec, st_old)
                    scores = jnp.where(seen, _dot_nt(q_dec, k * jnp.exp(-b)), 0.0)
                    outs.append(_dot(scores, v) + inter)
                    inters.append(inter)
            if with_output:
                (of_ref if direction == 0 else ob_ref)[bi] = jnp.concatenate(outs, axis=-1)
                b_s[bi, direction] = b_all
                inter_s[bi, direction] = jnp.concatenate(inters, axis=-1)
                worst_decay.append(jnp.max(-b_all[last:last + 1, :]))

    if with_output:
        @pl.when(functools.reduce(jnp.maximum, worst_decay) > GLA_SAFE_DECAY)
        def _():
            for direction, (qk_ref, v_ref, o_ref) in enumerate(((qk_f, v_f, of_ref), (qk_b, v_b, ob_ref))):
                last = CHUNK - 1 if direction == 0 else 0
                for bi in range(batch):
                    for h in range(HEADS):
                        b = b_s[bi, direction, :, h * QK:(h + 1) * QK]

                        @pl.when(jnp.max(-b[last:last + 1, :]) > GLA_SAFE_DECAY)
                        def _():
                            q = qk_ref[bi, :, h * QK:(h + 1) * QK].astype(F32) * (QK ** -0.5)
                            k = qk_ref[bi, :, QK_W + h * QK:QK_W + (h + 1) * QK].astype(F32)
                            v = v_ref[bi, :, h * DV:(h + 1) * DV].astype(F32)
                            o_ref[bi, :, h * DV:(h + 1) * DV] = (
                                inter_s[bi, direction, :, h * DV:(h + 1) * DV]
                                + _gla_exact_intra(q, k, v, b, direction))
    else:
        @pl.when(step == pl.num_programs(0) - 1)
        def _():
            s_out[...] = s_s[...]


def _gla_scan(main, small, w2p, b2, s0, *, with_output):
    batch, t, _ = main.shape
    nc = t // CHUNK
    in_specs = _scan_specs(batch, nc, (COL_QK_G, COL_V_G, 0), (2 * QK_W, V_W, LANES))
    in_specs += [_whole(w2p.shape), _whole(b2.shape), _whole(s0.shape)]
    scratch = [pltpu.VMEM(s0.shape, F32)]
    if with_output:
        out_specs = _scan_out_specs(batch, nc)
        out_shape = [jax.ShapeDtypeStruct((batch, t, V_W), F32)] * 2
        scratch += [pltpu.VMEM((batch, 2, CHUNK, QK_W), F32), pltpu.VMEM((batch, 2, CHUNK, V_W), F32)]
    else:
        out_specs = _whole(s0.shape)
        out_shape = jax.ShapeDtypeStruct(s0.shape, F32)
    return pl.pallas_call(
        functools.partial(_gla_kernel, with_output=with_output, batch=batch),
        grid=(nc,),
        in_specs=in_specs,
        out_specs=out_specs,
        out_shape=out_shape,
        scratch_shapes=scratch,
        compiler_params=_cparams(("arbitrary",)),
        name="gla_scan_out" if with_output else "gla_scan_state",
    )(main, main, small, main, main, small, w2p, b2, s0)


def _head_rms(a):
    return jnp.concatenate([_rms(a[:, h * DV:(h + 1) * DV]) for h in range(HEADS)], axis=-1)


def _merge_kernel(hmf, hmb, ogf, ogb, om, rg, mgm, mgg, x_ref, mln, gln, wpm, wpg, wo, gt1, g2, sc2, sh2,
                  wr, br, x1_ref, h2_ref, wt_ref):
    y_m = _head_rms(hmf[...] + hmb[...]) * mln[...] * jax.nn.sigmoid(om[...].astype(F32))
    y_g = _head_rms(ogf[...] + ogb[...]) * gln[...] * _silu(rg[...].astype(F32))
    y = (jax.nn.sigmoid(mgm[...].astype(F32)) * _dot(y_m, wpm[...])
         + jax.nn.sigmoid(mgg[...].astype(F32)) * _dot(y_g, wpg[...]))
    x1 = x_ref[...] + gt1[...] * _dot(y, wo[...])
    x1_ref[...] = x1
    h2 = _rms(x1) * g2[...] * (1.0 + sc2[...]) + sh2[...]
    h2_ref[...] = h2.astype(MXU_DTYPE)

    hh, hm_, _ = _split3(h2)
    wh, wm_, _ = _split3(wr[...])
    lg = _dot(hh, wh) + _dot(hh, wm_) + _dot(hm_, wh) + br[...]
    lane = lax.broadcasted_iota(jnp.int32, lg.shape, 1)

    def masked_softmax(mask):
        l = jnp.where(mask, lg, -jnp.inf)
        e = jnp.exp(l - jnp.max(l, axis=-1, keepdims=True))
        return e / jnp.sum(e, axis=-1, keepdims=True)

    def top1(p, mask):
        pm = jnp.where(mask, p, -1.0)
        best = jnp.max(pm, axis=-1, keepdims=True)
        idx = jnp.min(jnp.where(jnp.logical_and(mask, pm == best), lane, LANES), axis=-1, keepdims=True)
        return best, idx

    gmask = lane < N_GROUPS
    grp_p, grp = top1(masked_softmax(gmask), gmask)
    e_lo = ROUTE_E0 + grp * EXPERTS_PER_GROUP
    emask = jnp.logical_and(lane >= e_lo, lane < e_lo + EXPERTS_PER_GROUP)
    p_in = masked_softmax(emask)
    p1, i1 = top1(p_in, emask)
    p2, i2 = top1(p_in, jnp.logical_and(emask, lane != i1))
    tot = p1 + p2
    wt_ref[...] = (jnp.where(lane == i1, grp_p * p1 / tot, 0.0)
                   + jnp.where(lane == i2, grp_p * p2 / tot, 0.0)
                   + jnp.where(lane == grp, 1.0, 0.0))


def _merge(hmf, hmb, ogf, ogb, main, x2, mln, gln, wpm, wpg, wo, gt1, g2, sc2, sh2, wr, br, *, tm, rows_per_batch):
    m = x2.shape[0]
    tpb = rows_per_batch // tm
    rowblk = pl.BlockSpec((tm, D_MODEL), lambda i: (i, 0))
    colblk = lambda cb: pl.BlockSpec((tm, D_MODEL), lambda i, cb=cb: (i, cb))
    vec = pl.BlockSpec((1, D_MODEL), lambda i: (0, 0))
    bvec = pl.BlockSpec((None, 1, D_MODEL), lambda i: (i // tpb, 0, 0))
    wmat = pl.BlockSpec((D_MODEL, D_MODEL), lambda i: (0, 0))
    return pl.pallas_call(
        _merge_kernel,
        grid=(m // tm,),
        in_specs=[rowblk, rowblk, rowblk, rowblk, colblk(COL_O_M), colblk(COL_R_G), colblk(COL_MG_M),
                  colblk(COL_MG_G), rowblk, vec, vec, wmat, wmat, wmat, bvec, vec, bvec, bvec,
                  pl.BlockSpec((D_MODEL, LANES), lambda i: (0, 0)), pl.BlockSpec((1, LANES), lambda i: (0, 0))],
        out_specs=[rowblk, rowblk, pl.BlockSpec((tm, LANES), lambda i: (i, 0))],
        out_shape=[jax.ShapeDtypeStruct((m, D_MODEL), F32), jax.ShapeDtypeStruct((m, D_MODEL), MXU_DTYPE),
                   jax.ShapeDtypeStruct((m, LANES), F32)],
        compiler_params=_cparams(("parallel",)),
        name="merge_route",
    )(hmf, hmb, ogf, ogb, main, main, main, main, x2, mln, gln, wpm, wpg, wo, gt1, g2, sc2, sh2, wr, br)


MOE_BLK = 128
MOE_COMMON_BLKS = (2, 3)


def _moe_kernel(h2_ref, wt_ref, wup_ref, wdn_ref, x1_ref, gt2_ref, gf_ref, o_ref,
                xs_ref, ys_ref, ws_ref, dest_ref, blk_ref):
    e = pl.program_id(1)
    tm = h2_ref.shape[0]
    n_rows = xs_ref.shape[0]

    @pl.when(e == 0)
    def _():
        r = wt_ref[...]
        lane = lax.broadcasted_iota(jnp.int32, (tm, LANES), 1)
        lane1 = lax.broadcasted_iota(jnp.int32, (1, LANES), 1)
        gm = jnp.where(lane < N_GROUPS, r, 0.0)
        earlier = (lax.broadcasted_iota(jnp.int32, (tm, tm), 1)
                   < lax.broadcasted_iota(jnp.int32, (tm, tm), 0)).astype(MXU_DTYPE)
        before = _dot(earlier, gm)
        padded = jnp.floor((jnp.sum(gm, axis=0, keepdims=True) + (MOE_BLK - 1)) * (1.0 / MOE_BLK)) * MOE_BLK
        start = jnp.zeros((1, LANES), F32)
        run = jnp.zeros((1, 1), F32)
        for g in range(N_GROUPS):
            size = jnp.sum(jnp.where(lane1 == g, padded, 0.0), axis=-1, keepdims=True)
            start = jnp.where(lane1 == g, run, start)
            blk_ref[g] = (jnp.sum(run) * (1.0 / MOE_BLK)).astype(jnp.int32)
            blk_ref[N_GROUPS + g] = (jnp.sum(size) * (1.0 / MOE_BLK)).astype(jnp.int32)
            run = run + size
        dest = jnp.sum(gm * (start + before), axis=-1, keepdims=True)
        dest_ref[...] = jnp.broadcast_to(dest, (tm, LANES))
        dest_row = dest_ref[...].T[0:1, :].astype(jnp.int32)
        perm = (lax.broadcasted_iota(jnp.int32, (n_rows, tm), 0) == dest_row).astype(MXU_DTYPE)
        xs_ref[...] = _dot(perm, h2_ref[...]).astype(MXU_DTYPE)
        hi, mid, lo = _split3(r)
        ws_ref[...] = _dot(perm, hi) + _dot(perm, mid) + _dot(perm, lo)
        ys_ref[...] = jnp.zeros_like(ys_ref)

    group = e // EXPERTS_PER_GROUP
    first_blk = blk_ref[group]
    n_blk = blk_ref[N_GROUPS + group]

    def expert_on(r0, rows):
        gu = _dot(xs_ref[pl.ds(r0, rows), :], wup_ref[...])
        hidden = _silu(gu[:, :D_EXPERT]) * gu[:, D_EXPERT:]
        y = _dot(hidden, wdn_ref[...])
        lane_b = lax.broadcasted_iota(jnp.int32, (rows, LANES), 1)
        w_col = jnp.sum(jnp.where(lane_b == ROUTE_E0 + e, ws_ref[pl.ds(r0, rows), :], 0.0),
                        axis=-1, keepdims=True)
        ys_ref[pl.ds(r0, rows), :] += y * w_col

    for k in MOE_COMMON_BLKS:
        @pl.when(n_blk == k)
        def _():
            expert_on(pl.multiple_of(first_blk * MOE_BLK, MOE_BLK), k * MOE_BLK)

    @pl.when(functools.reduce(jnp.logical_and, [n_blk != k for k in MOE_COMMON_BLKS]))
    def _():
        def block(j, carry):
            expert_on(pl.multiple_of((first_blk + j) * MOE_BLK, MOE_BLK), MOE_BLK)
            return carry

        lax.fori_loop(0, n_blk, block, 0)

    @pl.when(e == N_EXPERTS - 1)
    def _():
        dest = dest_ref[...][:, :1].astype(jnp.int32)
        unperm = (lax.broadcasted_iota(jnp.int32, (tm, n_rows), 1) == dest).astype(MXU_DTYPE)
        ys = ys_ref[...]
        hi = ys.astype(MXU_DTYPE)
        lo = (ys - hi.astype(F32)).astype(MXU_DTYPE)
        y = _dot(unperm, hi) + _dot(unperm, lo)
        o_ref[...] = _rms(x1_ref[...] + gt2_ref[...] * y) * gf_ref[...]


def _moe_final(h2, wt, w_up, w_down, x1, gt2, g_final, *, tm, rows_per_batch):
    m = h2.shape[0]
    tpb = rows_per_batch // tm
    n_rows = tm + N_GROUPS * MOE_BLK
    rowblk = pl.BlockSpec((tm, D_MODEL), lambda i, e: (i, 0))
    return pl.pallas_call(
        _moe_kernel,
        grid=(m // tm, N_EXPERTS),
        in_specs=[rowblk,
                  pl.BlockSpec((tm, LANES), lambda i, e: (i, 0)),
                  pl.BlockSpec((None, D_MODEL, 2 * D_EXPERT), lambda i, e: (e, 0, 0)),
                  pl.BlockSpec((None, D_EXPERT, D_MODEL), lambda i, e: (e, 0, 0)),
                  rowblk,
                  pl.BlockSpec((None, 1, D_MODEL), lambda i, e: (i // tpb, 0, 0)),
                  pl.BlockSpec((1, D_MODEL), lambda i, e: (0, 0))],
        out_specs=rowblk,
        out_shape=jax.ShapeDtypeStruct((m, D_MODEL), F32),
        scratch_shapes=[pltpu.VMEM((n_rows, D_MODEL), MXU_DTYPE), pltpu.VMEM((n_rows, D_MODEL), F32),
                        pltpu.VMEM((n_rows, LANES), F32), pltpu.VMEM((tm, LANES), F32),
                        pltpu.SMEM((2 * N_GROUPS,), jnp.int32)],
        compiler_params=_cparams(("parallel", "arbitrary"), 56 * 1024 * 1024),
        name="moe_final",
    )(h2, wt, w_up, w_down, x1, gt2, g_final)


def _empty_states(batch):
    ml = (jnp.zeros((batch, 2, HEADS, QK, DV), F32),
          jnp.zeros((batch, 2, HEADS, 1, QK), F32),
          jnp.full((batch, 2, HEADS, 1, LANES), NEG_BIG, F32))
    gla = jnp.zeros((batch, 2, HEADS, DV, QK), F32)
    return ml, gla


def kernel(x, c, ctx, c_ctx, w_mod, b_mod, g_norm1, w_in, ml_conv, ml_conv_b, b_mgate, ml_norm, gla_w2, gla_b2,
           gla_norm, w_proj_m, w_proj_g, w_out, g_norm2, w_grp, b_grp, w_rexp, b_rexp, w_up, w_down, g_final):
    batch, t, d = x.shape
    t_ctx = ctx.shape[1]
    assert d == D_MODEL and w_mod.shape[0] == 1 and w_in.shape[2] == sum(IN_SIZES)
    assert t % (GRID_W * 16) == 0 and t % CHUNK == 0 and t_ctx % CHUNK == 0

    off = [0]
    for s in IN_SIZES:
        off.append(off[-1] + s)
    wi = w_in[0]
    w_main = jnp.concatenate([wi[:, off[0]:off[4]], wi[:, off[5]:off[9]], wi[:, off[10]:off[12]]], axis=1)
    w_small = jnp.concatenate([wi[:, off[9]:off[10]], wi[:, off[4]:off[5]],
                               jnp.zeros((d, LANES - 2 * GLA_RANK - 4 * HEADS), F32)], axis=1)
    w_main = w_main.astype(MXU_DTYPE)
    w_small = w_small.astype(MXU_DTYPE)
    gate_bias = jnp.zeros((1, LANES), F32).at[0, SMALL_GATE0:SMALL_GATE0 + 4 * HEADS].set(b_mgate[0])
    w2p = jnp.zeros((2, LANES, QK_W), F32)
    w2p = w2p.at[0, 0:GLA_RANK].set(gla_w2[0, 0]).at[1, GLA_RANK:2 * GLA_RANK].set(gla_w2[0, 1])
    b2 = gla_b2[0][:, None, :]
    conv_w = ml_conv[0].reshape(9, 2 * QK_W)
    conv_b = ml_conv_b[0][None, :]
    w_route = jnp.concatenate([w_grp[0], w_rexp[0], jnp.zeros((d, LANES - N_GROUPS - N_EXPERTS), F32)], axis=1)
    b_route = jnp.concatenate([b_grp[0], b_rexp[0], jnp.zeros((LANES - N_GROUPS - N_EXPERTS,), F32)])[None, :]

    cc = jnp.concatenate([c, c_ctx[None, :], jnp.zeros((8 - batch - 1, d), F32)], axis=0)
    mod = _modulation(cc, w_mod[0], b_mod[0][None, :])
    sh1, sc1, gt1, sh2, sc2, gt2 = [mod[:batch, i * d:(i + 1) * d][:, None, :] for i in range(6)]
    sh1c, sc1c = [jnp.broadcast_to(mod[batch:batch + 1, i * d:(i + 1) * d][:, None, :], (batch, 1, d)) for i in range(2)]
    g1 = g_norm1[0][None, :]

    main_c, small_c = _in_proj(ctx.reshape(batch * t_ctx, d), g1, sc1c, sh1c, w_main, w_small,
                               tm=t_ctx, rows_per_batch=t_ctx)
    qk_c = _conv_silu(main_c, conv_w, conv_b, batch=batch, rows=1, cols=t_ctx)
    ml0, gla0 = _empty_states(batch)
    main_c3, small_c3 = main_c.reshape(batch, t_ctx, MAIN_W), small_c.reshape(batch, t_ctx, LANES)
    ml_states = _mlstm_scan(qk_c.reshape(batch, t_ctx, 2 * QK_W), main_c3, small_c3, gate_bias, ml0, with_output=False)
    gla_state = _gla_scan(main_c3, small_c3, w2p, b2, gla0, with_output=False)

    x2 = x.reshape(batch * t, d)
    main, small = _in_proj(x2, g1, sc1, sh1, w_main, w_small, tm=1024, rows_per_batch=t)
    qk = _conv_silu(main, conv_w, conv_b, batch=batch, rows=t // GRID_W, cols=GRID_W)
    main3, small3 = main.reshape(batch, t, MAIN_W), small.reshape(batch, t, LANES)
    hm_f, hm_b = [a.reshape(batch * t, V_W) for a in
                  _mlstm_scan(qk.reshape(batch, t, 2 * QK_W), main3, small3, gate_bias, ml_states, with_output=True)]
    og_f, og_b = [a.reshape(batch * t, V_W) for a in _gla_scan(main3, small3, w2p, b2, gla_state, with_output=True)]

    x1, h2, wt = _merge(hm_f, hm_b, og_f, og_b, main, x2, ml_norm, gla_norm,
                        w_proj_m[0].astype(MXU_DTYPE), w_proj_g[0].astype(MXU_DTYPE), w_out[0].astype(MXU_DTYPE),
                        gt1, g_norm2, sc2, sh2, w_route, b_route, tm=256, rows_per_batch=t)
    out = _moe_final(h2, wt, w_up[0].astype(MXU_DTYPE), w_down[0].astype(MXU_DTYPE), x1, gt2, g_final[None, :],
                     tm=1024, rows_per_batch=t)
    return out.reshape(batch, t, d)
```

```python
import functools

import jax
import jax.numpy as jnp
from jax import lax
from jax.experimental import pallas as pl
from jax.experimental.pallas import tpu as pltpu

D_MODEL = 1024
GRID_W = 64
CHUNK = 256
EPS = 1e-6
NEG_BIG = -1e30
HEADS = 4
QK = D_MODEL // 8
DV = D_MODEL // 4
QK_W = HEADS * QK
V_W = HEADS * DV
GLA_RANK = 16
GLA_TAU = 16.0
N_GROUPS = 4
EXPERTS_PER_GROUP = 4
N_EXPERTS = N_GROUPS * EXPERTS_PER_GROUP
D_EXPERT = D_MODEL // 2
IN_SIZES = (QK_W, QK_W, V_W, V_W, 4 * HEADS, QK_W, QK_W, V_W, V_W, 2 * GLA_RANK, D_MODEL, D_MODEL)

LANES = 128
MXU_DTYPE = jnp.bfloat16
F32 = jnp.float32
VMEM_LIMIT = 48 * 1024 * 1024

COL_QK_M, COL_V_M, COL_O_M, COL_QK_G, COL_V_G, COL_R_G, COL_MG_M, COL_MG_G = range(8)
MAIN_W = 8 * D_MODEL
SMALL_GATE0 = 2 * GLA_RANK
GLA_SAFE_DECAY = 80.0
ROUTE_E0 = N_GROUPS


def _dot(a, b):
    return jnp.dot(a.astype(MXU_DTYPE), b.astype(MXU_DTYPE), preferred_element_type=F32)


def _dot_nt(a, b):
    return lax.dot_general(a.astype(MXU_DTYPE), b.astype(MXU_DTYPE), (((1,), (1,)), ((), ())),
                           preferred_element_type=F32)


def _dot_tn(a, b):
    return lax.dot_general(a.astype(MXU_DTYPE), b.astype(MXU_DTYPE), (((0,), (0,)), ((), ())),
                           preferred_element_type=F32)


def _split3(x):
    hi = x.astype(MXU_DTYPE)
    r1 = x - hi.astype(F32)
    mid = r1.astype(MXU_DTYPE)
    lo = (r1 - mid.astype(F32)).astype(MXU_DTYPE)
    return hi, mid, lo


def _dot_exact_lhs(a01, x):
    hi, mid, lo = _split3(x)
    return _dot(a01, hi) + _dot(a01, mid) + _dot(a01, lo)


def _log_sigmoid(x):
    return jnp.minimum(x, 0.0) - jnp.log1p(jnp.exp(-jnp.abs(x)))


def _silu(x):
    return x * jax.nn.sigmoid(x)


def _rms(x):
    return x * lax.rsqrt(jnp.mean(x * x, axis=-1, keepdims=True) + EPS)


def _cparams(sem, vmem_limit=VMEM_LIMIT):
    return pltpu.CompilerParams(dimension_semantics=sem, vmem_limit_bytes=vmem_limit)


def _mod_kernel(c_ref, w_ref, b_ref, o_ref):
    o_ref[...] = _dot(_silu(c_ref[...]), w_ref[...]) + b_ref[...]


def _modulation(cc, w_mod, b_mod):
    n = w_mod.shape[1]
    tn = 512
    return pl.pallas_call(
        _mod_kernel,
        grid=(n // tn,),
        in_specs=[pl.BlockSpec((8, D_MODEL), lambda j: (0, 0)),
                  pl.BlockSpec((D_MODEL, tn), lambda j: (0, j)),
                  pl.BlockSpec((1, tn), lambda j: (0, j))],
        out_specs=pl.BlockSpec((8, tn), lambda j: (0, j)),
        out_shape=jax.ShapeDtypeStruct((8, n), F32),
        compiler_params=_cparams(("arbitrary",)),
        name="modulation",
    )(cc, w_mod, b_mod)


def _inproj_kernel(x_ref, g_ref, sc_ref, sh_ref, w_ref, ws_ref, o_ref, os_ref, xn_ref):
    @pl.when(pl.program_id(1) == 0)
    def _():
        xn = _rms(x_ref[...]) * g_ref[...] * (1.0 + sc_ref[...]) + sh_ref[...]
        xn_ref[...] = xn.astype(MXU_DTYPE)
        os_ref[...] = _dot(xn_ref[...], ws_ref[...])

    o_ref[...] = _dot(xn_ref[...], w_ref[...]).astype(o_ref.dtype)


def _in_proj(x2, g, sc, sh, w_main, w_small, *, tm, rows_per_batch):
    m = x2.shape[0]
    tn = 2048
    tiles_per_batch = rows_per_batch // tm
    vec = pl.BlockSpec((None, 1, D_MODEL), lambda i, j: (i // tiles_per_batch, 0, 0))
    return pl.pallas_call(
        _inproj_kernel,
        grid=(m // tm, MAIN_W // tn),
        in_specs=[pl.BlockSpec((tm, D_MODEL), lambda i, j: (i, 0)),
                  pl.BlockSpec((1, D_MODEL), lambda i, j: (0, 0)),
                  vec, vec,
                  pl.BlockSpec((D_MODEL, tn), lambda i, j: (0, j)),
                  pl.BlockSpec((D_MODEL, LANES), lambda i, j: (0, 0))],
        out_specs=[pl.BlockSpec((tm, tn), lambda i, j: (i, j)),
                   pl.BlockSpec((tm, LANES), lambda i, j: (i, 0))],
        out_shape=[jax.ShapeDtypeStruct((m, MAIN_W), MXU_DTYPE), jax.ShapeDtypeStruct((m, LANES), F32)],
        scratch_shapes=[pltpu.VMEM((tm, D_MODEL), MXU_DTYPE)],
        compiler_params=_cparams(("parallel", "arbitrary")),
        name="in_proj",
    )(x2, g, sc, sh, w_main, w_small)


def _conv_kernel(x_ref, w_ref, b_ref, o_ref, *, rows, cols):
    scale = jnp.where(pl.program_id(1) * LANES >= QK_W, QK ** -0.5, 1.0).astype(F32)
    w = w_ref[...]
    bias = b_ref[...]
    tpos = lax.broadcasted_iota(jnp.int32, (cols, 1), 0)
    has_left = tpos >= 1
    has_right = tpos < cols - 1

    def row_filters(j):
        tile = x_ref[pl.ds(pl.multiple_of(j * cols, cols), cols), :].astype(F32)
        left = jnp.where(has_left, pltpu.roll(tile, 1, axis=0), 0.0)
        right = jnp.where(has_right, pltpu.roll(tile, cols - 1, axis=0), 0.0)
        return [left * w[3 * i:3 * i + 1, :] + tile * w[3 * i + 1:3 * i + 2, :] + right * w[3 * i + 2:3 * i + 3, :]
                for i in range(3)]

    def finish(j, acc):
        o_ref[pl.ds(pl.multiple_of(j * cols, cols), cols), :] = (_silu(acc + bias) * scale).astype(o_ref.dtype)

    first = row_filters(0)

    def body(j, carry):
        acc, below = carry
        h = row_filters(j)
        finish(j - 1, acc + h[2])
        return below + h[1], h[0]

    acc, _ = lax.fori_loop(1, rows, body, (first[1], first[0]))
    finish(rows - 1, acc)


def _conv_silu(main, conv_w, conv_b, *, batch, rows, cols):
    t = rows * cols
    nct = 2 * QK_W // LANES
    return pl.pallas_call(
        functools.partial(_conv_kernel, rows=rows, cols=cols),
        grid=(batch, nct),
        in_specs=[pl.BlockSpec((t, LANES), lambda b, c: (b, c)),
                  pl.BlockSpec((9, LANES), lambda b, c: (0, c)),
                  pl.BlockSpec((1, LANES), lambda b, c: (0, c))],
        out_specs=pl.BlockSpec((t, LANES), lambda b, c: (b, c)),
        out_shape=jax.ShapeDtypeStruct((batch * t, 2 * QK_W), MXU_DTYPE),
        compiler_params=_cparams(("parallel", "arbitrary")),
        name="conv_silu",
    )(main, conv_w, conv_b)


def _chunk_masks(direction):
    row = lax.broadcasted_iota(jnp.int32, (CHUNK, CHUNK), 0)
    col = lax.broadcasted_iota(jnp.int32, (CHUNK, CHUNK), 1)
    seen = (row >= col) if direction == 0 else (row <= col)
    return seen, seen.astype(MXU_DTYPE)


def _scan_specs(batch, nc, col_blocks, widths):
    specs = []
    for direction in (0, 1):
        for cb, wd in zip(col_blocks, widths):
            if direction == 0:
                specs.append(pl.BlockSpec((batch, CHUNK, wd), lambda c, cb=cb: (0, c, cb)))
            else:
                specs.append(pl.BlockSpec((batch, CHUNK, wd), lambda c, cb=cb: (0, nc - 1 - c, cb)))
    return specs


def _scan_out_specs(batch, nc):
    return [pl.BlockSpec((batch, CHUNK, V_W), lambda c: (0, c, 0)),
            pl.BlockSpec((batch, CHUNK, V_W), lambda c: (0, nc - 1 - c, 0))]


def _whole(shape):
    nd = len(shape)
    return pl.BlockSpec(tuple(shape), lambda c: (0,) * nd)


def _mlstm_kernel(*refs, with_output, batch):
    (qk_f, v_f, sm_f, qk_b, v_b, sm_b, bias_ref, c0_ref, n0_ref, m0_ref) = refs[:10]
    if with_output:
        hf_ref, hb_ref, c_s, n_s, m_s = refs[10:]
    else:
        c_out, n_out, m_out, c_s, n_s, m_s = refs[10:]
    step = pl.program_id(0)

    @pl.when(step == 0)
    def _():
        c_s[...] = c0_ref[...]
        n_s[...] = n0_ref[...]
        m_s[...] = m0_ref[...]

    lane = lax.broadcasted_iota(jnp.int32, (1, LANES), 1)
    gate_lane = jnp.logical_and(lane >= SMALL_GATE0, lane < SMALL_GATE0 + 4 * HEADS)
    forget_lane = jnp.logical_and(gate_lane, ((lane - SMALL_GATE0) % (2 * HEADS)) >= HEADS)
    eye = (lax.broadcasted_iota(jnp.int32, (LANES, LANES), 0)
           == lax.broadcasted_iota(jnp.int32, (LANES, LANES), 1)).astype(MXU_DTYPE)

    for direction, (qk_ref, v_ref, sm_ref) in enumerate(((qk_f, v_f, sm_f), (qk_b, v_b, sm_b))):
        seen, seen01 = _chunk_masks(direction)
        last = CHUNK - 1 if direction == 0 else 0
        for bi in range(batch):
            g = sm_ref[bi] + bias_ref[...]
            gp = jnp.where(forget_lane, _log_sigmoid(g), g)
            bc = _dot_exact_lhs(seen01, gp)
            hi, mid, lo = _split3(gp)
            gp_t = _dot_nt(eye, hi) + _dot_nt(eye, mid) + _dot_nt(eye, lo)
            hi, mid, lo = _split3(bc)
            bc_t = _dot_nt(eye, hi) + _dot_nt(eye, mid) + _dot_nt(eye, lo)
            bend_row = bc[last:last + 1, :]
            outs = []
            for h in range(HEADS):
                ji = SMALL_GATE0 + direction * 2 * HEADS + h
                jf = ji + HEADS
                q = qk_ref[bi, :, h * QK:(h + 1) * QK].astype(F32)
                k = qk_ref[bi, :, QK_W + h * QK:QK_W + (h + 1) * QK].astype(F32)
                v = v_ref[bi, :, h * DV:(h + 1) * DV]
                c_old = c_s[bi, direction, h]
                n_old = n_s[bi, direction, h]
                m_old = m_s[bi, direction, h][:, :1]
                b_end = bend_row[:, jf:jf + 1]
                i_col = gp[:, ji:ji + 1]
                b_col = bc[:, jf:jf + 1]
                log_w = b_end - b_col + i_col
                m_new = jnp.maximum(b_end + m_old, jnp.max(log_w, axis=0, keepdims=True))
                w_col = jnp.exp(log_w - m_new)
                decay = jnp.exp(b_end + m_old - m_new)
                kw = k * w_col
                c_s[bi, direction, h] = decay * c_old + _dot_tn(kw, v)
                n_s[bi, direction, h] = decay * n_old + jnp.sum(kw, axis=0, keepdims=True)
                m_s[bi, direction, h] = jnp.broadcast_to(m_new, (1, LANES))
                if with_output:
                    i_row = gp_t[ji:ji + 1, :]
                    b_row = bc_t[jf:jf + 1, :]
                    log_d = jnp.where(seen, b_col - b_row + i_row, -jnp.inf)
                    log_inter = b_col + m_old
                    m_t = jnp.maximum(log_inter, jnp.max(log_d, axis=-1, keepdims=True))
                    s = _dot_nt(q, k) * jnp.exp(log_d - m_t)
                    w_inter = jnp.exp(log_inter - m_t)
                    num = _dot(s, v) + w_inter * _dot(q, c_old)
                    den = jnp.abs(jnp.sum(s, axis=-1, keepdims=True)
                                  + w_inter * jnp.sum(q * n_old, axis=-1, keepdims=True))
                    outs.append(num / jnp.maximum(den, jnp.exp(-m_t)))
            if with_output:
                (hf_ref if direction == 0 else hb_ref)[bi] = jnp.concatenate(outs, axis=-1)

    if not with_output:
        @pl.when(step == pl.num_programs(0) - 1)
        def _():
            c_out[...] = c_s[...]
            n_out[...] = n_s[...]
            m_out[...] = m_s[...]


def _mlstm_scan(qk, main, small, gate_bias, states, *, with_output):
    c0, n0, m0 = states
    batch, t, _ = qk.shape
    nc = t // CHUNK
    in_specs = _scan_specs(batch, nc, (0, COL_V_M, 0), (2 * QK_W, V_W, LANES))
    in_specs += [_whole(gate_bias.shape), _whole(c0.shape), _whole(n0.shape), _whole(m0.shape)]
    if with_output:
        out_specs = _scan_out_specs(batch, nc)
        out_shape = [jax.ShapeDtypeStruct((batch, t, V_W), F32)] * 2
    else:
        out_specs = [_whole(c0.shape), _whole(n0.shape), _whole(m0.shape)]
        out_shape = [jax.ShapeDtypeStruct(s.shape, F32) for s in states]
    return pl.pallas_call(
        functools.partial(_mlstm_kernel, with_output=with_output, batch=batch),
        grid=(nc,),
        in_specs=in_specs,
        out_specs=out_specs,
        out_shape=out_shape,
        scratch_shapes=[pltpu.VMEM(c0.shape, F32), pltpu.VMEM(n0.shape, F32), pltpu.VMEM(m0.shape, F32)],
        compiler_params=_cparams(("arbitrary",)),
        name="mlstm_scan_out" if with_output else "mlstm_scan_state",
    )(qk, main, small, qk, main, small, gate_bias, c0, n0, m0)


def _gla_exact_intra(q, k, v, b, direction):
    row_id = lax.broadcasted_iota(jnp.int32, (CHUNK, 1), 0)

    def row(t, acc):
        pick = row_id == t
        b_t = jnp.sum(jnp.where(pick, b, 0.0), axis=0, keepdims=True)
        q_t = jnp.sum(jnp.where(pick, q, 0.0), axis=0, keepdims=True)
        ok = (row_id <= t) if direction == 0 else (row_id >= t)
        e = jnp.exp(jnp.where(ok, b_t - b, -jnp.inf))
        sc = jnp.sum(q_t * k * e, axis=-1, keepdims=True)
        o_t = jnp.sum(sc * v, axis=0, keepdims=True)
        return jnp.where(pick, o_t, acc)

    return lax.fori_loop(0, CHUNK, row, jnp.zeros((CHUNK, DV), F32))


def _gla_kernel(*refs, with_output, batch):
    (qk_f, v_f, sm_f, qk_b, v_b, sm_b, w2_ref, b2_ref, s0_ref) = refs[:9]
    if with_output:
        of_ref, ob_ref, s_s, b_s, inter_s = refs[9:]
    else:
        s_out, s_s = refs[9:]
    step = pl.program_id(0)

    @pl.when(step == 0)
    def _():
        s_s[...] = s0_ref[...]

    worst_decay = []
    for direction, (qk_ref, v_ref, sm_ref) in enumerate(((qk_f, v_f, sm_f), (qk_b, v_b, sm_b))):
        seen, seen01 = _chunk_masks(direction)
        last = CHUNK - 1 if direction == 0 else 0
        for bi in range(batch):
            z = _dot(sm_ref[bi], w2_ref[direction]) + b2_ref[direction]
            log_a = _log_sigmoid(z) * (1.0 / GLA_TAU)
            b_all = _dot_exact_lhs(seen01, log_a)
            outs, inters = [], []
            for h in range(HEADS):
                q = qk_ref[bi, :, h * QK:(h + 1) * QK].astype(F32) * (QK ** -0.5)
                k = qk_ref[bi, :, QK_W + h * QK:QK_W + (h + 1) * QK].astype(F32)
                v = v_ref[bi, :, h * DV:(h + 1) * DV]
                b = b_all[:, h * QK:(h + 1) * QK]
                b_end = b[last:last + 1, :]
                st_old = s_s[bi, direction, h]
                k_dec = k * jnp.exp(b_end - b)
                s_s[bi, direction, h] = st_old * jnp.exp(b_end) + _dot_tn(v, k_dec)
                if with_output:
                    q_dec = q * jnp.exp(b)
                    inter = _dot_nt(q_dec, st_old)
                    scores = jnp.where(seen, _dot_nt(q_dec, k * jnp.exp(-b)), 0.0)
                    outs.append(_dot(scores, v) + inter)
                    inters.append(inter)
            if with_output:
                (of_ref if direction == 0 else ob_ref)[bi] = jnp.concatenate(outs, axis=-1)
                b_s[bi, direction] = b_all
                inter_s[bi, direction] = jnp.concatenate(inters, axis=-1)
                worst_decay.append(jnp.max(-b_all[last:last + 1, :]))

    if with_output:
        @pl.when(functools.reduce(jnp.maximum, worst_decay) > GLA_SAFE_DECAY)
        def _():
            for direction, (qk_ref, v_ref, o_ref) in enumerate(((qk_f, v_f, of_ref), (qk_b, v_b, ob_ref))):
                last = CHUNK - 1 if direction == 0 else 0
                for bi in range(batch):
                    for h in range(HEADS):
                        b = b_s[bi, direction, :, h * QK:(h + 1) * QK]

                        @pl.when(jnp.max(-b[last:last + 1, :]) > GLA_SAFE_DECAY)
                        def _():
                            q = qk_ref[bi, :, h * QK:(h + 1) * QK].astype(F32) * (QK ** -0.5)
                            k = qk_ref[bi, :, QK_W + h * QK:QK_W + (h + 1) * QK].astype(F32)
                            v = v_ref[bi, :, h * DV:(h + 1) * DV].astype(F32)
                            o_ref[bi, :, h * DV:(h + 1) * DV] = (
                                inter_s[bi, direction, :, h * DV:(h + 1) * DV]
                                + _gla_exact_intra(q, k, v, b, direction))
    else:
        @pl.when(step == pl.num_programs(0) - 1)
        def _():
            s_out[...] = s_s[...]


def _gla_scan(main, small, w2p, b2, s0, *, with_output):
    batch, t, _ = main.shape
    nc = t // CHUNK
    in_specs = _scan_specs(batch, nc, (COL_QK_G, COL_V_G, 0), (2 * QK_W, V_W, LANES))
    in_specs += [_whole(w2p.shape), _whole(b2.shape), _whole(s0.shape)]
    scratch = [pltpu.VMEM(s0.shape, F32)]
    if with_output:
        out_specs = _scan_out_specs(batch, nc)
        out_shape = [jax.ShapeDtypeStruct((batch, t, V_W), F32)] * 2
        scratch += [pltpu.VMEM((batch, 2, CHUNK, QK_W), F32), pltpu.VMEM((batch, 2, CHUNK, V_W), F32)]
    else:
        out_specs = _whole(s0.shape)
        out_shape = jax.ShapeDtypeStruct(s0.shape, F32)
    return pl.pallas_call(
        functools.partial(_gla_kernel, with_output=with_output, batch=batch),
        grid=(nc,),
        in_specs=in_specs,
        out_specs=out_specs,
        out_shape=out_shape,
        scratch_shapes=scratch,
        compiler_params=_cparams(("arbitrary",)),
        name="gla_scan_out" if with_output else "gla_scan_state",
    )(main, main, small, main, main, small, w2p, b2, s0)


def _head_rms(a):
    return jnp.concatenate([_rms(a[:, h * DV:(h + 1) * DV]) for h in range(HEADS)], axis=-1)


def _merge_kernel(hmf, hmb, ogf, ogb, om, rg, mgm, mgg, x_ref, mln, gln, wpm, wpg, wo, gt1, g2, sc2, sh2,
                  wr, br, x1_ref, h2_ref, wt_ref):
    y_m = _head_rms(hmf[...] + hmb[...]) * mln[...] * jax.nn.sigmoid(om[...].astype(F32))
    y_g = _head_rms(ogf[...] + ogb[...]) * gln[...] * _silu(rg[...].astype(F32))
    y = (jax.nn.sigmoid(mgm[...].astype(F32)) * _dot(y_m, wpm[...])
         + jax.nn.sigmoid(mgg[...].astype(F32)) * _dot(y_g, wpg[...]))
    x1 = x_ref[...] + gt1[...] * _dot(y, wo[...])
    x1_ref[...] = x1
    h2 = _rms(x1) * g2[...] * (1.0 + sc2[...]) + sh2[...]
    h2_ref[...] = h2.astype(MXU_DTYPE)

    hh, hm_, _ = _split3(h2)
    wh, wm_, _ = _split3(wr[...])
    lg = _dot(hh, wh) + _dot(hh, wm_) + _dot(hm_, wh) + br[...]
    lane = lax.broadcasted_iota(jnp.int32, lg.shape, 1)

    def masked_softmax(mask):
        l = jnp.where(mask, lg, -jnp.inf)
        e = jnp.exp(l - jnp.max(l, axis=-1, keepdims=True))
        return e / jnp.sum(e, axis=-1, keepdims=True)

    def top1(p, mask):
        pm = jnp.where(mask, p, -1.0)
        best = jnp.max(pm, axis=-1, keepdims=True)
        idx = jnp.min(jnp.where(jnp.logical_and(mask, pm == best), lane, LANES), axis=-1, keepdims=True)
        return best, idx

    gmask = lane < N_GROUPS
    grp_p, grp = top1(masked_softmax(gmask), gmask)
    e_lo = ROUTE_E0 + grp * EXPERTS_PER_GROUP
    emask = jnp.logical_and(lane >= e_lo, lane < e_lo + EXPERTS_PER_GROUP)
    p_in = masked_softmax(emask)
    p1, i1 = top1(p_in, emask)
    p2, i2 = top1(p_in, jnp.logical_and(emask, lane != i1))
    tot = p1 + p2
    wt_ref[...] = (jnp.where(lane == i1, grp_p * p1 / tot, 0.0)
                   + jnp.where(lane == i2, grp_p * p2 / tot, 0.0)
                   + jnp.where(lane == grp, 1.0, 0.0))


def _merge(hmf, hmb, ogf, ogb, main, x2, mln, gln, wpm, wpg, wo, gt1, g2, sc2, sh2, wr, br, *, tm, rows_per_batch):
    m = x2.shape[0]
    tpb = rows_per_batch // tm
    rowblk = pl.BlockSpec((tm, D_MODEL), lambda i: (i, 0))
    colblk = lambda cb: pl.BlockSpec((tm, D_MODEL), lambda i, cb=cb: (i, cb))
    vec = pl.BlockSpec((1, D_MODEL), lambda i: (0, 0))
    bvec = pl.BlockSpec((None, 1, D_MODEL), lambda i: (i // tpb, 0, 0))
    wmat = pl.BlockSpec((D_MODEL, D_MODEL), lambda i: (0, 0))
    return pl.pallas_call(
        _merge_kernel,
        grid=(m // tm,),
        in_specs=[rowblk, rowblk, rowblk, rowblk, colblk(COL_O_M), colblk(COL_R_G), colblk(COL_MG_M),
                  colblk(COL_MG_G), rowblk, vec, vec, wmat, wmat, wmat, bvec, vec, bvec, bvec,
                  pl.BlockSpec((D_MODEL, LANES), lambda i: (0, 0)), pl.BlockSpec((1, LANES), lambda i: (0, 0))],
        out_specs=[rowblk, rowblk, pl.BlockSpec((tm, LANES), lambda i: (i, 0))],
        out_shape=[jax.ShapeDtypeStruct((m, D_MODEL), F32), jax.ShapeDtypeStruct((m, D_MODEL), MXU_DTYPE),
                   jax.ShapeDtypeStruct((m, LANES), F32)],
        compiler_params=_cparams(("parallel",)),
        name="merge_route",
    )(hmf, hmb, ogf, ogb, main, main, main, main, x2, mln, gln, wpm, wpg, wo, gt1, g2, sc2, sh2, wr, br)


MOE_BLK = 128
MOE_COMMON_BLKS = (2, 3)
MOE_EXPERTS_PER_STEP = 2


def _moe_kernel(h2_ref, wt_ref, wup_ref, wdn_ref, x1_ref, gt2_ref, gf_ref, o_ref,
                xs_ref, ys_ref, ws_ref, dest_ref, blk_ref):
    step = pl.program_id(1)
    tm = h2_ref.shape[0]
    n_rows = xs_ref.shape[0]

    @pl.when(step == 0)
    def _():
        r = wt_ref[...]
        lane = lax.broadcasted_iota(jnp.int32, (tm, LANES), 1)
        lane1 = lax.broadcasted_iota(jnp.int32, (1, LANES), 1)
        gm = jnp.where(lane < N_GROUPS, r, 0.0)
        earlier = (lax.broadcasted_iota(jnp.int32, (tm, tm), 1)
                   < lax.broadcasted_iota(jnp.int32, (tm, tm), 0)).astype(MXU_DTYPE)
        before = _dot(earlier, gm)
        padded = jnp.floor((jnp.sum(gm, axis=0, keepdims=True) + (MOE_BLK - 1)) * (1.0 / MOE_BLK)) * MOE_BLK
        start = jnp.zeros((1, LANES), F32)
        run = jnp.zeros((1, 1), F32)
        for g in range(N_GROUPS):
            size = jnp.sum(jnp.where(lane1 == g, padded, 0.0), axis=-1, keepdims=True)
            start = jnp.where(lane1 == g, run, start)
            blk_ref[g] = (jnp.sum(run) * (1.0 / MOE_BLK)).astype(jnp.int32)
            blk_ref[N_GROUPS + g] = (jnp.sum(size) * (1.0 / MOE_BLK)).astype(jnp.int32)
            run = run + size
        dest = jnp.sum(gm * (start + before), axis=-1, keepdims=True)
        dest_ref[...] = jnp.broadcast_to(dest, (tm, LANES))
        dest_row = dest_ref[...].T[0:1, :].astype(jnp.int32)
        perm = (lax.broadcasted_iota(jnp.int32, (n_rows, tm), 0) == dest_row).astype(MXU_DTYPE)
        xs_ref[...] = _dot(perm, h2_ref[...]).astype(MXU_DTYPE)
        hi, mid, lo = _split3(r)
        ws_ref[...] = _dot(perm, hi) + _dot(perm, mid) + _dot(perm, lo)
        ys_ref[...] = jnp.zeros_like(ys_ref)

    group = step // (EXPERTS_PER_GROUP // MOE_EXPERTS_PER_STEP)
    first_blk = blk_ref[group]
    n_blk = blk_ref[N_GROUPS + group]

    def expert_on(r0, rows):
        x = xs_ref[pl.ds(r0, rows), :]
        ws = ws_ref[pl.ds(r0, rows), :]
        lane_b = lax.broadcasted_iota(jnp.int32, (rows, LANES), 1)
        total = jnp.zeros((rows, D_MODEL), F32)
        for k in range(MOE_EXPERTS_PER_STEP):
            gu = _dot(x, wup_ref[k])
            hidden = _silu(gu[:, :D_EXPERT]) * gu[:, D_EXPERT:]
            y = _dot(hidden, wdn_ref[k])
            lane_e = ROUTE_E0 + step * MOE_EXPERTS_PER_STEP + k
            total = total + y * jnp.sum(jnp.where(lane_b == lane_e, ws, 0.0), axis=-1, keepdims=True)
        ys_ref[pl.ds(r0, rows), :] += total

    for k in MOE_COMMON_BLKS:
        @pl.when(n_blk == k)
        def _():
            expert_on(pl.multiple_of(first_blk * MOE_BLK, MOE_BLK), k * MOE_BLK)

    @pl.when(functools.reduce(jnp.logical_and, [n_blk != k for k in MOE_COMMON_BLKS]))
    def _():
        def block(j, carry):
            expert_on(pl.multiple_of((first_blk + j) * MOE_BLK, MOE_BLK), MOE_BLK)
            return carry

        lax.fori_loop(0, n_blk, block, 0)

    @pl.when(step == pl.num_programs(1) - 1)
    def _():
        dest = dest_ref[...][:, :1].astype(jnp.int32)
        unperm = (lax.broadcasted_iota(jnp.int32, (tm, n_rows), 1) == dest).astype(MXU_DTYPE)
        ys = ys_ref[...]
        hi = ys.astype(MXU_DTYPE)
        lo = (ys - hi.astype(F32)).astype(MXU_DTYPE)
        y = _dot(unperm, hi) + _dot(unperm, lo)
        o_ref[...] = _rms(x1_ref[...] + gt2_ref[...] * y) * gf_ref[...]


def _moe_final(h2, wt, w_up, w_down, x1, gt2, g_final, *, tm, rows_per_batch):
    m = h2.shape[0]
    tpb = rows_per_batch // tm
    n_rows = tm + N_GROUPS * MOE_BLK
    rowblk = pl.BlockSpec((tm, D_MODEL), lambda i, e: (i, 0))
    return pl.pallas_call(
        _moe_kernel,
        grid=(m // tm, N_EXPERTS // MOE_EXPERTS_PER_STEP),
        in_specs=[rowblk,
                  pl.BlockSpec((tm, LANES), lambda i, e: (i, 0)),
                  pl.BlockSpec((MOE_EXPERTS_PER_STEP, D_MODEL, 2 * D_EXPERT), lambda i, e: (e, 0, 0)),
                  pl.BlockSpec((MOE_EXPERTS_PER_STEP, D_EXPERT, D_MODEL), lambda i, e: (e, 0, 0)),
                  rowblk,
                  pl.BlockSpec((None, 1, D_MODEL), lambda i, e: (i // tpb, 0, 0)),
                  pl.BlockSpec((1, D_MODEL), lambda i, e: (0, 0))],
        out_specs=rowblk,
        out_shape=jax.ShapeDtypeStruct((m, D_MODEL), F32),
        scratch_shapes=[pltpu.VMEM((n_rows, D_MODEL), MXU_DTYPE), pltpu.VMEM((n_rows, D_MODEL), F32),
                        pltpu.VMEM((n_rows, LANES), F32), pltpu.VMEM((tm, LANES), F32),
                        pltpu.SMEM((2 * N_GROUPS,), jnp.int32)],
        compiler_params=_cparams(("parallel", "arbitrary"), 56 * 1024 * 1024),
        name="moe_final",
    )(h2, wt, w_up, w_down, x1, gt2, g_final)


def _empty_states(batch):
    ml = (jnp.zeros((batch, 2, HEADS, QK, DV), F32),
          jnp.zeros((batch, 2, HEADS, 1, QK), F32),
          jnp.full((batch, 2, HEADS, 1, LANES), NEG_BIG, F32))
    gla = jnp.zeros((batch, 2, HEADS, DV, QK), F32)
    return ml, gla


def kernel(x, c, ctx, c_ctx, w_mod, b_mod, g_norm1, w_in, ml_conv, ml_conv_b, b_mgate, ml_norm, gla_w2, gla_b2,
           gla_norm, w_proj_m, w_proj_g, w_out, g_norm2, w_grp, b_grp, w_rexp, b_rexp, w_up, w_down, g_final):
    batch, t, d = x.shape
    t_ctx = ctx.shape[1]
    assert d == D_MODEL and w_mod.shape[0] == 1 and w_in.shape[2] == sum(IN_SIZES)
    assert t % (GRID_W * 16) == 0 and t % CHUNK == 0 and t_ctx % CHUNK == 0

    off = [0]
    for s in IN_SIZES:
        off.append(off[-1] + s)
    wi = w_in[0]
    w_main = jnp.concatenate([wi[:, off[0]:off[4]], wi[:, off[5]:off[9]], wi[:, off[10]:off[12]]], axis=1)
    w_small = jnp.concatenate([wi[:, off[9]:off[10]], wi[:, off[4]:off[5]],
                               jnp.zeros((d, LANES - 2 * GLA_RANK - 4 * HEADS), F32)], axis=1)
    w_main = w_main.astype(MXU_DTYPE)
    w_small = w_small.astype(MXU_DTYPE)
    gate_bias = jnp.zeros((1, LANES), F32).at[0, SMALL_GATE0:SMALL_GATE0 + 4 * HEADS].set(b_mgate[0])
    w2p = jnp.zeros((2, LANES, QK_W), F32)
    w2p = w2p.at[0, 0:GLA_RANK].set(gla_w2[0, 0]).at[1, GLA_RANK:2 * GLA_RANK].set(gla_w2[0, 1])
    b2 = gla_b2[0][:, None, :]
    conv_w = ml_conv[0].reshape(9, 2 * QK_W)
    conv_b = ml_conv_b[0][None, :]
    w_route = jnp.concatenate([w_grp[0], w_rexp[0], jnp.zeros((d, LANES - N_GROUPS - N_EXPERTS), F32)], axis=1)
    b_route = jnp.concatenate([b_grp[0], b_rexp[0], jnp.zeros((LANES - N_GROUPS - N_EXPERTS,), F32)])[None, :]

    cc = jnp.concatenate([c, c_ctx[None, :], jnp.zeros((8 - batch - 1, d), F32)], axis=0)
    mod = _modulation(cc, w_mod[0], b_mod[0][None, :])
    sh1, sc1, gt1, sh2, sc2, gt2 = [mod[:batch, i * d:(i + 1) * d][:, None, :] for i in range(6)]
    sh1c, sc1c = [jnp.broadcast_to(mod[batch:batch + 1, i * d:(i + 1) * d][:, None, :], (batch, 1, d)) for i in range(2)]
    g1 = g_norm1[0][None, :]

    main_c, small_c = _in_proj(ctx.reshape(batch * t_ctx, d), g1, sc1c, sh1c, w_main, w_small,
                               tm=t_ctx, rows_per_batch=t_ctx)
    qk_c = _conv_silu(main_c, conv_w, conv_b, batch=batch, rows=1, cols=t_ctx)
    ml0, gla0 = _empty_states(batch)
    main_c3, small_c3 = main_c.reshape(batch, t_ctx, MAIN_W), small_c.reshape(batch, t_ctx, LANES)
    ml_states = _mlstm_scan(qk_c.reshape(batch, t_ctx, 2 * QK_W), main_c3, small_c3, gate_bias, ml0, with_output=False)
    gla_state = _gla_scan(main_c3, small_c3, w2p, b2, gla0, with_output=False)

    x2 = x.reshape(batch * t, d)
    main, small = _in_proj(x2, g1, sc1, sh1, w_main, w_small, tm=1024, rows_per_batch=t)
    qk = _conv_silu(main, conv_w, conv_b, batch=batch, rows=t // GRID_W, cols=GRID_W)
    main3, small3 = main.reshape(batch, t, MAIN_W), small.reshape(batch, t, LANES)
    hm_f, hm_b = [a.reshape(batch * t, V_W) for a in
                  _mlstm_scan(qk.reshape(batch, t, 2 * QK_W), main3, small3, gate_bias, ml_states, with_output=True)]
    og_f, og_b = [a.reshape(batch * t, V_W) for a in _gla_scan(main3, small3, w2p, b2, gla_state, with_output=True)]

    x1, h2, wt = _merge(hm_f, hm_b, og_f, og_b, main, x2, ml_norm, gla_norm,
                        w_proj_m[0].astype(MXU_DTYPE), w_proj_g[0].astype(MXU_DTYPE), w_out[0].astype(MXU_DTYPE),
                        gt1, g_norm2, sc2, sh2, w_route, b_route, tm=256, rows_per_batch=t)
    out = _moe_final(h2, wt, w_up[0].astype(MXU_DTYPE), w_down[0].astype(MXU_DTYPE), x1, gt2, g_final[None, :],
                     tm=1024, rows_per_batch=t)
    return out.reshape(batch, t, d)
```

```python
import functools

import jax
import jax.numpy as jnp
from jax import lax
from jax.experimental import pallas as pl
from jax.experimental.pallas import tpu as pltpu

D_MODEL = 1024
GRID_W = 64
CHUNK = 256
EPS = 1e-6
NEG_BIG = -1e30
HEADS = 4
QK = D_MODEL // 8
DV = D_MODEL // 4
QK_W = HEADS * QK
V_W = HEADS * DV
GLA_RANK = 16
GLA_TAU = 16.0
N_GROUPS = 4
EXPERTS_PER_GROUP = 4
N_EXPERTS = N_GROUPS * EXPERTS_PER_GROUP
D_EXPERT = D_MODEL // 2
IN_SIZES = (QK_W, QK_W, V_W, V_W, 4 * HEADS, QK_W, QK_W, V_W, V_W, 2 * GLA_RANK, D_MODEL, D_MODEL)

LANES = 128
MXU_DTYPE = jnp.bfloat16
F32 = jnp.float32
VMEM_LIMIT = 48 * 1024 * 1024

COL_QK_M, COL_V_M, COL_O_M, COL_QK_G, COL_V_G, COL_R_G, COL_MG_M, COL_MG_G = range(8)
MAIN_W = 8 * D_MODEL
SMALL_GATE0 = 2 * GLA_RANK
GLA_SAFE_DECAY = 80.0
ROUTE_E0 = N_GROUPS


def _dot(a, b):
    return jnp.dot(a.astype(MXU_DTYPE), b.astype(MXU_DTYPE), preferred_element_type=F32)


def _dot_nt(a, b):
    return lax.dot_general(a.astype(MXU_DTYPE), b.astype(MXU_DTYPE), (((1,), (1,)), ((), ())),
                           preferred_element_type=F32)


def _transpose_mxu(a):
    m = a.shape[1]
    eye = (lax.broadcasted_iota(jnp.int32, (m, m), 0) == lax.broadcasted_iota(jnp.int32, (m, m), 1))
    return _dot_nt(eye.astype(MXU_DTYPE), a).astype(MXU_DTYPE)


def _dot_tn(a, b):
    return _dot(_transpose_mxu(a.astype(MXU_DTYPE)), b)


def _dot_tn_xlu(a, b):
    return lax.dot_general(a.astype(MXU_DTYPE), b.astype(MXU_DTYPE), (((0,), (0,)), ((), ())),
                           preferred_element_type=F32)


def _split3(x):
    hi = x.astype(MXU_DTYPE)
    r1 = x - hi.astype(F32)
    mid = r1.astype(MXU_DTYPE)
    lo = (r1 - mid.astype(F32)).astype(MXU_DTYPE)
    return hi, mid, lo


def _dot_exact_lhs(a01, x):
    hi, mid, lo = _split3(x)
    return _dot(a01, hi) + _dot(a01, mid) + _dot(a01, lo)


def _log_sigmoid(x):
    return jnp.minimum(x, 0.0) - jnp.log1p(jnp.exp(-jnp.abs(x)))


def _silu(x):
    return x * jax.nn.sigmoid(x)


def _rms(x):
    return x * lax.rsqrt(jnp.mean(x * x, axis=-1, keepdims=True) + EPS)


def _cparams(sem, vmem_limit=VMEM_LIMIT):
    return pltpu.CompilerParams(dimension_semantics=sem, vmem_limit_bytes=vmem_limit)


def _mod_kernel(c_ref, w_ref, b_ref, o_ref):
    o_ref[...] = _dot(_silu(c_ref[...]), w_ref[...]) + b_ref[...]


def _modulation(cc, w_mod, b_mod):
    n = w_mod.shape[1]
    tn = 512
    return pl.pallas_call(
        _mod_kernel,
        grid=(n // tn,),
        in_specs=[pl.BlockSpec((8, D_MODEL), lambda j: (0, 0)),
                  pl.BlockSpec((D_MODEL, tn), lambda j: (0, j)),
                  pl.BlockSpec((1, tn), lambda j: (0, j))],
        out_specs=pl.BlockSpec((8, tn), lambda j: (0, j)),
        out_shape=jax.ShapeDtypeStruct((8, n), F32),
        compiler_params=_cparams(("arbitrary",)),
        name="modulation",
    )(cc, w_mod, b_mod)


def _inproj_kernel(x_ref, g_ref, sc_ref, sh_ref, w_ref, ws_ref, o_ref, os_ref, xn_ref):
    @pl.when(pl.program_id(1) == 0)
    def _():
        xn = _rms(x_ref[...]) * g_ref[...] * (1.0 + sc_ref[...]) + sh_ref[...]
        xn_ref[...] = xn.astype(MXU_DTYPE)
        os_ref[...] = _dot(xn_ref[...], ws_ref[...])

    o_ref[...] = _dot(xn_ref[...], w_ref[...]).astype(o_ref.dtype)


def _in_proj(x2, g, sc, sh, w_main, w_small, *, tm, rows_per_batch):
    m = x2.shape[0]
    tn = 2048
    tiles_per_batch = rows_per_batch // tm
    vec = pl.BlockSpec((None, 1, D_MODEL), lambda i, j: (i // tiles_per_batch, 0, 0))
    return pl.pallas_call(
        _inproj_kernel,
        grid=(m // tm, MAIN_W // tn),
        in_specs=[pl.BlockSpec((tm, D_MODEL), lambda i, j: (i, 0)),
                  pl.BlockSpec((1, D_MODEL), lambda i, j: (0, 0)),
                  vec, vec,
                  pl.BlockSpec((D_MODEL, tn), lambda i, j: (0, j)),
                  pl.BlockSpec((D_MODEL, LANES), lambda i, j: (0, 0))],
        out_specs=[pl.BlockSpec((tm, tn), lambda i, j: (i, j)),
                   pl.BlockSpec((tm, LANES), lambda i, j: (i, 0))],
        out_shape=[jax.ShapeDtypeStruct((m, MAIN_W), MXU_DTYPE), jax.ShapeDtypeStruct((m, LANES), F32)],
        scratch_shapes=[pltpu.VMEM((tm, D_MODEL), MXU_DTYPE)],
        compiler_params=_cparams(("parallel", "arbitrary")),
        name="in_proj",
    )(x2, g, sc, sh, w_main, w_small)


def _conv_kernel(x_ref, w_ref, b_ref, o_ref, *, rows, cols):
    scale = jnp.where(pl.program_id(1) * LANES >= QK_W, QK ** -0.5, 1.0).astype(F32)
    w = w_ref[...]
    bias = b_ref[...]
    tpos = lax.broadcasted_iota(jnp.int32, (cols, 1), 0)
    has_left = tpos >= 1
    has_right = tpos < cols - 1

    def row_filters(j):
        tile = x_ref[pl.ds(pl.multiple_of(j * cols, cols), cols), :].astype(F32)
        left = jnp.where(has_left, pltpu.roll(tile, 1, axis=0), 0.0)
        right = jnp.where(has_right, pltpu.roll(tile, cols - 1, axis=0), 0.0)
        return [left * w[3 * i:3 * i + 1, :] + tile * w[3 * i + 1:3 * i + 2, :] + right * w[3 * i + 2:3 * i + 3, :]
                for i in range(3)]

    def finish(j, acc):
        o_ref[pl.ds(pl.multiple_of(j * cols, cols), cols), :] = (_silu(acc + bias) * scale).astype(o_ref.dtype)

    first = row_filters(0)

    def body(j, carry):
        acc, below = carry
        h = row_filters(j)
        finish(j - 1, acc + h[2])
        return below + h[1], h[0]

    acc, _ = lax.fori_loop(1, rows, body, (first[1], first[0]))
    finish(rows - 1, acc)


def _conv_silu(main, conv_w, conv_b, *, batch, rows, cols):
    t = rows * cols
    nct = 2 * QK_W // LANES
    return pl.pallas_call(
        functools.partial(_conv_kernel, rows=rows, cols=cols),
        grid=(batch, nct),
        in_specs=[pl.BlockSpec((t, LANES), lambda b, c: (b, c)),
                  pl.BlockSpec((9, LANES), lambda b, c: (0, c)),
                  pl.BlockSpec((1, LANES), lambda b, c: (0, c))],
        out_specs=pl.BlockSpec((t, LANES), lambda b, c: (b, c)),
        out_shape=jax.ShapeDtypeStruct((batch * t, 2 * QK_W), MXU_DTYPE),
        compiler_params=_cparams(("parallel", "arbitrary")),
        name="conv_silu",
    )(main, conv_w, conv_b)


def _chunk_masks(direction):
    row = lax.broadcasted_iota(jnp.int32, (CHUNK, CHUNK), 0)
    col = lax.broadcasted_iota(jnp.int32, (CHUNK, CHUNK), 1)
    seen = (row >= col) if direction == 0 else (row <= col)
    return seen, seen.astype(MXU_DTYPE)


def _scan_specs(batch, nc, col_blocks, widths):
    specs = []
    for direction in (0, 1):
        for cb, wd in zip(col_blocks, widths):
            if direction == 0:
                specs.append(pl.BlockSpec((batch, CHUNK, wd), lambda c, cb=cb: (0, c, cb)))
            else:
                specs.append(pl.BlockSpec((batch, CHUNK, wd), lambda c, cb=cb: (0, nc - 1 - c, cb)))
    return specs


def _scan_out_specs(batch, nc):
    return [pl.BlockSpec((batch, CHUNK, V_W), lambda c: (0, c, 0)),
            pl.BlockSpec((batch, CHUNK, V_W), lambda c: (0, nc - 1 - c, 0))]


def _whole(shape):
    nd = len(shape)
    return pl.BlockSpec(tuple(shape), lambda c: (0,) * nd)


def _lane_tile(x, width):
    return jnp.concatenate([x] * (width // LANES), axis=-1)


def _mlstm_kernel(*refs, with_output, batch):
    (qk_f, v_f, sm_f, qk_b, v_b, sm_b, bias_ref, cn0_ref, m0_ref) = refs[:9]
    if with_output:
        hf_ref, hb_ref, cn_s, m_s = refs[9:]
    else:
        cn_out, m_out, cn_s, m_s = refs[9:]
    step = pl.program_id(0)

    @pl.when(step == 0)
    def _():
        cn_s[...] = cn0_ref[...]
        m_s[...] = m0_ref[...]

    lane = lax.broadcasted_iota(jnp.int32, (1, LANES), 1)
    gate_lane = jnp.logical_and(lane >= SMALL_GATE0, lane < SMALL_GATE0 + 4 * HEADS)
    forget_lane = jnp.logical_and(gate_lane, ((lane - SMALL_GATE0) % (2 * HEADS)) >= HEADS)
    eye = (lax.broadcasted_iota(jnp.int32, (LANES, LANES), 0)
           == lax.broadcasted_iota(jnp.int32, (LANES, LANES), 1)).astype(MXU_DTYPE)
    ones_cols = jnp.ones((CHUNK, LANES), MXU_DTYPE)

    for direction, (qk_ref, v_ref, sm_ref) in enumerate(((qk_f, v_f, sm_f), (qk_b, v_b, sm_b))):
        seen, seen01 = _chunk_masks(direction)
        last = CHUNK - 1 if direction == 0 else 0
        for bi in range(batch):
            g = sm_ref[bi] + bias_ref[...]
            gp = jnp.where(forget_lane, _log_sigmoid(g), g)
            bc = _dot_exact_lhs(seen01, gp)
            hi, mid, lo = _split3(gp)
            gp_t = _dot_nt(eye, hi) + _dot_nt(eye, mid) + _dot_nt(eye, lo)
            hi, mid, lo = _split3(bc)
            bc_t = _dot_nt(eye, hi) + _dot_nt(eye, mid) + _dot_nt(eye, lo)
            bend_row = bc[last:last + 1, :]
            outs = []
            for h in range(HEADS):
                ji = SMALL_GATE0 + direction * 2 * HEADS + h
                jf = ji + HEADS
                q = qk_ref[bi, :, h * QK:(h + 1) * QK].astype(F32)
                k = qk_ref[bi, :, QK_W + h * QK:QK_W + (h + 1) * QK].astype(F32)
                v_ext = jnp.concatenate([v_ref[bi, :, h * DV:(h + 1) * DV].astype(MXU_DTYPE), ones_cols], axis=-1)
                cn_old = cn_s[bi, direction, h]
                m_old = m_s[bi, direction, h]
                b_end = jnp.broadcast_to(bend_row[:, jf:jf + 1], (1, LANES))
                i_col = jnp.broadcast_to(gp[:, ji:ji + 1], (CHUNK, LANES))
                b_col = jnp.broadcast_to(bc[:, jf:jf + 1], (CHUNK, LANES))
                log_w = b_end - b_col + i_col
                m_new = jnp.maximum(b_end + m_old, jnp.max(log_w, axis=0, keepdims=True))
                kw = k * jnp.exp(log_w - m_new)
                decay = jnp.exp(b_end + m_old - m_new)
                cn_s[bi, direction, h] = _lane_tile(decay, DV + LANES) * cn_old + _dot_tn(kw, v_ext)
                m_s[bi, direction, h] = m_new
                if with_output:
                    i_row = gp_t[ji:ji + 1, :]
                    b_row = bc_t[jf:jf + 1, :]
                    log_d = jnp.where(seen, _lane_tile(b_col, CHUNK) - b_row + i_row, -jnp.inf)
                    log_inter = b_col + m_old
                    m_t = jnp.maximum(log_inter, jnp.max(log_d, axis=-1, keepdims=True))
                    s = _dot_nt(q, k) * jnp.exp(log_d - _lane_tile(m_t, CHUNK))
                    w_inter = jnp.exp(log_inter - m_t)
                    sv = _dot(s, v_ext)
                    qc = _dot(q, cn_old)
                    num = sv[:, :DV] + _lane_tile(w_inter, DV) * qc[:, :DV]
                    den = jnp.abs(sv[:, DV:] + w_inter * qc[:, DV:])
                    outs.append(num / _lane_tile(jnp.maximum(den, jnp.exp(-m_t)), DV))
            if with_output:
                (hf_ref if direction == 0 else hb_ref)[bi] = jnp.concatenate(outs, axis=-1)

    if not with_output:
        @pl.when(step == pl.num_programs(0) - 1)
        def _():
            cn_out[...] = cn_s[...]
            m_out[...] = m_s[...]


def _mlstm_scan(qk, main, small, gate_bias, states, *, with_output):
    cn0, m0 = states
    batch, t, _ = qk.shape
    nc = t // CHUNK
    in_specs = _scan_specs(batch, nc, (0, COL_V_M, 0), (2 * QK_W, V_W, LANES))
    in_specs += [_whole(gate_bias.shape), _whole(cn0.shape), _whole(m0.shape)]
    if with_output:
        out_specs = _scan_out_specs(batch, nc)
        out_shape = [jax.ShapeDtypeStruct((batch, t, V_W), F32)] * 2
    else:
        out_specs = [_whole(cn0.shape), _whole(m0.shape)]
        out_shape = [jax.ShapeDtypeStruct(s.shape, F32) for s in states]
    return pl.pallas_call(
        functools.partial(_mlstm_kernel, with_output=with_output, batch=batch),
        grid=(nc,),
        in_specs=in_specs,
        out_specs=out_specs,
        out_shape=out_shape,
        scratch_shapes=[pltpu.VMEM(cn0.shape, F32), pltpu.VMEM(m0.shape, F32)],
        compiler_params=_cparams(("arbitrary",)),
        name="mlstm_scan_out" if with_output else "mlstm_scan_state",
    )(qk, main, small, qk, main, small, gate_bias, cn0, m0)


def _gla_exact_intra(q, k, v, b, direction):
    row_id = lax.broadcasted_iota(jnp.int32, (CHUNK, 1), 0)

    def row(t, acc):
        pick = row_id == t
        b_t = jnp.sum(jnp.where(pick, b, 0.0), axis=0, keepdims=True)
        q_t = jnp.sum(jnp.where(pick, q, 0.0), axis=0, keepdims=True)
        ok = (row_id <= t) if direction == 0 else (row_id >= t)
        e = jnp.exp(jnp.where(ok, b_t - b, -jnp.inf))
        sc = jnp.sum(q_t * k * e, axis=-1, keepdims=True)
        o_t = jnp.sum(sc * v, axis=0, keepdims=True)
        return jnp.where(pick, o_t, acc)

    return lax.fori_loop(0, CHUNK, row, jnp.zeros((CHUNK, DV), F32))


def _gla_kernel(*refs, with_output, batch):
    (qk_f, v_f, sm_f, qk_b, v_b, sm_b, w2_ref, b2_ref, s0_ref) = refs[:9]
    if with_output:
        of_ref, ob_ref, s_s, b_s, inter_s = refs[9:]
    else:
        s_out, s_s = refs[9:]
    step = pl.program_id(0)

    @pl.when(step == 0)
    def _():
        s_s[...] = s0_ref[...]

    worst_decay = []
    for direction, (qk_ref, v_ref, sm_ref) in enumerate(((qk_f, v_f, sm_f), (qk_b, v_b, sm_b))):
        seen, seen01 = _chunk_masks(direction)
        last = CHUNK - 1 if direction == 0 else 0
        for bi in range(batch):
            z = _dot(sm_ref[bi], w2_ref[direction]) + b2_ref[direction]
            log_a = _log_sigmoid(z) * (1.0 / GLA_TAU)
            b_all = _dot_exact_lhs(seen01, log_a)
            outs, inters = [], []
            for h in range(HEADS):
                q = qk_ref[bi, :, h * QK:(h + 1) * QK].astype(F32) * (QK ** -0.5)
                k = qk_ref[bi, :, QK_W + h * QK:QK_W + (h + 1) * QK].astype(F32)
                v = v_ref[bi, :, h * DV:(h + 1) * DV]
                b = b_all[:, h * QK:(h + 1) * QK]
                b_end = b[last:last + 1, :]
                st_old = s_s[bi, direction, h]
                k_dec = k * jnp.exp(b_end - b)
                s_s[bi, direction, h] = st_old * jnp.exp(b_end) + _dot_tn_xlu(v, k_dec)
                if with_output:
                    q_dec = q * jnp.exp(b)
                    inter = _dot_nt(q_dec, st_old)
                    scores = jnp.where(seen, _dot_nt(q_dec, k * jnp.exp(-b)), 0.0)
                    outs.append(_dot(scores, v) + inter)
                    inters.append(inter)
            if with_output:
                (of_ref if direction == 0 else ob_ref)[bi] = jnp.concatenate(outs, axis=-1)
                b_s[bi, direction] = b_all
                inter_s[bi, direction] = jnp.concatenate(inters, axis=-1)
                worst_decay.append(jnp.max(-b_all[last:last + 1, :]))

    if with_output:
        @pl.when(functools.reduce(jnp.maximum, worst_decay) > GLA_SAFE_DECAY)
        def _():
            for direction, (qk_ref, v_ref, o_ref) in enumerate(((qk_f, v_f, of_ref), (qk_b, v_b, ob_ref))):
                last = CHUNK - 1 if direction == 0 else 0
                for bi in range(batch):
                    for h in range(HEADS):
                        b = b_s[bi, direction, :, h * QK:(h + 1) * QK]

                        @pl.when(jnp.max(-b[last:last + 1, :]) > GLA_SAFE_DECAY)
                        def _():
                            q = qk_ref[bi, :, h * QK:(h + 1) * QK].astype(F32) * (QK ** -0.5)
                            k = qk_ref[bi, :, QK_W + h * QK:QK_W + (h + 1) * QK].astype(F32)
                            v = v_ref[bi, :, h * DV:(h + 1) * DV].astype(F32)
                            o_ref[bi, :, h * DV:(h + 1) * DV] = (
                                inter_s[bi, direction, :, h * DV:(h + 1) * DV]
                                + _gla_exact_intra(q, k, v, b, direction))
    else:
        @pl.when(step == pl.num_programs(0) - 1)
        def _():
            s_out[...] = s_s[...]


def _gla_scan(main, small, w2p, b2, s0, *, with_output):
    batch, t, _ = main.shape
    nc = t // CHUNK
    in_specs = _scan_specs(batch, nc, (COL_QK_G, COL_V_G, 0), (2 * QK_W, V_W, LANES))
    in_specs += [_whole(w2p.shape), _whole(b2.shape), _whole(s0.shape)]
    scratch = [pltpu.VMEM(s0.shape, F32)]
    if with_output:
        out_specs = _scan_out_specs(batch, nc)
        out_shape = [jax.ShapeDtypeStruct((batch, t, V_W), F32)] * 2
        scratch += [pltpu.VMEM((batch, 2, CHUNK, QK_W), F32), pltpu.VMEM((batch, 2, CHUNK, V_W), F32)]
    else:
        out_specs = _whole(s0.shape)
        out_shape = jax.ShapeDtypeStruct(s0.shape, F32)
    return pl.pallas_call(
        functools.partial(_gla_kernel, with_output=with_output, batch=batch),
        grid=(nc,),
        in_specs=in_specs,
        out_specs=out_specs,
        out_shape=out_shape,
        scratch_shapes=scratch,
        compiler_params=_cparams(("arbitrary",)),
        name="gla_scan_out" if with_output else "gla_scan_state",
    )(main, main, small, main, main, small, w2p, b2, s0)


def _head_rms(a):
    return jnp.concatenate([_rms(a[:, h * DV:(h + 1) * DV]) for h in range(HEADS)], axis=-1)


def _merge_kernel(hmf, hmb, ogf, ogb, om, rg, mgm, mgg, x_ref, mln, gln, wpm, wpg, wo, gt1, g2, sc2, sh2,
                  wr, br, x1_ref, h2_ref, wt_ref):
    y_m = _head_rms(hmf[...] + hmb[...]) * mln[...] * jax.nn.sigmoid(om[...].astype(F32))
    y_g = _head_rms(ogf[...] + ogb[...]) * gln[...] * _silu(rg[...].astype(F32))
    y = (jax.nn.sigmoid(mgm[...].astype(F32)) * _dot(y_m, wpm[...])
         + jax.nn.sigmoid(mgg[...].astype(F32)) * _dot(y_g, wpg[...]))
    x1 = x_ref[...] + gt1[...] * _dot(y, wo[...])
    x1_ref[...] = x1
    h2 = _rms(x1) * g2[...] * (1.0 + sc2[...]) + sh2[...]
    h2_ref[...] = h2.astype(MXU_DTYPE)

    hh, hm_, _ = _split3(h2)
    wh, wm_, _ = _split3(wr[...])
    lg = _dot(hh, wh) + _dot(hh, wm_) + _dot(hm_, wh) + br[...]
    lane = lax.broadcasted_iota(jnp.int32, lg.shape, 1)

    def masked_softmax(mask):
        l = jnp.where(mask, lg, -jnp.inf)
        e = jnp.exp(l - jnp.max(l, axis=-1, keepdims=True))
        return e / jnp.sum(e, axis=-1, keepdims=True)

    def top1(p, mask):
        pm = jnp.where(mask, p, -1.0)
        best = jnp.max(pm, axis=-1, keepdims=True)
        idx = jnp.min(jnp.where(jnp.logical_and(mask, pm == best), lane, LANES), axis=-1, keepdims=True)
        return best, idx

    gmask = lane < N_GROUPS
    grp_p, grp = top1(masked_softmax(gmask), gmask)
    e_lo = ROUTE_E0 + grp * EXPERTS_PER_GROUP
    emask = jnp.logical_and(lane >= e_lo, lane < e_lo + EXPERTS_PER_GROUP)
    p_in = masked_softmax(emask)
    p1, i1 = top1(p_in, emask)
    p2, i2 = top1(p_in, jnp.logical_and(emask, lane != i1))
    tot = p1 + p2
    wt_ref[...] = (jnp.where(lane == i1, grp_p * p1 / tot, 0.0)
                   + jnp.where(lane == i2, grp_p * p2 / tot, 0.0)
                   + jnp.where(lane == grp, 1.0, 0.0))


def _merge(hmf, hmb, ogf, ogb, main, x2, mln, gln, wpm, wpg, wo, gt1, g2, sc2, sh2, wr, br, *, tm, rows_per_batch):
    m = x2.shape[0]
    tpb = rows_per_batch // tm
    rowblk = pl.BlockSpec((tm, D_MODEL), lambda i: (i, 0))
    colblk = lambda cb: pl.BlockSpec((tm, D_MODEL), lambda i, cb=cb: (i, cb))
    vec = pl.BlockSpec((1, D_MODEL), lambda i: (0, 0))
    bvec = pl.BlockSpec((None, 1, D_MODEL), lambda i: (i // tpb, 0, 0))
    wmat = pl.BlockSpec((D_MODEL, D_MODEL), lambda i: (0, 0))
    return pl.pallas_call(
        _merge_kernel,
        grid=(m // tm,),
        in_specs=[rowblk, rowblk, rowblk, rowblk, colblk(COL_O_M), colblk(COL_R_G), colblk(COL_MG_M),
                  colblk(COL_MG_G), rowblk, vec, vec, wmat, wmat, wmat, bvec, vec, bvec, bvec,
                  pl.BlockSpec((D_MODEL, LANES), lambda i: (0, 0)), pl.BlockSpec((1, LANES), lambda i: (0, 0))],
        out_specs=[rowblk, rowblk, pl.BlockSpec((tm, LANES), lambda i: (i, 0))],
        out_shape=[jax.ShapeDtypeStruct((m, D_MODEL), F32), jax.ShapeDtypeStruct((m, D_MODEL), MXU_DTYPE),
                   jax.ShapeDtypeStruct((m, LANES), F32)],
        compiler_params=_cparams(("parallel",)),
        name="merge_route",
    )(hmf, hmb, ogf, ogb, main, main, main, main, x2, mln, gln, wpm, wpg, wo, gt1, g2, sc2, sh2, wr, br)


MOE_BLK = 128
MOE_COMMON_BLKS = (2, 3)
MOE_EXPERTS_PER_STEP = 2


def _moe_kernel(h2_ref, wt_ref, wup_ref, wdn_ref, x1_ref, gt2_ref, gf_ref, o_ref,
                xs_ref, ys_ref, ws_ref, dest_ref, blk_ref):
    step = pl.program_id(1)
    tm = h2_ref.shape[0]
    n_rows = xs_ref.shape[0]

    @pl.when(step == 0)
    def _():
        r = wt_ref[...]
        lane = lax.broadcasted_iota(jnp.int32, (tm, LANES), 1)
        lane1 = lax.broadcasted_iota(jnp.int32, (1, LANES), 1)
        gm = jnp.where(lane < N_GROUPS, r, 0.0)
        earlier = (lax.broadcasted_iota(jnp.int32, (tm, tm), 1)
                   < lax.broadcasted_iota(jnp.int32, (tm, tm), 0)).astype(MXU_DTYPE)
        before = _dot(earlier, gm)
        padded = jnp.floor((jnp.sum(gm, axis=0, keepdims=True) + (MOE_BLK - 1)) * (1.0 / MOE_BLK)) * MOE_BLK
        start = jnp.zeros((1, LANES), F32)
        run = jnp.zeros((1, 1), F32)
        for g in range(N_GROUPS):
            size = jnp.sum(jnp.where(lane1 == g, padded, 0.0), axis=-1, keepdims=True)
            start = jnp.where(lane1 == g, run, start)
            blk_ref[g] = (jnp.sum(run) * (1.0 / MOE_BLK)).astype(jnp.int32)
            blk_ref[N_GROUPS + g] = (jnp.sum(size) * (1.0 / MOE_BLK)).astype(jnp.int32)
            run = run + size
        dest = jnp.sum(gm * (start + before), axis=-1, keepdims=True)
        dest_ref[...] = jnp.broadcast_to(dest, (tm, LANES))
        dest_row = dest_ref[...].T[0:1, :].astype(jnp.int32)
        perm = (lax.broadcasted_iota(jnp.int32, (n_rows, tm), 0) == dest_row).astype(MXU_DTYPE)
        xs_ref[...] = _dot(perm, h2_ref[...]).astype(MXU_DTYPE)
        hi, mid, lo = _split3(r)
        ws_ref[...] = _dot(perm, hi) + _dot(perm, mid) + _dot(perm, lo)
        ys_ref[...] = jnp.zeros_like(ys_ref)

    group = step // (EXPERTS_PER_GROUP // MOE_EXPERTS_PER_STEP)
    first_blk = blk_ref[group]
    n_blk = blk_ref[N_GROUPS + group]

    def expert_on(r0, rows):
        x = xs_ref[pl.ds(r0, rows), :]
        ws = ws_ref[pl.ds(r0, rows), :]
        lane_b = lax.broadcasted_iota(jnp.int32, (rows, LANES), 1)
        total = jnp.zeros((rows, D_MODEL), F32)
        for k in range(MOE_EXPERTS_PER_STEP):
            gu = _dot(x, wup_ref[k])
            hidden = _silu(gu[:, :D_EXPERT]) * gu[:, D_EXPERT:]
            y = _dot(hidden, wdn_ref[k])
            lane_e = ROUTE_E0 + step * MOE_EXPERTS_PER_STEP + k
            total = total + y * jnp.sum(jnp.where(lane_b == lane_e, ws, 0.0), axis=-1, keepdims=True)
        ys_ref[pl.ds(r0, rows), :] += total

    for k in MOE_COMMON_BLKS:
        @pl.when(n_blk == k)
        def _():
            expert_on(pl.multiple_of(first_blk * MOE_BLK, MOE_BLK), k * MOE_BLK)

    @pl.when(functools.reduce(jnp.logical_and, [n_blk != k for k in MOE_COMMON_BLKS]))
    def _():
        def block(j, carry):
            expert_on(pl.multiple_of((first_blk + j) * MOE_BLK, MOE_BLK), MOE_BLK)
            return carry

        lax.fori_loop(0, n_blk, block, 0)

    @pl.when(step == pl.num_programs(1) - 1)
    def _():
        dest = dest_ref[...][:, :1].astype(jnp.int32)
        unperm = (lax.broadcasted_iota(jnp.int32, (tm, n_rows), 1) == dest).astype(MXU_DTYPE)
        ys = ys_ref[...]
        hi = ys.astype(MXU_DTYPE)
        lo = (ys - hi.astype(F32)).astype(MXU_DTYPE)
        y = _dot(unperm, hi) + _dot(unperm, lo)
        o_ref[...] = _rms(x1_ref[...] + gt2_ref[...] * y) * gf_ref[...]


def _moe_final(h2, wt, w_up, w_down, x1, gt2, g_final, *, tm, rows_per_batch):
    m = h2.shape[0]
    tpb = rows_per_batch // tm
    n_rows = tm + N_GROUPS * MOE_BLK
    rowblk = pl.BlockSpec((tm, D_MODEL), lambda i, e: (i, 0))
    return pl.pallas_call(
        _moe_kernel,
        grid=(m // tm, N_EXPERTS // MOE_EXPERTS_PER_STEP),
        in_specs=[rowblk,
                  pl.BlockSpec((tm, LANES), lambda i, e: (i, 0)),
                  pl.BlockSpec((MOE_EXPERTS_PER_STEP, D_MODEL, 2 * D_EXPERT), lambda i, e: (e, 0, 0)),
                  pl.BlockSpec((MOE_EXPERTS_PER_STEP, D_EXPERT, D_MODEL), lambda i, e: (e, 0, 0)),
                  rowblk,
                  pl.BlockSpec((None, 1, D_MODEL), lambda i, e: (i // tpb, 0, 0)),
                  pl.BlockSpec((1, D_MODEL), lambda i, e: (0, 0))],
        out_specs=rowblk,
        out_shape=jax.ShapeDtypeStruct((m, D_MODEL), F32),
        scratch_shapes=[pltpu.VMEM((n_rows, D_MODEL), MXU_DTYPE), pltpu.VMEM((n_rows, D_MODEL), F32),
                        pltpu.VMEM((n_rows, LANES), F32), pltpu.VMEM((tm, LANES), F32),
                        pltpu.SMEM((2 * N_GROUPS,), jnp.int32)],
        compiler_params=_cparams(("parallel", "arbitrary"), 56 * 1024 * 1024),
        name="moe_final",
    )(h2, wt, w_up, w_down, x1, gt2, g_final)


def _empty_states(batch):
    ml = (jnp.zeros((batch, 2, HEADS, QK, DV + LANES), F32),
          jnp.full((batch, 2, HEADS, 1, LANES), NEG_BIG, F32))
    gla = jnp.zeros((batch, 2, HEADS, DV, QK), F32)
    return ml, gla


def kernel(x, c, ctx, c_ctx, w_mod, b_mod, g_norm1, w_in, ml_conv, ml_conv_b, b_mgate, ml_norm, gla_w2, gla_b2,
           gla_norm, w_proj_m, w_proj_g, w_out, g_norm2, w_grp, b_grp, w_rexp, b_rexp, w_up, w_down, g_final):
    batch, t, d = x.shape
    t_ctx = ctx.shape[1]
    assert d == D_MODEL and w_mod.shape[0] == 1 and w_in.shape[2] == sum(IN_SIZES)
    assert t % (GRID_W * 16) == 0 and t % CHUNK == 0 and t_ctx % CHUNK == 0

    off = [0]
    for s in IN_SIZES:
        off.append(off[-1] + s)
    wi = w_in[0]
    w_main = jnp.concatenate([wi[:, off[0]:off[4]], wi[:, off[5]:off[9]], wi[:, off[10]:off[12]]], axis=1)
    w_small = jnp.concatenate([wi[:, off[9]:off[10]], wi[:, off[4]:off[5]],
                               jnp.zeros((d, LANES - 2 * GLA_RANK - 4 * HEADS), F32)], axis=1)
    w_main = w_main.astype(MXU_DTYPE)
    w_small = w_small.astype(MXU_DTYPE)
    gate_bias = jnp.zeros((1, LANES), F32).at[0, SMALL_GATE0:SMALL_GATE0 + 4 * HEADS].set(b_mgate[0])
    w2p = jnp.zeros((2, LANES, QK_W), F32)
    w2p = w2p.at[0, 0:GLA_RANK].set(gla_w2[0, 0]).at[1, GLA_RANK:2 * GLA_RANK].set(gla_w2[0, 1])
    b2 = gla_b2[0][:, None, :]
    conv_w = ml_conv[0].reshape(9, 2 * QK_W)
    conv_b = ml_conv_b[0][None, :]
    w_route = jnp.concatenate([w_grp[0], w_rexp[0], jnp.zeros((d, LANES - N_GROUPS - N_EXPERTS), F32)], axis=1)
    b_route = jnp.concatenate([b_grp[0], b_rexp[0], jnp.zeros((LANES - N_GROUPS - N_EXPERTS,), F32)])[None, :]

    cc = jnp.concatenate([c, c_ctx[None, :], jnp.zeros((8 - batch - 1, d), F32)], axis=0)
    mod = _modulation(cc, w_mod[0], b_mod[0][None, :])
    sh1, sc1, gt1, sh2, sc2, gt2 = [mod[:batch, i * d:(i + 1) * d][:, None, :] for i in range(6)]
    sh1c, sc1c = [jnp.broadcast_to(mod[batch:batch + 1, i * d:(i + 1) * d][:, None, :], (batch, 1, d)) for i in range(2)]
    g1 = g_norm1[0][None, :]

    main_c, small_c = _in_proj(ctx.reshape(batch * t_ctx, d), g1, sc1c, sh1c, w_main, w_small,
                               tm=t_ctx, rows_per_batch=t_ctx)
    qk_c = _conv_silu(main_c, conv_w, conv_b, batch=batch, rows=1, cols=t_ctx)
    ml0, gla0 = _empty_states(batch)
    main_c3, small_c3 = main_c.reshape(batch, t_ctx, MAIN_W), small_c.reshape(batch, t_ctx, LANES)
    ml_states = _mlstm_scan(qk_c.reshape(batch, t_ctx, 2 * QK_W), main_c3, small_c3, gate_bias, ml0, with_output=False)
    gla_state = _gla_scan(main_c3, small_c3, w2p, b2, gla0, with_output=False)

    x2 = x.reshape(batch * t, d)
    main, small = _in_proj(x2, g1, sc1, sh1, w_main, w_small, tm=1024, rows_per_batch=t)
    qk = _conv_silu(main, conv_w, conv_b, batch=batch, rows=t // GRID_W, cols=GRID_W)
    main3, small3 = main.reshape(batch, t, MAIN_W), small.reshape(batch, t, LANES)
    hm_f, hm_b = [a.reshape(batch * t, V_W) for a in
                  _mlstm_scan(qk.reshape(batch, t, 2 * QK_W), main3, small3, gate_bias, ml_states, with_output=True)]
    og_f, og_b = [a.reshape(batch * t, V_W) for a in _gla_scan(main3, small3, w2p, b2, gla_state, with_output=True)]

    x1, h2, wt = _merge(hm_f, hm_b, og_f, og_b, main, x2, ml_norm, gla_norm,
                        w_proj_m[0].astype(MXU_DTYPE), w_proj_g[0].astype(MXU_DTYPE), w_out[0].astype(MXU_DTYPE),
                        gt1, g_norm2, sc2, sh2, w_route, b_route, tm=256, rows_per_batch=t)
    out = _moe_final(h2, wt, w_up[0].astype(MXU_DTYPE), w_down[0].astype(MXU_DTYPE), x1, gt2, g_final[None, :],
                     tm=1024, rows_per_batch=t)
    return out.reshape(batch, t, d)
```

```python
import functools

import jax
import jax.numpy as jnp
from jax import lax
from jax.experimental import pallas as pl
from jax.experimental.pallas import tpu as pltpu

D_MODEL = 1024
GRID_W = 64
CHUNK = 256
EPS = 1e-6
NEG_BIG = -1e30
HEADS = 4
QK = D_MODEL // 8
DV = D_MODEL // 4
QK_W = HEADS * QK
V_W = HEADS * DV
GLA_RANK = 16
GLA_TAU = 16.0
N_GROUPS = 4
EXPERTS_PER_GROUP = 4
N_EXPERTS = N_GROUPS * EXPERTS_PER_GROUP
D_EXPERT = D_MODEL // 2
IN_SIZES = (QK_W, QK_W, V_W, V_W, 4 * HEADS, QK_W, QK_W, V_W, V_W, 2 * GLA_RANK, D_MODEL, D_MODEL)

LANES = 128
MXU_DTYPE = jnp.bfloat16
F32 = jnp.float32
VMEM_LIMIT = 48 * 1024 * 1024

COL_QK_M, COL_V_M, COL_O_M, COL_QK_G, COL_V_G, COL_R_G, COL_MG_M, COL_MG_G = range(8)
MAIN_W = 8 * D_MODEL
SMALL_GATE0 = 2 * GLA_RANK
GLA_SAFE_DECAY = 80.0
ROUTE_E0 = N_GROUPS


def _dot(a, b):
    return jnp.dot(a.astype(MXU_DTYPE), b.astype(MXU_DTYPE), preferred_element_type=F32)


def _dot_nt(a, b):
    return lax.dot_general(a.astype(MXU_DTYPE), b.astype(MXU_DTYPE), (((1,), (1,)), ((), ())),
                           preferred_element_type=F32)


def _transpose_mxu(a):
    m = a.shape[1]
    eye = (lax.broadcasted_iota(jnp.int32, (m, m), 0) == lax.broadcasted_iota(jnp.int32, (m, m), 1))
    return _dot_nt(eye.astype(MXU_DTYPE), a).astype(MXU_DTYPE)


def _dot_tn_xlu(a, b):
    return lax.dot_general(a.astype(MXU_DTYPE), b.astype(MXU_DTYPE), (((0,), (0,)), ((), ())),
                           preferred_element_type=F32)


def _split3(x):
    hi = x.astype(MXU_DTYPE)
    r1 = x - hi.astype(F32)
    mid = r1.astype(MXU_DTYPE)
    lo = (r1 - mid.astype(F32)).astype(MXU_DTYPE)
    return hi, mid, lo


def _dot_exact_lhs(a01, x):
    hi, mid, lo = _split3(x)
    return _dot(a01, hi) + _dot(a01, mid) + _dot(a01, lo)


def _log_sigmoid(x):
    return jnp.minimum(x, 0.0) - jnp.log1p(jnp.exp(-jnp.abs(x)))


def _silu(x):
    return x * jax.nn.sigmoid(x)


def _rms(x):
    return x * lax.rsqrt(jnp.mean(x * x, axis=-1, keepdims=True) + EPS)


def _cparams(sem, vmem_limit=VMEM_LIMIT):
    return pltpu.CompilerParams(dimension_semantics=sem, vmem_limit_bytes=vmem_limit)


def _mod_kernel(c_ref, w_ref, b_ref, o_ref):
    o_ref[...] = _dot(_silu(c_ref[...]), w_ref[...]) + b_ref[...]


def _modulation(cc, w_mod, b_mod):
    n = w_mod.shape[1]
    tn = 512
    return pl.pallas_call(
        _mod_kernel,
        grid=(n // tn,),
        in_specs=[pl.BlockSpec((8, D_MODEL), lambda j: (0, 0)),
                  pl.BlockSpec((D_MODEL, tn), lambda j: (0, j)),
                  pl.BlockSpec((1, tn), lambda j: (0, j))],
        out_specs=pl.BlockSpec((8, tn), lambda j: (0, j)),
        out_shape=jax.ShapeDtypeStruct((8, n), F32),
        compiler_params=_cparams(("arbitrary",)),
        name="modulation",
    )(cc, w_mod, b_mod)


def _inproj_kernel(x_ref, g_ref, sc_ref, sh_ref, w_ref, ws_ref, o_ref, os_ref, xn_ref):
    @pl.when(pl.program_id(1) == 0)
    def _():
        xn = _rms(x_ref[...]) * g_ref[...] * (1.0 + sc_ref[...]) + sh_ref[...]
        xn_ref[...] = xn.astype(MXU_DTYPE)
        os_ref[...] = _dot(xn_ref[...], ws_ref[...])

    o_ref[...] = _dot(xn_ref[...], w_ref[...]).astype(o_ref.dtype)


def _in_proj(x2, g, sc, sh, w_main, w_small, *, tm, rows_per_batch):
    m = x2.shape[0]
    tn = 2048
    tiles_per_batch = rows_per_batch // tm
    vec = pl.BlockSpec((None, 1, D_MODEL), lambda i, j: (i // tiles_per_batch, 0, 0))
    return pl.pallas_call(
        _inproj_kernel,
        grid=(m // tm, MAIN_W // tn),
        in_specs=[pl.BlockSpec((tm, D_MODEL), lambda i, j: (i, 0)),
                  pl.BlockSpec((1, D_MODEL), lambda i, j: (0, 0)),
                  vec, vec,
                  pl.BlockSpec((D_MODEL, tn), lambda i, j: (0, j)),
                  pl.BlockSpec((D_MODEL, LANES), lambda i, j: (0, 0))],
        out_specs=[pl.BlockSpec((tm, tn), lambda i, j: (i, j)),
                   pl.BlockSpec((tm, LANES), lambda i, j: (i, 0))],
        out_shape=[jax.ShapeDtypeStruct((m, MAIN_W), MXU_DTYPE), jax.ShapeDtypeStruct((m, LANES), F32)],
        scratch_shapes=[pltpu.VMEM((tm, D_MODEL), MXU_DTYPE)],
        compiler_params=_cparams(("parallel", "arbitrary")),
        name="in_proj",
    )(x2, g, sc, sh, w_main, w_small)


def _conv_kernel(x_ref, w_ref, b_ref, o_ref, *, rows, cols):
    scale = jnp.where(pl.program_id(1) * LANES >= QK_W, QK ** -0.5, 1.0).astype(F32)
    w = w_ref[...]
    bias = b_ref[...]
    tpos = lax.broadcasted_iota(jnp.int32, (cols, 1), 0)
    has_left = tpos >= 1
    has_right = tpos < cols - 1

    def row_filters(j):
        tile = x_ref[pl.ds(pl.multiple_of(j * cols, cols), cols), :].astype(F32)
        left = jnp.where(has_left, pltpu.roll(tile, 1, axis=0), 0.0)
        right = jnp.where(has_right, pltpu.roll(tile, cols - 1, axis=0), 0.0)
        return [left * w[3 * i:3 * i + 1, :] + tile * w[3 * i + 1:3 * i + 2, :] + right * w[3 * i + 2:3 * i + 3, :]
                for i in range(3)]

    def finish(j, acc):
        o_ref[pl.ds(pl.multiple_of(j * cols, cols), cols), :] = (_silu(acc + bias) * scale).astype(o_ref.dtype)

    first = row_filters(0)

    def body(j, carry):
        acc, below = carry
        h = row_filters(j)
        finish(j - 1, acc + h[2])
        return below + h[1], h[0]

    acc, _ = lax.fori_loop(1, rows, body, (first[1], first[0]))
    finish(rows - 1, acc)


def _conv_silu(main, conv_w, conv_b, *, batch, rows, cols):
    t = rows * cols
    nct = 2 * QK_W // LANES
    return pl.pallas_call(
        functools.partial(_conv_kernel, rows=rows, cols=cols),
        grid=(batch, nct),
        in_specs=[pl.BlockSpec((t, LANES), lambda b, c: (b, c)),
                  pl.BlockSpec((9, LANES), lambda b, c: (0, c)),
                  pl.BlockSpec((1, LANES), lambda b, c: (0, c))],
        out_specs=pl.BlockSpec((t, LANES), lambda b, c: (b, c)),
        out_shape=jax.ShapeDtypeStruct((batch * t, 2 * QK_W), MXU_DTYPE),
        compiler_params=_cparams(("parallel", "arbitrary")),
        name="conv_silu",
    )(main, conv_w, conv_b)


def _chunk_masks(direction):
    row = lax.broadcasted_iota(jnp.int32, (CHUNK, CHUNK), 0)
    col = lax.broadcasted_iota(jnp.int32, (CHUNK, CHUNK), 1)
    seen = (row >= col) if direction == 0 else (row <= col)
    return seen, seen.astype(MXU_DTYPE)


def _scan_specs(batch, nc, col_blocks, widths):
    specs = []
    for direction in (0, 1):
        for cb, wd in zip(col_blocks, widths):
            if direction == 0:
                specs.append(pl.BlockSpec((batch, CHUNK, wd), lambda c, cb=cb: (0, c, cb)))
            else:
                specs.append(pl.BlockSpec((batch, CHUNK, wd), lambda c, cb=cb: (0, nc - 1 - c, cb)))
    return specs


def _scan_out_specs(batch, nc):
    return [pl.BlockSpec((batch, CHUNK, V_W), lambda c: (0, c, 0)),
            pl.BlockSpec((batch, CHUNK, V_W), lambda c: (0, nc - 1 - c, 0))]


def _whole(shape):
    nd = len(shape)
    return pl.BlockSpec(tuple(shape), lambda c: (0,) * nd)


def _lane_tile(x, width):
    return jnp.concatenate([x] * (width // LANES), axis=-1)


def _mlstm_kernel(*refs, with_output, batch):
    (qk_f, v_f, sm_f, qk_b, v_b, sm_b, bias_ref, cn0_ref, m0_ref) = refs[:9]
    if with_output:
        hf_ref, hb_ref, cn_s, m_s = refs[9:]
    else:
        cn_out, m_out, cn_s, m_s = refs[9:]
    step = pl.program_id(0)

    @pl.when(step == 0)
    def _():
        cn_s[...] = cn0_ref[...]
        m_s[...] = m0_ref[...]

    lane = lax.broadcasted_iota(jnp.int32, (1, LANES), 1)
    gate_lane = jnp.logical_and(lane >= SMALL_GATE0, lane < SMALL_GATE0 + 4 * HEADS)
    forget_lane = jnp.logical_and(gate_lane, ((lane - SMALL_GATE0) % (2 * HEADS)) >= HEADS)
    eye = (lax.broadcasted_iota(jnp.int32, (LANES, LANES), 0)
           == lax.broadcasted_iota(jnp.int32, (LANES, LANES), 1)).astype(MXU_DTYPE)
    ones_cols = jnp.ones((CHUNK, LANES), MXU_DTYPE)

    for direction, (qk_ref, v_ref, sm_ref) in enumerate(((qk_f, v_f, sm_f), (qk_b, v_b, sm_b))):
        seen, seen01 = _chunk_masks(direction)
        last = CHUNK - 1 if direction == 0 else 0
        for bi in range(batch):
            g = sm_ref[bi] + bias_ref[...]
            gp = jnp.where(forget_lane, _log_sigmoid(g), g)
            bc = _dot_exact_lhs(seen01, gp)
            hi, mid, lo = _split3(gp)
            gp_t = _dot_nt(eye, hi) + _dot_nt(eye, mid) + _dot_nt(eye, lo)
            hi, mid, lo = _split3(bc)
            bc_t = _dot_nt(eye, hi) + _dot_nt(eye, mid) + _dot_nt(eye, lo)
            bend_row = bc[last:last + 1, :]
            heads = []
            for h in range(HEADS):
                ji = SMALL_GATE0 + direction * 2 * HEADS + h
                jf = ji + HEADS
                c = dict(ji=ji, jf=jf)
                c["q"] = qk_ref[bi, :, h * QK:(h + 1) * QK].astype(F32)
                c["k"] = qk_ref[bi, :, QK_W + h * QK:QK_W + (h + 1) * QK].astype(F32)
                c["v_ext"] = jnp.concatenate([v_ref[bi, :, h * DV:(h + 1) * DV].astype(MXU_DTYPE), ones_cols], axis=-1)
                c["cn_old"] = cn_s[bi, direction, h]
                c["m_old"] = m_s[bi, direction, h]
                heads.append(c)
            if with_output:
                for c in heads:
                    c["qk"] = _dot_nt(c["q"], c["k"])
                    c["qc"] = _dot(c["q"], c["cn_old"])
            for c in heads:
                b_end = jnp.broadcast_to(bend_row[:, c["jf"]:c["jf"] + 1], (1, LANES))
                i_col = jnp.broadcast_to(gp[:, c["ji"]:c["ji"] + 1], (CHUNK, LANES))
                c["b_col"] = jnp.broadcast_to(bc[:, c["jf"]:c["jf"] + 1], (CHUNK, LANES))
                log_w = b_end - c["b_col"] + i_col
                c["m_new"] = jnp.maximum(b_end + c["m_old"], jnp.max(log_w, axis=0, keepdims=True))
                c["kw"] = (c["k"] * jnp.exp(log_w - c["m_new"])).astype(MXU_DTYPE)
                c["decay"] = jnp.exp(b_end + c["m_old"] - c["m_new"])
            for c in heads:
                c["kw_t"] = _transpose_mxu(c["kw"])
            if with_output:
                for c in heads:
                    i_row = gp_t[c["ji"]:c["ji"] + 1, :]
                    b_row = bc_t[c["jf"]:c["jf"] + 1, :]
                    log_d = jnp.where(seen, _lane_tile(c["b_col"], CHUNK) - b_row + i_row, -jnp.inf)
                    log_inter = c["b_col"] + c["m_old"]
                    c["m_t"] = jnp.maximum(log_inter, jnp.max(log_d, axis=-1, keepdims=True))
                    c["s"] = (c["qk"] * jnp.exp(log_d - _lane_tile(c["m_t"], CHUNK))).astype(MXU_DTYPE)
                    c["w_inter"] = jnp.exp(log_inter - c["m_t"])
                for c in heads:
                    c["sv"] = _dot(c["s"], c["v_ext"])
            for h, c in enumerate(heads):
                cn_s[bi, direction, h] = _lane_tile(c["decay"], DV + LANES) * c["cn_old"] + _dot(c["kw_t"], c["v_ext"])
                m_s[bi, direction, h] = c["m_new"]
            outs = []
            if with_output:
                for c in heads:
                    sv, qc, w_inter = c["sv"], c["qc"], c["w_inter"]
                    num = sv[:, :DV] + _lane_tile(w_inter, DV) * qc[:, :DV]
                    den = jnp.abs(sv[:, DV:] + w_inter * qc[:, DV:])
                    outs.append(num / _lane_tile(jnp.maximum(den, jnp.exp(-c["m_t"])), DV))
            if with_output:
                (hf_ref if direction == 0 else hb_ref)[bi] = jnp.concatenate(outs, axis=-1)

    if not with_output:
        @pl.when(step == pl.num_programs(0) - 1)
        def _():
            cn_out[...] = cn_s[...]
            m_out[...] = m_s[...]


def _mlstm_scan(qk, main, small, gate_bias, states, *, with_output):
    cn0, m0 = states
    batch, t, _ = qk.shape
    nc = t // CHUNK
    in_specs = _scan_specs(batch, nc, (0, COL_V_M, 0), (2 * QK_W, V_W, LANES))
    in_specs += [_whole(gate_bias.shape), _whole(cn0.shape), _whole(m0.shape)]
    if with_output:
        out_specs = _scan_out_specs(batch, nc)
        out_shape = [jax.ShapeDtypeStruct((batch, t, V_W), F32)] * 2
    else:
        out_specs = [_whole(cn0.shape), _whole(m0.shape)]
        out_shape = [jax.ShapeDtypeStruct(s.shape, F32) for s in states]
    return pl.pallas_call(
        functools.partial(_mlstm_kernel, with_output=with_output, batch=batch),
        grid=(nc,),
        in_specs=in_specs,
        out_specs=out_specs,
        out_shape=out_shape,
        scratch_shapes=[pltpu.VMEM(cn0.shape, F32), pltpu.VMEM(m0.shape, F32)],
        compiler_params=_cparams(("arbitrary",)),
        name="mlstm_scan_out" if with_output else "mlstm_scan_state",
    )(qk, main, small, qk, main, small, gate_bias, cn0, m0)


def _gla_exact_intra(q, k, v, b, direction):
    row_id = lax.broadcasted_iota(jnp.int32, (CHUNK, 1), 0)

    def row(t, acc):
        pick = row_id == t
        b_t = jnp.sum(jnp.where(pick, b, 0.0), axis=0, keepdims=True)
        q_t = jnp.sum(jnp.where(pick, q, 0.0), axis=0, keepdims=True)
        ok = (row_id <= t) if direction == 0 else (row_id >= t)
        e = jnp.exp(jnp.where(ok, b_t - b, -jnp.inf))
        sc = jnp.sum(q_t * k * e, axis=-1, keepdims=True)
        o_t = jnp.sum(sc * v, axis=0, keepdims=True)
        return jnp.where(pick, o_t, acc)

    return lax.fori_loop(0, CHUNK, row, jnp.zeros((CHUNK, DV), F32))


def _gla_kernel(*refs, with_output, batch):
    (qk_f, v_f, sm_f, qk_b, v_b, sm_b, w2_ref, b2_ref, s0_ref) = refs[:9]
    if with_output:
        of_ref, ob_ref, s_s, b_s, inter_s = refs[9:]
    else:
        s_out, s_s = refs[9:]
    step = pl.program_id(0)

    @pl.when(step == 0)
    def _():
        s_s[...] = s0_ref[...]

    worst_decay = []
    for direction, (qk_ref, v_ref, sm_ref) in enumerate(((qk_f, v_f, sm_f), (qk_b, v_b, sm_b))):
        seen, seen01 = _chunk_masks(direction)
        last = CHUNK - 1 if direction == 0 else 0
        for bi in range(batch):
            z = _dot(sm_ref[bi], w2_ref[direction]) + b2_ref[direction]
            log_a = _log_sigmoid(z) * (1.0 / GLA_TAU)
            b_all = _dot_exact_lhs(seen01, log_a)
            outs, inters = [], []
            for h in range(HEADS):
                q = qk_ref[bi, :, h * QK:(h + 1) * QK].astype(F32) * (QK ** -0.5)
                k = qk_ref[bi, :, QK_W + h * QK:QK_W + (h + 1) * QK].astype(F32)
                v = v_ref[bi, :, h * DV:(h + 1) * DV]
                b = b_all[:, h * QK:(h + 1) * QK]
                b_end = b[last:last + 1, :]
                st_old = s_s[bi, direction, h]
                k_dec = k * jnp.exp(b_end - b)
                s_s[bi, direction, h] = st_old * jnp.exp(b_end) + _dot_tn_xlu(v, k_dec)
                if with_output:
                    q_dec = q * jnp.exp(b)
                    inter = _dot_nt(q_dec, st_old)
                    scores = jnp.where(seen, _dot_nt(q_dec, k * jnp.exp(-b)), 0.0)
                    outs.append(_dot(scores, v) + inter)
                    inters.append(inter)
            if with_output:
                (of_ref if direction == 0 else ob_ref)[bi] = jnp.concatenate(outs, axis=-1)
                b_s[bi, direction] = b_all
                inter_s[bi, direction] = jnp.concatenate(inters, axis=-1)
                worst_decay.append(jnp.max(-b_all[last:last + 1, :]))

    if with_output:
        @pl.when(functools.reduce(jnp.maximum, worst_decay) > GLA_SAFE_DECAY)
        def _():
            for direction, (qk_ref, v_ref, o_ref) in enumerate(((qk_f, v_f, of_ref), (qk_b, v_b, ob_ref))):
                last = CHUNK - 1 if direction == 0 else 0
                for bi in range(batch):
                    for h in range(HEADS):
                        b = b_s[bi, direction, :, h * QK:(h + 1) * QK]

                        @pl.when(jnp.max(-b[last:last + 1, :]) > GLA_SAFE_DECAY)
                        def _():
                            q = qk_ref[bi, :, h * QK:(h + 1) * QK].astype(F32) * (QK ** -0.5)
                            k = qk_ref[bi, :, QK_W + h * QK:QK_W + (h + 1) * QK].astype(F32)
                            v = v_ref[bi, :, h * DV:(h + 1) * DV].astype(F32)
                            o_ref[bi, :, h * DV:(h + 1) * DV] = (
                                inter_s[bi, direction, :, h * DV:(h + 1) * DV]
                                + _gla_exact_intra(q, k, v, b, direction))
    else:
        @pl.when(step == pl.num_programs(0) - 1)
        def _():
            s_out[...] = s_s[...]


def _gla_scan(main, small, w2p, b2, s0, *, with_output):
    batch, t, _ = main.shape
    nc = t // CHUNK
    in_specs = _scan_specs(batch, nc, (COL_QK_G, COL_V_G, 0), (2 * QK_W, V_W, LANES))
    in_specs += [_whole(w2p.shape), _whole(b2.shape), _whole(s0.shape)]
    scratch = [pltpu.VMEM(s0.shape, F32)]
    if with_output:
        out_specs = _scan_out_specs(batch, nc)
        out_shape = [jax.ShapeDtypeStruct((batch, t, V_W), F32)] * 2
        scratch += [pltpu.VMEM((batch, 2, CHUNK, QK_W), F32), pltpu.VMEM((batch, 2, CHUNK, V_W), F32)]
    else:
        out_specs = _whole(s0.shape)
        out_shape = jax.ShapeDtypeStruct(s0.shape, F32)
    return pl.pallas_call(
        functools.partial(_gla_kernel, with_output=with_output, batch=batch),
        grid=(nc,),
        in_specs=in_specs,
        out_specs=out_specs,
        out_shape=out_shape,
        scratch_shapes=scratch,
        compiler_params=_cparams(("arbitrary",)),
        name="gla_scan_out" if with_output else "gla_scan_state",
    )(main, main, small, main, main, small, w2p, b2, s0)


def _head_rms(a):
    return jnp.concatenate([_rms(a[:, h * DV:(h + 1) * DV]) for h in range(HEADS)], axis=-1)


def _merge_kernel(hmf, hmb, ogf, ogb, om, rg, mgm, mgg, x_ref, mln, gln, wpm, wpg, wo, gt1, g2, sc2, sh2,
                  wr, br, x1_ref, h2_ref, wt_ref):
    y_m = _head_rms(hmf[...] + hmb[...]) * mln[...] * jax.nn.sigmoid(om[...].astype(F32))
    y_g = _head_rms(ogf[...] + ogb[...]) * gln[...] * _silu(rg[...].astype(F32))
    y = (jax.nn.sigmoid(mgm[...].astype(F32)) * _dot(y_m, wpm[...])
         + jax.nn.sigmoid(mgg[...].astype(F32)) * _dot(y_g, wpg[...]))
    x1 = x_ref[...] + gt1[...] * _dot(y, wo[...])
    x1_ref[...] = x1
    h2 = _rms(x1) * g2[...] * (1.0 + sc2[...]) + sh2[...]
    h2_ref[...] = h2.astype(MXU_DTYPE)

    hh, hm_, _ = _split3(h2)
    wh, wm_, _ = _split3(wr[...])
    lg = _dot(hh, wh) + _dot(hh, wm_) + _dot(hm_, wh) + br[...]
    lane = lax.broadcasted_iota(jnp.int32, lg.shape, 1)

    def masked_softmax(mask):
        l = jnp.where(mask, lg, -jnp.inf)
        e = jnp.exp(l - jnp.max(l, axis=-1, keepdims=True))
        return e / jnp.sum(e, axis=-1, keepdims=True)

    def top1(p, mask):
        pm = jnp.where(mask, p, -1.0)
        best = jnp.max(pm, axis=-1, keepdims=True)
        idx = jnp.min(jnp.where(jnp.logical_and(mask, pm == best), lane, LANES), axis=-1, keepdims=True)
        return best, idx

    gmask = lane < N_GROUPS
    grp_p, grp = top1(masked_softmax(gmask), gmask)
    e_lo = ROUTE_E0 + grp * EXPERTS_PER_GROUP
    emask = jnp.logical_and(lane >= e_lo, lane < e_lo + EXPERTS_PER_GROUP)
    p_in = masked_softmax(emask)
    p1, i1 = top1(p_in, emask)
    p2, i2 = top1(p_in, jnp.logical_and(emask, lane != i1))
    tot = p1 + p2
    wt_ref[...] = (jnp.where(lane == i1, grp_p * p1 / tot, 0.0)
                   + jnp.where(lane == i2, grp_p * p2 / tot, 0.0)
                   + jnp.where(lane == grp, 1.0, 0.0))


def _merge(hmf, hmb, ogf, ogb, main, x2, mln, gln, wpm, wpg, wo, gt1, g2, sc2, sh2, wr, br, *, tm, rows_per_batch):
    m = x2.shape[0]
    tpb = rows_per_batch // tm
    rowblk = pl.BlockSpec((tm, D_MODEL), lambda i: (i, 0))
    colblk = lambda cb: pl.BlockSpec((tm, D_MODEL), lambda i, cb=cb: (i, cb))
    vec = pl.BlockSpec((1, D_MODEL), lambda i: (0, 0))
    bvec = pl.BlockSpec((None, 1, D_MODEL), lambda i: (i // tpb, 0, 0))
    wmat = pl.BlockSpec((D_MODEL, D_MODEL), lambda i: (0, 0))
    return pl.pallas_call(
        _merge_kernel,
        grid=(m // tm,),
        in_specs=[rowblk, rowblk, rowblk, rowblk, colblk(COL_O_M), colblk(COL_R_G), colblk(COL_MG_M),
                  colblk(COL_MG_G), rowblk, vec, vec, wmat, wmat, wmat, bvec, vec, bvec, bvec,
                  pl.BlockSpec((D_MODEL, LANES), lambda i: (0, 0)), pl.BlockSpec((1, LANES), lambda i: (0, 0))],
        out_specs=[rowblk, rowblk, pl.BlockSpec((tm, LANES), lambda i: (i, 0))],
        out_shape=[jax.ShapeDtypeStruct((m, D_MODEL), F32), jax.ShapeDtypeStruct((m, D_MODEL), MXU_DTYPE),
                   jax.ShapeDtypeStruct((m, LANES), F32)],
        compiler_params=_cparams(("parallel",)),
        name="merge_route",
    )(hmf, hmb, ogf, ogb, main, main, main, main, x2, mln, gln, wpm, wpg, wo, gt1, g2, sc2, sh2, wr, br)


MOE_BLK = 128
MOE_COMMON_BLKS = (2, 3)
MOE_EXPERTS_PER_STEP = 2


def _moe_kernel(h2_ref, wt_ref, wup_ref, wdn_ref, x1_ref, gt2_ref, gf_ref, o_ref,
                xs_ref, ys_ref, ws_ref, dest_ref, blk_ref):
    step = pl.program_id(1)
    tm = h2_ref.shape[0]
    n_rows = xs_ref.shape[0]

    @pl.when(step == 0)
    def _():
        r = wt_ref[...]
        lane = lax.broadcasted_iota(jnp.int32, (tm, LANES), 1)
        lane1 = lax.broadcasted_iota(jnp.int32, (1, LANES), 1)
        gm = jnp.where(lane < N_GROUPS, r, 0.0)
        earlier = (lax.broadcasted_iota(jnp.int32, (tm, tm), 1)
                   < lax.broadcasted_iota(jnp.int32, (tm, tm), 0)).astype(MXU_DTYPE)
        before = _dot(earlier, gm)
        padded = jnp.floor((jnp.sum(gm, axis=0, keepdims=True) + (MOE_BLK - 1)) * (1.0 / MOE_BLK)) * MOE_BLK
        start = jnp.zeros((1, LANES), F32)
        run = jnp.zeros((1, 1), F32)
        for g in range(N_GROUPS):
            size = jnp.sum(jnp.where(lane1 == g, padded, 0.0), axis=-1, keepdims=True)
            start = jnp.where(lane1 == g, run, start)
            blk_ref[g] = (jnp.sum(run) * (1.0 / MOE_BLK)).astype(jnp.int32)
            blk_ref[N_GROUPS + g] = (jnp.sum(size) * (1.0 / MOE_BLK)).astype(jnp.int32)
            run = run + size
        dest = jnp.sum(gm * (start + before), axis=-1, keepdims=True)
        dest_ref[...] = jnp.broadcast_to(dest, (tm, LANES))
        dest_row = dest_ref[...].T[0:1, :].astype(jnp.int32)
        perm = (lax.broadcasted_iota(jnp.int32, (n_rows, tm), 0) == dest_row).astype(MXU_DTYPE)
        xs_ref[...] = _dot(perm, h2_ref[...]).astype(MXU_DTYPE)
        hi, mid, lo = _split3(r)
        ws_ref[...] = _dot(perm, hi) + _dot(perm, mid) + _dot(perm, lo)
        ys_ref[...] = jnp.zeros_like(ys_ref)

    group = step // (EXPERTS_PER_GROUP // MOE_EXPERTS_PER_STEP)
    first_blk = blk_ref[group]
    n_blk = blk_ref[N_GROUPS + group]

    def expert_on(r0, rows):
        x = xs_ref[pl.ds(r0, rows), :]
        ws = ws_ref[pl.ds(r0, rows), :]
        lane_b = lax.broadcasted_iota(jnp.int32, (rows, LANES), 1)
        total = jnp.zeros((rows, D_MODEL), F32)
        for k in range(MOE_EXPERTS_PER_STEP):
            gu = _dot(x, wup_ref[k])
            hidden = _silu(gu[:, :D_EXPERT]) * gu[:, D_EXPERT:]
            y = _dot(hidden, wdn_ref[k])
            lane_e = ROUTE_E0 + step * MOE_EXPERTS_PER_STEP + k
            total = total + y * jnp.sum(jnp.where(lane_b == lane_e, ws, 0.0), axis=-1, keepdims=True)
        ys_ref[pl.ds(r0, rows), :] += total

    for k in MOE_COMMON_BLKS:
        @pl.when(n_blk == k)
        def _():
            expert_on(pl.multiple_of(first_blk * MOE_BLK, MOE_BLK), k * MOE_BLK)

    @pl.when(functools.reduce(jnp.logical_and, [n_blk != k for k in MOE_COMMON_BLKS]))
    def _():
        def block(j, carry):
            expert_on(pl.multiple_of((first_blk + j) * MOE_BLK, MOE_BLK), MOE_BLK)
            return carry

        lax.fori_loop(0, n_blk, block, 0)

    @pl.when(step == pl.num_programs(1) - 1)
    def _():
        dest = dest_ref[...][:, :1].astype(jnp.int32)
        unperm = (lax.broadcasted_iota(jnp.int32, (tm, n_rows), 1) == dest).astype(MXU_DTYPE)
        y = _dot(unperm, ys_ref[...])
        o_ref[...] = _rms(x1_ref[...] + gt2_ref[...] * y) * gf_ref[...]


def _moe_final(h2, wt, w_up, w_down, x1, gt2, g_final, *, tm, rows_per_batch):
    m = h2.shape[0]
    tpb = rows_per_batch // tm
    n_rows = tm + N_GROUPS * MOE_BLK
    rowblk = pl.BlockSpec((tm, D_MODEL), lambda i, e: (i, 0))
    return pl.pallas_call(
        _moe_kernel,
        grid=(m // tm, N_EXPERTS // MOE_EXPERTS_PER_STEP),
        in_specs=[rowblk,
                  pl.BlockSpec((tm, LANES), lambda i, e: (i, 0)),
                  pl.BlockSpec((MOE_EXPERTS_PER_STEP, D_MODEL, 2 * D_EXPERT), lambda i, e: (e, 0, 0)),
                  pl.BlockSpec((MOE_EXPERTS_PER_STEP, D_EXPERT, D_MODEL), lambda i, e: (e, 0, 0)),
                  rowblk,
                  pl.BlockSpec((None, 1, D_MODEL), lambda i, e: (i // tpb, 0, 0)),
                  pl.BlockSpec((1, D_MODEL), lambda i, e: (0, 0))],
        out_specs=rowblk,
        out_shape=jax.ShapeDtypeStruct((m, D_MODEL), F32),
        scratch_shapes=[pltpu.VMEM((n_rows, D_MODEL), MXU_DTYPE), pltpu.VMEM((n_rows, D_MODEL), F32),
                        pltpu.VMEM((n_rows, LANES), F32), pltpu.VMEM((tm, LANES), F32),
                        pltpu.SMEM((2 * N_GROUPS,), jnp.int32)],
        compiler_params=_cparams(("parallel", "arbitrary"), 56 * 1024 * 1024),
        name="moe_final",
    )(h2, wt, w_up, w_down, x1, gt2, g_final)


def _empty_states(batch):
    ml = (jnp.zeros((batch, 2, HEADS, QK, DV + LANES), F32),
          jnp.full((batch, 2, HEADS, 1, LANES), NEG_BIG, F32))
    gla = jnp.zeros((batch, 2, HEADS, DV, QK), F32)
    return ml, gla


def kernel(x, c, ctx, c_ctx, w_mod, b_mod, g_norm1, w_in, ml_conv, ml_conv_b, b_mgate, ml_norm, gla_w2, gla_b2,
           gla_norm, w_proj_m, w_proj_g, w_out, g_norm2, w_grp, b_grp, w_rexp, b_rexp, w_up, w_down, g_final):
    batch, t, d = x.shape
    t_ctx = ctx.shape[1]
    assert d == D_MODEL and w_mod.shape[0] == 1 and w_in.shape[2] == sum(IN_SIZES)
    assert t % (GRID_W * 16) == 0 and t % CHUNK == 0 and t_ctx % CHUNK == 0

    off = [0]
    for s in IN_SIZES:
        off.append(off[-1] + s)
    wi = w_in[0].astype(MXU_DTYPE)
    w_main = jnp.concatenate([wi[:, off[0]:off[4]], wi[:, off[5]:off[9]], wi[:, off[10]:off[12]]], axis=1)
    w_small = jnp.concatenate([wi[:, off[9]:off[10]], wi[:, off[4]:off[5]],
                               jnp.zeros((d, LANES - 2 * GLA_RANK - 4 * HEADS), MXU_DTYPE)], axis=1)
    gate_bias = jnp.zeros((1, LANES), F32).at[0, SMALL_GATE0:SMALL_GATE0 + 4 * HEADS].set(b_mgate[0])
    w2p = jnp.zeros((2, LANES, QK_W), F32)
    w2p = w2p.at[0, 0:GLA_RANK].set(gla_w2[0, 0]).at[1, GLA_RANK:2 * GLA_RANK].set(gla_w2[0, 1])
    b2 = gla_b2[0][:, None, :]
    conv_w = ml_conv[0].reshape(9, 2 * QK_W)
    conv_b = ml_conv_b[0][None, :]
    w_route = jnp.concatenate([w_grp[0], w_rexp[0], jnp.zeros((d, LANES - N_GROUPS - N_EXPERTS), F32)], axis=1)
    b_route = jnp.concatenate([b_grp[0], b_rexp[0], jnp.zeros((LANES - N_GROUPS - N_EXPERTS,), F32)])[None, :]

    cc = jnp.concatenate([c, c_ctx[None, :], jnp.zeros((8 - batch - 1, d), F32)], axis=0)
    mod = _modulation(cc, w_mod[0], b_mod[0][None, :])
    sh1, sc1, gt1, sh2, sc2, gt2 = [mod[:batch, i * d:(i + 1) * d][:, None, :] for i in range(6)]
    sh1c, sc1c = [jnp.broadcast_to(mod[batch:batch + 1, i * d:(i + 1) * d][:, None, :], (batch, 1, d)) for i in range(2)]
    g1 = g_norm1[0][None, :]

    main_c, small_c = _in_proj(ctx.reshape(batch * t_ctx, d), g1, sc1c, sh1c, w_main, w_small,
                               tm=t_ctx, rows_per_batch=t_ctx)
    qk_c = _conv_silu(main_c, conv_w, conv_b, batch=batch, rows=1, cols=t_ctx)
    ml0, gla0 = _empty_states(batch)
    main_c3, small_c3 = main_c.reshape(batch, t_ctx, MAIN_W), small_c.reshape(batch, t_ctx, LANES)
    ml_states = _mlstm_scan(qk_c.reshape(batch, t_ctx, 2 * QK_W), main_c3, small_c3, gate_bias, ml0, with_output=False)
    gla_state = _gla_scan(main_c3, small_c3, w2p, b2, gla0, with_output=False)

    x2 = x.reshape(batch * t, d)
    main, small = _in_proj(x2, g1, sc1, sh1, w_main, w_small, tm=1024, rows_per_batch=t)
    qk = _conv_silu(main, conv_w, conv_b, batch=batch, rows=t // GRID_W, cols=GRID_W)
    main3, small3 = main.reshape(batch, t, MAIN_W), small.reshape(batch, t, LANES)
    hm_f, hm_b = [a.reshape(batch * t, V_W) for a in
                  _mlstm_scan(qk.reshape(batch, t, 2 * QK_W), main3, small3, gate_bias, ml_states, with_output=True)]
    og_f, og_b = [a.reshape(batch * t, V_W) for a in _gla_scan(main3, small3, w2p, b2, gla_state, with_output=True)]

    x1, h2, wt = _merge(hm_f, hm_b, og_f, og_b, main, x2, ml_norm, gla_norm,
                        w_proj_m[0].astype(MXU_DTYPE), w_proj_g[0].astype(MXU_DTYPE), w_out[0].astype(MXU_DTYPE),
                        gt1, g_norm2, sc2, sh2, w_route, b_route, tm=256, rows_per_batch=t)
    out = _moe_final(h2, wt, w_up[0].astype(MXU_DTYPE), w_down[0].astype(MXU_DTYPE), x1, gt2, g_final[None, :],
                     tm=1024, rows_per_batch=t)
    return out.reshape(batch, t, d)
```

```python
import functools

import jax
import jax.numpy as jnp
from jax import lax
from jax.experimental import pallas as pl
from jax.experimental.pallas import tpu as pltpu

D_MODEL = 1024
GRID_W = 64
CHUNK = 256
EPS = 1e-6
NEG_BIG = -1e30
HEADS = 4
QK = D_MODEL // 8
DV = D_MODEL // 4
QK_W = HEADS * QK
V_W = HEADS * DV
GLA_RANK = 16
GLA_TAU = 16.0
N_GROUPS = 4
EXPERTS_PER_GROUP = 4
N_EXPERTS = N_GROUPS * EXPERTS_PER_GROUP
D_EXPERT = D_MODEL // 2
IN_SIZES = (QK_W, QK_W, V_W, V_W, 4 * HEADS, QK_W, QK_W, V_W, V_W, 2 * GLA_RANK, D_MODEL, D_MODEL)

LANES = 128
MXU_DTYPE = jnp.bfloat16
F32 = jnp.float32
VMEM_LIMIT = 48 * 1024 * 1024

COL_QK_M, COL_V_M, COL_O_M, COL_QK_G, COL_V_G, COL_R_G, COL_MG_M, COL_MG_G = range(8)
MAIN_W = 8 * D_MODEL
SMALL_GATE0 = 2 * GLA_RANK
GLA_SAFE_DECAY = 80.0
ROUTE_E0 = N_GROUPS


def _dot(a, b):
    return jnp.dot(a.astype(MXU_DTYPE), b.astype(MXU_DTYPE), preferred_element_type=F32)


def _dot_nt(a, b):
    return lax.dot_general(a.astype(MXU_DTYPE), b.astype(MXU_DTYPE), (((1,), (1,)), ((), ())),
                           preferred_element_type=F32)


def _transpose_mxu(a):
    m = a.shape[1]
    eye = (lax.broadcasted_iota(jnp.int32, (m, m), 0) == lax.broadcasted_iota(jnp.int32, (m, m), 1))
    return _dot_nt(eye.astype(MXU_DTYPE), a).astype(MXU_DTYPE)


def _dot_tn_xlu(a, b):
    return lax.dot_general(a.astype(MXU_DTYPE), b.astype(MXU_DTYPE), (((0,), (0,)), ((), ())),
                           preferred_element_type=F32)


def _split3(x):
    hi = x.astype(MXU_DTYPE)
    r1 = x - hi.astype(F32)
    mid = r1.astype(MXU_DTYPE)
    lo = (r1 - mid.astype(F32)).astype(MXU_DTYPE)
    return hi, mid, lo


def _dot_exact_lhs(a01, x):
    hi, mid, lo = _split3(x)
    return _dot(a01, hi) + _dot(a01, mid) + _dot(a01, lo)


def _log_sigmoid(x):
    return jnp.minimum(x, 0.0) - jnp.log(1.0 + jnp.exp(-jnp.abs(x)))


def _silu(x):
    return x * jax.nn.sigmoid(x)


def _rms(x):
    return x * lax.rsqrt(jnp.mean(x * x, axis=-1, keepdims=True) + EPS)


def _cparams(sem, vmem_limit=VMEM_LIMIT):
    return pltpu.CompilerParams(dimension_semantics=sem, vmem_limit_bytes=vmem_limit)


def _mod_kernel(c_ref, w_ref, b_ref, o_ref):
    o_ref[...] = _dot(_silu(c_ref[...]), w_ref[...]) + b_ref[...]


def _modulation(cc, w_mod, b_mod):
    n = w_mod.shape[1]
    tn = 512
    return pl.pallas_call(
        _mod_kernel,
        grid=(n // tn,),
        in_specs=[pl.BlockSpec((8, D_MODEL), lambda j: (0, 0)),
                  pl.BlockSpec((D_MODEL, tn), lambda j: (0, j)),
                  pl.BlockSpec((1, tn), lambda j: (0, j))],
        out_specs=pl.BlockSpec((8, tn), lambda j: (0, j)),
        out_shape=jax.ShapeDtypeStruct((8, n), F32),
        compiler_params=_cparams(("arbitrary",)),
        name="modulation",
    )(cc, w_mod, b_mod)


def _inproj_kernel(x_ref, g_ref, sc_ref, sh_ref, w_ref, ws_ref, o_ref, os_ref, xn_ref):
    @pl.when(pl.program_id(1) == 0)
    def _():
        xn = _rms(x_ref[...]) * g_ref[...] * (1.0 + sc_ref[...]) + sh_ref[...]
        xn_ref[...] = xn.astype(MXU_DTYPE)
        os_ref[...] = _dot(xn_ref[...], ws_ref[...])

    o_ref[...] = _dot(xn_ref[...], w_ref[...]).astype(o_ref.dtype)


def _in_proj(x2, g, sc, sh, w_main, w_small, *, tm, rows_per_batch):
    m = x2.shape[0]
    tn = 2048
    tiles_per_batch = rows_per_batch // tm
    vec = pl.BlockSpec((None, 1, D_MODEL), lambda i, j: (i // tiles_per_batch, 0, 0))
    return pl.pallas_call(
        _inproj_kernel,
        grid=(m // tm, MAIN_W // tn),
        in_specs=[pl.BlockSpec((tm, D_MODEL), lambda i, j: (i, 0)),
                  pl.BlockSpec((1, D_MODEL), lambda i, j: (0, 0)),
                  vec, vec,
                  pl.BlockSpec((D_MODEL, tn), lambda i, j: (0, j)),
                  pl.BlockSpec((D_MODEL, LANES), lambda i, j: (0, 0))],
        out_specs=[pl.BlockSpec((tm, tn), lambda i, j: (i, j)),
                   pl.BlockSpec((tm, LANES), lambda i, j: (i, 0))],
        out_shape=[jax.ShapeDtypeStruct((m, MAIN_W), MXU_DTYPE), jax.ShapeDtypeStruct((m, LANES), F32)],
        scratch_shapes=[pltpu.VMEM((tm, D_MODEL), MXU_DTYPE)],
        compiler_params=_cparams(("parallel", "arbitrary")),
        name="in_proj",
    )(x2, g, sc, sh, w_main, w_small)


def _conv_kernel(x_ref, w_ref, b_ref, o_ref, *, rows, cols):
    scale = jnp.where(pl.program_id(1) * LANES >= QK_W, QK ** -0.5, 1.0).astype(F32)
    w = w_ref[...]
    bias = b_ref[...]
    tpos = lax.broadcasted_iota(jnp.int32, (cols, 1), 0)
    has_left = tpos >= 1
    has_right = tpos < cols - 1

    def row_filters(j):
        tile = x_ref[pl.ds(pl.multiple_of(j * cols, cols), cols), :].astype(F32)
        left = jnp.where(has_left, pltpu.roll(tile, 1, axis=0), 0.0)
        right = jnp.where(has_right, pltpu.roll(tile, cols - 1, axis=0), 0.0)
        return [left * w[3 * i:3 * i + 1, :] + tile * w[3 * i + 1:3 * i + 2, :] + right * w[3 * i + 2:3 * i + 3, :]
                for i in range(3)]

    def finish(j, acc):
        o_ref[pl.ds(pl.multiple_of(j * cols, cols), cols), :] = (_silu(acc + bias) * scale).astype(o_ref.dtype)

    first = row_filters(0)

    def body(j, carry):
        acc, below = carry
        h = row_filters(j)
        finish(j - 1, acc + h[2])
        return below + h[1], h[0]

    acc, _ = lax.fori_loop(1, rows, body, (first[1], first[0]))
    finish(rows - 1, acc)


def _conv_silu(main, conv_w, conv_b, *, batch, rows, cols):
    t = rows * cols
    nct = 2 * QK_W // LANES
    return pl.pallas_call(
        functools.partial(_conv_kernel, rows=rows, cols=cols),
        grid=(batch, nct),
        in_specs=[pl.BlockSpec((t, LANES), lambda b, c: (b, c)),
                  pl.BlockSpec((9, LANES), lambda b, c: (0, c)),
                  pl.BlockSpec((1, LANES), lambda b, c: (0, c))],
        out_specs=pl.BlockSpec((t, LANES), lambda b, c: (b, c)),
        out_shape=jax.ShapeDtypeStruct((batch * t, 2 * QK_W), MXU_DTYPE),
        compiler_params=_cparams(("parallel", "arbitrary")),
        name="conv_silu",
    )(main, conv_w, conv_b)


def _chunk_masks(direction):
    row = lax.broadcasted_iota(jnp.int32, (CHUNK, CHUNK), 0)
    col = lax.broadcasted_iota(jnp.int32, (CHUNK, CHUNK), 1)
    seen = (row >= col) if direction == 0 else (row <= col)
    return seen, seen.astype(MXU_DTYPE)


def _scan_specs(batch, nc, col_blocks, widths):
    specs = []
    for direction in (0, 1):
        for cb, wd in zip(col_blocks, widths):
            if direction == 0:
                specs.append(pl.BlockSpec((batch, CHUNK, wd), lambda c, cb=cb: (0, c, cb)))
            else:
                specs.append(pl.BlockSpec((batch, CHUNK, wd), lambda c, cb=cb: (0, nc - 1 - c, cb)))
    return specs


def _scan_out_specs(batch, nc):
    return [pl.BlockSpec((batch, CHUNK, V_W), lambda c: (0, c, 0)),
            pl.BlockSpec((batch, CHUNK, V_W), lambda c: (0, nc - 1 - c, 0))]


def _whole(shape):
    nd = len(shape)
    return pl.BlockSpec(tuple(shape), lambda c: (0,) * nd)


def _lane_tile(x, width):
    return jnp.concatenate([x] * (width // LANES), axis=-1)


def _mlstm_kernel(*refs, with_output, batch):
    (qk_f, v_f, sm_f, qk_b, v_b, sm_b, bias_ref, cn0_ref, m0_ref) = refs[:9]
    if with_output:
        hf_ref, hb_ref, cn_s, m_s = refs[9:]
    else:
        cn_out, m_out, cn_s, m_s = refs[9:]
    step = pl.program_id(0)

    @pl.when(step == 0)
    def _():
        cn_s[...] = cn0_ref[...]
        m_s[...] = m0_ref[...]

    lane = lax.broadcasted_iota(jnp.int32, (1, LANES), 1)
    gate_lane = jnp.logical_and(lane >= SMALL_GATE0, lane < SMALL_GATE0 + 4 * HEADS)
    forget_lane = jnp.logical_and(gate_lane, ((lane - SMALL_GATE0) % (2 * HEADS)) >= HEADS)
    eye = (lax.broadcasted_iota(jnp.int32, (LANES, LANES), 0)
           == lax.broadcasted_iota(jnp.int32, (LANES, LANES), 1)).astype(MXU_DTYPE)
    ones_cols = jnp.ones((CHUNK, LANES), MXU_DTYPE)

    for direction, (qk_ref, v_ref, sm_ref) in enumerate(((qk_f, v_f, sm_f), (qk_b, v_b, sm_b))):
        seen, seen01 = _chunk_masks(direction)
        last = CHUNK - 1 if direction == 0 else 0
        for bi in range(batch):
            g = sm_ref[bi] + bias_ref[...]
            gp = jnp.where(forget_lane, _log_sigmoid(g), g)
            bc = _dot_exact_lhs(seen01, gp)
            hi, mid, lo = _split3(gp)
            gp_t = _dot_nt(eye, hi) + _dot_nt(eye, mid) + _dot_nt(eye, lo)
            hi, mid, lo = _split3(bc)
            bc_t = _dot_nt(eye, hi) + _dot_nt(eye, mid) + _dot_nt(eye, lo)
            bend_row = bc[last:last + 1, :]
            heads = []
            for h in range(HEADS):
                ji = SMALL_GATE0 + direction * 2 * HEADS + h
                jf = ji + HEADS
                c = dict(ji=ji, jf=jf)
                c["q"] = qk_ref[bi, :, h * QK:(h + 1) * QK].astype(F32)
                c["k"] = qk_ref[bi, :, QK_W + h * QK:QK_W + (h + 1) * QK].astype(F32)
                c["v_ext"] = jnp.concatenate([v_ref[bi, :, h * DV:(h + 1) * DV].astype(MXU_DTYPE), ones_cols], axis=-1)
                c["cn_old"] = cn_s[bi, direction, h]
                c["m_old"] = m_s[bi, direction, h]
                heads.append(c)
            if with_output:
                for c in heads:
                    c["qk"] = _dot_nt(c["q"], c["k"])
                    c["qc"] = _dot(c["q"], c["cn_old"])
            for c in heads:
                b_end = jnp.broadcast_to(bend_row[:, c["jf"]:c["jf"] + 1], (1, LANES))
                i_col = jnp.broadcast_to(gp[:, c["ji"]:c["ji"] + 1], (CHUNK, LANES))
                c["b_col"] = jnp.broadcast_to(bc[:, c["jf"]:c["jf"] + 1], (CHUNK, LANES))
                log_w = b_end - c["b_col"] + i_col
                c["m_new"] = jnp.maximum(b_end + c["m_old"], jnp.max(log_w, axis=0, keepdims=True))
                c["kw"] = (c["k"] * jnp.exp(log_w - c["m_new"])).astype(MXU_DTYPE)
                c["decay"] = jnp.exp(b_end + c["m_old"] - c["m_new"])
            for c in heads:
                c["kw_t"] = _transpose_mxu(c["kw"])
            if with_output:
                for c in heads:
                    i_row = gp_t[c["ji"]:c["ji"] + 1, :]
                    b_row = bc_t[c["jf"]:c["jf"] + 1, :]
                    log_d = jnp.where(seen, _lane_tile(c["b_col"], CHUNK) - b_row + i_row, -jnp.inf)
                    log_inter = c["b_col"] + c["m_old"]
                    c["m_t"] = jnp.maximum(log_inter, jnp.max(log_d, axis=-1, keepdims=True))
                    c["s"] = (c["qk"] * jnp.exp(log_d - _lane_tile(c["m_t"], CHUNK))).astype(MXU_DTYPE)
                    c["w_inter"] = jnp.exp(log_inter - c["m_t"])
                for c in heads:
                    c["sv"] = _dot(c["s"], c["v_ext"])
            for h, c in enumerate(heads):
                cn_s[bi, direction, h] = _lane_tile(c["decay"], DV + LANES) * c["cn_old"] + _dot(c["kw_t"], c["v_ext"])
                m_s[bi, direction, h] = c["m_new"]
            outs = []
            if with_output:
                for c in heads:
                    sv, qc, w_inter = c["sv"], c["qc"], c["w_inter"]
                    num = sv[:, :DV] + _lane_tile(w_inter, DV) * qc[:, :DV]
                    den = jnp.abs(sv[:, DV:] + w_inter * qc[:, DV:])
                    outs.append(num / _lane_tile(jnp.maximum(den, jnp.exp(-c["m_t"])), DV))
            if with_output:
                (hf_ref if direction == 0 else hb_ref)[bi] = jnp.concatenate(outs, axis=-1)

    if not with_output:
        @pl.when(step == pl.num_programs(0) - 1)
        def _():
            cn_out[...] = cn_s[...]
            m_out[...] = m_s[...]


def _mlstm_scan(qk, main, small, gate_bias, states, *, with_output):
    cn0, m0 = states
    batch, t, _ = qk.shape
    nc = t // CHUNK
    in_specs = _scan_specs(batch, nc, (0, COL_V_M, 0), (2 * QK_W, V_W, LANES))
    in_specs += [_whole(gate_bias.shape), _whole(cn0.shape), _whole(m0.shape)]
    if with_output:
        out_specs = _scan_out_specs(batch, nc)
        out_shape = [jax.ShapeDtypeStruct((batch, t, V_W), F32)] * 2
    else:
        out_specs = [_whole(cn0.shape), _whole(m0.shape)]
        out_shape = [jax.ShapeDtypeStruct(s.shape, F32) for s in states]
    return pl.pallas_call(
        functools.partial(_mlstm_kernel, with_output=with_output, batch=batch),
        grid=(nc,),
        in_specs=in_specs,
        out_specs=out_specs,
        out_shape=out_shape,
        scratch_shapes=[pltpu.VMEM(cn0.shape, F32), pltpu.VMEM(m0.shape, F32)],
        compiler_params=_cparams(("arbitrary",)),
        name="mlstm_scan_out" if with_output else "mlstm_scan_state",
    )(qk, main, small, qk, main, small, gate_bias, cn0, m0)


def _gla_exact_intra(q, k, v, b, direction):
    row_id = lax.broadcasted_iota(jnp.int32, (CHUNK, 1), 0)

    def row(t, acc):
        pick = row_id == t
        b_t = jnp.sum(jnp.where(pick, b, 0.0), axis=0, keepdims=True)
        q_t = jnp.sum(jnp.where(pick, q, 0.0), axis=0, keepdims=True)
        ok = (row_id <= t) if direction == 0 else (row_id >= t)
        e = jnp.exp(jnp.where(ok, b_t - b, -jnp.inf))
        sc = jnp.sum(q_t * k * e, axis=-1, keepdims=True)
        o_t = jnp.sum(sc * v, axis=0, keepdims=True)
        return jnp.where(pick, o_t, acc)

    return lax.fori_loop(0, CHUNK, row, jnp.zeros((CHUNK, DV), F32))


def _gla_kernel(*refs, with_output, batch):
    (qk_f, v_f, sm_f, qk_b, v_b, sm_b, w2_ref, b2_ref, s0_ref) = refs[:9]
    if with_output:
        of_ref, ob_ref, s_s, b_s, inter_s = refs[9:]
    else:
        s_out, s_s = refs[9:]
    step = pl.program_id(0)

    @pl.when(step == 0)
    def _():
        s_s[...] = s0_ref[...]

    worst_decay = []
    for direction, (qk_ref, v_ref, sm_ref) in enumerate(((qk_f, v_f, sm_f), (qk_b, v_b, sm_b))):
        seen, seen01 = _chunk_masks(direction)
        last = CHUNK - 1 if direction == 0 else 0
        for bi in range(batch):
            z = _dot(sm_ref[bi], w2_ref[direction]) + b2_ref[direction]
            log_a = _log_sigmoid(z) * (1.0 / GLA_TAU)
            b_all = _dot_exact_lhs(seen01, log_a)
            outs, inters = [], []
            for h in range(HEADS):
                q = qk_ref[bi, :, h * QK:(h + 1) * QK].astype(F32) * (QK ** -0.5)
                k = qk_ref[bi, :, QK_W + h * QK:QK_W + (h + 1) * QK].astype(F32)
                v = v_ref[bi, :, h * DV:(h + 1) * DV]
                b = b_all[:, h * QK:(h + 1) * QK]
                b_end = b[last:last + 1, :]
                st_old = s_s[bi, direction, h]
                k_dec = k * jnp.exp(b_end - b)
                s_s[bi, direction, h] = st_old * jnp.exp(b_end) + _dot_tn_xlu(v, k_dec)
                if with_output:
                    q_dec = q * jnp.exp(b)
                    inter = _dot_nt(q_dec, st_old)
                    scores = jnp.where(seen, _dot_nt(q_dec, k * jnp.exp(-b)), 0.0)
                    outs.append(_dot(scores, v) + inter)
                    inters.append(inter)
            if with_output:
                (of_ref if direction == 0 else ob_ref)[bi] = jnp.concatenate(outs, axis=-1)
                b_s[bi, direction] = b_all
                inter_s[bi, direction] = jnp.concatenate(inters, axis=-1)
                worst_decay.append(jnp.max(-b_all[last:last + 1, :]))

    if with_output:
        @pl.when(functools.reduce(jnp.maximum, worst_decay) > GLA_SAFE_DECAY)
        def _():
            for direction, (qk_ref, v_ref, o_ref) in enumerate(((qk_f, v_f, of_ref), (qk_b, v_b, ob_ref))):
                last = CHUNK - 1 if direction == 0 else 0
                for bi in range(batch):
                    for h in range(HEADS):
                        b = b_s[bi, direction, :, h * QK:(h + 1) * QK]

                        @pl.when(jnp.max(-b[last:last + 1, :]) > GLA_SAFE_DECAY)
                        def _():
                            q = qk_ref[bi, :, h * QK:(h + 1) * QK].astype(F32) * (QK ** -0.5)
                            k = qk_ref[bi, :, QK_W + h * QK:QK_W + (h + 1) * QK].astype(F32)
                            v = v_ref[bi, :, h * DV:(h + 1) * DV].astype(F32)
                            o_ref[bi, :, h * DV:(h + 1) * DV] = (
                                inter_s[bi, direction, :, h * DV:(h + 1) * DV]
                                + _gla_exact_intra(q, k, v, b, direction))
    else:
        @pl.when(step == pl.num_programs(0) - 1)
        def _():
            s_out[...] = s_s[...]


def _gla_scan(main, small, w2p, b2, s0, *, with_output):
    batch, t, _ = main.shape
    nc = t // CHUNK
    in_specs = _scan_specs(batch, nc, (COL_QK_G, COL_V_G, 0), (2 * QK_W, V_W, LANES))
    in_specs += [_whole(w2p.shape), _whole(b2.shape), _whole(s0.shape)]
    scratch = [pltpu.VMEM(s0.shape, F32)]
    if with_output:
        out_specs = _scan_out_specs(batch, nc)
        out_shape = [jax.ShapeDtypeStruct((batch, t, V_W), F32)] * 2
        scratch += [pltpu.VMEM((batch, 2, CHUNK, QK_W), F32), pltpu.VMEM((batch, 2, CHUNK, V_W), F32)]
    else:
        out_specs = _whole(s0.shape)
        out_shape = jax.ShapeDtypeStruct(s0.shape, F32)
    return pl.pallas_call(
        functools.partial(_gla_kernel, with_output=with_output, batch=batch),
        grid=(nc,),
        in_specs=in_specs,
        out_specs=out_specs,
        out_shape=out_shape,
        scratch_shapes=scratch,
        compiler_params=_cparams(("arbitrary",)),
        name="gla_scan_out" if with_output else "gla_scan_state",
    )(main, main, small, main, main, small, w2p, b2, s0)


def _head_rms(a):
    return jnp.concatenate([_rms(a[:, h * DV:(h + 1) * DV]) for h in range(HEADS)], axis=-1)


def _merge_kernel(hmf, hmb, ogf, ogb, om, rg, mgm, mgg, x_ref, mln, gln, wpm, wpg, wo, gt1, g2, sc2, sh2,
                  wr, br, x1_ref, h2_ref, wt_ref):
    y_m = _head_rms(hmf[...] + hmb[...]) * mln[...] * jax.nn.sigmoid(om[...].astype(F32))
    y_g = _head_rms(ogf[...] + ogb[...]) * gln[...] * _silu(rg[...].astype(F32))
    y = (jax.nn.sigmoid(mgm[...].astype(F32)) * _dot(y_m, wpm[...])
         + jax.nn.sigmoid(mgg[...].astype(F32)) * _dot(y_g, wpg[...]))
    x1 = x_ref[...] + gt1[...] * _dot(y, wo[...])
    x1_ref[...] = x1
    h2 = _rms(x1) * g2[...] * (1.0 + sc2[...]) + sh2[...]
    h2_ref[...] = h2.astype(MXU_DTYPE)

    hh, hm_, _ = _split3(h2)
    wh, wm_, _ = _split3(wr[...])
    lg = _dot(hh, wh) + _dot(hh, wm_) + _dot(hm_, wh) + br[...]
    lane = lax.broadcasted_iota(jnp.int32, lg.shape, 1)

    def masked_softmax(mask):
        l = jnp.where(mask, lg, -jnp.inf)
        e = jnp.exp(l - jnp.max(l, axis=-1, keepdims=True))
        return e / jnp.sum(e, axis=-1, keepdims=True)

    def top1(p, mask):
        pm = jnp.where(mask, p, -1.0)
        best = jnp.max(pm, axis=-1, keepdims=True)
        idx = jnp.min(jnp.where(jnp.logical_and(mask, pm == best), lane, LANES), axis=-1, keepdims=True)
        return best, idx

    gmask = lane < N_GROUPS
    grp_p, grp = top1(masked_softmax(gmask), gmask)
    e_lo = ROUTE_E0 + grp * EXPERTS_PER_GROUP
    emask = jnp.logical_and(lane >= e_lo, lane < e_lo + EXPERTS_PER_GROUP)
    p_in = masked_softmax(emask)
    p1, i1 = top1(p_in, emask)
    p2, i2 = top1(p_in, jnp.logical_and(emask, lane != i1))
    tot = p1 + p2
    wt_ref[...] = (jnp.where(lane == i1, grp_p * p1 / tot, 0.0)
                   + jnp.where(lane == i2, grp_p * p2 / tot, 0.0)
                   + jnp.where(lane == grp, 1.0, 0.0))


def _merge(hmf, hmb, ogf, ogb, main, x2, mln, gln, wpm, wpg, wo, gt1, g2, sc2, sh2, wr, br, *, tm, rows_per_batch):
    m = x2.shape[0]
    tpb = rows_per_batch // tm
    rowblk = pl.BlockSpec((tm, D_MODEL), lambda i: (i, 0))
    colblk = lambda cb: pl.BlockSpec((tm, D_MODEL), lambda i, cb=cb: (i, cb))
    vec = pl.BlockSpec((1, D_MODEL), lambda i: (0, 0))
    bvec = pl.BlockSpec((None, 1, D_MODEL), lambda i: (i // tpb, 0, 0))
    wmat = pl.BlockSpec((D_MODEL, D_MODEL), lambda i: (0, 0))
    return pl.pallas_call(
        _merge_kernel,
        grid=(m // tm,),
        in_specs=[rowblk, rowblk, rowblk, rowblk, colblk(COL_O_M), colblk(COL_R_G), colblk(COL_MG_M),
                  colblk(COL_MG_G), rowblk, vec, vec, wmat, wmat, wmat, bvec, vec, bvec, bvec,
                  pl.BlockSpec((D_MODEL, LANES), lambda i: (0, 0)), pl.BlockSpec((1, LANES), lambda i: (0, 0))],
        out_specs=[rowblk, rowblk, pl.BlockSpec((tm, LANES), lambda i: (i, 0))],
        out_shape=[jax.ShapeDtypeStruct((m, D_MODEL), F32), jax.ShapeDtypeStruct((m, D_MODEL), MXU_DTYPE),
                   jax.ShapeDtypeStruct((m, LANES), F32)],
        compiler_params=_cparams(("parallel",)),
        name="merge_route",
    )(hmf, hmb, ogf, ogb, main, main, main, main, x2, mln, gln, wpm, wpg, wo, gt1, g2, sc2, sh2, wr, br)


MOE_BLK = 128
MOE_COMMON_BLKS = (2, 3)
MOE_EXPERTS_PER_STEP = 2


def _moe_kernel(h2_ref, wt_ref, wup_ref, wdn_ref, x1_ref, gt2_ref, gf_ref, o_ref,
                xs_ref, ys_ref, ws_ref, dest_ref, blk_ref):
    step = pl.program_id(1)
    tm = h2_ref.shape[0]
    n_rows = xs_ref.shape[0]

    @pl.when(step == 0)
    def _():
        r = wt_ref[...]
        lane = lax.broadcasted_iota(jnp.int32, (tm, LANES), 1)
        lane1 = lax.broadcasted_iota(jnp.int32, (1, LANES), 1)
        gm = jnp.where(lane < N_GROUPS, r, 0.0)
        earlier = (lax.broadcasted_iota(jnp.int32, (tm, tm), 1)
                   < lax.broadcasted_iota(jnp.int32, (tm, tm), 0)).astype(MXU_DTYPE)
        before = _dot(earlier, gm)
        padded = jnp.floor((jnp.sum(gm, axis=0, keepdims=True) + (MOE_BLK - 1)) * (1.0 / MOE_BLK)) * MOE_BLK
        start = jnp.zeros((1, LANES), F32)
        run = jnp.zeros((1, 1), F32)
        for g in range(N_GROUPS):
            size = jnp.sum(jnp.where(lane1 == g, padded, 0.0), axis=-1, keepdims=True)
            start = jnp.where(lane1 == g, run, start)
            blk_ref[g] = (jnp.sum(run) * (1.0 / MOE_BLK)).astype(jnp.int32)
            blk_ref[N_GROUPS + g] = (jnp.sum(size) * (1.0 / MOE_BLK)).astype(jnp.int32)
            run = run + size
        dest = jnp.sum(gm * (start + before), axis=-1, keepdims=True)
        dest_ref[...] = jnp.broadcast_to(dest, (tm, LANES))
        dest_row = dest_ref[...].T[0:1, :].astype(jnp.int32)
        perm = (lax.broadcasted_iota(jnp.int32, (n_rows, tm), 0) == dest_row).astype(MXU_DTYPE)
        xs_ref[...] = _dot(perm, h2_ref[...]).astype(MXU_DTYPE)
        moved = _dot(perm, jnp.concatenate(_split3(r), axis=-1))
        ws_ref[...] = moved[:, :LANES] + moved[:, LANES:2 * LANES] + moved[:, 2 * LANES:]
        ys_ref[...] = jnp.zeros_like(ys_ref)

    group = step // (EXPERTS_PER_GROUP // MOE_EXPERTS_PER_STEP)
    first_blk = blk_ref[group]
    n_blk = blk_ref[N_GROUPS + group]

    def expert_on(r0, rows):
        x = xs_ref[pl.ds(r0, rows), :]
        ws = ws_ref[pl.ds(r0, rows), :]
        lane_b = lax.broadcasted_iota(jnp.int32, (rows, LANES), 1)
        experts = range(MOE_EXPERTS_PER_STEP)
        gus = [_dot(x, wup_ref[k]) for k in experts]
        hiddens = [(_silu(gu[:, :D_EXPERT]) * gu[:, D_EXPERT:]).astype(MXU_DTYPE) for gu in gus]
        w_cols = [jnp.sum(jnp.where(lane_b == ROUTE_E0 + step * MOE_EXPERTS_PER_STEP + k, ws, 0.0),
                          axis=-1, keepdims=True) for k in experts]
        ys = [_dot(hiddens[k], wdn_ref[k]) for k in experts]
        ys_ref[pl.ds(r0, rows), :] += sum(y * w for y, w in zip(ys, w_cols))

    for k in MOE_COMMON_BLKS:
        @pl.when(n_blk == k)
        def _():
            expert_on(pl.multiple_of(first_blk * MOE_BLK, MOE_BLK), k * MOE_BLK)

    @pl.when(functools.reduce(jnp.logical_and, [n_blk != k for k in MOE_COMMON_BLKS]))
    def _():
        def block(j, carry):
            expert_on(pl.multiple_of((first_blk + j) * MOE_BLK, MOE_BLK), MOE_BLK)
            return carry

        lax.fori_loop(0, n_blk, block, 0)

    @pl.when(step == pl.num_programs(1) - 1)
    def _():
        dest = dest_ref[...][:, :1].astype(jnp.int32)
        unperm = (lax.broadcasted_iota(jnp.int32, (tm, n_rows), 1) == dest).astype(MXU_DTYPE)
        y = _dot(unperm, ys_ref[...])
        o_ref[...] = _rms(x1_ref[...] + gt2_ref[...] * y) * gf_ref[...]


def _moe_final(h2, wt, w_up, w_down, x1, gt2, g_final, *, tm, rows_per_batch):
    m = h2.shape[0]
    tpb = rows_per_batch // tm
    n_rows = tm + N_GROUPS * MOE_BLK
    rowblk = pl.BlockSpec((tm, D_MODEL), lambda i, e: (i, 0))
    return pl.pallas_call(
        _moe_kernel,
        grid=(m // tm, N_EXPERTS // MOE_EXPERTS_PER_STEP),
        in_specs=[rowblk,
                  pl.BlockSpec((tm, LANES), lambda i, e: (i, 0)),
                  pl.BlockSpec((MOE_EXPERTS_PER_STEP, D_MODEL, 2 * D_EXPERT), lambda i, e: (e, 0, 0)),
                  pl.BlockSpec((MOE_EXPERTS_PER_STEP, D_EXPERT, D_MODEL), lambda i, e: (e, 0, 0)),
                  rowblk,
                  pl.BlockSpec((None, 1, D_MODEL), lambda i, e: (i // tpb, 0, 0)),
                  pl.BlockSpec((1, D_MODEL), lambda i, e: (0, 0))],
        out_specs=rowblk,
        out_shape=jax.ShapeDtypeStruct((m, D_MODEL), F32),
        scratch_shapes=[pltpu.VMEM((n_rows, D_MODEL), MXU_DTYPE), pltpu.VMEM((n_rows, D_MODEL), F32),
                        pltpu.VMEM((n_rows, LANES), F32), pltpu.VMEM((tm, LANES), F32),
                        pltpu.SMEM((2 * N_GROUPS,), jnp.int32)],
        compiler_params=_cparams(("parallel", "arbitrary"), 56 * 1024 * 1024),
        name="moe_final",
    )(h2, wt, w_up, w_down, x1, gt2, g_final)


def _empty_states(batch):
    ml = (jnp.zeros((batch, 2, HEADS, QK, DV + LANES), F32),
          jnp.full((batch, 2, HEADS, 1, LANES), NEG_BIG, F32))
    gla = jnp.zeros((batch, 2, HEADS, DV, QK), F32)
    return ml, gla


def kernel(x, c, ctx, c_ctx, w_mod, b_mod, g_norm1, w_in, ml_conv, ml_conv_b, b_mgate, ml_norm, gla_w2, gla_b2,
           gla_norm, w_proj_m, w_proj_g, w_out, g_norm2, w_grp, b_grp, w_rexp, b_rexp, w_up, w_down, g_final):
    batch, t, d = x.shape
    t_ctx = ctx.shape[1]
    assert d == D_MODEL and w_mod.shape[0] == 1 and w_in.shape[2] == sum(IN_SIZES)
    assert t % (GRID_W * 16) == 0 and t % CHUNK == 0 and t_ctx % CHUNK == 0

    off = [0]
    for s in IN_SIZES:
        off.append(off[-1] + s)
    wi = w_in[0].astype(MXU_DTYPE)
    w_main = jnp.concatenate([wi[:, off[0]:off[4]], wi[:, off[5]:off[9]], wi[:, off[10]:off[12]]], axis=1)
    w_small = jnp.concatenate([wi[:, off[9]:off[10]], wi[:, off[4]:off[5]],
                               jnp.zeros((d, LANES - 2 * GLA_RANK - 4 * HEADS), MXU_DTYPE)], axis=1)
    gate_bias = jnp.zeros((1, LANES), F32).at[0, SMALL_GATE0:SMALL_GATE0 + 4 * HEADS].set(b_mgate[0])
    w2p = jnp.zeros((2, LANES, QK_W), F32)
    w2p = w2p.at[0, 0:GLA_RANK].set(gla_w2[0, 0]).at[1, GLA_RANK:2 * GLA_RANK].set(gla_w2[0, 1])
    b2 = gla_b2[0][:, None, :]
    conv_w = ml_conv[0].reshape(9, 2 * QK_W)
    conv_b = ml_conv_b[0][None, :]
    w_route = jnp.concatenate([w_grp[0], w_rexp[0], jnp.zeros((d, LANES - N_GROUPS - N_EXPERTS), F32)], axis=1)
    b_route = jnp.concatenate([b_grp[0], b_rexp[0], jnp.zeros((LANES - N_GROUPS - N_EXPERTS,), F32)])[None, :]

    cc = jnp.concatenate([c, c_ctx[None, :], jnp.zeros((8 - batch - 1, d), F32)], axis=0)
    mod = _modulation(cc, w_mod[0], b_mod[0][None, :])
    sh1, sc1, gt1, sh2, sc2, gt2 = [mod[:batch, i * d:(i + 1) * d][:, None, :] for i in range(6)]
    sh1c, sc1c = [jnp.broadcast_to(mod[batch:batch + 1, i * d:(i + 1) * d][:, None, :], (batch, 1, d)) for i in range(2)]
    g1 = g_norm1[0][None, :]

    main_c, small_c = _in_proj(ctx.reshape(batch * t_ctx, d), g1, sc1c, sh1c, w_main, w_small,
                               tm=t_ctx, rows_per_batch=t_ctx)
    qk_c = _conv_silu(main_c, conv_w, conv_b, batch=batch, rows=1, cols=t_ctx)
    ml0, gla0 = _empty_states(batch)
    main_c3, small_c3 = main_c.reshape(batch, t_ctx, MAIN_W), small_c.reshape(batch, t_ctx, LANES)
    ml_states = _mlstm_scan(qk_c.reshape(batch, t_ctx, 2 * QK_W), main_c3, small_c3, gate_bias, ml0, with_output=False)
    gla_state = _gla_scan(main_c3, small_c3, w2p, b2, gla0, with_output=False)

    x2 = x.reshape(batch * t, d)
    main, small = _in_proj(x2, g1, sc1, sh1, w_main, w_small, tm=1024, rows_per_batch=t)
    qk = _conv_silu(main, conv_w, conv_b, batch=batch, rows=t // GRID_W, cols=GRID_W)
    main3, small3 = main.reshape(batch, t, MAIN_W), small.reshape(batch, t, LANES)
    hm_f, hm_b = [a.reshape(batch * t, V_W) for a in
                  _mlstm_scan(qk.reshape(batch, t, 2 * QK_W), main3, small3, gate_bias, ml_states, with_output=True)]
    og_f, og_b = [a.reshape(batch * t, V_W) for a in _gla_scan(main3, small3, w2p, b2, gla_state, with_output=True)]

    x1, h2, wt = _merge(hm_f, hm_b, og_f, og_b, main, x2, ml_norm, gla_norm,
                        w_proj_m[0].astype(MXU_DTYPE), w_proj_g[0].astype(MXU_DTYPE), w_out[0].astype(MXU_DTYPE),
                        gt1, g_norm2, sc2, sh2, w_route, b_route, tm=256, rows_per_batch=t)
    out = _moe_final(h2, wt, w_up[0].astype(MXU_DTYPE), w_down[0].astype(MXU_DTYPE), x1, gt2, g_final[None, :],
                     tm=1024, rows_per_batch=t)
    return out.reshape(batch, t, d)
```

```python
import functools

import jax
import jax.numpy as jnp
from jax import lax
from jax.experimental import pallas as pl
from jax.experimental.pallas import tpu as pltpu

D_MODEL = 1024
GRID_W = 64
CHUNK = 256
EPS = 1e-6
NEG_BIG = -1e30
HEADS = 4
QK = D_MODEL // 8
DV = D_MODEL // 4
QK_W = HEADS * QK
V_W = HEADS * DV
GLA_RANK = 16
GLA_TAU = 16.0
N_GROUPS = 4
EXPERTS_PER_GROUP = 4
N_EXPERTS = N_GROUPS * EXPERTS_PER_GROUP
D_EXPERT = D_MODEL // 2
IN_SIZES = (QK_W, QK_W, V_W, V_W, 4 * HEADS, QK_W, QK_W, V_W, V_W, 2 * GLA_RANK, D_MODEL, D_MODEL)

LANES = 128
MXU_DTYPE = jnp.bfloat16
F32 = jnp.float32
VMEM_LIMIT = 48 * 1024 * 1024

COL_QK_M, COL_V_M, COL_O_M, COL_QK_G, COL_V_G, COL_R_G, COL_MG_M, COL_MG_G = range(8)
MAIN_W = 8 * D_MODEL
SMALL_GATE0 = 2 * GLA_RANK
GLA_SAFE_DECAY = 80.0
ROUTE_E0 = N_GROUPS


def _dot(a, b):
    return jnp.dot(a.astype(MXU_DTYPE), b.astype(MXU_DTYPE), preferred_element_type=F32)


def _dot_nt(a, b):
    return lax.dot_general(a.astype(MXU_DTYPE), b.astype(MXU_DTYPE), (((1,), (1,)), ((), ())),
                           preferred_element_type=F32)


def _transpose_mxu(a):
    m = a.shape[1]
    eye = (lax.broadcasted_iota(jnp.int32, (m, m), 0) == lax.broadcasted_iota(jnp.int32, (m, m), 1))
    return _dot_nt(eye.astype(MXU_DTYPE), a).astype(MXU_DTYPE)


def _dot_tn_xlu(a, b):
    return lax.dot_general(a.astype(MXU_DTYPE), b.astype(MXU_DTYPE), (((0,), (0,)), ((), ())),
                           preferred_element_type=F32)


def _split3(x):
    hi = x.astype(MXU_DTYPE)
    r1 = x - hi.astype(F32)
    mid = r1.astype(MXU_DTYPE)
    lo = (r1 - mid.astype(F32)).astype(MXU_DTYPE)
    return hi, mid, lo


def _dot_exact_lhs(a01, x):
    hi, mid, lo = _split3(x)
    return _dot(a01, hi) + _dot(a01, mid) + _dot(a01, lo)


def _log_sigmoid(x):
    return jnp.minimum(x, 0.0) - jnp.log(1.0 + jnp.exp(-jnp.abs(x)))


def _silu(x):
    return x * jax.nn.sigmoid(x)


def _rms(x):
    return x * lax.rsqrt(jnp.mean(x * x, axis=-1, keepdims=True) + EPS)


def _cparams(sem, vmem_limit=VMEM_LIMIT):
    return pltpu.CompilerParams(dimension_semantics=sem, vmem_limit_bytes=vmem_limit)


def _mod_kernel(c_ref, w_ref, b_ref, o_ref):
    o_ref[...] = _dot(_silu(c_ref[...]), w_ref[...]) + b_ref[...]


def _modulation(cc, w_mod, b_mod):
    n = w_mod.shape[1]
    tn = 512
    return pl.pallas_call(
        _mod_kernel,
        grid=(n // tn,),
        in_specs=[pl.BlockSpec((8, D_MODEL), lambda j: (0, 0)),
                  pl.BlockSpec((D_MODEL, tn), lambda j: (0, j)),
                  pl.BlockSpec((1, tn), lambda j: (0, j))],
        out_specs=pl.BlockSpec((8, tn), lambda j: (0, j)),
        out_shape=jax.ShapeDtypeStruct((8, n), F32),
        compiler_params=_cparams(("arbitrary",)),
        name="modulation",
    )(cc, w_mod, b_mod)


def _inproj_kernel(x_ref, g_ref, sc_ref, sh_ref, w_ref, ws_ref, o_ref, os_ref, xn_ref):
    @pl.when(pl.program_id(1) == 0)
    def _():
        xn = _rms(x_ref[...]) * g_ref[...] * (1.0 + sc_ref[...]) + sh_ref[...]
        xn_ref[...] = xn.astype(MXU_DTYPE)
        os_ref[...] = _dot(xn_ref[...], ws_ref[...])

    o_ref[...] = _dot(xn_ref[...], w_ref[...]).astype(o_ref.dtype)


def _in_proj(x2, g, sc, sh, w_main, w_small, *, tm, rows_per_batch):
    m = x2.shape[0]
    tn = 2048
    tiles_per_batch = rows_per_batch // tm
    vec = pl.BlockSpec((None, 1, D_MODEL), lambda i, j: (i // tiles_per_batch, 0, 0))
    return pl.pallas_call(
        _inproj_kernel,
        grid=(m // tm, MAIN_W // tn),
        in_specs=[pl.BlockSpec((tm, D_MODEL), lambda i, j: (i, 0)),
                  pl.BlockSpec((1, D_MODEL), lambda i, j: (0, 0)),
                  vec, vec,
                  pl.BlockSpec((D_MODEL, tn), lambda i, j: (0, j)),
                  pl.BlockSpec((D_MODEL, LANES), lambda i, j: (0, 0))],
        out_specs=[pl.BlockSpec((tm, tn), lambda i, j: (i, j)),
                   pl.BlockSpec((tm, LANES), lambda i, j: (i, 0))],
        out_shape=[jax.ShapeDtypeStruct((m, MAIN_W), MXU_DTYPE), jax.ShapeDtypeStruct((m, LANES), F32)],
        scratch_shapes=[pltpu.VMEM((tm, D_MODEL), MXU_DTYPE)],
        compiler_params=_cparams(("parallel", "arbitrary")),
        name="in_proj",
    )(x2, g, sc, sh, w_main, w_small)


def _conv_kernel(x_ref, w_ref, b_ref, o_ref, *, rows, cols):
    scale = jnp.where(pl.program_id(1) * LANES >= QK_W, QK ** -0.5, 1.0).astype(F32)
    w = w_ref[...]
    bias = b_ref[...]
    tpos = lax.broadcasted_iota(jnp.int32, (cols, 1), 0)
    has_left = tpos >= 1
    has_right = tpos < cols - 1

    def row_filters(j):
        tile = x_ref[pl.ds(pl.multiple_of(j * cols, cols), cols), :].astype(F32)
        left = jnp.where(has_left, pltpu.roll(tile, 1, axis=0), 0.0)
        right = jnp.where(has_right, pltpu.roll(tile, cols - 1, axis=0), 0.0)
        return [left * w[3 * i:3 * i + 1, :] + tile * w[3 * i + 1:3 * i + 2, :] + right * w[3 * i + 2:3 * i + 3, :]
                for i in range(3)]

    def finish(j, acc):
        o_ref[pl.ds(pl.multiple_of(j * cols, cols), cols), :] = (_silu(acc + bias) * scale).astype(o_ref.dtype)

    first = row_filters(0)

    def body(j, carry):
        acc, below = carry
        h = row_filters(j)
        finish(j - 1, acc + h[2])
        return below + h[1], h[0]

    acc, _ = lax.fori_loop(1, rows, body, (first[1], first[0]))
    finish(rows - 1, acc)


def _conv_silu(main, conv_w, conv_b, *, batch, rows, cols):
    t = rows * cols
    nct = 2 * QK_W // LANES
    return pl.pallas_call(
        functools.partial(_conv_kernel, rows=rows, cols=cols),
        grid=(batch, nct),
        in_specs=[pl.BlockSpec((t, LANES), lambda b, c: (b, c)),
                  pl.BlockSpec((9, LANES), lambda b, c: (0, c)),
                  pl.BlockSpec((1, LANES), lambda b, c: (0, c))],
        out_specs=pl.BlockSpec((t, LANES), lambda b, c: (b, c)),
        out_shape=jax.ShapeDtypeStruct((batch * t, 2 * QK_W), MXU_DTYPE),
        compiler_params=_cparams(("parallel", "arbitrary")),
        name="conv_silu",
    )(main, conv_w, conv_b)


def _chunk_masks(direction):
    row = lax.broadcasted_iota(jnp.int32, (CHUNK, CHUNK), 0)
    col = lax.broadcasted_iota(jnp.int32, (CHUNK, CHUNK), 1)
    seen = (row >= col) if direction == 0 else (row <= col)
    return seen, seen.astype(MXU_DTYPE)


def _scan_specs(batch, nc, col_blocks, widths):
    specs = []
    for direction in (0, 1):
        for cb, wd in zip(col_blocks, widths):
            if direction == 0:
                specs.append(pl.BlockSpec((batch, CHUNK, wd), lambda c, cb=cb: (0, c, cb)))
            else:
                specs.append(pl.BlockSpec((batch, CHUNK, wd), lambda c, cb=cb: (0, nc - 1 - c, cb)))
    return specs


def _scan_out_specs(batch, nc):
    return [pl.BlockSpec((batch, CHUNK, V_W), lambda c: (0, c, 0)),
            pl.BlockSpec((batch, CHUNK, V_W), lambda c: (0, nc - 1 - c, 0))]


def _whole(shape):
    nd = len(shape)
    return pl.BlockSpec(tuple(shape), lambda c: (0,) * nd)


def _lane_tile(x, width):
    return jnp.concatenate([x] * (width // LANES), axis=-1)


def _mlstm_kernel(*refs, with_output, batch):
    (qk_f, v_f, sm_f, qk_b, v_b, sm_b, bias_ref, cn0_ref, m0_ref) = refs[:9]
    if with_output:
        hf_ref, hb_ref, cn_s, m_s = refs[9:]
    else:
        cn_out, m_out, cn_s, m_s = refs[9:]
    step = pl.program_id(0)

    @pl.when(step == 0)
    def _():
        cn_s[...] = cn0_ref[...]
        m_s[...] = m0_ref[...]

    lane = lax.broadcasted_iota(jnp.int32, (1, LANES), 1)
    gate_lane = jnp.logical_and(lane >= SMALL_GATE0, lane < SMALL_GATE0 + 4 * HEADS)
    forget_lane = jnp.logical_and(gate_lane, ((lane - SMALL_GATE0) % (2 * HEADS)) >= HEADS)
    eye = (lax.broadcasted_iota(jnp.int32, (LANES, LANES), 0)
           == lax.broadcasted_iota(jnp.int32, (LANES, LANES), 1)).astype(MXU_DTYPE)
    ones_cols = jnp.ones((CHUNK, LANES), MXU_DTYPE)

    for direction, (qk_ref, v_ref, sm_ref) in enumerate(((qk_f, v_f, sm_f), (qk_b, v_b, sm_b))):
        seen, seen01 = _chunk_masks(direction)
        last = CHUNK - 1 if direction == 0 else 0
        for bi in range(batch):
            g = sm_ref[bi] + bias_ref[...]
            gp = jnp.where(forget_lane, _log_sigmoid(g), g)
            bc = _dot_exact_lhs(seen01, gp)
            hi, mid, lo = _split3(gp)
            gp_t = _dot_nt(eye, hi) + _dot_nt(eye, mid) + _dot_nt(eye, lo)
            hi, mid, lo = _split3(bc)
            bc_t = _dot_nt(eye, hi) + _dot_nt(eye, mid) + _dot_nt(eye, lo)
            bend_row = bc[last:last + 1, :]
            heads = []
            for h in range(HEADS):
                ji = SMALL_GATE0 + direction * 2 * HEADS + h
                jf = ji + HEADS
                c = dict(ji=ji, jf=jf)
                c["q"] = qk_ref[bi, :, h * QK:(h + 1) * QK].astype(F32)
                c["k"] = qk_ref[bi, :, QK_W + h * QK:QK_W + (h + 1) * QK].astype(F32)
                c["v_ext"] = jnp.concatenate([v_ref[bi, :, h * DV:(h + 1) * DV].astype(MXU_DTYPE), ones_cols], axis=-1)
                c["cn_old"] = cn_s[bi, direction, h]
                c["m_old"] = m_s[bi, direction, h]
                heads.append(c)
            if with_output:
                for c in heads:
                    c["qk"] = _dot_nt(c["q"], c["k"])
                    c["qc"] = _dot(c["q"], c["cn_old"])
            for c in heads:
                b_end = jnp.broadcast_to(bend_row[:, c["jf"]:c["jf"] + 1], (1, LANES))
                i_col = jnp.broadcast_to(gp[:, c["ji"]:c["ji"] + 1], (CHUNK, LANES))
                c["b_col"] = jnp.broadcast_to(bc[:, c["jf"]:c["jf"] + 1], (CHUNK, LANES))
                log_w = b_end - c["b_col"] + i_col
                c["m_new"] = jnp.maximum(b_end + c["m_old"], jnp.max(log_w, axis=0, keepdims=True))
                c["kw"] = (c["k"] * jnp.exp(log_w - c["m_new"])).astype(MXU_DTYPE)
                c["decay"] = jnp.exp(b_end + c["m_old"] - c["m_new"])
            for c in heads:
                c["kw_t"] = _transpose_mxu(c["kw"])
            if with_output:
                for c in heads:
                    i_row = gp_t[c["ji"]:c["ji"] + 1, :]
                    b_row = bc_t[c["jf"]:c["jf"] + 1, :]
                    log_d = jnp.where(seen, _lane_tile(c["b_col"], CHUNK) - b_row + i_row, -jnp.inf)
                    log_inter = c["b_col"] + c["m_old"]
                    c["m_t"] = jnp.maximum(log_inter, jnp.max(log_d, axis=-1, keepdims=True))
                    c["s"] = (c["qk"] * jnp.exp(log_d - _lane_tile(c["m_t"], CHUNK))).astype(MXU_DTYPE)
                    c["w_inter"] = jnp.exp(log_inter - c["m_t"])
                for c in heads:
                    c["sv"] = _dot(c["s"], c["v_ext"])
            for h, c in enumerate(heads):
                cn_s[bi, direction, h] = _lane_tile(c["decay"], DV + LANES) * c["cn_old"] + _dot(c["kw_t"], c["v_ext"])
                m_s[bi, direction, h] = c["m_new"]
            outs = []
            if with_output:
                for c in heads:
                    sv, qc, w_inter = c["sv"], c["qc"], c["w_inter"]
                    num = sv[:, :DV] + _lane_tile(w_inter, DV) * qc[:, :DV]
                    den = jnp.abs(sv[:, DV:] + w_inter * qc[:, DV:])
                    outs.append(num / _lane_tile(jnp.maximum(den, jnp.exp(-c["m_t"])), DV))
            if with_output:
                (hf_ref if direction == 0 else hb_ref)[bi] = jnp.concatenate(outs, axis=-1)

    if not with_output:
        @pl.when(step == pl.num_programs(0) - 1)
        def _():
            cn_out[...] = cn_s[...]
            m_out[...] = m_s[...]


def _mlstm_scan(qk, main, small, gate_bias, states, *, with_output):
    cn0, m0 = states
    batch, t, _ = qk.shape
    nc = t // CHUNK
    in_specs = _scan_specs(batch, nc, (0, COL_V_M, 0), (2 * QK_W, V_W, LANES))
    in_specs += [_whole(gate_bias.shape), _whole(cn0.shape), _whole(m0.shape)]
    if with_output:
        out_specs = _scan_out_specs(batch, nc)
        out_shape = [jax.ShapeDtypeStruct((batch, t, V_W), F32)] * 2
    else:
        out_specs = [_whole(cn0.shape), _whole(m0.shape)]
        out_shape = [jax.ShapeDtypeStruct(s.shape, F32) for s in states]
    return pl.pallas_call(
        functools.partial(_mlstm_kernel, with_output=with_output, batch=batch),
        grid=(nc,),
        in_specs=in_specs,
        out_specs=out_specs,
        out_shape=out_shape,
        scratch_shapes=[pltpu.VMEM(cn0.shape, F32), pltpu.VMEM(m0.shape, F32)],
        compiler_params=_cparams(("arbitrary",)),
        name="mlstm_scan_out" if with_output else "mlstm_scan_state",
    )(qk, main, small, qk, main, small, gate_bias, cn0, m0)


def _gla_exact_intra(q, k, v, b, direction):
    row_id = lax.broadcasted_iota(jnp.int32, (CHUNK, 1), 0)

    def row(t, acc):
        pick = row_id == t
        b_t = jnp.sum(jnp.where(pick, b, 0.0), axis=0, keepdims=True)
        q_t = jnp.sum(jnp.where(pick, q, 0.0), axis=0, keepdims=True)
        ok = (row_id <= t) if direction == 0 else (row_id >= t)
        e = jnp.exp(jnp.where(ok, b_t - b, -jnp.inf))
        sc = jnp.sum(q_t * k * e, axis=-1, keepdims=True)
        o_t = jnp.sum(sc * v, axis=0, keepdims=True)
        return jnp.where(pick, o_t, acc)

    return lax.fori_loop(0, CHUNK, row, jnp.zeros((CHUNK, DV), F32))


def _gla_kernel(*refs, with_output, batch):
    (qk_f, v_f, sm_f, qk_b, v_b, sm_b, w2_ref, b2_ref, s0_ref) = refs[:9]
    if with_output:
        of_ref, ob_ref, s_s, b_s, inter_s = refs[9:]
    else:
        s_out, s_s = refs[9:]
    step = pl.program_id(0)

    @pl.when(step == 0)
    def _():
        s_s[...] = s0_ref[...]

    worst_decay = []
    for direction, (qk_ref, v_ref, sm_ref) in enumerate(((qk_f, v_f, sm_f), (qk_b, v_b, sm_b))):
        seen, seen01 = _chunk_masks(direction)
        last = CHUNK - 1 if direction == 0 else 0
        for bi in range(batch):
            z = _dot(sm_ref[bi], w2_ref[direction]) + b2_ref[direction]
            log_a = _log_sigmoid(z) * (1.0 / GLA_TAU)
            b_all = _dot_exact_lhs(seen01, log_a)
            outs, inters = [], []
            for h in range(HEADS):
                q = qk_ref[bi, :, h * QK:(h + 1) * QK].astype(F32) * (QK ** -0.5)
                k = qk_ref[bi, :, QK_W + h * QK:QK_W + (h + 1) * QK].astype(F32)
                v = v_ref[bi, :, h * DV:(h + 1) * DV]
                b = b_all[:, h * QK:(h + 1) * QK]
                b_end = b[last:last + 1, :]
                st_old = s_s[bi, direction, h]
                k_dec = k * jnp.exp(b_end - b)
                s_s[bi, direction, h] = st_old * jnp.exp(b_end) + _dot_tn_xlu(v, k_dec)
                if with_output:
                    q_dec = q * jnp.exp(b)
                    inter = _dot_nt(q_dec, st_old)
                    scores = jnp.where(seen, _dot_nt(q_dec, k * jnp.exp(-b)), 0.0)
                    outs.append(_dot(scores, v) + inter)
                    inters.append(inter)
            if with_output:
                (of_ref if direction == 0 else ob_ref)[bi] = jnp.concatenate(outs, axis=-1)
                b_s[bi, direction] = b_all
                inter_s[bi, direction] = jnp.concatenate(inters, axis=-1)
                worst_decay.append(jnp.max(-b_all[last:last + 1, :]))

    if with_output:
        @pl.when(functools.reduce(jnp.maximum, worst_decay) > GLA_SAFE_DECAY)
        def _():
            for direction, (qk_ref, v_ref, o_ref) in enumerate(((qk_f, v_f, of_ref), (qk_b, v_b, ob_ref))):
                last = CHUNK - 1 if direction == 0 else 0
                for bi in range(batch):
                    for h in range(HEADS):
                        b = b_s[bi, direction, :, h * QK:(h + 1) * QK]

                        @pl.when(jnp.max(-b[last:last + 1, :]) > GLA_SAFE_DECAY)
                        def _():
                            q = qk_ref[bi, :, h * QK:(h + 1) * QK].astype(F32) * (QK ** -0.5)
                            k = qk_ref[bi, :, QK_W + h * QK:QK_W + (h + 1) * QK].astype(F32)
                            v = v_ref[bi, :, h * DV:(h + 1) * DV].astype(F32)
                            o_ref[bi, :, h * DV:(h + 1) * DV] = (
                                inter_s[bi, direction, :, h * DV:(h + 1) * DV]
                                + _gla_exact_intra(q, k, v, b, direction))
    else:
        @pl.when(step == pl.num_programs(0) - 1)
        def _():
            s_out[...] = s_s[...]


def _gla_scan(main, small, w2p, b2, s0, *, with_output):
    batch, t, _ = main.shape
    nc = t // CHUNK
    in_specs = _scan_specs(batch, nc, (COL_QK_G, COL_V_G, 0), (2 * QK_W, V_W, LANES))
    in_specs += [_whole(w2p.shape), _whole(b2.shape), _whole(s0.shape)]
    scratch = [pltpu.VMEM(s0.shape, F32)]
    if with_output:
        out_specs = _scan_out_specs(batch, nc)
        out_shape = [jax.ShapeDtypeStruct((batch, t, V_W), F32)] * 2
        scratch += [pltpu.VMEM((batch, 2, CHUNK, QK_W), F32), pltpu.VMEM((batch, 2, CHUNK, V_W), F32)]
    else:
        out_specs = _whole(s0.shape)
        out_shape = jax.ShapeDtypeStruct(s0.shape, F32)
    return pl.pallas_call(
        functools.partial(_gla_kernel, with_output=with_output, batch=batch),
        grid=(nc,),
        in_specs=in_specs,
        out_specs=out_specs,
        out_shape=out_shape,
        scratch_shapes=scratch,
        compiler_params=_cparams(("arbitrary",)),
        name="gla_scan_out" if with_output else "gla_scan_state",
    )(main, main, small, main, main, small, w2p, b2, s0)


def _head_rms(a):
    return jnp.concatenate([_rms(a[:, h * DV:(h + 1) * DV]) for h in range(HEADS)], axis=-1)


def _merge_kernel(hmf, hmb, ogf, ogb, om, rg, mgm, mgg, x_ref, mln, gln, wpm, wpg, wo, gt1, g2, sc2, sh2,
                  wr, br, x1_ref, h2_ref, wt_ref):
    y_m = _head_rms(hmf[...] + hmb[...]) * mln[...] * jax.nn.sigmoid(om[...].astype(F32))
    y_g = _head_rms(ogf[...] + ogb[...]) * gln[...] * _silu(rg[...].astype(F32))
    y = (jax.nn.sigmoid(mgm[...].astype(F32)) * _dot(y_m, wpm[...])
         + jax.nn.sigmoid(mgg[...].astype(F32)) * _dot(y_g, wpg[...]))
    x1 = x_ref[...] + gt1[...] * _dot(y, wo[...])
    x1_ref[...] = x1
    h2 = _rms(x1) * g2[...] * (1.0 + sc2[...]) + sh2[...]
    h2_ref[...] = h2.astype(MXU_DTYPE)

    hh, hm_, _ = _split3(h2)
    wh, wm_, _ = _split3(wr[...])
    lg = _dot(hh, wh) + _dot(hh, wm_) + _dot(hm_, wh) + br[...]
    lane = lax.broadcasted_iota(jnp.int32, lg.shape, 1)

    def masked_softmax(mask):
        l = jnp.where(mask, lg, -jnp.inf)
        e = jnp.exp(l - jnp.max(l, axis=-1, keepdims=True))
        return e / jnp.sum(e, axis=-1, keepdims=True)

    def top1(p, mask):
        pm = jnp.where(mask, p, -1.0)
        best = jnp.max(pm, axis=-1, keepdims=True)
        idx = jnp.min(jnp.where(jnp.logical_and(mask, pm == best), lane, LANES), axis=-1, keepdims=True)
        return best, idx

    gmask = lane < N_GROUPS
    grp_p, grp = top1(masked_softmax(gmask), gmask)
    e_lo = ROUTE_E0 + grp * EXPERTS_PER_GROUP
    emask = jnp.logical_and(lane >= e_lo, lane < e_lo + EXPERTS_PER_GROUP)
    p_in = masked_softmax(emask)
    p1, i1 = top1(p_in, emask)
    p2, i2 = top1(p_in, jnp.logical_and(emask, lane != i1))
    tot = p1 + p2
    wt_ref[...] = (jnp.where(lane == i1, grp_p * p1 / tot, 0.0)
                   + jnp.where(lane == i2, grp_p * p2 / tot, 0.0)
                   + jnp.where(lane == grp, 1.0, 0.0))


def _merge(hmf, hmb, ogf, ogb, main, x2, mln, gln, wpm, wpg, wo, gt1, g2, sc2, sh2, wr, br, *, tm, rows_per_batch):
    m = x2.shape[0]
    tpb = rows_per_batch // tm
    rowblk = pl.BlockSpec((tm, D_MODEL), lambda i: (i, 0))
    colblk = lambda cb: pl.BlockSpec((tm, D_MODEL), lambda i, cb=cb: (i, cb))
    vec = pl.BlockSpec((1, D_MODEL), lambda i: (0, 0))
    bvec = pl.BlockSpec((None, 1, D_MODEL), lambda i: (i // tpb, 0, 0))
    wmat = pl.BlockSpec((D_MODEL, D_MODEL), lambda i: (0, 0))
    return pl.pallas_call(
        _merge_kernel,
        grid=(m // tm,),
        in_specs=[rowblk, rowblk, rowblk, rowblk, colblk(COL_O_M), colblk(COL_R_G), colblk(COL_MG_M),
                  colblk(COL_MG_G), rowblk, vec, vec, wmat, wmat, wmat, bvec, vec, bvec, bvec,
                  pl.BlockSpec((D_MODEL, LANES), lambda i: (0, 0)), pl.BlockSpec((1, LANES), lambda i: (0, 0))],
        out_specs=[rowblk, rowblk, pl.BlockSpec((tm, LANES), lambda i: (i, 0))],
        out_shape=[jax.ShapeDtypeStruct((m, D_MODEL), F32), jax.ShapeDtypeStruct((m, D_MODEL), MXU_DTYPE),
                   jax.ShapeDtypeStruct((m, LANES), F32)],
        compiler_params=_cparams(("parallel",)),
        name="merge_route",
    )(hmf, hmb, ogf, ogb, main, main, main, main, x2, mln, gln, wpm, wpg, wo, gt1, g2, sc2, sh2, wr, br)


MOE_BLK = 64
MOE_COMMON_BLKS = (4, 5)
MOE_EXPERTS_PER_STEP = 2


def _moe_kernel(h2_ref, wt_ref, wup_ref, wdn_ref, x1_ref, gt2_ref, gf_ref, o_ref,
                xs_ref, ys_ref, ws_ref, dest_ref, blk_ref):
    step = pl.program_id(1)
    tm = h2_ref.shape[0]
    n_rows = xs_ref.shape[0]

    @pl.when(step == 0)
    def _():
        r = wt_ref[...]
        lane = lax.broadcasted_iota(jnp.int32, (tm, LANES), 1)
        lane1 = lax.broadcasted_iota(jnp.int32, (1, LANES), 1)
        gm = jnp.where(lane < N_GROUPS, r, 0.0)
        earlier = (lax.broadcasted_iota(jnp.int32, (tm, tm), 1)
                   < lax.broadcasted_iota(jnp.int32, (tm, tm), 0)).astype(MXU_DTYPE)
        before = _dot(earlier, gm)
        padded = jnp.floor((jnp.sum(gm, axis=0, keepdims=True) + (MOE_BLK - 1)) * (1.0 / MOE_BLK)) * MOE_BLK
        start = jnp.zeros((1, LANES), F32)
        run = jnp.zeros((1, 1), F32)
        for g in range(N_GROUPS):
            size = jnp.sum(jnp.where(lane1 == g, padded, 0.0), axis=-1, keepdims=True)
            start = jnp.where(lane1 == g, run, start)
            blk_ref[g] = (jnp.sum(run) * (1.0 / MOE_BLK)).astype(jnp.int32)
            blk_ref[N_GROUPS + g] = (jnp.sum(size) * (1.0 / MOE_BLK)).astype(jnp.int32)
            run = run + size
        dest = jnp.sum(gm * (start + before), axis=-1, keepdims=True)
        dest_ref[...] = jnp.broadcast_to(dest, (tm, LANES))
        dest_row = dest_ref[...].T[0:1, :].astype(jnp.int32)
        perm = (lax.broadcasted_iota(jnp.int32, (n_rows, tm), 0) == dest_row).astype(MXU_DTYPE)
        xs_ref[...] = _dot(perm, h2_ref[...]).astype(MXU_DTYPE)
        moved = _dot(perm, jnp.concatenate(_split3(r), axis=-1))
        ws_ref[...] = moved[:, :LANES] + moved[:, LANES:2 * LANES] + moved[:, 2 * LANES:]
        ys_ref[...] = jnp.zeros_like(ys_ref)

    group = step // (EXPERTS_PER_GROUP // MOE_EXPERTS_PER_STEP)
    first_blk = blk_ref[group]
    n_blk = blk_ref[N_GROUPS + group]

    def expert_on(r0, rows):
        x = xs_ref[pl.ds(r0, rows), :]
        ws = ws_ref[pl.ds(r0, rows), :]
        lane_b = lax.broadcasted_iota(jnp.int32, (rows, LANES), 1)
        experts = range(MOE_EXPERTS_PER_STEP)
        gus = [_dot(x, wup_ref[k]) for k in experts]
        hiddens = [(_silu(gu[:, :D_EXPERT]) * gu[:, D_EXPERT:]).astype(MXU_DTYPE) for gu in gus]
        w_cols = [jnp.sum(jnp.where(lane_b == ROUTE_E0 + step * MOE_EXPERTS_PER_STEP + k, ws, 0.0),
                          axis=-1, keepdims=True) for k in experts]
        ys = [_dot(hiddens[k], wdn_ref[k]) for k in experts]
        ys_ref[pl.ds(r0, rows), :] += sum(y * w for y, w in zip(ys, w_cols))

    for k in MOE_COMMON_BLKS:
        @pl.when(n_blk == k)
        def _():
            expert_on(pl.multiple_of(first_blk * MOE_BLK, MOE_BLK), k * MOE_BLK)

    @pl.when(functools.reduce(jnp.logical_and, [n_blk != k for k in MOE_COMMON_BLKS]))
    def _():
        def block(j, carry):
            expert_on(pl.multiple_of((first_blk + j) * MOE_BLK, MOE_BLK), MOE_BLK)
            return carry

        lax.fori_loop(0, n_blk, block, 0)

    @pl.when(step == pl.num_programs(1) - 1)
    def _():
        dest = dest_ref[...][:, :1].astype(jnp.int32)
        unperm = (lax.broadcasted_iota(jnp.int32, (tm, n_rows), 1) == dest).astype(MXU_DTYPE)
        y = _dot(unperm, ys_ref[...])
        o_ref[...] = _rms(x1_ref[...] + gt2_ref[...] * y) * gf_ref[...]


def _moe_final(h2, wt, w_up, w_down, x1, gt2, g_final, *, tm, rows_per_batch):
    m = h2.shape[0]
    tpb = rows_per_batch // tm
    n_rows = tm + N_GROUPS * MOE_BLK
    rowblk = pl.BlockSpec((tm, D_MODEL), lambda i, e: (i, 0))
    return pl.pallas_call(
        _moe_kernel,
        grid=(m // tm, N_EXPERTS // MOE_EXPERTS_PER_STEP),
        in_specs=[rowblk,
                  pl.BlockSpec((tm, LANES), lambda i, e: (i, 0)),
                  pl.BlockSpec((MOE_EXPERTS_PER_STEP, D_MODEL, 2 * D_EXPERT), lambda i, e: (e, 0, 0)),
                  pl.BlockSpec((MOE_EXPERTS_PER_STEP, D_EXPERT, D_MODEL), lambda i, e: (e, 0, 0)),
                  rowblk,
                  pl.BlockSpec((None, 1, D_MODEL), lambda i, e: (i // tpb, 0, 0)),
                  pl.BlockSpec((1, D_MODEL), lambda i, e: (0, 0))],
        out_specs=rowblk,
        out_shape=jax.ShapeDtypeStruct((m, D_MODEL), F32),
        scratch_shapes=[pltpu.VMEM((n_rows, D_MODEL), MXU_DTYPE), pltpu.VMEM((n_rows, D_MODEL), F32),
                        pltpu.VMEM((n_rows, LANES), F32), pltpu.VMEM((tm, LANES), F32),
                        pltpu.SMEM((2 * N_GROUPS,), jnp.int32)],
        compiler_params=_cparams(("parallel", "arbitrary"), 56 * 1024 * 1024),
        name="moe_final",
    )(h2, wt, w_up, w_down, x1, gt2, g_final)


def _empty_states(batch):
    ml = (jnp.zeros((batch, 2, HEADS, QK, DV + LANES), F32),
          jnp.full((batch, 2, HEADS, 1, LANES), NEG_BIG, F32))
    gla = jnp.zeros((batch, 2, HEADS, DV, QK), F32)
    return ml, gla


def kernel(x, c, ctx, c_ctx, w_mod, b_mod, g_norm1, w_in, ml_conv, ml_conv_b, b_mgate, ml_norm, gla_w2, gla_b2,
           gla_norm, w_proj_m, w_proj_g, w_out, g_norm2, w_grp, b_grp, w_rexp, b_rexp, w_up, w_down, g_final):
    batch, t, d = x.shape
    t_ctx = ctx.shape[1]
    assert d == D_MODEL and w_mod.shape[0] == 1 and w_in.shape[2] == sum(IN_SIZES)
    assert t % (GRID_W * 16) == 0 and t % CHUNK == 0 and t_ctx % CHUNK == 0

    off = [0]
    for s in IN_SIZES:
        off.append(off[-1] + s)
    wi = w_in[0].astype(MXU_DTYPE)
    w_main = jnp.concatenate([wi[:, off[0]:off[4]], wi[:, off[5]:off[9]], wi[:, off[10]:off[12]]], axis=1)
    w_small = jnp.concatenate([wi[:, off[9]:off[10]], wi[:, off[4]:off[5]],
                               jnp.zeros((d, LANES - 2 * GLA_RANK - 4 * HEADS), MXU_DTYPE)], axis=1)
    gate_bias = jnp.zeros((1, LANES), F32).at[0, SMALL_GATE0:SMALL_GATE0 + 4 * HEADS].set(b_mgate[0])
    w2p = jnp.zeros((2, LANES, QK_W), F32)
    w2p = w2p.at[0, 0:GLA_RANK].set(gla_w2[0, 0]).at[1, GLA_RANK:2 * GLA_RANK].set(gla_w2[0, 1])
    b2 = gla_b2[0][:, None, :]
    conv_w = ml_conv[0].reshape(9, 2 * QK_W)
    conv_b = ml_conv_b[0][None, :]
    w_route = jnp.concatenate([w_grp[0], w_rexp[0], jnp.zeros((d, LANES - N_GROUPS - N_EXPERTS), F32)], axis=1)
    b_route = jnp.concatenate([b_grp[0], b_rexp[0], jnp.zeros((LANES - N_GROUPS - N_EXPERTS,), F32)])[None, :]

    cc = jnp.concatenate([c, c_ctx[None, :], jnp.zeros((8 - batch - 1, d), F32)], axis=0)
    mod = _modulation(cc, w_mod[0], b_mod[0][None, :])
    sh1, sc1, gt1, sh2, sc2, gt2 = [mod[:batch, i * d:(i + 1) * d][:, None, :] for i in range(6)]
    sh1c, sc1c = [jnp.broadcast_to(mod[batch:batch + 1, i * d:(i + 1) * d][:, None, :], (batch, 1, d)) for i in range(2)]
    g1 = g_norm1[0][None, :]

    main_c, small_c = _in_proj(ctx.reshape(batch * t_ctx, d), g1, sc1c, sh1c, w_main, w_small,
                               tm=t_ctx, rows_per_batch=t_ctx)
    qk_c = _conv_silu(main_c, conv_w, conv_b, batch=batch, rows=1, cols=t_ctx)
    ml0, gla0 = _empty_states(batch)
    main_c3, small_c3 = main_c.reshape(batch, t_ctx, MAIN_W), small_c.reshape(batch, t_ctx, LANES)
    ml_states = _mlstm_scan(qk_c.reshape(batch, t_ctx, 2 * QK_W), main_c3, small_c3, gate_bias, ml0, with_output=False)
    gla_state = _gla_scan(main_c3, small_c3, w2p, b2, gla0, with_output=False)

    x2 = x.reshape(batch * t, d)
    main, small = _in_proj(x2, g1, sc1, sh1, w_main, w_small, tm=1024, rows_per_batch=t)
    qk = _conv_silu(main, conv_w, conv_b, batch=batch, rows=t // GRID_W, cols=GRID_W)
    main3, small3 = main.reshape(batch, t, MAIN_W), small.reshape(batch, t, LANES)
    hm_f, hm_b = [a.reshape(batch * t, V_W) for a in
                  _mlstm_scan(qk.reshape(batch, t, 2 * QK_W), main3, small3, gate_bias, ml_states, with_output=True)]
    og_f, og_b = [a.reshape(batch * t, V_W) for a in _gla_scan(main3, small3, w2p, b2, gla_state, with_output=True)]

    x1, h2, wt = _merge(hm_f, hm_b, og_f, og_b, main, x2, ml_norm, gla_norm,
                        w_proj_m[0].astype(MXU_DTYPE), w_proj_g[0].astype(MXU_DTYPE), w_out[0].astype(MXU_DTYPE),
                        gt1, g_norm2, sc2, sh2, w_route, b_route, tm=256, rows_per_batch=t)
    out = _moe_final(h2, wt, w_up[0].astype(MXU_DTYPE), w_down[0].astype(MXU_DTYPE), x1, gt2, g_final[None, :],
                     tm=1024, rows_per_batch=t)
    return out.reshape(batch, t, d)
```

```python
import functools

import jax
import jax.numpy as jnp
from jax import lax
from jax.experimental import pallas as pl
from jax.experimental.pallas import tpu as pltpu

D_MODEL = 1024
GRID_W = 64
CHUNK = 256
EPS = 1e-6
NEG_BIG = -1e30
HEADS = 4
QK = D_MODEL // 8
DV = D_MODEL // 4
QK_W = HEADS * QK
V_W = HEADS * DV
GLA_RANK = 16
GLA_TAU = 16.0
N_GROUPS = 4
EXPERTS_PER_GROUP = 4
N_EXPERTS = N_GROUPS * EXPERTS_PER_GROUP
D_EXPERT = D_MODEL // 2
IN_SIZES = (QK_W, QK_W, V_W, V_W, 4 * HEADS, QK_W, QK_W, V_W, V_W, 2 * GLA_RANK, D_MODEL, D_MODEL)

LANES = 128
MXU_DTYPE = jnp.bfloat16
F32 = jnp.float32
VMEM_LIMIT = 48 * 1024 * 1024

COL_QK_M, COL_V_M, COL_O_M, COL_QK_G, COL_V_G, COL_R_G, COL_MG_M, COL_MG_G = range(8)
MAIN_W = 8 * D_MODEL
SMALL_GATE0 = 2 * GLA_RANK
GLA_SAFE_DECAY = 80.0
ROUTE_E0 = N_GROUPS


def _dot(a, b):
    return jnp.dot(a.astype(MXU_DTYPE), b.astype(MXU_DTYPE), preferred_element_type=F32)


def _dot_nt(a, b):
    return lax.dot_general(a.astype(MXU_DTYPE), b.astype(MXU_DTYPE), (((1,), (1,)), ((), ())),
                           preferred_element_type=F32)


def _transpose_mxu(a):
    m = a.shape[1]
    eye = (lax.broadcasted_iota(jnp.int32, (m, m), 0) == lax.broadcasted_iota(jnp.int32, (m, m), 1))
    return _dot_nt(eye.astype(MXU_DTYPE), a).astype(MXU_DTYPE)


def _dot_tn_xlu(a, b):
    return lax.dot_general(a.astype(MXU_DTYPE), b.astype(MXU_DTYPE), (((0,), (0,)), ((), ())),
                           preferred_element_type=F32)


def _split3(x):
    hi = x.astype(MXU_DTYPE)
    r1 = x - hi.astype(F32)
    mid = r1.astype(MXU_DTYPE)
    lo = (r1 - mid.astype(F32)).astype(MXU_DTYPE)
    return hi, mid, lo


def _dot_exact_lhs(a01, x):
    hi, mid, lo = _split3(x)
    return _dot(a01, hi) + _dot(a01, mid) + _dot(a01, lo)


def _log_sigmoid(x):
    return jnp.minimum(x, 0.0) - jnp.log(1.0 + jnp.exp(-jnp.abs(x)))


def _silu(x):
    return x * jax.nn.sigmoid(x)


def _rms(x):
    return x * lax.rsqrt(jnp.mean(x * x, axis=-1, keepdims=True) + EPS)


def _cparams(sem, vmem_limit=VMEM_LIMIT):
    return pltpu.CompilerParams(dimension_semantics=sem, vmem_limit_bytes=vmem_limit)


def _mod_kernel(c_ref, w_ref, b_ref, o_ref):
    o_ref[...] = _dot(_silu(c_ref[...]), w_ref[...]) + b_ref[...]


def _modulation(cc, w_mod, b_mod):
    n = w_mod.shape[1]
    tn = 512
    return pl.pallas_call(
        _mod_kernel,
        grid=(n // tn,),
        in_specs=[pl.BlockSpec((8, D_MODEL), lambda j: (0, 0)),
                  pl.BlockSpec((D_MODEL, tn), lambda j: (0, j)),
                  pl.BlockSpec((1, tn), lambda j: (0, j))],
        out_specs=pl.BlockSpec((8, tn), lambda j: (0, j)),
        out_shape=jax.ShapeDtypeStruct((8, n), F32),
        compiler_params=_cparams(("arbitrary",)),
        name="modulation",
    )(cc, w_mod, b_mod)


def _inproj_kernel(x_ref, g_ref, sc_ref, sh_ref, w_ref, ws_ref, o_ref, os_ref, xn_ref):
    @pl.when(pl.program_id(1) == 0)
    def _():
        xn = _rms(x_ref[...]) * g_ref[...] * (1.0 + sc_ref[...]) + sh_ref[...]
        xn_ref[...] = xn.astype(MXU_DTYPE)
        os_ref[...] = _dot(xn_ref[...], ws_ref[...])

    o_ref[...] = _dot(xn_ref[...], w_ref[...]).astype(o_ref.dtype)


def _in_proj(x2, g, sc, sh, w_main, w_small, *, tm, rows_per_batch):
    m = x2.shape[0]
    tn = 2048
    tiles_per_batch = rows_per_batch // tm
    vec = pl.BlockSpec((None, 1, D_MODEL), lambda i, j: (i // tiles_per_batch, 0, 0))
    return pl.pallas_call(
        _inproj_kernel,
        grid=(m // tm, MAIN_W // tn),
        in_specs=[pl.BlockSpec((tm, D_MODEL), lambda i, j: (i, 0)),
                  pl.BlockSpec((1, D_MODEL), lambda i, j: (0, 0)),
                  vec, vec,
                  pl.BlockSpec((D_MODEL, tn), lambda i, j: (0, j)),
                  pl.BlockSpec((D_MODEL, LANES), lambda i, j: (0, 0))],
        out_specs=[pl.BlockSpec((tm, tn), lambda i, j: (i, j)),
                   pl.BlockSpec((tm, LANES), lambda i, j: (i, 0))],
        out_shape=[jax.ShapeDtypeStruct((m, MAIN_W), MXU_DTYPE), jax.ShapeDtypeStruct((m, LANES), F32)],
        scratch_shapes=[pltpu.VMEM((tm, D_MODEL), MXU_DTYPE)],
        compiler_params=_cparams(("parallel", "arbitrary")),
        name="in_proj",
    )(x2, g, sc, sh, w_main, w_small)


def _conv_kernel(x_ref, w_ref, b_ref, o_ref, *, rows, cols):
    scale = jnp.where(pl.program_id(1) * LANES >= QK_W, QK ** -0.5, 1.0).astype(F32)
    w = w_ref[...]
    bias = b_ref[...]
    tpos = lax.broadcasted_iota(jnp.int32, (cols, 1), 0)
    has_left = tpos >= 1
    has_right = tpos < cols - 1

    def row_filters(j):
        tile = x_ref[pl.ds(pl.multiple_of(j * cols, cols), cols), :].astype(F32)
        left = jnp.where(has_left, pltpu.roll(tile, 1, axis=0), 0.0)
        right = jnp.where(has_right, pltpu.roll(tile, cols - 1, axis=0), 0.0)
        return [left * w[3 * i:3 * i + 1, :] + tile * w[3 * i + 1:3 * i + 2, :] + right * w[3 * i + 2:3 * i + 3, :]
                for i in range(3)]

    def finish(j, acc):
        o_ref[pl.ds(pl.multiple_of(j * cols, cols), cols), :] = (_silu(acc + bias) * scale).astype(o_ref.dtype)

    first = row_filters(0)

    def body(j, carry):
        acc, below = carry
        h = row_filters(j)
        finish(j - 1, acc + h[2])
        return below + h[1], h[0]

    acc, _ = lax.fori_loop(1, rows, body, (first[1], first[0]))
    finish(rows - 1, acc)


def _conv_silu(main, conv_w, conv_b, *, batch, rows, cols):
    t = rows * cols
    nct = 2 * QK_W // LANES
    return pl.pallas_call(
        functools.partial(_conv_kernel, rows=rows, cols=cols),
        grid=(batch, nct),
        in_specs=[pl.BlockSpec((t, LANES), lambda b, c: (b, c)),
                  pl.BlockSpec((9, LANES), lambda b, c: (0, c)),
                  pl.BlockSpec((1, LANES), lambda b, c: (0, c))],
        out_specs=pl.BlockSpec((t, LANES), lambda b, c: (b, c)),
        out_shape=jax.ShapeDtypeStruct((batch * t, 2 * QK_W), MXU_DTYPE),
        compiler_params=_cparams(("parallel", "arbitrary")),
        name="conv_silu",
    )(main, conv_w, conv_b)


def _chunk_masks(direction):
    row = lax.broadcasted_iota(jnp.int32, (CHUNK, CHUNK), 0)
    col = lax.broadcasted_iota(jnp.int32, (CHUNK, CHUNK), 1)
    seen = (row >= col) if direction == 0 else (row <= col)
    return seen, seen.astype(MXU_DTYPE)


def _scan_specs(batch, nc, col_blocks, widths):
    specs = []
    for direction in (0, 1):
        for cb, wd in zip(col_blocks, widths):
            if direction == 0:
                specs.append(pl.BlockSpec((batch, CHUNK, wd), lambda c, cb=cb: (0, c, cb)))
            else:
                specs.append(pl.BlockSpec((batch, CHUNK, wd), lambda c, cb=cb: (0, nc - 1 - c, cb)))
    return specs


def _scan_out_specs(batch, nc):
    return [pl.BlockSpec((batch, CHUNK, V_W), lambda c: (0, c, 0)),
            pl.BlockSpec((batch, CHUNK, V_W), lambda c: (0, nc - 1 - c, 0))]


def _whole(shape):
    nd = len(shape)
    return pl.BlockSpec(tuple(shape), lambda c: (0,) * nd)


def _lane_tile(x, width):
    return jnp.concatenate([x] * (width // LANES), axis=-1)


def _mlstm_kernel(*refs, with_output, batch):
    (qk_f, v_f, sm_f, qk_b, v_b, sm_b, bias_ref, cn0_ref, m0_ref) = refs[:9]
    if with_output:
        hf_ref, hb_ref, cn_s, m_s = refs[9:]
    else:
        cn_out, m_out, cn_s, m_s = refs[9:]
    step = pl.program_id(0)

    @pl.when(step == 0)
    def _():
        cn_s[...] = cn0_ref[...]
        m_s[...] = m0_ref[...]

    lane = lax.broadcasted_iota(jnp.int32, (1, LANES), 1)
    gate_lane = jnp.logical_and(lane >= SMALL_GATE0, lane < SMALL_GATE0 + 4 * HEADS)
    forget_lane = jnp.logical_and(gate_lane, ((lane - SMALL_GATE0) % (2 * HEADS)) >= HEADS)
    eye = (lax.broadcasted_iota(jnp.int32, (LANES, LANES), 0)
           == lax.broadcasted_iota(jnp.int32, (LANES, LANES), 1)).astype(MXU_DTYPE)
    ones_cols = jnp.ones((CHUNK, LANES), MXU_DTYPE)

    for direction, (qk_ref, v_ref, sm_ref) in enumerate(((qk_f, v_f, sm_f), (qk_b, v_b, sm_b))):
        seen, seen01 = _chunk_masks(direction)
        last = CHUNK - 1 if direction == 0 else 0
        for bi in range(batch):
            g = sm_ref[bi] + bias_ref[...]
            gp = jnp.where(forget_lane, _log_sigmoid(g), g)
            bc = _dot_exact_lhs(seen01, gp)
            hi, mid, lo = _split3(gp)
            gp_t = _dot_nt(eye, hi) + _dot_nt(eye, mid) + _dot_nt(eye, lo)
            hi, mid, lo = _split3(bc)
            bc_t = _dot_nt(eye, hi) + _dot_nt(eye, mid) + _dot_nt(eye, lo)
            bend_row = bc[last:last + 1, :]
            heads = []
            for h in range(HEADS):
                ji = SMALL_GATE0 + direction * 2 * HEADS + h
                jf = ji + HEADS
                c = dict(ji=ji, jf=jf)
                c["q"] = qk_ref[bi, :, h * QK:(h + 1) * QK].astype(F32)
                c["k"] = qk_ref[bi, :, QK_W + h * QK:QK_W + (h + 1) * QK].astype(F32)
                c["v_ext"] = jnp.concatenate([v_ref[bi, :, h * DV:(h + 1) * DV].astype(MXU_DTYPE), ones_cols], axis=-1)
                c["cn_old"] = cn_s[bi, direction, h]
                c["m_old"] = m_s[bi, direction, h]
                heads.append(c)
            if with_output:
                for c in heads:
                    c["qk"] = _dot_nt(c["q"], c["k"])
                    c["qc"] = _dot(c["q"], c["cn_old"])
            for c in heads:
                b_end = jnp.broadcast_to(bend_row[:, c["jf"]:c["jf"] + 1], (1, LANES))
                i_col = jnp.broadcast_to(gp[:, c["ji"]:c["ji"] + 1], (CHUNK, LANES))
                c["b_col"] = jnp.broadcast_to(bc[:, c["jf"]:c["jf"] + 1], (CHUNK, LANES))
                log_w = b_end - c["b_col"] + i_col
                c["m_new"] = jnp.maximum(b_end + c["m_old"], jnp.max(log_w, axis=0, keepdims=True))
                c["kw"] = (c["k"] * jnp.exp(log_w - c["m_new"])).astype(MXU_DTYPE)
                c["decay"] = jnp.exp(b_end + c["m_old"] - c["m_new"])
            for c in heads:
                c["kw_t"] = _transpose_mxu(c["kw"])
            if with_output:
                for c in heads:
                    i_row = gp_t[c["ji"]:c["ji"] + 1, :]
                    b_row = bc_t[c["jf"]:c["jf"] + 1, :]
                    log_d = jnp.where(seen, _lane_tile(c["b_col"], CHUNK) - b_row + i_row, -jnp.inf)
                    log_inter = c["b_col"] + c["m_old"]
                    c["m_t"] = jnp.maximum(log_inter, jnp.max(log_d, axis=-1, keepdims=True))
                    c["s"] = (c["qk"] * jnp.exp(log_d - _lane_tile(c["m_t"], CHUNK))).astype(MXU_DTYPE)
                    c["w_inter"] = jnp.exp(log_inter - c["m_t"])
                for c in heads:
                    c["sv"] = _dot(c["s"], c["v_ext"])
            for h, c in enumerate(heads):
                cn_s[bi, direction, h] = _lane_tile(c["decay"], DV + LANES) * c["cn_old"] + _dot(c["kw_t"], c["v_ext"])
                m_s[bi, direction, h] = c["m_new"]
            outs = []
            if with_output:
                for c in heads:
                    sv, qc, w_inter = c["sv"], c["qc"], c["w_inter"]
                    num = sv[:, :DV] + _lane_tile(w_inter, DV) * qc[:, :DV]
                    den = jnp.abs(sv[:, DV:] + w_inter * qc[:, DV:])
                    outs.append(num / _lane_tile(jnp.maximum(den, jnp.exp(-c["m_t"])), DV))
            if with_output:
                (hf_ref if direction == 0 else hb_ref)[bi] = jnp.concatenate(outs, axis=-1)

    if not with_output:
        @pl.when(step == pl.num_programs(0) - 1)
        def _():
            cn_out[...] = cn_s[...]
            m_out[...] = m_s[...]


def _mlstm_scan(qk, main, small, gate_bias, states, *, with_output):
    cn0, m0 = states
    batch, t, _ = qk.shape
    nc = t // CHUNK
    in_specs = _scan_specs(batch, nc, (0, COL_V_M, 0), (2 * QK_W, V_W, LANES))
    in_specs += [_whole(gate_bias.shape), _whole(cn0.shape), _whole(m0.shape)]
    if with_output:
        out_specs = _scan_out_specs(batch, nc)
        out_shape = [jax.ShapeDtypeStruct((batch, t, V_W), F32)] * 2
    else:
        out_specs = [_whole(cn0.shape), _whole(m0.shape)]
        out_shape = [jax.ShapeDtypeStruct(s.shape, F32) for s in states]
    return pl.pallas_call(
        functools.partial(_mlstm_kernel, with_output=with_output, batch=batch),
        grid=(nc,),
        in_specs=in_specs,
        out_specs=out_specs,
        out_shape=out_shape,
        scratch_shapes=[pltpu.VMEM(cn0.shape, F32), pltpu.VMEM(m0.shape, F32)],
        compiler_params=_cparams(("arbitrary",)),
        name="mlstm_scan_out" if with_output else "mlstm_scan_state",
    )(qk, main, small, qk, main, small, gate_bias, cn0, m0)


def _gla_exact_intra(q, k, v, b, direction):
    row_id = lax.broadcasted_iota(jnp.int32, (CHUNK, 1), 0)

    def row(t, acc):
        pick = row_id == t
        b_t = jnp.sum(jnp.where(pick, b, 0.0), axis=0, keepdims=True)
        q_t = jnp.sum(jnp.where(pick, q, 0.0), axis=0, keepdims=True)
        ok = (row_id <= t) if direction == 0 else (row_id >= t)
        e = jnp.exp(jnp.where(ok, b_t - b, -jnp.inf))
        sc = jnp.sum(q_t * k * e, axis=-1, keepdims=True)
        o_t = jnp.sum(sc * v, axis=0, keepdims=True)
        return jnp.where(pick, o_t, acc)

    return lax.fori_loop(0, CHUNK, row, jnp.zeros((CHUNK, DV), F32))


def _gla_kernel(*refs, with_output, batch):
    (qk_f, v_f, sm_f, qk_b, v_b, sm_b, w2_ref, b2_ref, s0_ref) = refs[:9]
    if with_output:
        of_ref, ob_ref, s_s, b_s, inter_s = refs[9:]
    else:
        s_out, s_s = refs[9:]
    step = pl.program_id(0)

    @pl.when(step == 0)
    def _():
        s_s[...] = s0_ref[...]

    worst_decay = []
    for direction, (qk_ref, v_ref, sm_ref) in enumerate(((qk_f, v_f, sm_f), (qk_b, v_b, sm_b))):
        seen, seen01 = _chunk_masks(direction)
        last = CHUNK - 1 if direction == 0 else 0
        for bi in range(batch):
            z = _dot(sm_ref[bi], w2_ref[direction]) + b2_ref[direction]
            log_a = _log_sigmoid(z) * (1.0 / GLA_TAU)
            b_all = _dot_exact_lhs(seen01, log_a)
            outs, inters = [], []
            for h in range(HEADS):
                q = qk_ref[bi, :, h * QK:(h + 1) * QK].astype(F32) * (QK ** -0.5)
                k = qk_ref[bi, :, QK_W + h * QK:QK_W + (h + 1) * QK].astype(F32)
                v = v_ref[bi, :, h * DV:(h + 1) * DV]
                b = b_all[:, h * QK:(h + 1) * QK]
                b_end = b[last:last + 1, :]
                st_old = s_s[bi, direction, h]
                k_dec = k * jnp.exp(b_end - b)
                s_s[bi, direction, h] = st_old * jnp.exp(b_end) + _dot_tn_xlu(v, k_dec)
                if with_output:
                    q_dec = q * jnp.exp(b)
                    inter = _dot_nt(q_dec, st_old)
                    scores = jnp.where(seen, _dot_nt(q_dec, k * jnp.exp(-b)), 0.0)
                    outs.append(_dot(scores, v) + inter)
                    inters.append(inter)
            if with_output:
                (of_ref if direction == 0 else ob_ref)[bi] = jnp.concatenate(outs, axis=-1)
                b_s[bi, direction] = b_all
                inter_s[bi, direction] = jnp.concatenate(inters, axis=-1)
                worst_decay.append(jnp.max(-b_all[last:last + 1, :]))

    if with_output:
        @pl.when(functools.reduce(jnp.maximum, worst_decay) > GLA_SAFE_DECAY)
        def _():
            for direction, (qk_ref, v_ref, o_ref) in enumerate(((qk_f, v_f, of_ref), (qk_b, v_b, ob_ref))):
                last = CHUNK - 1 if direction == 0 else 0
                for bi in range(batch):
                    for h in range(HEADS):
                        b = b_s[bi, direction, :, h * QK:(h + 1) * QK]

                        @pl.when(jnp.max(-b[last:last + 1, :]) > GLA_SAFE_DECAY)
                        def _():
                            q = qk_ref[bi, :, h * QK:(h + 1) * QK].astype(F32) * (QK ** -0.5)
                            k = qk_ref[bi, :, QK_W + h * QK:QK_W + (h + 1) * QK].astype(F32)
                            v = v_ref[bi, :, h * DV:(h + 1) * DV].astype(F32)
                            o_ref[bi, :, h * DV:(h + 1) * DV] = (
                                inter_s[bi, direction, :, h * DV:(h + 1) * DV]
                                + _gla_exact_intra(q, k, v, b, direction))
    else:
        @pl.when(step == pl.num_programs(0) - 1)
        def _():
            s_out[...] = s_s[...]


def _gla_scan(main, small, w2p, b2, s0, *, with_output):
    batch, t, _ = main.shape
    nc = t // CHUNK
    in_specs = _scan_specs(batch, nc, (COL_QK_G, COL_V_G, 0), (2 * QK_W, V_W, LANES))
    in_specs += [_whole(w2p.shape), _whole(b2.shape), _whole(s0.shape)]
    scratch = [pltpu.VMEM(s0.shape, F32)]
    if with_output:
        out_specs = _scan_out_specs(batch, nc)
        out_shape = [jax.ShapeDtypeStruct((batch, t, V_W), F32)] * 2
        scratch += [pltpu.VMEM((batch, 2, CHUNK, QK_W), F32), pltpu.VMEM((batch, 2, CHUNK, V_W), F32)]
    else:
        out_specs = _whole(s0.shape)
        out_shape = jax.ShapeDtypeStruct(s0.shape, F32)
    return pl.pallas_call(
        functools.partial(_gla_kernel, with_output=with_output, batch=batch),
        grid=(nc,),
        in_specs=in_specs,
        out_specs=out_specs,
        out_shape=out_shape,
        scratch_shapes=scratch,
        compiler_params=_cparams(("arbitrary",)),
        name="gla_scan_out" if with_output else "gla_scan_state",
    )(main, main, small, main, main, small, w2p, b2, s0)


def _head_rms(a):
    return jnp.concatenate([_rms(a[:, h * DV:(h + 1) * DV]) for h in range(HEADS)], axis=-1)


def _merge_kernel(hmf, hmb, ogf, ogb, om, rg, mgm, mgg, x_ref, mln, gln, wpm, wpg, wo, gt1, g2, sc2, sh2,
                  wr, br, x1_ref, h2_ref, wt_ref):
    y_m = _head_rms(hmf[...] + hmb[...]) * mln[...] * jax.nn.sigmoid(om[...].astype(F32))
    y_g = _head_rms(ogf[...] + ogb[...]) * gln[...] * _silu(rg[...].astype(F32))
    y = (jax.nn.sigmoid(mgm[...].astype(F32)) * _dot(y_m, wpm[...])
         + jax.nn.sigmoid(mgg[...].astype(F32)) * _dot(y_g, wpg[...]))
    x1 = x_ref[...] + gt1[...] * _dot(y, wo[...])
    x1_ref[...] = x1
    h2 = _rms(x1) * g2[...] * (1.0 + sc2[...]) + sh2[...]
    h2_ref[...] = h2.astype(MXU_DTYPE)

    hh, hm_, _ = _split3(h2)
    wh, wm_, _ = _split3(wr[...])
    lg = _dot(hh, wh) + _dot(hh, wm_) + _dot(hm_, wh) + br[...]
    lane = lax.broadcasted_iota(jnp.int32, lg.shape, 1)

    def masked_softmax(mask):
        l = jnp.where(mask, lg, -jnp.inf)
        e = jnp.exp(l - jnp.max(l, axis=-1, keepdims=True))
        return e / jnp.sum(e, axis=-1, keepdims=True)

    def top1(p, mask):
        pm = jnp.where(mask, p, -1.0)
        best = jnp.max(pm, axis=-1, keepdims=True)
        idx = jnp.min(jnp.where(jnp.logical_and(mask, pm == best), lane, LANES), axis=-1, keepdims=True)
        return best, idx

    gmask = lane < N_GROUPS
    grp_p, grp = top1(masked_softmax(gmask), gmask)
    e_lo = ROUTE_E0 + grp * EXPERTS_PER_GROUP
    emask = jnp.logical_and(lane >= e_lo, lane < e_lo + EXPERTS_PER_GROUP)
    p_in = masked_softmax(emask)
    p1, i1 = top1(p_in, emask)
    p2, i2 = top1(p_in, jnp.logical_and(emask, lane != i1))
    tot = p1 + p2
    wt_ref[...] = (jnp.where(lane == i1, grp_p * p1 / tot, 0.0)
                   + jnp.where(lane == i2, grp_p * p2 / tot, 0.0)
                   + jnp.where(lane == grp, 1.0, 0.0))


def _merge(hmf, hmb, ogf, ogb, main, x2, mln, gln, wpm, wpg, wo, gt1, g2, sc2, sh2, wr, br, *, tm, rows_per_batch):
    m = x2.shape[0]
    tpb = rows_per_batch // tm
    rowblk = pl.BlockSpec((tm, D_MODEL), lambda i: (i, 0))
    colblk = lambda cb: pl.BlockSpec((tm, D_MODEL), lambda i, cb=cb: (i, cb))
    vec = pl.BlockSpec((1, D_MODEL), lambda i: (0, 0))
    bvec = pl.BlockSpec((None, 1, D_MODEL), lambda i: (i // tpb, 0, 0))
    wmat = pl.BlockSpec((D_MODEL, D_MODEL), lambda i: (0, 0))
    return pl.pallas_call(
        _merge_kernel,
        grid=(m // tm,),
        in_specs=[rowblk, rowblk, rowblk, rowblk, colblk(COL_O_M), colblk(COL_R_G), colblk(COL_MG_M),
                  colblk(COL_MG_G), rowblk, vec, vec, wmat, wmat, wmat, bvec, vec, bvec, bvec,
                  pl.BlockSpec((D_MODEL, LANES), lambda i: (0, 0)), pl.BlockSpec((1, LANES), lambda i: (0, 0))],
        out_specs=[rowblk, rowblk, pl.BlockSpec((tm, LANES), lambda i: (i, 0))],
        out_shape=[jax.ShapeDtypeStruct((m, D_MODEL), F32), jax.ShapeDtypeStruct((m, D_MODEL), MXU_DTYPE),
                   jax.ShapeDtypeStruct((m, LANES), F32)],
        compiler_params=_cparams(("parallel",)),
        name="merge_route",
    )(hmf, hmb, ogf, ogb, main, main, main, main, x2, mln, gln, wpm, wpg, wo, gt1, g2, sc2, sh2, wr, br)


MOE_BLK = 64
MOE_COMMON_BLKS = (3, 4, 5, 6)
MOE_LOOP_BLKS = 4
MOE_EXPERTS_PER_STEP = 2


def _moe_kernel(h2_ref, wt_ref, wup_ref, wdn_ref, x1_ref, gt2_ref, gf_ref, o_ref,
                xs_ref, ys_ref, ws_ref, dest_ref, blk_ref):
    step = pl.program_id(1)
    tm = h2_ref.shape[0]
    n_rows = xs_ref.shape[0]

    @pl.when(step == 0)
    def _():
        r = wt_ref[...]
        lane = lax.broadcasted_iota(jnp.int32, (tm, LANES), 1)
        lane1 = lax.broadcasted_iota(jnp.int32, (1, LANES), 1)
        gm = jnp.where(lane < N_GROUPS, r, 0.0)
        earlier = (lax.broadcasted_iota(jnp.int32, (tm, tm), 1)
                   < lax.broadcasted_iota(jnp.int32, (tm, tm), 0)).astype(MXU_DTYPE)
        before = _dot(earlier, gm)
        padded = jnp.floor((jnp.sum(gm, axis=0, keepdims=True) + (MOE_BLK - 1)) * (1.0 / MOE_BLK)) * MOE_BLK
        start = jnp.zeros((1, LANES), F32)
        run = jnp.zeros((1, 1), F32)
        for g in range(N_GROUPS):
            size = jnp.sum(jnp.where(lane1 == g, padded, 0.0), axis=-1, keepdims=True)
            start = jnp.where(lane1 == g, run, start)
            blk_ref[g] = (jnp.sum(run) * (1.0 / MOE_BLK)).astype(jnp.int32)
            blk_ref[N_GROUPS + g] = (jnp.sum(size) * (1.0 / MOE_BLK)).astype(jnp.int32)
            run = run + size
        dest = jnp.sum(gm * (start + before), axis=-1, keepdims=True)
        dest_ref[...] = jnp.broadcast_to(dest, (tm, LANES))
        dest_row = dest_ref[...].T[0:1, :].astype(jnp.int32)
        perm = (lax.broadcasted_iota(jnp.int32, (n_rows, tm), 0) == dest_row).astype(MXU_DTYPE)
        xs_ref[...] = _dot(perm, h2_ref[...]).astype(MXU_DTYPE)
        moved = _dot(perm, jnp.concatenate(_split3(r), axis=-1))
        ws_ref[...] = moved[:, :LANES] + moved[:, LANES:2 * LANES] + moved[:, 2 * LANES:]
        ys_ref[...] = jnp.zeros_like(ys_ref)

    group = step // (EXPERTS_PER_GROUP // MOE_EXPERTS_PER_STEP)
    first_blk = blk_ref[group]
    n_blk = blk_ref[N_GROUPS + group]

    def expert_on(r0, rows):
        x = xs_ref[pl.ds(r0, rows), :]
        ws = ws_ref[pl.ds(r0, rows), :]
        lane_b = lax.broadcasted_iota(jnp.int32, (rows, LANES), 1)
        experts = range(MOE_EXPERTS_PER_STEP)
        gus = [_dot(x, wup_ref[k]) for k in experts]
        hiddens = [(_silu(gu[:, :D_EXPERT]) * gu[:, D_EXPERT:]).astype(MXU_DTYPE) for gu in gus]
        w_cols = [jnp.sum(jnp.where(lane_b == ROUTE_E0 + step * MOE_EXPERTS_PER_STEP + k, ws, 0.0),
                          axis=-1, keepdims=True) for k in experts]
        ys = [_dot(hiddens[k], wdn_ref[k]) for k in experts]
        ys_ref[pl.ds(r0, rows), :] += sum(y * w for y, w in zip(ys, w_cols))

    for k in MOE_COMMON_BLKS:
        @pl.when(n_blk == k)
        def _():
            expert_on(pl.multiple_of(first_blk * MOE_BLK, MOE_BLK), k * MOE_BLK)

    @pl.when(functools.reduce(jnp.logical_and, [n_blk != k for k in MOE_COMMON_BLKS]))
    def _():
        n_big = n_blk // MOE_LOOP_BLKS

        def big(j, carry):
            expert_on(pl.multiple_of((first_blk + j * MOE_LOOP_BLKS) * MOE_BLK, MOE_BLK), MOE_LOOP_BLKS * MOE_BLK)
            return carry

        def single(j, carry):
            expert_on(pl.multiple_of((first_blk + j) * MOE_BLK, MOE_BLK), MOE_BLK)
            return carry

        lax.fori_loop(0, n_big, big, 0)
        lax.fori_loop(n_big * MOE_LOOP_BLKS, n_blk, single, 0)

    @pl.when(step == pl.num_programs(1) - 1)
    def _():
        dest = dest_ref[...][:, :1].astype(jnp.int32)
        unperm = (lax.broadcasted_iota(jnp.int32, (tm, n_rows), 1) == dest).astype(MXU_DTYPE)
        y = _dot(unperm, ys_ref[...])
        o_ref[...] = _rms(x1_ref[...] + gt2_ref[...] * y) * gf_ref[...]


def _moe_final(h2, wt, w_up, w_down, x1, gt2, g_final, *, tm, rows_per_batch):
    m = h2.shape[0]
    tpb = rows_per_batch // tm
    n_rows = tm + N_GROUPS * MOE_BLK
    rowblk = pl.BlockSpec((tm, D_MODEL), lambda i, e: (i, 0))
    return pl.pallas_call(
        _moe_kernel,
        grid=(m // tm, N_EXPERTS // MOE_EXPERTS_PER_STEP),
        in_specs=[rowblk,
                  pl.BlockSpec((tm, LANES), lambda i, e: (i, 0)),
                  pl.BlockSpec((MOE_EXPERTS_PER_STEP, D_MODEL, 2 * D_EXPERT), lambda i, e: (e, 0, 0)),
                  pl.BlockSpec((MOE_EXPERTS_PER_STEP, D_EXPERT, D_MODEL), lambda i, e: (e, 0, 0)),
                  rowblk,
                  pl.BlockSpec((None, 1, D_MODEL), lambda i, e: (i // tpb, 0, 0)),
                  pl.BlockSpec((1, D_MODEL), lambda i, e: (0, 0))],
        out_specs=rowblk,
        out_shape=jax.ShapeDtypeStruct((m, D_MODEL), F32),
        scratch_shapes=[pltpu.VMEM((n_rows, D_MODEL), MXU_DTYPE), pltpu.VMEM((n_rows, D_MODEL), F32),
                        pltpu.VMEM((n_rows, LANES), F32), pltpu.VMEM((tm, LANES), F32),
                        pltpu.SMEM((2 * N_GROUPS,), jnp.int32)],
        compiler_params=_cparams(("parallel", "arbitrary"), 56 * 1024 * 1024),
        name="moe_final",
    )(h2, wt, w_up, w_down, x1, gt2, g_final)


def _empty_states(batch):
    ml = (jnp.zeros((batch, 2, HEADS, QK, DV + LANES), F32),
          jnp.full((batch, 2, HEADS, 1, LANES), NEG_BIG, F32))
    gla = jnp.zeros((batch, 2, HEADS, DV, QK), F32)
    return ml, gla


def kernel(x, c, ctx, c_ctx, w_mod, b_mod, g_norm1, w_in, ml_conv, ml_conv_b, b_mgate, ml_norm, gla_w2, gla_b2,
           gla_norm, w_proj_m, w_proj_g, w_out, g_norm2, w_grp, b_grp, w_rexp, b_rexp, w_up, w_down, g_final):
    batch, t, d = x.shape
    t_ctx = ctx.shape[1]
    assert d == D_MODEL and w_mod.shape[0] == 1 and w_in.shape[2] == sum(IN_SIZES)
    assert t % (GRID_W * 16) == 0 and t % CHUNK == 0 and t_ctx % CHUNK == 0

    off = [0]
    for s in IN_SIZES:
        off.append(off[-1] + s)
    wi = w_in[0].astype(MXU_DTYPE)
    w_main = jnp.concatenate([wi[:, off[0]:off[4]], wi[:, off[5]:off[9]], wi[:, off[10]:off[12]]], axis=1)
    w_small = jnp.concatenate([wi[:, off[9]:off[10]], wi[:, off[4]:off[5]],
                               jnp.zeros((d, LANES - 2 * GLA_RANK - 4 * HEADS), MXU_DTYPE)], axis=1)
    gate_bias = jnp.zeros((1, LANES), F32).at[0, SMALL_GATE0:SMALL_GATE0 + 4 * HEADS].set(b_mgate[0])
    w2p = jnp.zeros((2, LANES, QK_W), F32)
    w2p = w2p.at[0, 0:GLA_RANK].set(gla_w2[0, 0]).at[1, GLA_RANK:2 * GLA_RANK].set(gla_w2[0, 1])
    b2 = gla_b2[0][:, None, :]
    conv_w = ml_conv[0].reshape(9, 2 * QK_W)
    conv_b = ml_conv_b[0][None, :]
    w_route = jnp.concatenate([w_grp[0], w_rexp[0], jnp.zeros((d, LANES - N_GROUPS - N_EXPERTS), F32)], axis=1)
    b_route = jnp.concatenate([b_grp[0], b_rexp[0], jnp.zeros((LANES - N_GROUPS - N_EXPERTS,), F32)])[None, :]

    cc = jnp.concatenate([c, c_ctx[None, :], jnp.zeros((8 - batch - 1, d), F32)], axis=0)
    mod = _modulation(cc, w_mod[0], b_mod[0][None, :])
    sh1, sc1, gt1, sh2, sc2, gt2 = [mod[:batch, i * d:(i + 1) * d][:, None, :] for i in range(6)]
    sh1c, sc1c = [jnp.broadcast_to(mod[batch:batch + 1, i * d:(i + 1) * d][:, None, :], (batch, 1, d)) for i in range(2)]
    g1 = g_norm1[0][None, :]

    main_c, small_c = _in_proj(ctx.reshape(batch * t_ctx, d), g1, sc1c, sh1c, w_main, w_small,
                               tm=t_ctx, rows_per_batch=t_ctx)
    qk_c = _conv_silu(main_c, conv_w, conv_b, batch=batch, rows=1, cols=t_ctx)
    ml0, gla0 = _empty_states(batch)
    main_c3, small_c3 = main_c.reshape(batch, t_ctx, MAIN_W), small_c.reshape(batch, t_ctx, LANES)
    ml_states = _mlstm_scan(qk_c.reshape(batch, t_ctx, 2 * QK_W), main_c3, small_c3, gate_bias, ml0, with_output=False)
    gla_state = _gla_scan(main_c3, small_c3, w2p, b2, gla0, with_output=False)

    x2 = x.reshape(batch * t, d)
    main, small = _in_proj(x2, g1, sc1, sh1, w_main, w_small, tm=1024, rows_per_batch=t)
    qk = _conv_silu(main, conv_w, conv_b, batch=batch, rows=t // GRID_W, cols=GRID_W)
    main3, small3 = main.reshape(batch, t, MAIN_W), small.reshape(batch, t, LANES)
    hm_f, hm_b = [a.reshape(batch * t, V_W) for a in
                  _mlstm_scan(qk.reshape(batch, t, 2 * QK_W), main3, small3, gate_bias, ml_states, with_output=True)]
    og_f, og_b = [a.reshape(batch * t, V_W) for a in _gla_scan(main3, small3, w2p, b2, gla_state, with_output=True)]

    x1, h2, wt = _merge(hm_f, hm_b, og_f, og_b, main, x2, ml_norm, gla_norm,
                        w_proj_m[0].astype(MXU_DTYPE), w_proj_g[0].astype(MXU_DTYPE), w_out[0].astype(MXU_DTYPE),
                        gt1, g_norm2, sc2, sh2, w_route, b_route, tm=256, rows_per_batch=t)
    out = _moe_final(h2, wt, w_up[0].astype(MXU_DTYPE), w_down[0].astype(MXU_DTYPE), x1, gt2, g_final[None, :],
                     tm=1024, rows_per_batch=t)
    return out.reshape(batch, t, d)
```

```python
import functools

import jax
import jax.numpy as jnp
from jax import lax
from jax.experimental import pallas as pl
from jax.experimental.pallas import tpu as pltpu

D_MODEL = 1024
GRID_W = 64
CHUNK = 256
EPS = 1e-6
NEG_BIG = -1e30
HEADS = 4
QK = D_MODEL // 8
DV = D_MODEL // 4
QK_W = HEADS * QK
V_W = HEADS * DV
GLA_RANK = 16
GLA_TAU = 16.0
N_GROUPS = 4
EXPERTS_PER_GROUP = 4
N_EXPERTS = N_GROUPS * EXPERTS_PER_GROUP
D_EXPERT = D_MODEL // 2
IN_SIZES = (QK_W, QK_W, V_W, V_W, 4 * HEADS, QK_W, QK_W, V_W, V_W, 2 * GLA_RANK, D_MODEL, D_MODEL)

LANES = 128
MXU_DTYPE = jnp.bfloat16
F32 = jnp.float32
VMEM_LIMIT = 48 * 1024 * 1024

COL_QK_M, COL_V_M, COL_O_M, COL_QK_G, COL_V_G, COL_R_G, COL_MG_M, COL_MG_G = range(8)
MAIN_W = 8 * D_MODEL
SMALL_GATE0 = 2 * GLA_RANK
GLA_SAFE_DECAY = 80.0
ROUTE_E0 = N_GROUPS


def _dot(a, b):
    return jnp.dot(a.astype(MXU_DTYPE), b.astype(MXU_DTYPE), preferred_element_type=F32)


def _dot_nt(a, b):
    return lax.dot_general(a.astype(MXU_DTYPE), b.astype(MXU_DTYPE), (((1,), (1,)), ((), ())),
                           preferred_element_type=F32)


def _transpose_mxu(a):
    m = a.shape[1]
    eye = (lax.broadcasted_iota(jnp.int32, (m, m), 0) == lax.broadcasted_iota(jnp.int32, (m, m), 1))
    return _dot_nt(eye.astype(MXU_DTYPE), a).astype(MXU_DTYPE)


def _dot_tn_xlu(a, b):
    return lax.dot_general(a.astype(MXU_DTYPE), b.astype(MXU_DTYPE), (((0,), (0,)), ((), ())),
                           preferred_element_type=F32)


def _split3(x):
    hi = x.astype(MXU_DTYPE)
    r1 = x - hi.astype(F32)
    mid = r1.astype(MXU_DTYPE)
    lo = (r1 - mid.astype(F32)).astype(MXU_DTYPE)
    return hi, mid, lo


def _dot_exact_lhs(a01, x):
    hi, mid, lo = _split3(x)
    return _dot(a01, hi) + _dot(a01, mid) + _dot(a01, lo)


def _log_sigmoid(x):
    return jnp.minimum(x, 0.0) - jnp.log(1.0 + jnp.exp(-jnp.abs(x)))


def _silu(x):
    return x * jax.nn.sigmoid(x)


def _rms(x):
    return x * lax.rsqrt(jnp.mean(x * x, axis=-1, keepdims=True) + EPS)


def _cparams(sem, vmem_limit=VMEM_LIMIT):
    return pltpu.CompilerParams(dimension_semantics=sem, vmem_limit_bytes=vmem_limit)


def _mod_kernel(c_ref, w_ref, b_ref, o_ref):
    o_ref[...] = _dot(_silu(c_ref[...]), w_ref[...]) + b_ref[...]


def _modulation(cc, w_mod, b_mod):
    n = w_mod.shape[1]
    tn = 512
    return pl.pallas_call(
        _mod_kernel,
        grid=(n // tn,),
        in_specs=[pl.BlockSpec((8, D_MODEL), lambda j: (0, 0)),
                  pl.BlockSpec((D_MODEL, tn), lambda j: (0, j)),
                  pl.BlockSpec((1, tn), lambda j: (0, j))],
        out_specs=pl.BlockSpec((8, tn), lambda j: (0, j)),
        out_shape=jax.ShapeDtypeStruct((8, n), F32),
        compiler_params=_cparams(("arbitrary",)),
        name="modulation",
    )(cc, w_mod, b_mod)


def _inproj_kernel(x_ref, g_ref, sc_ref, sh_ref, w_ref, ws_ref, o_ref, os_ref, xn_ref):
    @pl.when(pl.program_id(1) == 0)
    def _():
        xn = _rms(x_ref[...]) * g_ref[...] * (1.0 + sc_ref[...]) + sh_ref[...]
        xn_ref[...] = xn.astype(MXU_DTYPE)
        os_ref[...] = _dot(xn_ref[...], ws_ref[...])

    o_ref[...] = _dot(xn_ref[...], w_ref[...]).astype(o_ref.dtype)


def _in_proj(x2, g, sc, sh, w_main, w_small, *, tm, rows_per_batch):
    m = x2.shape[0]
    tn = 2048
    tiles_per_batch = rows_per_batch // tm
    vec = pl.BlockSpec((None, 1, D_MODEL), lambda i, j: (i // tiles_per_batch, 0, 0))
    return pl.pallas_call(
        _inproj_kernel,
        grid=(m // tm, MAIN_W // tn),
        in_specs=[pl.BlockSpec((tm, D_MODEL), lambda i, j: (i, 0)),
                  pl.BlockSpec((1, D_MODEL), lambda i, j: (0, 0)),
                  vec, vec,
                  pl.BlockSpec((D_MODEL, tn), lambda i, j: (0, j)),
                  pl.BlockSpec((D_MODEL, LANES), lambda i, j: (0, 0))],
        out_specs=[pl.BlockSpec((tm, tn), lambda i, j: (i, j)),
                   pl.BlockSpec((tm, LANES), lambda i, j: (i, 0))],
        out_shape=[jax.ShapeDtypeStruct((m, MAIN_W), MXU_DTYPE), jax.ShapeDtypeStruct((m, LANES), F32)],
        scratch_shapes=[pltpu.VMEM((tm, D_MODEL), MXU_DTYPE)],
        compiler_params=_cparams(("parallel", "arbitrary")),
        name="in_proj",
    )(x2, g, sc, sh, w_main, w_small)


def _conv_kernel(x_ref, w_ref, b_ref, o_ref, *, rows, cols):
    scale = jnp.where(pl.program_id(1) * LANES >= QK_W, QK ** -0.5, 1.0).astype(F32)
    w = w_ref[...]
    bias = b_ref[...]
    tpos = lax.broadcasted_iota(jnp.int32, (cols, 1), 0)
    has_left = tpos >= 1
    has_right = tpos < cols - 1

    def row_filters(j):
        tile = x_ref[pl.ds(pl.multiple_of(j * cols, cols), cols), :].astype(F32)
        left = jnp.where(has_left, pltpu.roll(tile, 1, axis=0), 0.0)
        right = jnp.where(has_right, pltpu.roll(tile, cols - 1, axis=0), 0.0)
        return [left * w[3 * i:3 * i + 1, :] + tile * w[3 * i + 1:3 * i + 2, :] + right * w[3 * i + 2:3 * i + 3, :]
                for i in range(3)]

    def finish(j, acc):
        o_ref[pl.ds(pl.multiple_of(j * cols, cols), cols), :] = (_silu(acc + bias) * scale).astype(o_ref.dtype)

    first = row_filters(0)

    def body(j, carry):
        acc, below = carry
        h = row_filters(j)
        finish(j - 1, acc + h[2])
        return below + h[1], h[0]

    acc, _ = lax.fori_loop(1, rows, body, (first[1], first[0]))
    finish(rows - 1, acc)


def _conv_silu(main, conv_w, conv_b, *, batch, rows, cols):
    t = rows * cols
    nct = 2 * QK_W // LANES
    return pl.pallas_call(
        functools.partial(_conv_kernel, rows=rows, cols=cols),
        grid=(batch, nct),
        in_specs=[pl.BlockSpec((t, LANES), lambda b, c: (b, c)),
                  pl.BlockSpec((9, LANES), lambda b, c: (0, c)),
                  pl.BlockSpec((1, LANES), lambda b, c: (0, c))],
        out_specs=pl.BlockSpec((t, LANES), lambda b, c: (b, c)),
        out_shape=jax.ShapeDtypeStruct((batch * t, 2 * QK_W), MXU_DTYPE),
        compiler_params=_cparams(("parallel", "arbitrary")),
        name="conv_silu",
    )(main, conv_w, conv_b)


def _chunk_masks(direction):
    row = lax.broadcasted_iota(jnp.int32, (CHUNK, CHUNK), 0)
    col = lax.broadcasted_iota(jnp.int32, (CHUNK, CHUNK), 1)
    seen = (row >= col) if direction == 0 else (row <= col)
    return seen, seen.astype(MXU_DTYPE)


def _scan_specs(batch, nc, col_blocks, widths):
    specs = []
    for direction in (0, 1):
        for cb, wd in zip(col_blocks, widths):
            if direction == 0:
                specs.append(pl.BlockSpec((batch, CHUNK, wd), lambda c, cb=cb: (0, c, cb)))
            else:
                specs.append(pl.BlockSpec((batch, CHUNK, wd), lambda c, cb=cb: (0, nc - 1 - c, cb)))
    return specs


def _scan_out_specs(batch, nc):
    return [pl.BlockSpec((batch, CHUNK, V_W), lambda c: (0, c, 0)),
            pl.BlockSpec((batch, CHUNK, V_W), lambda c: (0, nc - 1 - c, 0))]


def _whole(shape):
    nd = len(shape)
    return pl.BlockSpec(tuple(shape), lambda c: (0,) * nd)


def _lane_tile(x, width):
    return jnp.concatenate([x] * (width // LANES), axis=-1)


def _mlstm_kernel(*refs, with_output, batch):
    (qk_f, v_f, sm_f, qk_b, v_b, sm_b, bias_ref, cn0_ref, m0_ref) = refs[:9]
    if with_output:
        hf_ref, hb_ref, cn_s, m_s = refs[9:]
    else:
        cn_out, m_out, cn_s, m_s = refs[9:]
    step = pl.program_id(0)

    @pl.when(step == 0)
    def _():
        cn_s[...] = cn0_ref[...]
        m_s[...] = m0_ref[...]

    lane = lax.broadcasted_iota(jnp.int32, (1, LANES), 1)
    gate_lane = jnp.logical_and(lane >= SMALL_GATE0, lane < SMALL_GATE0 + 4 * HEADS)
    forget_lane = jnp.logical_and(gate_lane, ((lane - SMALL_GATE0) % (2 * HEADS)) >= HEADS)
    eye = (lax.broadcasted_iota(jnp.int32, (LANES, LANES), 0)
           == lax.broadcasted_iota(jnp.int32, (LANES, LANES), 1)).astype(MXU_DTYPE)
    ones_cols = jnp.ones((CHUNK, LANES), MXU_DTYPE)

    for direction, (qk_ref, v_ref, sm_ref) in enumerate(((qk_f, v_f, sm_f), (qk_b, v_b, sm_b))):
        seen, seen01 = _chunk_masks(direction)
        last = CHUNK - 1 if direction == 0 else 0
        for bi in range(batch):
            g = sm_ref[bi] + bias_ref[...]
            gp = jnp.where(forget_lane, _log_sigmoid(g), g)
            bc = _dot_exact_lhs(seen01, gp)
            hi, mid, lo = _split3(gp)
            gp_t = _dot_nt(eye, hi) + _dot_nt(eye, mid) + _dot_nt(eye, lo)
            hi, mid, lo = _split3(bc)
            bc_t = _dot_nt(eye, hi) + _dot_nt(eye, mid) + _dot_nt(eye, lo)
            bend_row = bc[last:last + 1, :]
            heads = []
            for h in range(HEADS):
                ji = SMALL_GATE0 + direction * 2 * HEADS + h
                jf = ji + HEADS
                c = dict(ji=ji, jf=jf)
                c["q"] = qk_ref[bi, :, h * QK:(h + 1) * QK].astype(F32)
                c["k"] = qk_ref[bi, :, QK_W + h * QK:QK_W + (h + 1) * QK].astype(F32)
                c["v_ext"] = jnp.concatenate([v_ref[bi, :, h * DV:(h + 1) * DV].astype(MXU_DTYPE), ones_cols], axis=-1)
                c["cn_old"] = cn_s[bi, direction, h]
                c["m_old"] = m_s[bi, direction, h]
                heads.append(c)
            if with_output:
                for c in heads:
                    c["qk"] = _dot_nt(c["q"], c["k"])
                    c["qc"] = _dot(c["q"], c["cn_old"])
            for c in heads:
                b_end = jnp.broadcast_to(bend_row[:, c["jf"]:c["jf"] + 1], (1, LANES))
                i_col = jnp.broadcast_to(gp[:, c["ji"]:c["ji"] + 1], (CHUNK, LANES))
                c["b_col"] = jnp.broadcast_to(bc[:, c["jf"]:c["jf"] + 1], (CHUNK, LANES))
                log_w = b_end - c["b_col"] + i_col
                c["m_new"] = jnp.maximum(b_end + c["m_old"], jnp.max(log_w, axis=0, keepdims=True))
                c["kw"] = (c["k"] * jnp.exp(log_w - c["m_new"])).astype(MXU_DTYPE)
                c["decay"] = jnp.exp(b_end + c["m_old"] - c["m_new"])
            for c in heads:
                c["kw_t"] = _transpose_mxu(c["kw"])
            if with_output:
                for c in heads:
                    i_row = gp_t[c["ji"]:c["ji"] + 1, :]
                    b_row = bc_t[c["jf"]:c["jf"] + 1, :]
                    log_d = jnp.where(seen, _lane_tile(c["b_col"], CHUNK) - b_row + i_row, -jnp.inf)
                    log_inter = c["b_col"] + c["m_old"]
                    c["m_t"] = jnp.maximum(log_inter, jnp.max(log_d, axis=-1, keepdims=True))
                    c["s"] = (c["qk"] * jnp.exp(log_d - _lane_tile(c["m_t"], CHUNK))).astype(MXU_DTYPE)
                    c["w_inter"] = jnp.exp(log_inter - c["m_t"])
                for c in heads:
                    c["sv"] = _dot(c["s"], c["v_ext"])
            for h, c in enumerate(heads):
                cn_s[bi, direction, h] = _lane_tile(c["decay"], DV + LANES) * c["cn_old"] + _dot(c["kw_t"], c["v_ext"])
                m_s[bi, direction, h] = c["m_new"]
            outs = []
            if with_output:
                for c in heads:
                    sv, qc, w_inter = c["sv"], c["qc"], c["w_inter"]
                    num = sv[:, :DV] + _lane_tile(w_inter, DV) * qc[:, :DV]
                    den = jnp.abs(sv[:, DV:] + w_inter * qc[:, DV:])
                    outs.append(num / _lane_tile(jnp.maximum(den, jnp.exp(-c["m_t"])), DV))
            if with_output:
                (hf_ref if direction == 0 else hb_ref)[bi] = jnp.concatenate(outs, axis=-1)

    if not with_output:
        @pl.when(step == pl.num_programs(0) - 1)
        def _():
            cn_out[...] = cn_s[...]
            m_out[...] = m_s[...]


def _mlstm_scan(qk, main, small, gate_bias, states, *, with_output):
    cn0, m0 = states
    batch, t, _ = qk.shape
    nc = t // CHUNK
    in_specs = _scan_specs(batch, nc, (0, COL_V_M, 0), (2 * QK_W, V_W, LANES))
    in_specs += [_whole(gate_bias.shape), _whole(cn0.shape), _whole(m0.shape)]
    if with_output:
        out_specs = _scan_out_specs(batch, nc)
        out_shape = [jax.ShapeDtypeStruct((batch, t, V_W), F32)] * 2
    else:
        out_specs = [_whole(cn0.shape), _whole(m0.shape)]
        out_shape = [jax.ShapeDtypeStruct(s.shape, F32) for s in states]
    return pl.pallas_call(
        functools.partial(_mlstm_kernel, with_output=with_output, batch=batch),
        grid=(nc,),
        in_specs=in_specs,
        out_specs=out_specs,
        out_shape=out_shape,
        scratch_shapes=[pltpu.VMEM(cn0.shape, F32), pltpu.VMEM(m0.shape, F32)],
        compiler_params=_cparams(("arbitrary",)),
        name="mlstm_scan_out" if with_output else "mlstm_scan_state",
    )(qk, main, small, qk, main, small, gate_bias, cn0, m0)


def _gla_exact_intra(q, k, v, b, direction):
    row_id = lax.broadcasted_iota(jnp.int32, (CHUNK, 1), 0)

    def row(t, acc):
        pick = row_id == t
        b_t = jnp.sum(jnp.where(pick, b, 0.0), axis=0, keepdims=True)
        q_t = jnp.sum(jnp.where(pick, q, 0.0), axis=0, keepdims=True)
        ok = (row_id <= t) if direction == 0 else (row_id >= t)
        e = jnp.exp(jnp.where(ok, b_t - b, -jnp.inf))
        sc = jnp.sum(q_t * k * e, axis=-1, keepdims=True)
        o_t = jnp.sum(sc * v, axis=0, keepdims=True)
        return jnp.where(pick, o_t, acc)

    return lax.fori_loop(0, CHUNK, row, jnp.zeros((CHUNK, DV), F32))


def _gla_kernel(*refs, with_output, batch):
    (qk_f, v_f, sm_f, qk_b, v_b, sm_b, w2_ref, b2_ref, s0_ref) = refs[:9]
    if with_output:
        of_ref, ob_ref, s_s, b_s, inter_s = refs[9:]
    else:
        s_out, s_s = refs[9:]
    step = pl.program_id(0)

    @pl.when(step == 0)
    def _():
        s_s[...] = s0_ref[...]

    worst_decay = []
    for direction, (qk_ref, v_ref, sm_ref) in enumerate(((qk_f, v_f, sm_f), (qk_b, v_b, sm_b))):
        seen, seen01 = _chunk_masks(direction)
        last = CHUNK - 1 if direction == 0 else 0
        for bi in range(batch):
            z = _dot(sm_ref[bi], w2_ref[direction]) + b2_ref[direction]
            log_a = _log_sigmoid(z) * (1.0 / GLA_TAU)
            b_all = _dot_exact_lhs(seen01, log_a)
            outs, inters = [], []
            for h in range(HEADS):
                q = qk_ref[bi, :, h * QK:(h + 1) * QK].astype(F32) * (QK ** -0.5)
                k = qk_ref[bi, :, QK_W + h * QK:QK_W + (h + 1) * QK].astype(F32)
                v = v_ref[bi, :, h * DV:(h + 1) * DV]
                b = b_all[:, h * QK:(h + 1) * QK]
                b_end = b[last:last + 1, :]
                st_old = s_s[bi, direction, h]
                k_dec = k * jnp.exp(b_end - b)
                s_s[bi, direction, h] = st_old * jnp.exp(b_end) + _dot_tn_xlu(v, k_dec)
                if with_output:
                    q_dec = q * jnp.exp(b)
                    inter = _dot_nt(q_dec, st_old)
                    scores = jnp.where(seen, _dot_nt(q_dec, k * jnp.exp(-b)), 0.0)
                    outs.append(_dot(scores, v) + inter)
                    inters.append(inter)
            if with_output:
                (of_ref if direction == 0 else ob_ref)[bi] = jnp.concatenate(outs, axis=-1)
                b_s[bi, direction] = b_all
                inter_s[bi, direction] = jnp.concatenate(inters, axis=-1)
                worst_decay.append(jnp.max(-b_all[last:last + 1, :]))

    if with_output:
        @pl.when(functools.reduce(jnp.maximum, worst_decay) > GLA_SAFE_DECAY)
        def _():
            for direction, (qk_ref, v_ref, o_ref) in enumerate(((qk_f, v_f, of_ref), (qk_b, v_b, ob_ref))):
                last = CHUNK - 1 if direction == 0 else 0
                for bi in range(batch):
                    for h in range(HEADS):
                        b = b_s[bi, direction, :, h * QK:(h + 1) * QK]

                        @pl.when(jnp.max(-b[last:last + 1, :]) > GLA_SAFE_DECAY)
                        def _():
                            q = qk_ref[bi, :, h * QK:(h + 1) * QK].astype(F32) * (QK ** -0.5)
                            k = qk_ref[bi, :, QK_W + h * QK:QK_W + (h + 1) * QK].astype(F32)
                            v = v_ref[bi, :, h * DV:(h + 1) * DV].astype(F32)
                            o_ref[bi, :, h * DV:(h + 1) * DV] = (
                                inter_s[bi, direction, :, h * DV:(h + 1) * DV]
                                + _gla_exact_intra(q, k, v, b, direction))
    else:
        @pl.when(step == pl.num_programs(0) - 1)
        def _():
            s_out[...] = s_s[...]


def _gla_scan(main, small, w2p, b2, s0, *, with_output):
    batch, t, _ = main.shape
    nc = t // CHUNK
    in_specs = _scan_specs(batch, nc, (COL_QK_G, COL_V_G, 0), (2 * QK_W, V_W, LANES))
    in_specs += [_whole(w2p.shape), _whole(b2.shape), _whole(s0.shape)]
    scratch = [pltpu.VMEM(s0.shape, F32)]
    if with_output:
        out_specs = _scan_out_specs(batch, nc)
        out_shape = [jax.ShapeDtypeStruct((batch, t, V_W), F32)] * 2
        scratch += [pltpu.VMEM((batch, 2, CHUNK, QK_W), F32), pltpu.VMEM((batch, 2, CHUNK, V_W), F32)]
    else:
        out_specs = _whole(s0.shape)
        out_shape = jax.ShapeDtypeStruct(s0.shape, F32)
    return pl.pallas_call(
        functools.partial(_gla_kernel, with_output=with_output, batch=batch),
        grid=(nc,),
        in_specs=in_specs,
        out_specs=out_specs,
        out_shape=out_shape,
        scratch_shapes=scratch,
        compiler_params=_cparams(("arbitrary",)),
        name="gla_scan_out" if with_output else "gla_scan_state",
    )(main, main, small, main, main, small, w2p, b2, s0)


def _head_rms(a):
    return jnp.concatenate([_rms(a[:, h * DV:(h + 1) * DV]) for h in range(HEADS)], axis=-1)


def _merge_kernel(hmf, hmb, ogf, ogb, om, rg, mgm, mgg, x_ref, mln, gln, wpm, wpg, wo, gt1, g2, sc2, sh2,
                  wr, br, x1_ref, h2_ref, wt_ref):
    tm = x_ref.shape[0]
    parts = [slice(i * tm // MERGE_SPLIT, (i + 1) * tm // MERGE_SPLIT) for i in range(MERGE_SPLIT)]
    y_m = [_head_rms(hmf[r, :] + hmb[r, :]) * mln[...] * jax.nn.sigmoid(om[r, :].astype(F32)) for r in parts]
    p_m = [_dot(a, wpm[...]) for a in y_m]
    y_g = [_head_rms(ogf[r, :] + ogb[r, :]) * gln[...] * _silu(rg[r, :].astype(F32)) for r in parts]
    p_g = [_dot(a, wpg[...]) for a in y_g]
    y = [jax.nn.sigmoid(mgm[r, :].astype(F32)) * a + jax.nn.sigmoid(mgg[r, :].astype(F32)) * b
         for r, a, b in zip(parts, p_m, p_g)]
    mix = [_dot(a, wo[...]) for a in y]
    h2_parts = []
    for r, a in zip(parts, mix):
        x1 = x_ref[r, :] + gt1[...] * a
        x1_ref[r, :] = x1
        h2_parts.append(_rms(x1) * g2[...] * (1.0 + sc2[...]) + sh2[...])
    h2 = jnp.concatenate(h2_parts, axis=0)
    h2_ref[...] = h2.astype(MXU_DTYPE)

    hh, hm_, _ = _split3(h2)
    wh, wm_, _ = _split3(wr[...])
    lg = _dot(hh, wh) + _dot(hh, wm_) + _dot(hm_, wh) + br[...]
    lane = lax.broadcasted_iota(jnp.int32, lg.shape, 1)

    def masked_softmax(mask):
        l = jnp.where(mask, lg, -jnp.inf)
        e = jnp.exp(l - jnp.max(l, axis=-1, keepdims=True))
        return e / jnp.sum(e, axis=-1, keepdims=True)

    def top1(p, mask):
        pm = jnp.where(mask, p, -1.0)
        best = jnp.max(pm, axis=-1, keepdims=True)
        idx = jnp.min(jnp.where(jnp.logical_and(mask, pm == best), lane, LANES), axis=-1, keepdims=True)
        return best, idx

    gmask = lane < N_GROUPS
    grp_p, grp = top1(masked_softmax(gmask), gmask)
    e_lo = ROUTE_E0 + grp * EXPERTS_PER_GROUP
    emask = jnp.logical_and(lane >= e_lo, lane < e_lo + EXPERTS_PER_GROUP)
    p_in = masked_softmax(emask)
    p1, i1 = top1(p_in, emask)
    p2, i2 = top1(p_in, jnp.logical_and(emask, lane != i1))
    tot = p1 + p2
    wt_ref[...] = (jnp.where(lane == i1, grp_p * p1 / tot, 0.0)
                   + jnp.where(lane == i2, grp_p * p2 / tot, 0.0)
                   + jnp.where(lane == grp, 1.0, 0.0))


def _merge(hmf, hmb, ogf, ogb, main, x2, mln, gln, wpm, wpg, wo, gt1, g2, sc2, sh2, wr, br, *, tm, rows_per_batch):
    m = x2.shape[0]
    tpb = rows_per_batch // tm
    rowblk = pl.BlockSpec((tm, D_MODEL), lambda i: (i, 0))
    colblk = lambda cb: pl.BlockSpec((tm, D_MODEL), lambda i, cb=cb: (i, cb))
    vec = pl.BlockSpec((1, D_MODEL), lambda i: (0, 0))
    bvec = pl.BlockSpec((None, 1, D_MODEL), lambda i: (i // tpb, 0, 0))
    wmat = pl.BlockSpec((D_MODEL, D_MODEL), lambda i: (0, 0))
    return pl.pallas_call(
        _merge_kernel,
        grid=(m // tm,),
        in_specs=[rowblk, rowblk, rowblk, rowblk, colblk(COL_O_M), colblk(COL_R_G), colblk(COL_MG_M),
                  colblk(COL_MG_G), rowblk, vec, vec, wmat, wmat, wmat, bvec, vec, bvec, bvec,
                  pl.BlockSpec((D_MODEL, LANES), lambda i: (0, 0)), pl.BlockSpec((1, LANES), lambda i: (0, 0))],
        out_specs=[rowblk, rowblk, pl.BlockSpec((tm, LANES), lambda i: (i, 0))],
        out_shape=[jax.ShapeDtypeStruct((m, D_MODEL), F32), jax.ShapeDtypeStruct((m, D_MODEL), MXU_DTYPE),
                   jax.ShapeDtypeStruct((m, LANES), F32)],
        compiler_params=_cparams(("parallel",), 56 * 1024 * 1024),
        name="merge_route",
    )(hmf, hmb, ogf, ogb, main, main, main, main, x2, mln, gln, wpm, wpg, wo, gt1, g2, sc2, sh2, wr, br)


MERGE_SPLIT = 2
MOE_BLK = 64
MOE_COMMON_BLKS = (3, 4, 5, 6)
MOE_LOOP_BLKS = 4
MOE_EXPERTS_PER_STEP = 2


def _moe_kernel(h2_ref, wt_ref, wup_ref, wdn_ref, x1_ref, gt2_ref, gf_ref, o_ref,
                xs_ref, ys_ref, ws_ref, dest_ref, blk_ref):
    step = pl.program_id(1)
    tm = h2_ref.shape[0]
    n_rows = xs_ref.shape[0]

    @pl.when(step == 0)
    def _():
        r = wt_ref[...]
        lane = lax.broadcasted_iota(jnp.int32, (tm, LANES), 1)
        lane1 = lax.broadcasted_iota(jnp.int32, (1, LANES), 1)
        gm = jnp.where(lane < N_GROUPS, r, 0.0)
        earlier = (lax.broadcasted_iota(jnp.int32, (tm, tm), 1)
                   < lax.broadcasted_iota(jnp.int32, (tm, tm), 0)).astype(MXU_DTYPE)
        before = _dot(earlier, gm)
        padded = jnp.floor((jnp.sum(gm, axis=0, keepdims=True) + (MOE_BLK - 1)) * (1.0 / MOE_BLK)) * MOE_BLK
        start = jnp.zeros((1, LANES), F32)
        run = jnp.zeros((1, 1), F32)
        for g in range(N_GROUPS):
            size = jnp.sum(jnp.where(lane1 == g, padded, 0.0), axis=-1, keepdims=True)
            start = jnp.where(lane1 == g, run, start)
            blk_ref[g] = (jnp.sum(run) * (1.0 / MOE_BLK)).astype(jnp.int32)
            blk_ref[N_GROUPS + g] = (jnp.sum(size) * (1.0 / MOE_BLK)).astype(jnp.int32)
            run = run + size
        dest = jnp.sum(gm * (start + before), axis=-1, keepdims=True)
        dest_ref[...] = jnp.broadcast_to(dest, (tm, LANES))
        dest_row = dest_ref[...].T[0:1, :].astype(jnp.int32)
        perm = (lax.broadcasted_iota(jnp.int32, (n_rows, tm), 0) == dest_row).astype(MXU_DTYPE)
        xs_ref[...] = _dot(perm, h2_ref[...]).astype(MXU_DTYPE)
        moved = _dot(perm, jnp.concatenate(_split3(r), axis=-1))
        ws_ref[...] = moved[:, :LANES] + moved[:, LANES:2 * LANES] + moved[:, 2 * LANES:]
        ys_ref[...] = jnp.zeros_like(ys_ref)

    group = step // (EXPERTS_PER_GROUP // MOE_EXPERTS_PER_STEP)
    first_blk = blk_ref[group]
    n_blk = blk_ref[N_GROUPS + group]

    def expert_on(r0, rows):
        x = xs_ref[pl.ds(r0, rows), :]
        ws = ws_ref[pl.ds(r0, rows), :]
        lane_b = lax.broadcasted_iota(jnp.int32, (rows, LANES), 1)
        experts = range(MOE_EXPERTS_PER_STEP)
        gus = [_dot(x, wup_ref[k]) for k in experts]
        hiddens = [(_silu(gu[:, :D_EXPERT]) * gu[:, D_EXPERT:]).astype(MXU_DTYPE) for gu in gus]
        w_cols = [jnp.sum(jnp.where(lane_b == ROUTE_E0 + step * MOE_EXPERTS_PER_STEP + k, ws, 0.0),
                          axis=-1, keepdims=True) for k in experts]
        ys = [_dot(hiddens[k], wdn_ref[k]) for k in experts]
        ys_ref[pl.ds(r0, rows), :] += sum(y * w for y, w in zip(ys, w_cols))

    for k in MOE_COMMON_BLKS:
        @pl.when(n_blk == k)
        def _():
            expert_on(pl.multiple_of(first_blk * MOE_BLK, MOE_BLK), k * MOE_BLK)

    @pl.when(functools.reduce(jnp.logical_and, [n_blk != k for k in MOE_COMMON_BLKS]))
    def _():
        n_big = n_blk // MOE_LOOP_BLKS

        def big(j, carry):
            expert_on(pl.multiple_of((first_blk + j * MOE_LOOP_BLKS) * MOE_BLK, MOE_BLK), MOE_LOOP_BLKS * MOE_BLK)
            return carry

        def single(j, carry):
            expert_on(pl.multiple_of((first_blk + j) * MOE_BLK, MOE_BLK), MOE_BLK)
            return carry

        lax.fori_loop(0, n_big, big, 0)
        lax.fori_loop(n_big * MOE_LOOP_BLKS, n_blk, single, 0)

    @pl.when(step == pl.num_programs(1) - 1)
    def _():
        dest = dest_ref[...][:, :1].astype(jnp.int32)
        unperm = (lax.broadcasted_iota(jnp.int32, (tm, n_rows), 1) == dest).astype(MXU_DTYPE)
        y = _dot(unperm, ys_ref[...])
        o_ref[...] = _rms(x1_ref[...] + gt2_ref[...] * y) * gf_ref[...]


def _moe_final(h2, wt, w_up, w_down, x1, gt2, g_final, *, tm, rows_per_batch):
    m = h2.shape[0]
    tpb = rows_per_batch // tm
    n_rows = tm + N_GROUPS * MOE_BLK
    rowblk = pl.BlockSpec((tm, D_MODEL), lambda i, e: (i, 0))
    return pl.pallas_call(
        _moe_kernel,
        grid=(m // tm, N_EXPERTS // MOE_EXPERTS_PER_STEP),
        in_specs=[rowblk,
                  pl.BlockSpec((tm, LANES), lambda i, e: (i, 0)),
                  pl.BlockSpec((MOE_EXPERTS_PER_STEP, D_MODEL, 2 * D_EXPERT), lambda i, e: (e, 0, 0)),
                  pl.BlockSpec((MOE_EXPERTS_PER_STEP, D_EXPERT, D_MODEL), lambda i, e: (e, 0, 0)),
                  rowblk,
                  pl.BlockSpec((None, 1, D_MODEL), lambda i, e: (i // tpb, 0, 0)),
                  pl.BlockSpec((1, D_MODEL), lambda i, e: (0, 0))],
        out_specs=rowblk,
        out_shape=jax.ShapeDtypeStruct((m, D_MODEL), F32),
        scratch_shapes=[pltpu.VMEM((n_rows, D_MODEL), MXU_DTYPE), pltpu.VMEM((n_rows, D_MODEL), F32),
                        pltpu.VMEM((n_rows, LANES), F32), pltpu.VMEM((tm, LANES), F32),
                        pltpu.SMEM((2 * N_GROUPS,), jnp.int32)],
        compiler_params=_cparams(("parallel", "arbitrary"), 56 * 1024 * 1024),
        name="moe_final",
    )(h2, wt, w_up, w_down, x1, gt2, g_final)


def _empty_states(batch):
    ml = (jnp.zeros((batch, 2, HEADS, QK, DV + LANES), F32),
          jnp.full((batch, 2, HEADS, 1, LANES), NEG_BIG, F32))
    gla = jnp.zeros((batch, 2, HEADS, DV, QK), F32)
    return ml, gla


def kernel(x, c, ctx, c_ctx, w_mod, b_mod, g_norm1, w_in, ml_conv, ml_conv_b, b_mgate, ml_norm, gla_w2, gla_b2,
           gla_norm, w_proj_m, w_proj_g, w_out, g_norm2, w_grp, b_grp, w_rexp, b_rexp, w_up, w_down, g_final):
    batch, t, d = x.shape
    t_ctx = ctx.shape[1]
    assert d == D_MODEL and w_mod.shape[0] == 1 and w_in.shape[2] == sum(IN_SIZES)
    assert t % (GRID_W * 16) == 0 and t % CHUNK == 0 and t_ctx % CHUNK == 0

    off = [0]
    for s in IN_SIZES:
        off.append(off[-1] + s)
    wi = w_in[0].astype(MXU_DTYPE)
    w_main = jnp.concatenate([wi[:, off[0]:off[4]], wi[:, off[5]:off[9]], wi[:, off[10]:off[12]]], axis=1)
    w_small = jnp.concatenate([wi[:, off[9]:off[10]], wi[:, off[4]:off[5]],
                               jnp.zeros((d, LANES - 2 * GLA_RANK - 4 * HEADS), MXU_DTYPE)], axis=1)
    gate_bias = jnp.zeros((1, LANES), F32).at[0, SMALL_GATE0:SMALL_GATE0 + 4 * HEADS].set(b_mgate[0])
    w2p = jnp.zeros((2, LANES, QK_W), F32)
    w2p = w2p.at[0, 0:GLA_RANK].set(gla_w2[0, 0]).at[1, GLA_RANK:2 * GLA_RANK].set(gla_w2[0, 1])
    b2 = gla_b2[0][:, None, :]
    conv_w = ml_conv[0].reshape(9, 2 * QK_W)
    conv_b = ml_conv_b[0][None, :]
    w_route = jnp.concatenate([w_grp[0], w_rexp[0], jnp.zeros((d, LANES - N_GROUPS - N_EXPERTS), F32)], axis=1)
    b_route = jnp.concatenate([b_grp[0], b_rexp[0], jnp.zeros((LANES - N_GROUPS - N_EXPERTS,), F32)])[None, :]

    cc = jnp.concatenate([c, c_ctx[None, :], jnp.zeros((8 - batch - 1, d), F32)], axis=0)
    mod = _modulation(cc, w_mod[0], b_mod[0][None, :])
    sh1, sc1, gt1, sh2, sc2, gt2 = [mod[:batch, i * d:(i + 1) * d][:, None, :] for i in range(6)]
    sh1c, sc1c = [jnp.broadcast_to(mod[batch:batch + 1, i * d:(i + 1) * d][:, None, :], (batch, 1, d)) for i in range(2)]
    g1 = g_norm1[0][None, :]

    main_c, small_c = _in_proj(ctx.reshape(batch * t_ctx, d), g1, sc1c, sh1c, w_main, w_small,
                               tm=t_ctx, rows_per_batch=t_ctx)
    qk_c = _conv_silu(main_c, conv_w, conv_b, batch=batch, rows=1, cols=t_ctx)
    ml0, gla0 = _empty_states(batch)
    main_c3, small_c3 = main_c.reshape(batch, t_ctx, MAIN_W), small_c.reshape(batch, t_ctx, LANES)
    ml_states = _mlstm_scan(qk_c.reshape(batch, t_ctx, 2 * QK_W), main_c3, small_c3, gate_bias, ml0, with_output=False)
    gla_state = _gla_scan(main_c3, small_c3, w2p, b2, gla0, with_output=False)

    x2 = x.reshape(batch * t, d)
    main, small = _in_proj(x2, g1, sc1, sh1, w_main, w_small, tm=1024, rows_per_batch=t)
    qk = _conv_silu(main, conv_w, conv_b, batch=batch, rows=t // GRID_W, cols=GRID_W)
    main3, small3 = main.reshape(batch, t, MAIN_W), small.reshape(batch, t, LANES)
    hm_f, hm_b = [a.reshape(batch * t, V_W) for a in
                  _mlstm_scan(qk.reshape(batch, t, 2 * QK_W), main3, small3, gate_bias, ml_states, with_output=True)]
    og_f, og_b = [a.reshape(batch * t, V_W) for a in _gla_scan(main3, small3, w2p, b2, gla_state, with_output=True)]

    x1, h2, wt = _merge(hm_f, hm_b, og_f, og_b, main, x2, ml_norm, gla_norm,
                        w_proj_m[0].astype(MXU_DTYPE), w_proj_g[0].astype(MXU_DTYPE), w_out[0].astype(MXU_DTYPE),
                        gt1, g_norm2, sc2, sh2, w_route, b_route, tm=512, rows_per_batch=t)
    out = _moe_final(h2, wt, w_up[0].astype(MXU_DTYPE), w_down[0].astype(MXU_DTYPE), x1, gt2, g_final[None, :],
                     tm=1024, rows_per_batch=t)
    return out.reshape(batch, t, d)
```

```python
import functools

import jax
import jax.numpy as jnp
from jax import lax
from jax.experimental import pallas as pl
from jax.experimental.pallas import tpu as pltpu

D_MODEL = 1024
GRID_W = 64
CHUNK = 256
EPS = 1e-6
NEG_BIG = -1e30
HEADS = 4
QK = D_MODEL // 8
DV = D_MODEL // 4
QK_W = HEADS * QK
V_W = HEADS * DV
GLA_RANK = 16
GLA_TAU = 16.0
N_GROUPS = 4
EXPERTS_PER_GROUP = 4
N_EXPERTS = N_GROUPS * EXPERTS_PER_GROUP
D_EXPERT = D_MODEL // 2
IN_SIZES = (QK_W, QK_W, V_W, V_W, 4 * HEADS, QK_W, QK_W, V_W, V_W, 2 * GLA_RANK, D_MODEL, D_MODEL)

LANES = 128
CONV_TILE = 256
MXU_DTYPE = jnp.bfloat16
F32 = jnp.float32
VMEM_LIMIT = 48 * 1024 * 1024

COL_QK_M, COL_V_M, COL_O_M, COL_QK_G, COL_V_G, COL_R_G, COL_MG_M, COL_MG_G = range(8)
MAIN_W = 8 * D_MODEL
SMALL_GATE0 = 2 * GLA_RANK
GLA_SAFE_DECAY = 80.0
ROUTE_E0 = N_GROUPS


def _dot(a, b):
    return jnp.dot(a.astype(MXU_DTYPE), b.astype(MXU_DTYPE), preferred_element_type=F32)


def _dot_nt(a, b):
    return lax.dot_general(a.astype(MXU_DTYPE), b.astype(MXU_DTYPE), (((1,), (1,)), ((), ())),
                           preferred_element_type=F32)


def _transpose_mxu(a):
    m = a.shape[1]
    eye = (lax.broadcasted_iota(jnp.int32, (m, m), 0) == lax.broadcasted_iota(jnp.int32, (m, m), 1))
    return _dot_nt(eye.astype(MXU_DTYPE), a).astype(MXU_DTYPE)


def _dot_tn_xlu(a, b):
    return lax.dot_general(a.astype(MXU_DTYPE), b.astype(MXU_DTYPE), (((0,), (0,)), ((), ())),
                           preferred_element_type=F32)


def _split3(x):
    hi = x.astype(MXU_DTYPE)
    r1 = x - hi.astype(F32)
    mid = r1.astype(MXU_DTYPE)
    lo = (r1 - mid.astype(F32)).astype(MXU_DTYPE)
    return hi, mid, lo


def _dot_exact_lhs(a01, x):
    hi, mid, lo = _split3(x)
    return _dot(a01, hi) + _dot(a01, mid) + _dot(a01, lo)


def _log_sigmoid(x):
    return jnp.minimum(x, 0.0) - jnp.log(1.0 + jnp.exp(-jnp.abs(x)))


def _silu(x):
    return x * jax.nn.sigmoid(x)


def _rms(x):
    return x * lax.rsqrt(jnp.mean(x * x, axis=-1, keepdims=True) + EPS)


def _cparams(sem, vmem_limit=VMEM_LIMIT):
    return pltpu.CompilerParams(dimension_semantics=sem, vmem_limit_bytes=vmem_limit)


def _mod_kernel(c_ref, w_ref, b_ref, o_ref):
    o_ref[...] = _dot(_silu(c_ref[...]), w_ref[...]) + b_ref[...]


def _modulation(cc, w_mod, b_mod):
    n = w_mod.shape[1]
    tn = 512
    return pl.pallas_call(
        _mod_kernel,
        grid=(n // tn,),
        in_specs=[pl.BlockSpec((8, D_MODEL), lambda j: (0, 0)),
                  pl.BlockSpec((D_MODEL, tn), lambda j: (0, j)),
                  pl.BlockSpec((1, tn), lambda j: (0, j))],
        out_specs=pl.BlockSpec((8, tn), lambda j: (0, j)),
        out_shape=jax.ShapeDtypeStruct((8, n), F32),
        compiler_params=_cparams(("arbitrary",)),
        name="modulation",
    )(cc, w_mod, b_mod)


def _inproj_kernel(x_ref, g_ref, sc_ref, sh_ref, w_ref, ws_ref, o_ref, os_ref, xn_ref):
    @pl.when(pl.program_id(1) == 0)
    def _():
        xn = _rms(x_ref[...]) * g_ref[...] * (1.0 + sc_ref[...]) + sh_ref[...]
        xn_ref[...] = xn.astype(MXU_DTYPE)
        os_ref[...] = _dot(xn_ref[...], ws_ref[...])

    o_ref[...] = _dot(xn_ref[...], w_ref[...]).astype(o_ref.dtype)


def _in_proj(x2, g, sc, sh, w_main, w_small, *, tm, rows_per_batch, last_col_block=COL_MG_G):
    m = x2.shape[0]
    tn = 2048
    col_tiles = last_col_block * D_MODEL // tn + 1
    tiles_per_batch = rows_per_batch // tm
    vec = pl.BlockSpec((None, 1, D_MODEL), lambda i, j: (i // tiles_per_batch, 0, 0))
    return pl.pallas_call(
        _inproj_kernel,
        grid=(m // tm, col_tiles),
        in_specs=[pl.BlockSpec((tm, D_MODEL), lambda i, j: (i, 0)),
                  pl.BlockSpec((1, D_MODEL), lambda i, j: (0, 0)),
                  vec, vec,
                  pl.BlockSpec((D_MODEL, tn), lambda i, j: (0, j)),
                  pl.BlockSpec((D_MODEL, LANES), lambda i, j: (0, 0))],
        out_specs=[pl.BlockSpec((tm, tn), lambda i, j: (i, j)),
                   pl.BlockSpec((tm, LANES), lambda i, j: (i, 0))],
        out_shape=[jax.ShapeDtypeStruct((m, col_tiles * tn), MXU_DTYPE), jax.ShapeDtypeStruct((m, LANES), F32)],
        scratch_shapes=[pltpu.VMEM((tm, D_MODEL), MXU_DTYPE)],
        compiler_params=_cparams(("parallel", "arbitrary")),
        name="in_proj",
    )(x2, g, sc, sh, w_main, w_small)


def _conv_kernel(x_ref, w_ref, b_ref, o_ref, *, rows, cols):
    scale = jnp.where(pl.program_id(1) * CONV_TILE >= QK_W, QK ** -0.5, 1.0).astype(F32)
    w = w_ref[...]
    bias = b_ref[...]
    tpos = lax.broadcasted_iota(jnp.int32, (cols, 1), 0)
    has_left = tpos >= 1
    has_right = tpos < cols - 1

    def row_filters(j):
        tile = x_ref[pl.ds(pl.multiple_of(j * cols, cols), cols), :].astype(F32)
        left = jnp.where(has_left, pltpu.roll(tile, 1, axis=0), 0.0)
        right = jnp.where(has_right, pltpu.roll(tile, cols - 1, axis=0), 0.0)
        return [left * w[3 * i:3 * i + 1, :] + tile * w[3 * i + 1:3 * i + 2, :] + right * w[3 * i + 2:3 * i + 3, :]
                for i in range(3)]

    def finish(j, acc):
        o_ref[pl.ds(pl.multiple_of(j * cols, cols), cols), :] = (_silu(acc + bias) * scale).astype(o_ref.dtype)

    first = row_filters(0)

    def body(j, carry):
        acc, below = carry
        h = row_filters(j)
        finish(j - 1, acc + h[2])
        return below + h[1], h[0]

    acc, _ = lax.fori_loop(1, rows, body, (first[1], first[0]))
    finish(rows - 1, acc)


def _conv_silu(main, conv_w, conv_b, *, batch, rows, cols):
    t = rows * cols
    nct = 2 * QK_W // CONV_TILE
    return pl.pallas_call(
        functools.partial(_conv_kernel, rows=rows, cols=cols),
        grid=(batch, nct),
        in_specs=[pl.BlockSpec((t, CONV_TILE), lambda b, c: (b, c)),
                  pl.BlockSpec((9, CONV_TILE), lambda b, c: (0, c)),
                  pl.BlockSpec((1, CONV_TILE), lambda b, c: (0, c))],
        out_specs=pl.BlockSpec((t, CONV_TILE), lambda b, c: (b, c)),
        out_shape=jax.ShapeDtypeStruct((batch * t, 2 * QK_W), MXU_DTYPE),
        compiler_params=_cparams(("parallel", "arbitrary")),
        name="conv_silu",
    )(main, conv_w, conv_b)


def _chunk_masks(direction):
    row = lax.broadcasted_iota(jnp.int32, (CHUNK, CHUNK), 0)
    col = lax.broadcasted_iota(jnp.int32, (CHUNK, CHUNK), 1)
    seen = (row >= col) if direction == 0 else (row <= col)
    return seen, seen.astype(MXU_DTYPE)


def _scan_specs(batch, nc, col_blocks, widths):
    specs = []
    for direction in (0, 1):
        for cb, wd in zip(col_blocks, widths):
            if direction == 0:
                specs.append(pl.BlockSpec((batch, CHUNK, wd), lambda c, cb=cb: (0, c, cb)))
            else:
                specs.append(pl.BlockSpec((batch, CHUNK, wd), lambda c, cb=cb: (0, nc - 1 - c, cb)))
    return specs


def _scan_out_specs(batch, nc):
    return [pl.BlockSpec((batch, CHUNK, V_W), lambda c: (0, c, 0)),
            pl.BlockSpec((batch, CHUNK, V_W), lambda c: (0, nc - 1 - c, 0))]


def _whole(shape):
    nd = len(shape)
    return pl.BlockSpec(tuple(shape), lambda c: (0,) * nd)


def _lane_tile(x, width):
    return jnp.concatenate([x] * (width // LANES), axis=-1)


def _mlstm_kernel(*refs, with_output, batch):
    (qk_f, v_f, sm_f, qk_b, v_b, sm_b, bias_ref, cn0_ref, m0_ref) = refs[:9]
    if with_output:
        hf_ref, hb_ref, cn_s, m_s = refs[9:]
    else:
        cn_out, m_out, cn_s, m_s = refs[9:]
    step = pl.program_id(0)

    @pl.when(step == 0)
    def _():
        cn_s[...] = cn0_ref[...]
        m_s[...] = m0_ref[...]

    lane = lax.broadcasted_iota(jnp.int32, (1, LANES), 1)
    gate_lane = jnp.logical_and(lane >= SMALL_GATE0, lane < SMALL_GATE0 + 4 * HEADS)
    forget_lane = jnp.logical_and(gate_lane, ((lane - SMALL_GATE0) % (2 * HEADS)) >= HEADS)
    eye = (lax.broadcasted_iota(jnp.int32, (LANES, LANES), 0)
           == lax.broadcasted_iota(jnp.int32, (LANES, LANES), 1)).astype(MXU_DTYPE)
    ones_cols = jnp.ones((CHUNK, LANES), MXU_DTYPE)

    for direction, (qk_ref, v_ref, sm_ref) in enumerate(((qk_f, v_f, sm_f), (qk_b, v_b, sm_b))):
        seen, seen01 = _chunk_masks(direction)
        last = CHUNK - 1 if direction == 0 else 0
        for bi in range(batch):
            g = sm_ref[bi] + bias_ref[...]
            gp = jnp.where(forget_lane, _log_sigmoid(g), g)
            bc = _dot_exact_lhs(seen01, gp)
            hi, mid, lo = _split3(gp)
            gp_t = _dot_nt(eye, hi) + _dot_nt(eye, mid) + _dot_nt(eye, lo)
            hi, mid, lo = _split3(bc)
            bc_t = _dot_nt(eye, hi) + _dot_nt(eye, mid) + _dot_nt(eye, lo)
            bend_row = bc[last:last + 1, :]
            heads = []
            for h in range(HEADS):
                ji = SMALL_GATE0 + direction * 2 * HEADS + h
                jf = ji + HEADS
                c = dict(ji=ji, jf=jf)
                c["q"] = qk_ref[bi, :, h * QK:(h + 1) * QK].astype(F32)
                c["k"] = qk_ref[bi, :, QK_W + h * QK:QK_W + (h + 1) * QK].astype(F32)
                c["v_ext"] = jnp.concatenate([v_ref[bi, :, h * DV:(h + 1) * DV].astype(MXU_DTYPE), ones_cols], axis=-1)
                c["cn_old"] = cn_s[bi, direction, h]
                c["m_old"] = m_s[bi, direction, h]
                heads.append(c)
            if with_output:
                for c in heads:
                    c["qk"] = _dot_nt(c["q"], c["k"])
                    c["qc"] = _dot(c["q"], c["cn_old"])
            for c in heads:
                b_end = jnp.broadcast_to(bend_row[:, c["jf"]:c["jf"] + 1], (1, LANES))
                i_col = jnp.broadcast_to(gp[:, c["ji"]:c["ji"] + 1], (CHUNK, LANES))
                c["b_col"] = jnp.broadcast_to(bc[:, c["jf"]:c["jf"] + 1], (CHUNK, LANES))
                log_w = b_end - c["b_col"] + i_col
                c["m_new"] = jnp.maximum(b_end + c["m_old"], jnp.max(log_w, axis=0, keepdims=True))
                c["kw"] = (c["k"] * jnp.exp(log_w - c["m_new"])).astype(MXU_DTYPE)
                c["decay"] = jnp.exp(b_end + c["m_old"] - c["m_new"])
            for c in heads:
                c["kw_t"] = _transpose_mxu(c["kw"])
            if with_output:
                for c in heads:
                    i_row = gp_t[c["ji"]:c["ji"] + 1, :]
                    b_row = bc_t[c["jf"]:c["jf"] + 1, :]
                    log_d = jnp.where(seen, _lane_tile(c["b_col"], CHUNK) - b_row + i_row, -jnp.inf)
                    log_inter = c["b_col"] + c["m_old"]
                    c["m_t"] = jnp.maximum(log_inter, jnp.max(log_d, axis=-1, keepdims=True))
                    c["s"] = (c["qk"] * jnp.exp(log_d - _lane_tile(c["m_t"], CHUNK))).astype(MXU_DTYPE)
                    c["w_inter"] = jnp.exp(log_inter - c["m_t"])
                for c in heads:
                    c["sv"] = _dot(c["s"], c["v_ext"])
            for h, c in enumerate(heads):
                cn_s[bi, direction, h] = _lane_tile(c["decay"], DV + LANES) * c["cn_old"] + _dot(c["kw_t"], c["v_ext"])
                m_s[bi, direction, h] = c["m_new"]
            outs = []
            if with_output:
                for c in heads:
                    sv, qc, w_inter = c["sv"], c["qc"], c["w_inter"]
                    num = sv[:, :DV] + _lane_tile(w_inter, DV) * qc[:, :DV]
                    den = jnp.abs(sv[:, DV:] + w_inter * qc[:, DV:])
                    outs.append(num / _lane_tile(jnp.maximum(den, jnp.exp(-c["m_t"])), DV))
            if with_output:
                (hf_ref if direction == 0 else hb_ref)[bi] = jnp.concatenate(outs, axis=-1)

    if not with_output:
        @pl.when(step == pl.num_programs(0) - 1)
        def _():
            cn_out[...] = cn_s[...]
            m_out[...] = m_s[...]


def _mlstm_scan(qk, main, small, gate_bias, states, *, with_output):
    cn0, m0 = states
    batch, t, _ = qk.shape
    nc = t // CHUNK
    in_specs = _scan_specs(batch, nc, (0, COL_V_M, 0), (2 * QK_W, V_W, LANES))
    in_specs += [_whole(gate_bias.shape), _whole(cn0.shape), _whole(m0.shape)]
    if with_output:
        out_specs = _scan_out_specs(batch, nc)
        out_shape = [jax.ShapeDtypeStruct((batch, t, V_W), F32)] * 2
    else:
        out_specs = [_whole(cn0.shape), _whole(m0.shape)]
        out_shape = [jax.ShapeDtypeStruct(s.shape, F32) for s in states]
    return pl.pallas_call(
        functools.partial(_mlstm_kernel, with_output=with_output, batch=batch),
        grid=(nc,),
        in_specs=in_specs,
        out_specs=out_specs,
        out_shape=out_shape,
        scratch_shapes=[pltpu.VMEM(cn0.shape, F32), pltpu.VMEM(m0.shape, F32)],
        compiler_params=_cparams(("arbitrary",)),
        name="mlstm_scan_out" if with_output else "mlstm_scan_state",
    )(qk, main, small, qk, main, small, gate_bias, cn0, m0)


def _gla_exact_intra(q, k, v, b, direction):
    row_id = lax.broadcasted_iota(jnp.int32, (CHUNK, 1), 0)

    def row(t, acc):
        pick = row_id == t
        b_t = jnp.sum(jnp.where(pick, b, 0.0), axis=0, keepdims=True)
        q_t = jnp.sum(jnp.where(pick, q, 0.0), axis=0, keepdims=True)
        ok = (row_id <= t) if direction == 0 else (row_id >= t)
        e = jnp.exp(jnp.where(ok, b_t - b, -jnp.inf))
        sc = jnp.sum(q_t * k * e, axis=-1, keepdims=True)
        o_t = jnp.sum(sc * v, axis=0, keepdims=True)
        return jnp.where(pick, o_t, acc)

    return lax.fori_loop(0, CHUNK, row, jnp.zeros((CHUNK, DV), F32))


def _gla_kernel(*refs, with_output, batch):
    (qk_f, v_f, sm_f, qk_b, v_b, sm_b, w2_ref, b2_ref, s0_ref) = refs[:9]
    if with_output:
        of_ref, ob_ref, s_s, b_s, inter_s = refs[9:]
    else:
        s_out, s_s = refs[9:]
    step = pl.program_id(0)

    @pl.when(step == 0)
    def _():
        s_s[...] = s0_ref[...]

    worst_decay = []
    for direction, (qk_ref, v_ref, sm_ref) in enumerate(((qk_f, v_f, sm_f), (qk_b, v_b, sm_b))):
        seen, seen01 = _chunk_masks(direction)
        last = CHUNK - 1 if direction == 0 else 0
        for bi in range(batch):
            z = _dot(sm_ref[bi], w2_ref[direction]) + b2_ref[direction]
            log_a = _log_sigmoid(z) * (1.0 / GLA_TAU)
            b_all = _dot_exact_lhs(seen01, log_a)
            outs, inters = [], []
            for h in range(HEADS):
                q = qk_ref[bi, :, h * QK:(h + 1) * QK].astype(F32) * (QK ** -0.5)
                k = qk_ref[bi, :, QK_W + h * QK:QK_W + (h + 1) * QK].astype(F32)
                v = v_ref[bi, :, h * DV:(h + 1) * DV]
                b = b_all[:, h * QK:(h + 1) * QK]
                b_end = b[last:last + 1, :]
                st_old = s_s[bi, direction, h]
                k_dec = k * jnp.exp(b_end - b)
                s_s[bi, direction, h] = st_old * jnp.exp(b_end) + _dot_tn_xlu(v, k_dec)
                if with_output:
                    q_dec = q * jnp.exp(b)
                    inter = _dot_nt(q_dec, st_old)
                    scores = jnp.where(seen, _dot_nt(q_dec, k * jnp.exp(-b)), 0.0)
                    outs.append(_dot(scores, v) + inter)
                    inters.append(inter)
            if with_output:
                (of_ref if direction == 0 else ob_ref)[bi] = jnp.concatenate(outs, axis=-1)
                b_s[bi, direction] = b_all
                inter_s[bi, direction] = jnp.concatenate(inters, axis=-1)
                worst_decay.append(jnp.max(-b_all[last:last + 1, :]))

    if with_output:
        @pl.when(functools.reduce(jnp.maximum, worst_decay) > GLA_SAFE_DECAY)
        def _():
            for direction, (qk_ref, v_ref, o_ref) in enumerate(((qk_f, v_f, of_ref), (qk_b, v_b, ob_ref))):
                last = CHUNK - 1 if direction == 0 else 0
                for bi in range(batch):
                    for h in range(HEADS):
                        b = b_s[bi, direction, :, h * QK:(h + 1) * QK]

                        @pl.when(jnp.max(-b[last:last + 1, :]) > GLA_SAFE_DECAY)
                        def _():
                            q = qk_ref[bi, :, h * QK:(h + 1) * QK].astype(F32) * (QK ** -0.5)
                            k = qk_ref[bi, :, QK_W + h * QK:QK_W + (h + 1) * QK].astype(F32)
                            v = v_ref[bi, :, h * DV:(h + 1) * DV].astype(F32)
                            o_ref[bi, :, h * DV:(h + 1) * DV] = (
                                inter_s[bi, direction, :, h * DV:(h + 1) * DV]
                                + _gla_exact_intra(q, k, v, b, direction))
    else:
        @pl.when(step == pl.num_programs(0) - 1)
        def _():
            s_out[...] = s_s[...]


def _gla_scan(main, small, w2p, b2, s0, *, with_output):
    batch, t, _ = main.shape
    nc = t // CHUNK
    in_specs = _scan_specs(batch, nc, (COL_QK_G, COL_V_G, 0), (2 * QK_W, V_W, LANES))
    in_specs += [_whole(w2p.shape), _whole(b2.shape), _whole(s0.shape)]
    scratch = [pltpu.VMEM(s0.shape, F32)]
    if with_output:
        out_specs = _scan_out_specs(batch, nc)
        out_shape = [jax.ShapeDtypeStruct((batch, t, V_W), F32)] * 2
        scratch += [pltpu.VMEM((batch, 2, CHUNK, QK_W), F32), pltpu.VMEM((batch, 2, CHUNK, V_W), F32)]
    else:
        out_specs = _whole(s0.shape)
        out_shape = jax.ShapeDtypeStruct(s0.shape, F32)
    return pl.pallas_call(
        functools.partial(_gla_kernel, with_output=with_output, batch=batch),
        grid=(nc,),
        in_specs=in_specs,
        out_specs=out_specs,
        out_shape=out_shape,
        scratch_shapes=scratch,
        compiler_params=_cparams(("arbitrary",)),
        name="gla_scan_out" if with_output else "gla_scan_state",
    )(main, main, small, main, main, small, w2p, b2, s0)


def _head_rms(a):
    return jnp.concatenate([_rms(a[:, h * DV:(h + 1) * DV]) for h in range(HEADS)], axis=-1)


def _merge_kernel(hmf, hmb, ogf, ogb, om, rg, mgm, mgg, x_ref, mln, gln, wpm, wpg, wo, gt1, g2, sc2, sh2,
                  wr, br, x1_ref, h2_ref, wt_ref):
    tm = x_ref.shape[0]
    parts = [slice(i * tm // MERGE_SPLIT, (i + 1) * tm // MERGE_SPLIT) for i in range(MERGE_SPLIT)]
    y_m = [_head_rms(hmf[r, :] + hmb[r, :]) * mln[...] * jax.nn.sigmoid(om[r, :].astype(F32)) for r in parts]
    p_m = [_dot(a, wpm[...]) for a in y_m]
    y_g = [_head_rms(ogf[r, :] + ogb[r, :]) * gln[...] * _silu(rg[r, :].astype(F32)) for r in parts]
    p_g = [_dot(a, wpg[...]) for a in y_g]
    y = [jax.nn.sigmoid(mgm[r, :].astype(F32)) * a + jax.nn.sigmoid(mgg[r, :].astype(F32)) * b
         for r, a, b in zip(parts, p_m, p_g)]
    mix = [_dot(a, wo[...]) for a in y]
    h2_parts = []
    for r, a in zip(parts, mix):
        x1 = x_ref[r, :] + gt1[...] * a
        x1_ref[r, :] = x1
        h2_parts.append(_rms(x1) * g2[...] * (1.0 + sc2[...]) + sh2[...])
    h2 = jnp.concatenate(h2_parts, axis=0)
    h2_ref[...] = h2.astype(MXU_DTYPE)

    hh, hm_, _ = _split3(h2)
    wh, wm_, _ = _split3(wr[...])
    lg = _dot(hh, wh) + _dot(hh, wm_) + _dot(hm_, wh) + br[...]
    lane = lax.broadcasted_iota(jnp.int32, lg.shape, 1)

    def masked_softmax(mask):
        l = jnp.where(mask, lg, -jnp.inf)
        e = jnp.exp(l - jnp.max(l, axis=-1, keepdims=True))
        return e / jnp.sum(e, axis=-1, keepdims=True)

    def top1(p, mask):
        pm = jnp.where(mask, p, -1.0)
        best = jnp.max(pm, axis=-1, keepdims=True)
        idx = jnp.min(jnp.where(jnp.logical_and(mask, pm == best), lane, LANES), axis=-1, keepdims=True)
        return best, idx

    gmask = lane < N_GROUPS
    grp_p, grp = top1(masked_softmax(gmask), gmask)
    e_lo = ROUTE_E0 + grp * EXPERTS_PER_GROUP
    emask = jnp.logical_and(lane >= e_lo, lane < e_lo + EXPERTS_PER_GROUP)
    p_in = masked_softmax(emask)
    p1, i1 = top1(p_in, emask)
    p2, i2 = top1(p_in, jnp.logical_and(emask, lane != i1))
    tot = p1 + p2
    wt_ref[...] = (jnp.where(lane == i1, grp_p * p1 / tot, 0.0)
                   + jnp.where(lane == i2, grp_p * p2 / tot, 0.0)
                   + jnp.where(lane == grp, 1.0, 0.0))


def _merge(hmf, hmb, ogf, ogb, main, x2, mln, gln, wpm, wpg, wo, gt1, g2, sc2, sh2, wr, br, *, tm, rows_per_batch):
    m = x2.shape[0]
    tpb = rows_per_batch // tm
    rowblk = pl.BlockSpec((tm, D_MODEL), lambda i: (i, 0))
    colblk = lambda cb: pl.BlockSpec((tm, D_MODEL), lambda i, cb=cb: (i, cb))
    vec = pl.BlockSpec((1, D_MODEL), lambda i: (0, 0))
    bvec = pl.BlockSpec((None, 1, D_MODEL), lambda i: (i // tpb, 0, 0))
    wmat = pl.BlockSpec((D_MODEL, D_MODEL), lambda i: (0, 0))
    return pl.pallas_call(
        _merge_kernel,
        grid=(m // tm,),
        in_specs=[rowblk, rowblk, rowblk, rowblk, colblk(COL_O_M), colblk(COL_R_G), colblk(COL_MG_M),
                  colblk(COL_MG_G), rowblk, vec, vec, wmat, wmat, wmat, bvec, vec, bvec, bvec,
                  pl.BlockSpec((D_MODEL, LANES), lambda i: (0, 0)), pl.BlockSpec((1, LANES), lambda i: (0, 0))],
        out_specs=[rowblk, rowblk, pl.BlockSpec((tm, LANES), lambda i: (i, 0))],
        out_shape=[jax.ShapeDtypeStruct((m, D_MODEL), F32), jax.ShapeDtypeStruct((m, D_MODEL), MXU_DTYPE),
                   jax.ShapeDtypeStruct((m, LANES), F32)],
        compiler_params=_cparams(("parallel",), 56 * 1024 * 1024),
        name="merge_route",
    )(hmf, hmb, ogf, ogb, main, main, main, main, x2, mln, gln, wpm, wpg, wo, gt1, g2, sc2, sh2, wr, br)


MERGE_SPLIT = 2
MOE_BLK = 64
MOE_COMMON_BLKS = (3, 4, 5, 6)
MOE_LOOP_BLKS = 4
MOE_EXPERTS_PER_STEP = 2


def _moe_kernel(h2_ref, wt_ref, wup_ref, wdn_ref, x1_ref, gt2_ref, gf_ref, o_ref,
                xs_ref, ys_ref, ws_ref, dest_ref, blk_ref):
    step = pl.program_id(1)
    tm = h2_ref.shape[0]
    n_rows = xs_ref.shape[0]

    @pl.when(step == 0)
    def _():
        r = wt_ref[...]
        lane = lax.broadcasted_iota(jnp.int32, (tm, LANES), 1)
        lane1 = lax.broadcasted_iota(jnp.int32, (1, LANES), 1)
        gm = jnp.where(lane < N_GROUPS, r, 0.0)
        earlier = (lax.broadcasted_iota(jnp.int32, (tm, tm), 1)
                   < lax.broadcasted_iota(jnp.int32, (tm, tm), 0)).astype(MXU_DTYPE)
        before = _dot(earlier, gm)
        padded = jnp.floor((jnp.sum(gm, axis=0, keepdims=True) + (MOE_BLK - 1)) * (1.0 / MOE_BLK)) * MOE_BLK
        start = jnp.zeros((1, LANES), F32)
        run = jnp.zeros((1, 1), F32)
        for g in range(N_GROUPS):
            size = jnp.sum(jnp.where(lane1 == g, padded, 0.0), axis=-1, keepdims=True)
            start = jnp.where(lane1 == g, run, start)
            blk_ref[g] = (jnp.sum(run) * (1.0 / MOE_BLK)).astype(jnp.int32)
            blk_ref[N_GROUPS + g] = (jnp.sum(size) * (1.0 / MOE_BLK)).astype(jnp.int32)
            run = run + size
        dest = jnp.sum(gm * (start + before), axis=-1, keepdims=True)
        dest_ref[...] = jnp.broadcast_to(dest, (tm, LANES))
        dest_row = dest_ref[...].T[0:1, :].astype(jnp.int32)
        perm = (lax.broadcasted_iota(jnp.int32, (n_rows, tm), 0) == dest_row).astype(MXU_DTYPE)
        xs_ref[...] = _dot(perm, h2_ref[...]).astype(MXU_DTYPE)
        moved = _dot(perm, jnp.concatenate(_split3(r), axis=-1))
        ws_ref[...] = moved[:, :LANES] + moved[:, LANES:2 * LANES] + moved[:, 2 * LANES:]
        ys_ref[...] = jnp.zeros_like(ys_ref)

    group = step // (EXPERTS_PER_GROUP // MOE_EXPERTS_PER_STEP)
    first_blk = blk_ref[group]
    n_blk = blk_ref[N_GROUPS + group]

    def expert_on(r0, rows):
        x = xs_ref[pl.ds(r0, rows), :]
        ws = ws_ref[pl.ds(r0, rows), :]
        lane_b = lax.broadcasted_iota(jnp.int32, (rows, LANES), 1)
        experts = range(MOE_EXPERTS_PER_STEP)
        gus = [_dot(x, wup_ref[k]) for k in experts]
        hiddens = [(_silu(gu[:, :D_EXPERT]) * gu[:, D_EXPERT:]).astype(MXU_DTYPE) for gu in gus]
        w_cols = [jnp.sum(jnp.where(lane_b == ROUTE_E0 + step * MOE_EXPERTS_PER_STEP + k, ws, 0.0),
                          axis=-1, keepdims=True) for k in experts]
        ys = [_dot(hiddens[k], wdn_ref[k]) for k in experts]
        ys_ref[pl.ds(r0, rows), :] += sum(y * w for y, w in zip(ys, w_cols))

    for k in MOE_COMMON_BLKS:
        @pl.when(n_blk == k)
        def _():
            expert_on(pl.multiple_of(first_blk * MOE_BLK, MOE_BLK), k * MOE_BLK)

    @pl.when(functools.reduce(jnp.logical_and, [n_blk != k for k in MOE_COMMON_BLKS]))
    def _():
        n_big = n_blk // MOE_LOOP_BLKS

        def big(j, carry):
            expert_on(pl.multiple_of((first_blk + j * MOE_LOOP_BLKS) * MOE_BLK, MOE_BLK), MOE_LOOP_BLKS * MOE_BLK)
            return carry

        def single(j, carry):
            expert_on(pl.multiple_of((first_blk + j) * MOE_BLK, MOE_BLK), MOE_BLK)
            return carry

        lax.fori_loop(0, n_big, big, 0)
        lax.fori_loop(n_big * MOE_LOOP_BLKS, n_blk, single, 0)

    @pl.when(step == pl.num_programs(1) - 1)
    def _():
        dest = dest_ref[...][:, :1].astype(jnp.int32)
        unperm = (lax.broadcasted_iota(jnp.int32, (tm, n_rows), 1) == dest).astype(MXU_DTYPE)
        y = _dot(unperm, ys_ref[...])
        o_ref[...] = _rms(x1_ref[...] + gt2_ref[...] * y) * gf_ref[...]


def _moe_final(h2, wt, w_up, w_down, x1, gt2, g_final, *, tm, rows_per_batch):
    m = h2.shape[0]
    tpb = rows_per_batch // tm
    n_rows = tm + N_GROUPS * MOE_BLK
    rowblk = pl.BlockSpec((tm, D_MODEL), lambda i, e: (i, 0))
    return pl.pallas_call(
        _moe_kernel,
        grid=(m // tm, N_EXPERTS // MOE_EXPERTS_PER_STEP),
        in_specs=[rowblk,
                  pl.BlockSpec((tm, LANES), lambda i, e: (i, 0)),
                  pl.BlockSpec((MOE_EXPERTS_PER_STEP, D_MODEL, 2 * D_EXPERT), lambda i, e: (e, 0, 0)),
                  pl.BlockSpec((MOE_EXPERTS_PER_STEP, D_EXPERT, D_MODEL), lambda i, e: (e, 0, 0)),
                  rowblk,
                  pl.BlockSpec((None, 1, D_MODEL), lambda i, e: (i // tpb, 0, 0)),
                  pl.BlockSpec((1, D_MODEL), lambda i, e: (0, 0))],
        out_specs=rowblk,
        out_shape=jax.ShapeDtypeStruct((m, D_MODEL), F32),
        scratch_shapes=[pltpu.VMEM((n_rows, D_MODEL), MXU_DTYPE), pltpu.VMEM((n_rows, D_MODEL), F32),
                        pltpu.VMEM((n_rows, LANES), F32), pltpu.VMEM((tm, LANES), F32),
                        pltpu.SMEM((2 * N_GROUPS,), jnp.int32)],
        compiler_params=_cparams(("parallel", "arbitrary"), 56 * 1024 * 1024),
        name="moe_final",
    )(h2, wt, w_up, w_down, x1, gt2, g_final)


def _empty_states(batch):
    ml = (jnp.zeros((batch, 2, HEADS, QK, DV + LANES), F32),
          jnp.full((batch, 2, HEADS, 1, LANES), NEG_BIG, F32))
    gla = jnp.zeros((batch, 2, HEADS, DV, QK), F32)
    return ml, gla


def kernel(x, c, ctx, c_ctx, w_mod, b_mod, g_norm1, w_in, ml_conv, ml_conv_b, b_mgate, ml_norm, gla_w2, gla_b2,
           gla_norm, w_proj_m, w_proj_g, w_out, g_norm2, w_grp, b_grp, w_rexp, b_rexp, w_up, w_down, g_final):
    batch, t, d = x.shape
    t_ctx = ctx.shape[1]
    assert d == D_MODEL and w_mod.shape[0] == 1 and w_in.shape[2] == sum(IN_SIZES)
    assert t % (GRID_W * 16) == 0 and t % CHUNK == 0 and t_ctx % CHUNK == 0

    off = [0]
    for s in IN_SIZES:
        off.append(off[-1] + s)
    wi = w_in[0].astype(MXU_DTYPE)
    w_main = jnp.concatenate([wi[:, off[0]:off[4]], wi[:, off[5]:off[9]], wi[:, off[10]:off[12]]], axis=1)
    w_small = jnp.concatenate([wi[:, off[9]:off[10]], wi[:, off[4]:off[5]],
                               jnp.zeros((d, LANES - 2 * GLA_RANK - 4 * HEADS), MXU_DTYPE)], axis=1)
    gate_bias = jnp.zeros((1, LANES), F32).at[0, SMALL_GATE0:SMALL_GATE0 + 4 * HEADS].set(b_mgate[0])
    w2p = jnp.zeros((2, LANES, QK_W), F32)
    w2p = w2p.at[0, 0:GLA_RANK].set(gla_w2[0, 0]).at[1, GLA_RANK:2 * GLA_RANK].set(gla_w2[0, 1])
    b2 = gla_b2[0][:, None, :]
    conv_w = ml_conv[0].reshape(9, 2 * QK_W)
    conv_b = ml_conv_b[0][None, :]
    w_route = jnp.concatenate([w_grp[0], w_rexp[0], jnp.zeros((d, LANES - N_GROUPS - N_EXPERTS), F32)], axis=1)
    b_route = jnp.concatenate([b_grp[0], b_rexp[0], jnp.zeros((LANES - N_GROUPS - N_EXPERTS,), F32)])[None, :]

    cc = jnp.concatenate([c, c_ctx[None, :], jnp.zeros((8 - batch - 1, d), F32)], axis=0)
    mod = _modulation(cc, w_mod[0], b_mod[0][None, :])
    sh1, sc1, gt1, sh2, sc2, gt2 = [mod[:batch, i * d:(i + 1) * d][:, None, :] for i in range(6)]
    sh1c, sc1c = [mod[batch:batch + 1, i * d:(i + 1) * d][:, None, :] for i in range(2)]
    g1 = g_norm1[0][None, :]

    main_c, small_c = _in_proj(ctx.reshape(batch * t_ctx, d), g1, sc1c, sh1c, w_main, w_small,
                               tm=batch * t_ctx, rows_per_batch=batch * t_ctx, last_col_block=COL_V_G)
    qk_c = _conv_silu(main_c, conv_w, conv_b, batch=batch, rows=1, cols=t_ctx)
    ml0, gla0 = _empty_states(batch)
    main_c3, small_c3 = main_c.reshape(batch, t_ctx, main_c.shape[1]), small_c.reshape(batch, t_ctx, LANES)
    ml_states = _mlstm_scan(qk_c.reshape(batch, t_ctx, 2 * QK_W), main_c3, small_c3, gate_bias, ml0, with_output=False)
    gla_state = _gla_scan(main_c3, small_c3, w2p, b2, gla0, with_output=False)

    x2 = x.reshape(batch * t, d)
    main, small = _in_proj(x2, g1, sc1, sh1, w_main, w_small, tm=1024, rows_per_batch=t)
    qk = _conv_silu(main, conv_w, conv_b, batch=batch, rows=t // GRID_W, cols=GRID_W)
    main3, small3 = main.reshape(batch, t, MAIN_W), small.reshape(batch, t, LANES)
    hm_f, hm_b = [a.reshape(batch * t, V_W) for a in
                  _mlstm_scan(qk.reshape(batch, t, 2 * QK_W), main3, small3, gate_bias, ml_states, with_output=True)]
    og_f, og_b = [a.reshape(batch * t, V_W) for a in _gla_scan(main3, small3, w2p, b2, gla_state, with_output=True)]

    x1, h2, wt = _merge(hm_f, hm_b, og_f, og_b, main, x2, ml_norm, gla_norm,
                        w_proj_m[0].astype(MXU_DTYPE), w_proj_g[0].astype(MXU_DTYPE), w_out[0].astype(MXU_DTYPE),
                        gt1, g_norm2, sc2, sh2, w_route, b_route, tm=512, rows_per_batch=t)
    out = _moe_final(h2, wt, w_up[0].astype(MXU_DTYPE), w_down[0].astype(MXU_DTYPE), x1, gt2, g_final[None, :],
                     tm=1024, rows_per_batch=t)
    return out.reshape(batch, t, d)
```

```python
import functools

import jax
import jax.numpy as jnp
from jax import lax
from jax.experimental import pallas as pl
from jax.experimental.pallas import tpu as pltpu

D_MODEL = 1024
GRID_W = 64
CHUNK = 256
EPS = 1e-6
NEG_BIG = -1e30
HEADS = 4
QK = D_MODEL // 8
DV = D_MODEL // 4
QK_W = HEADS * QK
V_W = HEADS * DV
GLA_RANK = 16
GLA_TAU = 16.0
N_GROUPS = 4
EXPERTS_PER_GROUP = 4
N_EXPERTS = N_GROUPS * EXPERTS_PER_GROUP
D_EXPERT = D_MODEL // 2
IN_SIZES = (QK_W, QK_W, V_W, V_W, 4 * HEADS, QK_W, QK_W, V_W, V_W, 2 * GLA_RANK, D_MODEL, D_MODEL)

LANES = 128
CONV_TILE = 256
MXU_DTYPE = jnp.bfloat16
F32 = jnp.float32
VMEM_LIMIT = 48 * 1024 * 1024

COL_QK_M, COL_V_M, COL_O_M, COL_QK_G, COL_V_G, COL_R_G, COL_MG_M, COL_MG_G = range(8)
MAIN_W = 8 * D_MODEL
SMALL_GATE0 = 2 * GLA_RANK
GLA_SAFE_DECAY = 80.0
ROUTE_E0 = N_GROUPS


def _dot(a, b):
    return jnp.dot(a.astype(MXU_DTYPE), b.astype(MXU_DTYPE), preferred_element_type=F32)


def _dot_nt(a, b):
    return lax.dot_general(a.astype(MXU_DTYPE), b.astype(MXU_DTYPE), (((1,), (1,)), ((), ())),
                           preferred_element_type=F32)


def _transpose_mxu(a):
    m = a.shape[1]
    eye = (lax.broadcasted_iota(jnp.int32, (m, m), 0) == lax.broadcasted_iota(jnp.int32, (m, m), 1))
    return _dot_nt(eye.astype(MXU_DTYPE), a).astype(MXU_DTYPE)


def _dot_tn_xlu(a, b):
    return lax.dot_general(a.astype(MXU_DTYPE), b.astype(MXU_DTYPE), (((0,), (0,)), ((), ())),
                           preferred_element_type=F32)


def _split3(x):
    hi = x.astype(MXU_DTYPE)
    r1 = x - hi.astype(F32)
    mid = r1.astype(MXU_DTYPE)
    lo = (r1 - mid.astype(F32)).astype(MXU_DTYPE)
    return hi, mid, lo


def _dot_exact_lhs(a01, x):
    hi, mid, lo = _split3(x)
    return _dot(a01, hi) + _dot(a01, mid) + _dot(a01, lo)


def _log_sigmoid(x):
    return jnp.minimum(x, 0.0) - jnp.log(1.0 + jnp.exp(-jnp.abs(x)))


def _silu(x):
    return x * jax.nn.sigmoid(x)


def _rms(x):
    return x * lax.rsqrt(jnp.mean(x * x, axis=-1, keepdims=True) + EPS)


def _cparams(sem, vmem_limit=VMEM_LIMIT):
    return pltpu.CompilerParams(dimension_semantics=sem, vmem_limit_bytes=vmem_limit)


def _mod_kernel(c_ref, w_ref, b_ref, o_ref):
    o_ref[...] = _dot(_silu(c_ref[...]), w_ref[...]) + b_ref[...]


def _modulation(cc, w_mod, b_mod):
    n = w_mod.shape[1]
    tn = 512
    return pl.pallas_call(
        _mod_kernel,
        grid=(n // tn,),
        in_specs=[pl.BlockSpec((8, D_MODEL), lambda j: (0, 0)),
                  pl.BlockSpec((D_MODEL, tn), lambda j: (0, j)),
                  pl.BlockSpec((1, tn), lambda j: (0, j))],
        out_specs=pl.BlockSpec((8, tn), lambda j: (0, j)),
        out_shape=jax.ShapeDtypeStruct((8, n), F32),
        compiler_params=_cparams(("arbitrary",)),
        name="modulation",
    )(cc, w_mod, b_mod)


def _inproj_kernel(x_ref, g_ref, sc_ref, sh_ref, w_ref, ws_ref, o_ref, os_ref, xn_ref):
    @pl.when(pl.program_id(1) == 0)
    def _():
        xn = _rms(x_ref[...]) * g_ref[...] * (1.0 + sc_ref[...]) + sh_ref[...]
        xn_ref[...] = xn.astype(MXU_DTYPE)
        os_ref[...] = _dot(xn_ref[...], ws_ref[...])

    o_ref[...] = _dot(xn_ref[...], w_ref[...]).astype(o_ref.dtype)


def _in_proj(x2, g, sc, sh, w_main, w_small, *, tm, rows_per_batch, last_col_block=COL_MG_G):
    m = x2.shape[0]
    tn = 2048
    col_tiles = last_col_block * D_MODEL // tn + 1
    tiles_per_batch = rows_per_batch // tm
    vec = pl.BlockSpec((None, 1, D_MODEL), lambda i, j: (i // tiles_per_batch, 0, 0))
    return pl.pallas_call(
        _inproj_kernel,
        grid=(m // tm, col_tiles),
        in_specs=[pl.BlockSpec((tm, D_MODEL), lambda i, j: (i, 0)),
                  pl.BlockSpec((1, D_MODEL), lambda i, j: (0, 0)),
                  vec, vec,
                  pl.BlockSpec((D_MODEL, tn), lambda i, j: (0, j)),
                  pl.BlockSpec((D_MODEL, LANES), lambda i, j: (0, 0))],
        out_specs=[pl.BlockSpec((tm, tn), lambda i, j: (i, j)),
                   pl.BlockSpec((tm, LANES), lambda i, j: (i, 0))],
        out_shape=[jax.ShapeDtypeStruct((m, col_tiles * tn), MXU_DTYPE), jax.ShapeDtypeStruct((m, LANES), F32)],
        scratch_shapes=[pltpu.VMEM((tm, D_MODEL), MXU_DTYPE)],
        compiler_params=_cparams(("parallel", "arbitrary")),
        name="in_proj",
    )(x2, g, sc, sh, w_main, w_small)


def _conv_kernel(x_ref, w_ref, b_ref, o_ref, *, rows, cols):
    scale = jnp.where(pl.program_id(1) * CONV_TILE >= QK_W, QK ** -0.5, 1.0).astype(F32)
    w = w_ref[...]
    bias = b_ref[...]
    tpos = lax.broadcasted_iota(jnp.int32, (cols, 1), 0)
    has_left = tpos >= 1
    has_right = tpos < cols - 1

    def row_filters(j):
        tile = x_ref[pl.ds(pl.multiple_of(j * cols, cols), cols), :].astype(F32)
        left = jnp.where(has_left, pltpu.roll(tile, 1, axis=0), 0.0)
        right = jnp.where(has_right, pltpu.roll(tile, cols - 1, axis=0), 0.0)
        return [left * w[3 * i:3 * i + 1, :] + tile * w[3 * i + 1:3 * i + 2, :] + right * w[3 * i + 2:3 * i + 3, :]
                for i in range(3)]

    def finish(j, acc):
        o_ref[pl.ds(pl.multiple_of(j * cols, cols), cols), :] = (_silu(acc + bias) * scale).astype(o_ref.dtype)

    first = row_filters(0)

    def body(j, carry):
        acc, below = carry
        h = row_filters(j)
        finish(j - 1, acc + h[2])
        return below + h[1], h[0]

    acc, _ = lax.fori_loop(1, rows, body, (first[1], first[0]))
    finish(rows - 1, acc)


def _conv_silu(main, conv_w, conv_b, *, batch, rows, cols):
    t = rows * cols
    nct = 2 * QK_W // CONV_TILE
    return pl.pallas_call(
        functools.partial(_conv_kernel, rows=rows, cols=cols),
        grid=(batch, nct),
        in_specs=[pl.BlockSpec((t, CONV_TILE), lambda b, c: (b, c)),
                  pl.BlockSpec((9, CONV_TILE), lambda b, c: (0, c)),
                  pl.BlockSpec((1, CONV_TILE), lambda b, c: (0, c))],
        out_specs=pl.BlockSpec((t, CONV_TILE), lambda b, c: (b, c)),
        out_shape=jax.ShapeDtypeStruct((batch * t, 2 * QK_W), MXU_DTYPE),
        compiler_params=_cparams(("parallel", "arbitrary")),
        name="conv_silu",
    )(main, conv_w, conv_b)


def _chunk_masks(direction):
    row = lax.broadcasted_iota(jnp.int32, (CHUNK, CHUNK), 0)
    col = lax.broadcasted_iota(jnp.int32, (CHUNK, CHUNK), 1)
    seen = (row >= col) if direction == 0 else (row <= col)
    return seen, seen.astype(MXU_DTYPE)


def _scan_specs(batch, nc, col_blocks, widths):
    specs = []
    for direction in (0, 1):
        for cb, wd in zip(col_blocks, widths):
            if direction == 0:
                specs.append(pl.BlockSpec((batch, CHUNK, wd), lambda c, cb=cb: (0, c, cb)))
            else:
                specs.append(pl.BlockSpec((batch, CHUNK, wd), lambda c, cb=cb: (0, nc - 1 - c, cb)))
    return specs


def _scan_out_specs(batch, nc):
    return [pl.BlockSpec((batch, CHUNK, V_W), lambda c: (0, c, 0)),
            pl.BlockSpec((batch, CHUNK, V_W), lambda c: (0, nc - 1 - c, 0))]


def _whole(shape):
    nd = len(shape)
    return pl.BlockSpec(tuple(shape), lambda c: (0,) * nd)


def _lane_tile(x, width):
    return jnp.concatenate([x] * (width // LANES), axis=-1)


def _mlstm_kernel(*refs, with_output, batch):
    (qk_f, v_f, sm_f, qk_b, v_b, sm_b, bias_ref, cn0_ref, m0_ref) = refs[:9]
    if with_output:
        hf_ref, hb_ref, cn_s, m_s = refs[9:]
    else:
        cn_out, m_out, cn_s, m_s = refs[9:]
    step = pl.program_id(0)

    @pl.when(step == 0)
    def _():
        cn_s[...] = cn0_ref[...]
        m_s[...] = m0_ref[...]

    lane = lax.broadcasted_iota(jnp.int32, (1, LANES), 1)
    gate_lane = jnp.logical_and(lane >= SMALL_GATE0, lane < SMALL_GATE0 + 4 * HEADS)
    forget_lane = jnp.logical_and(gate_lane, ((lane - SMALL_GATE0) % (2 * HEADS)) >= HEADS)
    eye = (lax.broadcasted_iota(jnp.int32, (LANES, LANES), 0)
           == lax.broadcasted_iota(jnp.int32, (LANES, LANES), 1)).astype(MXU_DTYPE)
    ones_cols = jnp.ones((CHUNK, LANES), MXU_DTYPE)

    for direction, (qk_ref, v_ref, sm_ref) in enumerate(((qk_f, v_f, sm_f), (qk_b, v_b, sm_b))):
        seen, seen01 = _chunk_masks(direction)
        last = CHUNK - 1 if direction == 0 else 0
        for bi in range(batch):
            g = sm_ref[bi] + bias_ref[...]
            gp = jnp.where(forget_lane, _log_sigmoid(g), g)
            bc = _dot_exact_lhs(seen01, gp)
            hi, mid, lo = _split3(gp)
            gp_t = _dot_nt(eye, hi) + _dot_nt(eye, mid) + _dot_nt(eye, lo)
            hi, mid, lo = _split3(bc)
            bc_t = _dot_nt(eye, hi) + _dot_nt(eye, mid) + _dot_nt(eye, lo)
            bend_row = bc[last:last + 1, :]
            heads = []
            for h in range(HEADS):
                ji = SMALL_GATE0 + direction * 2 * HEADS + h
                jf = ji + HEADS
                c = dict(ji=ji, jf=jf)
                c["q"] = qk_ref[bi, :, h * QK:(h + 1) * QK].astype(F32)
                c["k"] = qk_ref[bi, :, QK_W + h * QK:QK_W + (h + 1) * QK].astype(F32)
                c["v_ext"] = jnp.concatenate([v_ref[bi, :, h * DV:(h + 1) * DV].astype(MXU_DTYPE), ones_cols], axis=-1)
                c["cn_old"] = cn_s[bi, direction, h]
                c["m_old"] = m_s[bi, direction, h]
                heads.append(c)
            if with_output:
                for c in heads:
                    c["qk"] = _dot_nt(c["q"], c["k"])
                    c["qc"] = _dot(c["q"], c["cn_old"])
            for c in heads:
                b_end = jnp.broadcast_to(bend_row[:, c["jf"]:c["jf"] + 1], (1, LANES))
                i_col = jnp.broadcast_to(gp[:, c["ji"]:c["ji"] + 1], (CHUNK, LANES))
                c["b_col"] = jnp.broadcast_to(bc[:, c["jf"]:c["jf"] + 1], (CHUNK, LANES))
                log_w = b_end - c["b_col"] + i_col
                c["m_new"] = jnp.maximum(b_end + c["m_old"], jnp.max(log_w, axis=0, keepdims=True))
                c["kw"] = (c["k"] * jnp.exp(log_w - c["m_new"])).astype(MXU_DTYPE)
                c["decay"] = jnp.exp(b_end + c["m_old"] - c["m_new"])
            for c in heads:
                c["kw_t"] = _transpose_mxu(c["kw"])
            if with_output:
                for c in heads:
                    i_row = gp_t[c["ji"]:c["ji"] + 1, :]
                    b_row = bc_t[c["jf"]:c["jf"] + 1, :]
                    log_d = jnp.where(seen, _lane_tile(c["b_col"], CHUNK) - b_row + i_row, -jnp.inf)
                    log_inter = c["b_col"] + c["m_old"]
                    c["m_t"] = jnp.maximum(log_inter, jnp.max(log_d, axis=-1, keepdims=True))
                    c["s"] = (c["qk"] * jnp.exp(log_d - _lane_tile(c["m_t"], CHUNK))).astype(MXU_DTYPE)
                    c["w_inter"] = jnp.exp(log_inter - c["m_t"])
                for c in heads:
                    c["sv"] = _dot(c["s"], c["v_ext"])
            for h, c in enumerate(heads):
                cn_s[bi, direction, h] = _lane_tile(c["decay"], DV + LANES) * c["cn_old"] + _dot(c["kw_t"], c["v_ext"])
                m_s[bi, direction, h] = c["m_new"]
            outs = []
            if with_output:
                for c in heads:
                    sv, qc, w_inter = c["sv"], c["qc"], c["w_inter"]
                    num = sv[:, :DV] + _lane_tile(w_inter, DV) * qc[:, :DV]
                    den = jnp.abs(sv[:, DV:] + w_inter * qc[:, DV:])
                    outs.append(num / _lane_tile(jnp.maximum(den, jnp.exp(-c["m_t"])), DV))
            if with_output:
                (hf_ref if direction == 0 else hb_ref)[bi] = jnp.concatenate(outs, axis=-1)

    if not with_output:
        @pl.when(step == pl.num_programs(0) - 1)
        def _():
            cn_out[...] = cn_s[...]
            m_out[...] = m_s[...]


def _mlstm_scan(qk, main, small, gate_bias, states, *, with_output):
    cn0, m0 = states
    batch, t, _ = qk.shape
    nc = t // CHUNK
    in_specs = _scan_specs(batch, nc, (0, COL_V_M, 0), (2 * QK_W, V_W, LANES))
    in_specs += [_whole(gate_bias.shape), _whole(cn0.shape), _whole(m0.shape)]
    if with_output:
        out_specs = _scan_out_specs(batch, nc)
        out_shape = [jax.ShapeDtypeStruct((batch, t, V_W), F32)] * 2
    else:
        out_specs = [_whole(cn0.shape), _whole(m0.shape)]
        out_shape = [jax.ShapeDtypeStruct(s.shape, F32) for s in states]
    return pl.pallas_call(
        functools.partial(_mlstm_kernel, with_output=with_output, batch=batch),
        grid=(nc,),
        in_specs=in_specs,
        out_specs=out_specs,
        out_shape=out_shape,
        scratch_shapes=[pltpu.VMEM(cn0.shape, F32), pltpu.VMEM(m0.shape, F32)],
        compiler_params=_cparams(("arbitrary",)),
        name="mlstm_scan_out" if with_output else "mlstm_scan_state",
    )(qk, main, small, qk, main, small, gate_bias, cn0, m0)


def _gla_exact_intra(q, k, v, b, direction):
    row_id = lax.broadcasted_iota(jnp.int32, (CHUNK, 1), 0)

    def row(t, acc):
        pick = row_id == t
        b_t = jnp.sum(jnp.where(pick, b, 0.0), axis=0, keepdims=True)
        q_t = jnp.sum(jnp.where(pick, q, 0.0), axis=0, keepdims=True)
        ok = (row_id <= t) if direction == 0 else (row_id >= t)
        e = jnp.exp(jnp.where(ok, b_t - b, -jnp.inf))
        sc = jnp.sum(q_t * k * e, axis=-1, keepdims=True)
        o_t = jnp.sum(sc * v, axis=0, keepdims=True)
        return jnp.where(pick, o_t, acc)

    return lax.fori_loop(0, CHUNK, row, jnp.zeros((CHUNK, DV), F32))


def _gla_kernel(*refs, with_output, batch, n_cast=0):
    (qk_f, v_f, sm_f, qk_b, v_b, sm_b, w2_ref, b2_ref, s0_ref) = refs[:9]
    cast_in, rest = refs[9:9 + n_cast], refs[9 + n_cast:]
    if with_output:
        of_ref, ob_ref = rest[:2]
        cast_out = rest[2:2 + n_cast]
        s_s, b_s, inter_s = rest[2 + n_cast:]
    else:
        s_out, s_s = rest
    step = pl.program_id(0)

    for src, dst in zip(cast_in, cast_out if with_output else ()):
        dst[...] = src[...].astype(dst.dtype)

    @pl.when(step == 0)
    def _():
        s_s[...] = s0_ref[...]

    worst_decay = []
    for direction, (qk_ref, v_ref, sm_ref) in enumerate(((qk_f, v_f, sm_f), (qk_b, v_b, sm_b))):
        seen, seen01 = _chunk_masks(direction)
        last = CHUNK - 1 if direction == 0 else 0
        for bi in range(batch):
            z = _dot(sm_ref[bi], w2_ref[direction]) + b2_ref[direction]
            log_a = _log_sigmoid(z) * (1.0 / GLA_TAU)
            b_all = _dot_exact_lhs(seen01, log_a)
            outs, inters = [], []
            for h in range(HEADS):
                q = qk_ref[bi, :, h * QK:(h + 1) * QK].astype(F32) * (QK ** -0.5)
                k = qk_ref[bi, :, QK_W + h * QK:QK_W + (h + 1) * QK].astype(F32)
                v = v_ref[bi, :, h * DV:(h + 1) * DV]
                b = b_all[:, h * QK:(h + 1) * QK]
                b_end = b[last:last + 1, :]
                st_old = s_s[bi, direction, h]
                k_dec = k * jnp.exp(b_end - b)
                s_s[bi, direction, h] = st_old * jnp.exp(b_end) + _dot_tn_xlu(v, k_dec)
                if with_output:
                    q_dec = q * jnp.exp(b)
                    inter = _dot_nt(q_dec, st_old)
                    scores = jnp.where(seen, _dot_nt(q_dec, k * jnp.exp(-b)), 0.0)
                    outs.append(_dot(scores, v) + inter)
                    inters.append(inter)
            if with_output:
                (of_ref if direction == 0 else ob_ref)[bi] = jnp.concatenate(outs, axis=-1)
                b_s[bi, direction] = b_all
                inter_s[bi, direction] = jnp.concatenate(inters, axis=-1)
                worst_decay.append(jnp.max(-b_all[last:last + 1, :]))

    if with_output:
        @pl.when(functools.reduce(jnp.maximum, worst_decay) > GLA_SAFE_DECAY)
        def _():
            for direction, (qk_ref, v_ref, o_ref) in enumerate(((qk_f, v_f, of_ref), (qk_b, v_b, ob_ref))):
                last = CHUNK - 1 if direction == 0 else 0
                for bi in range(batch):
                    for h in range(HEADS):
                        b = b_s[bi, direction, :, h * QK:(h + 1) * QK]

                        @pl.when(jnp.max(-b[last:last + 1, :]) > GLA_SAFE_DECAY)
                        def _():
                            q = qk_ref[bi, :, h * QK:(h + 1) * QK].astype(F32) * (QK ** -0.5)
                            k = qk_ref[bi, :, QK_W + h * QK:QK_W + (h + 1) * QK].astype(F32)
                            v = v_ref[bi, :, h * DV:(h + 1) * DV].astype(F32)
                            o_ref[bi, :, h * DV:(h + 1) * DV] = (
                                inter_s[bi, direction, :, h * DV:(h + 1) * DV]
                                + _gla_exact_intra(q, k, v, b, direction))
    else:
        @pl.when(step == pl.num_programs(0) - 1)
        def _():
            s_out[...] = s_s[...]


def _gla_scan(main, small, w2p, b2, s0, *, with_output, cast_along=()):
    batch, t, _ = main.shape
    nc = t // CHUNK
    in_specs = _scan_specs(batch, nc, (COL_QK_G, COL_V_G, 0), (2 * QK_W, V_W, LANES))
    in_specs += [_whole(w2p.shape), _whole(b2.shape), _whole(s0.shape)]
    slab = lambda a: pl.BlockSpec((None,) + a.shape[1:], lambda c: (c, 0, 0))
    in_specs += [slab(a) for a in cast_along]
    scratch = [pltpu.VMEM(s0.shape, F32)]
    if with_output:
        assert all(a.shape[0] == nc for a in cast_along)
        out_specs = _scan_out_specs(batch, nc) + [slab(a) for a in cast_along]
        out_shape = ([jax.ShapeDtypeStruct((batch, t, V_W), F32)] * 2
                     + [jax.ShapeDtypeStruct(a.shape, MXU_DTYPE) for a in cast_along])
        scratch += [pltpu.VMEM((batch, 2, CHUNK, QK_W), F32), pltpu.VMEM((batch, 2, CHUNK, V_W), F32)]
    else:
        out_specs = _whole(s0.shape)
        out_shape = jax.ShapeDtypeStruct(s0.shape, F32)
    return pl.pallas_call(
        functools.partial(_gla_kernel, with_output=with_output, batch=batch, n_cast=len(cast_along)),
        grid=(nc,),
        in_specs=in_specs,
        out_specs=out_specs,
        out_shape=out_shape,
        scratch_shapes=scratch,
        compiler_params=_cparams(("arbitrary",)),
        name="gla_scan_out" if with_output else "gla_scan_state",
    )(main, main, small, main, main, small, w2p, b2, s0, *cast_along)


def _head_rms(a):
    return jnp.concatenate([_rms(a[:, h * DV:(h + 1) * DV]) for h in range(HEADS)], axis=-1)


def _merge_kernel(hmf, hmb, ogf, ogb, om, rg, mgm, mgg, x_ref, mln, gln, wpm, wpg, wo, gt1, g2, sc2, sh2,
                  wr, br, x1_ref, h2_ref, wt_ref):
    tm = x_ref.shape[0]
    parts = [slice(i * tm // MERGE_SPLIT, (i + 1) * tm // MERGE_SPLIT) for i in range(MERGE_SPLIT)]
    y_m = [_head_rms(hmf[r, :] + hmb[r, :]) * mln[...] * jax.nn.sigmoid(om[r, :].astype(F32)) for r in parts]
    p_m = [_dot(a, wpm[...]) for a in y_m]
    y_g = [_head_rms(ogf[r, :] + ogb[r, :]) * gln[...] * _silu(rg[r, :].astype(F32)) for r in parts]
    p_g = [_dot(a, wpg[...]) for a in y_g]
    y = [jax.nn.sigmoid(mgm[r, :].astype(F32)) * a + jax.nn.sigmoid(mgg[r, :].astype(F32)) * b
         for r, a, b in zip(parts, p_m, p_g)]
    mix = [_dot(a, wo[...]) for a in y]
    h2_parts = []
    for r, a in zip(parts, mix):
        x1 = x_ref[r, :] + gt1[...] * a
        x1_ref[r, :] = x1
        h2_parts.append(_rms(x1) * g2[...] * (1.0 + sc2[...]) + sh2[...])
    h2 = jnp.concatenate(h2_parts, axis=0)
    h2_ref[...] = h2.astype(MXU_DTYPE)

    hh, hm_, _ = _split3(h2)
    wh, wm_, _ = _split3(wr[...])
    lg = _dot(hh, wh) + _dot(hh, wm_) + _dot(hm_, wh) + br[...]
    lane = lax.broadcasted_iota(jnp.int32, lg.shape, 1)

    def masked_softmax(mask):
        l = jnp.where(mask, lg, -jnp.inf)
        e = jnp.exp(l - jnp.max(l, axis=-1, keepdims=True))
        return e / jnp.sum(e, axis=-1, keepdims=True)

    def top1(p, mask):
        pm = jnp.where(mask, p, -1.0)
        best = jnp.max(pm, axis=-1, keepdims=True)
        idx = jnp.min(jnp.where(jnp.logical_and(mask, pm == best), lane, LANES), axis=-1, keepdims=True)
        return best, idx

    gmask = lane < N_GROUPS
    grp_p, grp = top1(masked_softmax(gmask), gmask)
    e_lo = ROUTE_E0 + grp * EXPERTS_PER_GROUP
    emask = jnp.logical_and(lane >= e_lo, lane < e_lo + EXPERTS_PER_GROUP)
    p_in = masked_softmax(emask)
    p1, i1 = top1(p_in, emask)
    p2, i2 = top1(p_in, jnp.logical_and(emask, lane != i1))
    tot = p1 + p2
    wt_ref[...] = (jnp.where(lane == i1, grp_p * p1 / tot, 0.0)
                   + jnp.where(lane == i2, grp_p * p2 / tot, 0.0)
                   + jnp.where(lane == grp, 1.0, 0.0))


def _merge(hmf, hmb, ogf, ogb, main, x2, mln, gln, wpm, wpg, wo, gt1, g2, sc2, sh2, wr, br, *, tm, rows_per_batch):
    m = x2.shape[0]
    tpb = rows_per_batch // tm
    rowblk = pl.BlockSpec((tm, D_MODEL), lambda i: (i, 0))
    colblk = lambda cb: pl.BlockSpec((tm, D_MODEL), lambda i, cb=cb: (i, cb))
    vec = pl.BlockSpec((1, D_MODEL), lambda i: (0, 0))
    bvec = pl.BlockSpec((None, 1, D_MODEL), lambda i: (i // tpb, 0, 0))
    wmat = pl.BlockSpec((D_MODEL, D_MODEL), lambda i: (0, 0))
    return pl.pallas_call(
        _merge_kernel,
        grid=(m // tm,),
        in_specs=[rowblk, rowblk, rowblk, rowblk, colblk(COL_O_M), colblk(COL_R_G), colblk(COL_MG_M),
                  colblk(COL_MG_G), rowblk, vec, vec, wmat, wmat, wmat, bvec, vec, bvec, bvec,
                  pl.BlockSpec((D_MODEL, LANES), lambda i: (0, 0)), pl.BlockSpec((1, LANES), lambda i: (0, 0))],
        out_specs=[rowblk, rowblk, pl.BlockSpec((tm, LANES), lambda i: (i, 0))],
        out_shape=[jax.ShapeDtypeStruct((m, D_MODEL), F32), jax.ShapeDtypeStruct((m, D_MODEL), MXU_DTYPE),
                   jax.ShapeDtypeStruct((m, LANES), F32)],
        compiler_params=_cparams(("parallel",), 56 * 1024 * 1024),
        name="merge_route",
    )(hmf, hmb, ogf, ogb, main, main, main, main, x2, mln, gln, wpm, wpg, wo, gt1, g2, sc2, sh2, wr, br)


MERGE_SPLIT = 2
MOE_BLK = 64
MOE_COMMON_BLKS = (3, 4, 5, 6)
MOE_LOOP_BLKS = 4
MOE_EXPERTS_PER_STEP = 2


def _moe_kernel(h2_ref, wt_ref, wup_ref, wdn_ref, x1_ref, gt2_ref, gf_ref, o_ref,
                xs_ref, ys_ref, ws_ref, dest_ref, blk_ref):
    step = pl.program_id(1)
    tm = h2_ref.shape[0]
    n_rows = xs_ref.shape[0]

    @pl.when(step == 0)
    def _():
        r = wt_ref[...]
        lane = lax.broadcasted_iota(jnp.int32, (tm, LANES), 1)
        lane1 = lax.broadcasted_iota(jnp.int32, (1, LANES), 1)
        gm = jnp.where(lane < N_GROUPS, r, 0.0)
        earlier = (lax.broadcasted_iota(jnp.int32, (tm, tm), 1)
                   < lax.broadcasted_iota(jnp.int32, (tm, tm), 0)).astype(MXU_DTYPE)
        before = _dot(earlier, gm)
        padded = jnp.floor((jnp.sum(gm, axis=0, keepdims=True) + (MOE_BLK - 1)) * (1.0 / MOE_BLK)) * MOE_BLK
        start = jnp.zeros((1, LANES), F32)
        run = jnp.zeros((1, 1), F32)
        for g in range(N_GROUPS):
            size = jnp.sum(jnp.where(lane1 == g, padded, 0.0), axis=-1, keepdims=True)
            start = jnp.where(lane1 == g, run, start)
            blk_ref[g] = (jnp.sum(run) * (1.0 / MOE_BLK)).astype(jnp.int32)
            blk_ref[N_GROUPS + g] = (jnp.sum(size) * (1.0 / MOE_BLK)).astype(jnp.int32)
            run = run + size
        dest = jnp.sum(gm * (start + before), axis=-1, keepdims=True)
        dest_ref[...] = jnp.broadcast_to(dest, (tm, LANES))
        dest_row = dest_ref[...].T[0:1, :].astype(jnp.int32)
        perm = (lax.broadcasted_iota(jnp.int32, (n_rows, tm), 0) == dest_row).astype(MXU_DTYPE)
        xs_ref[...] = _dot(perm, h2_ref[...]).astype(MXU_DTYPE)
        moved = _dot(perm, jnp.concatenate(_split3(r), axis=-1))
        ws_ref[...] = moved[:, :LANES] + moved[:, LANES:2 * LANES] + moved[:, 2 * LANES:]
        ys_ref[...] = jnp.zeros_like(ys_ref)

    group = step // (EXPERTS_PER_GROUP // MOE_EXPERTS_PER_STEP)
    first_blk = blk_ref[group]
    n_blk = blk_ref[N_GROUPS + group]

    def expert_on(r0, rows):
        x = xs_ref[pl.ds(r0, rows), :]
        ws = ws_ref[pl.ds(r0, rows), :]
        lane_b = lax.broadcasted_iota(jnp.int32, (rows, LANES), 1)
        experts = range(MOE_EXPERTS_PER_STEP)
        gus = [_dot(x, wup_ref[k]) for k in experts]
        hiddens = [(_silu(gu[:, :D_EXPERT]) * gu[:, D_EXPERT:]).astype(MXU_DTYPE) for gu in gus]
        w_cols = [jnp.sum(jnp.where(lane_b == ROUTE_E0 + step * MOE_EXPERTS_PER_STEP + k, ws, 0.0),
                          axis=-1, keepdims=True) for k in experts]
        ys = [_dot(hiddens[k], wdn_ref[k]) for k in experts]
        ys_ref[pl.ds(r0, rows), :] += sum(y * w for y, w in zip(ys, w_cols))

    for k in MOE_COMMON_BLKS:
        @pl.when(n_blk == k)
        def _():
            expert_on(pl.multiple_of(first_blk * MOE_BLK, MOE_BLK), k * MOE_BLK)

    @pl.when(functools.reduce(jnp.logical_and, [n_blk != k for k in MOE_COMMON_BLKS]))
    def _():
        n_big = n_blk // MOE_LOOP_BLKS

        def big(j, carry):
            expert_on(pl.multiple_of((first_blk + j * MOE_LOOP_BLKS) * MOE_BLK, MOE_BLK), MOE_LOOP_BLKS * MOE_BLK)
            return carry

        def single(j, carry):
            expert_on(pl.multiple_of((first_blk + j) * MOE_BLK, MOE_BLK), MOE_BLK)
            return carry

        lax.fori_loop(0, n_big, big, 0)
        lax.fori_loop(n_big * MOE_LOOP_BLKS, n_blk, single, 0)

    @pl.when(step == pl.num_programs(1) - 1)
    def _():
        dest = dest_ref[...][:, :1].astype(jnp.int32)
        unperm = (lax.broadcasted_iota(jnp.int32, (tm, n_rows), 1) == dest).astype(MXU_DTYPE)
        y = _dot(unperm, ys_ref[...])
        o_ref[...] = _rms(x1_ref[...] + gt2_ref[...] * y) * gf_ref[...]


def _moe_final(h2, wt, w_up, w_down, x1, gt2, g_final, *, tm, rows_per_batch):
    m = h2.shape[0]
    tpb = rows_per_batch // tm
    n_rows = tm + N_GROUPS * MOE_BLK
    rowblk = pl.BlockSpec((tm, D_MODEL), lambda i, e: (i, 0))
    return pl.pallas_call(
        _moe_kernel,
        grid=(m // tm, N_EXPERTS // MOE_EXPERTS_PER_STEP),
        in_specs=[rowblk,
                  pl.BlockSpec((tm, LANES), lambda i, e: (i, 0)),
                  pl.BlockSpec((MOE_EXPERTS_PER_STEP, D_MODEL, 2 * D_EXPERT), lambda i, e: (e, 0, 0)),
                  pl.BlockSpec((MOE_EXPERTS_PER_STEP, D_EXPERT, D_MODEL), lambda i, e: (e, 0, 0)),
                  rowblk,
                  pl.BlockSpec((None, 1, D_MODEL), lambda i, e: (i // tpb, 0, 0)),
                  pl.BlockSpec((1, D_MODEL), lambda i, e: (0, 0))],
        out_specs=rowblk,
        out_shape=jax.ShapeDtypeStruct((m, D_MODEL), F32),
        scratch_shapes=[pltpu.VMEM((n_rows, D_MODEL), MXU_DTYPE), pltpu.VMEM((n_rows, D_MODEL), F32),
                        pltpu.VMEM((n_rows, LANES), F32), pltpu.VMEM((tm, LANES), F32),
                        pltpu.SMEM((2 * N_GROUPS,), jnp.int32)],
        compiler_params=_cparams(("parallel", "arbitrary"), 56 * 1024 * 1024),
        name="moe_final",
    )(h2, wt, w_up, w_down, x1, gt2, g_final)


def _empty_states(batch):
    ml = (jnp.zeros((batch, 2, HEADS, QK, DV + LANES), F32),
          jnp.full((batch, 2, HEADS, 1, LANES), NEG_BIG, F32))
    gla = jnp.zeros((batch, 2, HEADS, DV, QK), F32)
    return ml, gla


def kernel(x, c, ctx, c_ctx, w_mod, b_mod, g_norm1, w_in, ml_conv, ml_conv_b, b_mgate, ml_norm, gla_w2, gla_b2,
           gla_norm, w_proj_m, w_proj_g, w_out, g_norm2, w_grp, b_grp, w_rexp, b_rexp, w_up, w_down, g_final):
    batch, t, d = x.shape
    t_ctx = ctx.shape[1]
    assert d == D_MODEL and w_mod.shape[0] == 1 and w_in.shape[2] == sum(IN_SIZES)
    assert t % (GRID_W * 16) == 0 and t % CHUNK == 0 and t_ctx % CHUNK == 0

    off = [0]
    for s in IN_SIZES:
        off.append(off[-1] + s)
    wi = w_in[0].astype(MXU_DTYPE)
    w_main = jnp.concatenate([wi[:, off[0]:off[4]], wi[:, off[5]:off[9]], wi[:, off[10]:off[12]]], axis=1)
    w_small = jnp.concatenate([wi[:, off[9]:off[10]], wi[:, off[4]:off[5]],
                               jnp.zeros((d, LANES - 2 * GLA_RANK - 4 * HEADS), MXU_DTYPE)], axis=1)
    gate_bias = jnp.zeros((1, LANES), F32).at[0, SMALL_GATE0:SMALL_GATE0 + 4 * HEADS].set(b_mgate[0])
    w2p = jnp.zeros((2, LANES, QK_W), F32)
    w2p = w2p.at[0, 0:GLA_RANK].set(gla_w2[0, 0]).at[1, GLA_RANK:2 * GLA_RANK].set(gla_w2[0, 1])
    b2 = gla_b2[0][:, None, :]
    conv_w = ml_conv[0].reshape(9, 2 * QK_W)
    conv_b = ml_conv_b[0][None, :]
    w_route = jnp.concatenate([w_grp[0], w_rexp[0], jnp.zeros((d, LANES - N_GROUPS - N_EXPERTS), F32)], axis=1)
    b_route = jnp.concatenate([b_grp[0], b_rexp[0], jnp.zeros((LANES - N_GROUPS - N_EXPERTS,), F32)])[None, :]

    cc = jnp.concatenate([c, c_ctx[None, :], jnp.zeros((8 - batch - 1, d), F32)], axis=0)
    mod = _modulation(cc, w_mod[0], b_mod[0][None, :])
    sh1, sc1, gt1, sh2, sc2, gt2 = [mod[:batch, i * d:(i + 1) * d][:, None, :] for i in range(6)]
    sh1c, sc1c = [mod[batch:batch + 1, i * d:(i + 1) * d][:, None, :] for i in range(2)]
    g1 = g_norm1[0][None, :]

    main_c, small_c = _in_proj(ctx.reshape(batch * t_ctx, d), g1, sc1c, sh1c, w_main, w_small,
                               tm=batch * t_ctx, rows_per_batch=batch * t_ctx, last_col_block=COL_V_G)
    qk_c = _conv_silu(main_c, conv_w, conv_b, batch=batch, rows=1, cols=t_ctx)
    ml0, gla0 = _empty_states(batch)
    main_c3, small_c3 = main_c.reshape(batch, t_ctx, main_c.shape[1]), small_c.reshape(batch, t_ctx, LANES)
    ml_states = _mlstm_scan(qk_c.reshape(batch, t_ctx, 2 * QK_W), main_c3, small_c3, gate_bias, ml0, with_output=False)
    gla_state = _gla_scan(main_c3, small_c3, w2p, b2, gla0, with_output=False)

    x2 = x.reshape(batch * t, d)
    main, small = _in_proj(x2, g1, sc1, sh1, w_main, w_small, tm=1024, rows_per_batch=t)
    qk = _conv_silu(main, conv_w, conv_b, batch=batch, rows=t // GRID_W, cols=GRID_W)
    main3, small3 = main.reshape(batch, t, MAIN_W), small.reshape(batch, t, LANES)
    hm_f, hm_b = [a.reshape(batch * t, V_W) for a in
                  _mlstm_scan(qk.reshape(batch, t, 2 * QK_W), main3, small3, gate_bias, ml_states, with_output=True)]
    n_steps = t // CHUNK
    later_weights = (w_up[0], w_down[0], w_proj_m[0], w_proj_g[0], w_out[0])
    og_f, og_b, *cast_weights = _gla_scan(
        main3, small3, w2p, b2, gla_state, with_output=True,
        cast_along=tuple(w.reshape(n_steps, -1, w.shape[-1]) for w in later_weights))
    og_f, og_b = og_f.reshape(batch * t, V_W), og_b.reshape(batch * t, V_W)
    w_up_c, w_down_c, w_pm_c, w_pg_c, w_out_c = [c.reshape(w.shape) for c, w in zip(cast_weights, later_weights)]

    x1, h2, wt = _merge(hm_f, hm_b, og_f, og_b, main, x2, ml_norm, gla_norm,
                        w_pm_c, w_pg_c, w_out_c,
                        gt1, g_norm2, sc2, sh2, w_route, b_route, tm=512, rows_per_batch=t)
    out = _moe_final(h2, wt, w_up_c, w_down_c, x1, gt2, g_final[None, :],
                     tm=1024, rows_per_batch=t)
    return out.reshape(batch, t, d)
```

```python
import functools

import jax
import jax.numpy as jnp
from jax import lax
from jax.experimental import pallas as pl
from jax.experimental.pallas import tpu as pltpu

D_MODEL = 1024
GRID_W = 64
CHUNK = 256
EPS = 1e-6
NEG_BIG = -1e30
HEADS = 4
QK = D_MODEL // 8
DV = D_MODEL // 4
QK_W = HEADS * QK
V_W = HEADS * DV
GLA_RANK = 16
GLA_TAU = 16.0
N_GROUPS = 4
EXPERTS_PER_GROUP = 4
N_EXPERTS = N_GROUPS * EXPERTS_PER_GROUP
D_EXPERT = D_MODEL // 2
IN_SIZES = (QK_W, QK_W, V_W, V_W, 4 * HEADS, QK_W, QK_W, V_W, V_W, 2 * GLA_RANK, D_MODEL, D_MODEL)

LANES = 128
CONV_TILE = 256
MXU_DTYPE = jnp.bfloat16
F32 = jnp.float32
MIB = 1024 * 1024
VMEM_LIMIT = 48 * MIB
VMEM_LIMIT_BIG = 56 * MIB
MOD_TN = 2048
IN_PROJ_TM, IN_PROJ_TN = 1024, 4096
MERGE_TM = 512
MOE_TM = 1024

COL_QK_M, COL_V_M, COL_O_M, COL_QK_G, COL_V_G, COL_R_G, COL_MG_M, COL_MG_G = range(8)
MAIN_W = 8 * D_MODEL
SMALL_GATE0 = 2 * GLA_RANK
GLA_SAFE_DECAY = 80.0
ROUTE_E0 = N_GROUPS


def _dot(a, b):
    return jnp.dot(a.astype(MXU_DTYPE), b.astype(MXU_DTYPE), preferred_element_type=F32)


def _dot_nt(a, b):
    return lax.dot_general(a.astype(MXU_DTYPE), b.astype(MXU_DTYPE), (((1,), (1,)), ((), ())),
                           preferred_element_type=F32)


def _transpose_mxu(a):
    m = a.shape[1]
    eye = (lax.broadcasted_iota(jnp.int32, (m, m), 0) == lax.broadcasted_iota(jnp.int32, (m, m), 1))
    return _dot_nt(eye.astype(MXU_DTYPE), a).astype(MXU_DTYPE)


def _dot_tn_xlu(a, b):
    return lax.dot_general(a.astype(MXU_DTYPE), b.astype(MXU_DTYPE), (((0,), (0,)), ((), ())),
                           preferred_element_type=F32)


def _split3(x):
    hi = x.astype(MXU_DTYPE)
    r1 = x - hi.astype(F32)
    mid = r1.astype(MXU_DTYPE)
    lo = (r1 - mid.astype(F32)).astype(MXU_DTYPE)
    return hi, mid, lo


def _dot_exact_lhs(a01, x):
    hi, mid, lo = _split3(x)
    return _dot(a01, hi) + _dot(a01, mid) + _dot(a01, lo)


def _log_sigmoid(x):
    return jnp.minimum(x, 0.0) - jnp.log(1.0 + jnp.exp(-jnp.abs(x)))


def _silu(x):
    return x * jax.nn.sigmoid(x)


def _rms(x):
    return x * lax.rsqrt(jnp.mean(x * x, axis=-1, keepdims=True) + EPS)


def _cparams(sem, vmem_limit=VMEM_LIMIT):
    return pltpu.CompilerParams(dimension_semantics=sem, vmem_limit_bytes=vmem_limit)


def _mod_kernel(c_ref, w_ref, b_ref, o_ref):
    o_ref[...] = _dot(_silu(c_ref[...]), w_ref[...]) + b_ref[...]


def _modulation(cc, w_mod, b_mod):
    n = w_mod.shape[1]
    tn = MOD_TN
    return pl.pallas_call(
        _mod_kernel,
        grid=(n // tn,),
        in_specs=[pl.BlockSpec((8, D_MODEL), lambda j: (0, 0)),
                  pl.BlockSpec((D_MODEL, tn), lambda j: (0, j)),
                  pl.BlockSpec((1, tn), lambda j: (0, j))],
        out_specs=pl.BlockSpec((8, tn), lambda j: (0, j)),
        out_shape=jax.ShapeDtypeStruct((8, n), F32),
        compiler_params=_cparams(("arbitrary",)),
        name="modulation",
    )(cc, w_mod, b_mod)


def _inproj_kernel(x_ref, g_ref, sc_ref, sh_ref, w_ref, ws_ref, o_ref, os_ref, xn_ref):
    @pl.when(pl.program_id(1) == 0)
    def _():
        xn = _rms(x_ref[...]) * g_ref[...] * (1.0 + sc_ref[...]) + sh_ref[...]
        xn_ref[...] = xn.astype(MXU_DTYPE)
        os_ref[...] = _dot(xn_ref[...], ws_ref[...])

    half = o_ref.shape[1] // 2
    for k in range(2):
        o_ref[:, k * half:(k + 1) * half] = _dot(xn_ref[...], w_ref[:, k * half:(k + 1) * half]).astype(o_ref.dtype)


def _in_proj(x2, g, sc, sh, w_main, w_small, *, tm, rows_per_batch, last_col_block=COL_MG_G):
    m = x2.shape[0]
    tn = IN_PROJ_TN
    col_tiles = last_col_block * D_MODEL // tn + 1
    tiles_per_batch = rows_per_batch // tm
    vec = pl.BlockSpec((None, 1, D_MODEL), lambda i, j: (i // tiles_per_batch, 0, 0))
    return pl.pallas_call(
        _inproj_kernel,
        grid=(m // tm, col_tiles),
        in_specs=[pl.BlockSpec((tm, D_MODEL), lambda i, j: (i, 0)),
                  pl.BlockSpec((1, D_MODEL), lambda i, j: (0, 0)),
                  vec, vec,
                  pl.BlockSpec((D_MODEL, tn), lambda i, j: (0, j)),
                  pl.BlockSpec((D_MODEL, LANES), lambda i, j: (0, 0))],
        out_specs=[pl.BlockSpec((tm, tn), lambda i, j: (i, j)),
                   pl.BlockSpec((tm, LANES), lambda i, j: (i, 0))],
        out_shape=[jax.ShapeDtypeStruct((m, col_tiles * tn), MXU_DTYPE), jax.ShapeDtypeStruct((m, LANES), F32)],
        scratch_shapes=[pltpu.VMEM((tm, D_MODEL), MXU_DTYPE)],
        compiler_params=_cparams(("parallel", "arbitrary"), VMEM_LIMIT_BIG),
        name="in_proj",
    )(x2, g, sc, sh, w_main, w_small)


def _conv_kernel(x_ref, w_ref, b_ref, o_ref, *, rows, cols):
    scale = jnp.where(pl.program_id(1) * CONV_TILE >= QK_W, QK ** -0.5, 1.0).astype(F32)
    w = w_ref[...]
    bias = b_ref[...]
    tpos = lax.broadcasted_iota(jnp.int32, (cols, 1), 0)
    has_left = tpos >= 1
    has_right = tpos < cols - 1

    def row_filters(j):
        tile = x_ref[pl.ds(pl.multiple_of(j * cols, cols), cols), :].astype(F32)
        left = jnp.where(has_left, pltpu.roll(tile, 1, axis=0), 0.0)
        right = jnp.where(has_right, pltpu.roll(tile, cols - 1, axis=0), 0.0)
        return [left * w[3 * i:3 * i + 1, :] + tile * w[3 * i + 1:3 * i + 2, :] + right * w[3 * i + 2:3 * i + 3, :]
                for i in range(3)]

    def finish(j, acc):
        o_ref[pl.ds(pl.multiple_of(j * cols, cols), cols), :] = (_silu(acc + bias) * scale).astype(o_ref.dtype)

    first = row_filters(0)

    def body(j, carry):
        acc, below = carry
        h = row_filters(j)
        finish(j - 1, acc + h[2])
        return below + h[1], h[0]

    acc, _ = lax.fori_loop(1, rows, body, (first[1], first[0]))
    finish(rows - 1, acc)


def _conv_silu(main, conv_w, conv_b, *, batch, rows, cols):
    t = rows * cols
    nct = 2 * QK_W // CONV_TILE
    return pl.pallas_call(
        functools.partial(_conv_kernel, rows=rows, cols=cols),
        grid=(batch, nct),
        in_specs=[pl.BlockSpec((t, CONV_TILE), lambda b, c: (b, c)),
                  pl.BlockSpec((9, CONV_TILE), lambda b, c: (0, c)),
                  pl.BlockSpec((1, CONV_TILE), lambda b, c: (0, c))],
        out_specs=pl.BlockSpec((t, CONV_TILE), lambda b, c: (b, c)),
        out_shape=jax.ShapeDtypeStruct((batch * t, 2 * QK_W), MXU_DTYPE),
        compiler_params=_cparams(("parallel", "arbitrary")),
        name="conv_silu",
    )(main, conv_w, conv_b)


def _chunk_masks(direction):
    row = lax.broadcasted_iota(jnp.int32, (CHUNK, CHUNK), 0)
    col = lax.broadcasted_iota(jnp.int32, (CHUNK, CHUNK), 1)
    seen = (row >= col) if direction == 0 else (row <= col)
    return seen, seen.astype(MXU_DTYPE)


def _scan_specs(batch, nc, col_blocks, widths):
    specs = []
    for direction in (0, 1):
        for cb, wd in zip(col_blocks, widths):
            if direction == 0:
                specs.append(pl.BlockSpec((batch, CHUNK, wd), lambda c, cb=cb: (0, c, cb)))
            else:
                specs.append(pl.BlockSpec((batch, CHUNK, wd), lambda c, cb=cb: (0, nc - 1 - c, cb)))
    return specs


def _scan_out_specs(batch, nc):
    return [pl.BlockSpec((batch, CHUNK, V_W), lambda c: (0, c, 0)),
            pl.BlockSpec((batch, CHUNK, V_W), lambda c: (0, nc - 1 - c, 0))]


def _whole(shape):
    nd = len(shape)
    return pl.BlockSpec(tuple(shape), lambda c: (0,) * nd)


def _lane_tile(x, width):
    return jnp.concatenate([x] * (width // LANES), axis=-1)


def _mlstm_kernel(*refs, with_output, batch):
    (qk_f, v_f, sm_f, qk_b, v_b, sm_b, bias_ref, cn0_ref, m0_ref) = refs[:9]
    if with_output:
        hf_ref, hb_ref, cn_s, m_s = refs[9:]
    else:
        cn_out, m_out, cn_s, m_s = refs[9:]
    step = pl.program_id(0)

    @pl.when(step == 0)
    def _():
        cn_s[...] = cn0_ref[...]
        m_s[...] = m0_ref[...]

    lane = lax.broadcasted_iota(jnp.int32, (1, LANES), 1)
    gate_lane = jnp.logical_and(lane >= SMALL_GATE0, lane < SMALL_GATE0 + 4 * HEADS)
    forget_lane = jnp.logical_and(gate_lane, ((lane - SMALL_GATE0) % (2 * HEADS)) >= HEADS)
    eye = (lax.broadcasted_iota(jnp.int32, (LANES, LANES), 0)
           == lax.broadcasted_iota(jnp.int32, (LANES, LANES), 1)).astype(MXU_DTYPE)
    ones_cols = jnp.ones((CHUNK, LANES), MXU_DTYPE)

    for direction, (qk_ref, v_ref, sm_ref) in enumerate(((qk_f, v_f, sm_f), (qk_b, v_b, sm_b))):
        seen, seen01 = _chunk_masks(direction)
        last = CHUNK - 1 if direction == 0 else 0
        for bi in range(batch):
            g = sm_ref[bi] + bias_ref[...]
            gp = jnp.where(forget_lane, _log_sigmoid(g), g)
            bc = _dot_exact_lhs(seen01, gp)
            hi, mid, lo = _split3(gp)
            gp_t = _dot_nt(eye, hi) + _dot_nt(eye, mid) + _dot_nt(eye, lo)
            hi, mid, lo = _split3(bc)
            bc_t = _dot_nt(eye, hi) + _dot_nt(eye, mid) + _dot_nt(eye, lo)
            bend_row = bc[last:last + 1, :]
            heads = []
            for h in range(HEADS):
                ji = SMALL_GATE0 + direction * 2 * HEADS + h
                jf = ji + HEADS
                c = dict(ji=ji, jf=jf)
                c["q"] = qk_ref[bi, :, h * QK:(h + 1) * QK].astype(F32)
                c["k"] = qk_ref[bi, :, QK_W + h * QK:QK_W + (h + 1) * QK].astype(F32)
                c["v_ext"] = jnp.concatenate([v_ref[bi, :, h * DV:(h + 1) * DV].astype(MXU_DTYPE), ones_cols], axis=-1)
                c["cn_old"] = cn_s[bi, direction, h]
                c["m_old"] = m_s[bi, direction, h]
                heads.append(c)
            if with_output:
                for c in heads:
                    c["qk"] = _dot_nt(c["q"], c["k"])
                    c["qc"] = _dot(c["q"], c["cn_old"])
            for c in heads:
                b_end = jnp.broadcast_to(bend_row[:, c["jf"]:c["jf"] + 1], (1, LANES))
                i_col = jnp.broadcast_to(gp[:, c["ji"]:c["ji"] + 1], (CHUNK, LANES))
                c["b_col"] = jnp.broadcast_to(bc[:, c["jf"]:c["jf"] + 1], (CHUNK, LANES))
                log_w = b_end - c["b_col"] + i_col
                c["m_new"] = jnp.maximum(b_end + c["m_old"], jnp.max(log_w, axis=0, keepdims=True))
                c["kw"] = (c["k"] * jnp.exp(log_w - c["m_new"])).astype(MXU_DTYPE)
                c["decay"] = jnp.exp(b_end + c["m_old"] - c["m_new"])
            for c in heads:
                c["kw_t"] = _transpose_mxu(c["kw"])
            if with_output:
                for c in heads:
                    i_row = gp_t[c["ji"]:c["ji"] + 1, :]
                    b_row = bc_t[c["jf"]:c["jf"] + 1, :]
                    log_d = jnp.where(seen, _lane_tile(c["b_col"], CHUNK) - b_row + i_row, -jnp.inf)
                    log_inter = c["b_col"] + c["m_old"]
                    c["m_t"] = jnp.maximum(log_inter, jnp.max(log_d, axis=-1, keepdims=True))
                    c["s"] = (c["qk"] * jnp.exp(log_d - _lane_tile(c["m_t"], CHUNK))).astype(MXU_DTYPE)
                    c["w_inter"] = jnp.exp(log_inter - c["m_t"])
                for c in heads:
                    c["sv"] = _dot(c["s"], c["v_ext"])
            for h, c in enumerate(heads):
                cn_s[bi, direction, h] = _lane_tile(c["decay"], DV + LANES) * c["cn_old"] + _dot(c["kw_t"], c["v_ext"])
                m_s[bi, direction, h] = c["m_new"]
            outs = []
            if with_output:
                for c in heads:
                    sv, qc, w_inter = c["sv"], c["qc"], c["w_inter"]
                    num = sv[:, :DV] + _lane_tile(w_inter, DV) * qc[:, :DV]
                    den = jnp.abs(sv[:, DV:] + w_inter * qc[:, DV:])
                    outs.append(num / _lane_tile(jnp.maximum(den, jnp.exp(-c["m_t"])), DV))
            if with_output:
                (hf_ref if direction == 0 else hb_ref)[bi] = jnp.concatenate(outs, axis=-1)

    if not with_output:
        @pl.when(step == pl.num_programs(0) - 1)
        def _():
            cn_out[...] = cn_s[...]
            m_out[...] = m_s[...]


def _mlstm_scan(qk, main, small, gate_bias, states, *, with_output):
    cn0, m0 = states
    batch, t, _ = qk.shape
    nc = t // CHUNK
    in_specs = _scan_specs(batch, nc, (0, COL_V_M, 0), (2 * QK_W, V_W, LANES))
    in_specs += [_whole(gate_bias.shape), _whole(cn0.shape), _whole(m0.shape)]
    if with_output:
        out_specs = _scan_out_specs(batch, nc)
        out_shape = [jax.ShapeDtypeStruct((batch, t, V_W), F32)] * 2
    else:
        out_specs = [_whole(cn0.shape), _whole(m0.shape)]
        out_shape = [jax.ShapeDtypeStruct(s.shape, F32) for s in states]
    return pl.pallas_call(
        functools.partial(_mlstm_kernel, with_output=with_output, batch=batch),
        grid=(nc,),
        in_specs=in_specs,
        out_specs=out_specs,
        out_shape=out_shape,
        scratch_shapes=[pltpu.VMEM(cn0.shape, F32), pltpu.VMEM(m0.shape, F32)],
        compiler_params=_cparams(("arbitrary",)),
        name="mlstm_scan_out" if with_output else "mlstm_scan_state",
    )(qk, main, small, qk, main, small, gate_bias, cn0, m0)


def _gla_exact_intra(q, k, v, b, direction):
    row_id = lax.broadcasted_iota(jnp.int32, (CHUNK, 1), 0)

    def row(t, acc):
        pick = row_id == t
        b_t = jnp.sum(jnp.where(pick, b, 0.0), axis=0, keepdims=True)
        q_t = jnp.sum(jnp.where(pick, q, 0.0), axis=0, keepdims=True)
        ok = (row_id <= t) if direction == 0 else (row_id >= t)
        e = jnp.exp(jnp.where(ok, b_t - b, -jnp.inf))
        sc = jnp.sum(q_t * k * e, axis=-1, keepdims=True)
        o_t = jnp.sum(sc * v, axis=0, keepdims=True)
        return jnp.where(pick, o_t, acc)

    return lax.fori_loop(0, CHUNK, row, jnp.zeros((CHUNK, DV), F32))


def _gla_kernel(*refs, with_output, batch, n_cast=0):
    (qk_f, v_f, sm_f, qk_b, v_b, sm_b, w2_ref, b2_ref, s0_ref) = refs[:9]
    cast_in, rest = refs[9:9 + n_cast], refs[9 + n_cast:]
    if with_output:
        of_ref, ob_ref = rest[:2]
        cast_out = rest[2:2 + n_cast]
        s_s, b_s, inter_s = rest[2 + n_cast:]
    else:
        s_out, s_s = rest
    step = pl.program_id(0)

    for src, dst in zip(cast_in, cast_out if with_output else ()):
        dst[...] = src[...].astype(dst.dtype)

    @pl.when(step == 0)
    def _():
        s_s[...] = s0_ref[...]

    worst_decay = []
    for direction, (qk_ref, v_ref, sm_ref) in enumerate(((qk_f, v_f, sm_f), (qk_b, v_b, sm_b))):
        seen, seen01 = _chunk_masks(direction)
        last = CHUNK - 1 if direction == 0 else 0
        for bi in range(batch):
            z = _dot(sm_ref[bi], w2_ref[direction]) + b2_ref[direction]
            log_a = _log_sigmoid(z) * (1.0 / GLA_TAU)
            b_all = _dot_exact_lhs(seen01, log_a)
            outs, inters = [], []
            for h in range(HEADS):
                q = qk_ref[bi, :, h * QK:(h + 1) * QK].astype(F32) * (QK ** -0.5)
                k = qk_ref[bi, :, QK_W + h * QK:QK_W + (h + 1) * QK].astype(F32)
                v = v_ref[bi, :, h * DV:(h + 1) * DV]
                b = b_all[:, h * QK:(h + 1) * QK]
                b_end = b[last:last + 1, :]
                st_old = s_s[bi, direction, h]
                k_dec = k * jnp.exp(b_end - b)
                s_s[bi, direction, h] = st_old * jnp.exp(b_end) + _dot_tn_xlu(v, k_dec)
                if with_output:
                    q_dec = q * jnp.exp(b)
                    inter = _dot_nt(q_dec, st_old)
                    scores = jnp.where(seen, _dot_nt(q_dec, k * jnp.exp(-b)), 0.0)
                    outs.append(_dot(scores, v) + inter)
                    inters.append(inter)
            if with_output:
                (of_ref if direction == 0 else ob_ref)[bi] = jnp.concatenate(outs, axis=-1)
                b_s[bi, direction] = b_all
                inter_s[bi, direction] = jnp.concatenate(inters, axis=-1)
                worst_decay.append(jnp.max(-b_all[last:last + 1, :]))

    if with_output:
        @pl.when(functools.reduce(jnp.maximum, worst_decay) > GLA_SAFE_DECAY)
        def _():
            for direction, (qk_ref, v_ref, o_ref) in enumerate(((qk_f, v_f, of_ref), (qk_b, v_b, ob_ref))):
                last = CHUNK - 1 if direction == 0 else 0
                for bi in range(batch):
                    for h in range(HEADS):
                        b = b_s[bi, direction, :, h * QK:(h + 1) * QK]

                        @pl.when(jnp.max(-b[last:last + 1, :]) > GLA_SAFE_DECAY)
                        def _():
                            q = qk_ref[bi, :, h * QK:(h + 1) * QK].astype(F32) * (QK ** -0.5)
                            k = qk_ref[bi, :, QK_W + h * QK:QK_W + (h + 1) * QK].astype(F32)
                            v = v_ref[bi, :, h * DV:(h + 1) * DV].astype(F32)
                            o_ref[bi, :, h * DV:(h + 1) * DV] = (
                                inter_s[bi, direction, :, h * DV:(h + 1) * DV]
                                + _gla_exact_intra(q, k, v, b, direction))
    else:
        @pl.when(step == pl.num_programs(0) - 1)
        def _():
            s_out[...] = s_s[...]


def _gla_scan(main, small, w2p, b2, s0, *, with_output, cast_along=()):
    batch, t, _ = main.shape
    nc = t // CHUNK
    in_specs = _scan_specs(batch, nc, (COL_QK_G, COL_V_G, 0), (2 * QK_W, V_W, LANES))
    in_specs += [_whole(w2p.shape), _whole(b2.shape), _whole(s0.shape)]
    slab = lambda a: pl.BlockSpec((None,) + a.shape[1:], lambda c: (c, 0, 0))
    in_specs += [slab(a) for a in cast_along]
    scratch = [pltpu.VMEM(s0.shape, F32)]
    if with_output:
        assert all(a.shape[0] == nc for a in cast_along)
        out_specs = _scan_out_specs(batch, nc) + [slab(a) for a in cast_along]
        out_shape = ([jax.ShapeDtypeStruct((batch, t, V_W), F32)] * 2
                     + [jax.ShapeDtypeStruct(a.shape, MXU_DTYPE) for a in cast_along])
        scratch += [pltpu.VMEM((batch, 2, CHUNK, QK_W), F32), pltpu.VMEM((batch, 2, CHUNK, V_W), F32)]
    else:
        out_specs = _whole(s0.shape)
        out_shape = jax.ShapeDtypeStruct(s0.shape, F32)
    return pl.pallas_call(
        functools.partial(_gla_kernel, with_output=with_output, batch=batch, n_cast=len(cast_along)),
        grid=(nc,),
        in_specs=in_specs,
        out_specs=out_specs,
        out_shape=out_shape,
        scratch_shapes=scratch,
        compiler_params=_cparams(("arbitrary",)),
        name="gla_scan_out" if with_output else "gla_scan_state",
    )(main, main, small, main, main, small, w2p, b2, s0, *cast_along)


def _head_rms(a):
    return jnp.concatenate([_rms(a[:, h * DV:(h + 1) * DV]) for h in range(HEADS)], axis=-1)


def _merge_kernel(hmf, hmb, ogf, ogb, om, rg, mgm, mgg, x_ref, mln, gln, wpm, wpg, wo, gt1, g2, sc2, sh2,
                  wr, br, x1_ref, h2_ref, wt_ref):
    tm = x_ref.shape[0]
    parts = [slice(i * tm // MERGE_SPLIT, (i + 1) * tm // MERGE_SPLIT) for i in range(MERGE_SPLIT)]
    y_m = [_head_rms(hmf[r, :] + hmb[r, :]) * mln[...] * jax.nn.sigmoid(om[r, :].astype(F32)) for r in parts]
    p_m = [_dot(a, wpm[...]) for a in y_m]
    y_g = [_head_rms(ogf[r, :] + ogb[r, :]) * gln[...] * _silu(rg[r, :].astype(F32)) for r in parts]
    p_g = [_dot(a, wpg[...]) for a in y_g]
    y = [jax.nn.sigmoid(mgm[r, :].astype(F32)) * a + jax.nn.sigmoid(mgg[r, :].astype(F32)) * b
         for r, a, b in zip(parts, p_m, p_g)]
    mix = [_dot(a, wo[...]) for a in y]
    h2_parts = []
    for r, a in zip(parts, mix):
        x1 = x_ref[r, :] + gt1[...] * a
        x1_ref[r, :] = x1
        h2_parts.append(_rms(x1) * g2[...] * (1.0 + sc2[...]) + sh2[...])
    h2 = jnp.concatenate(h2_parts, axis=0)
    h2_ref[...] = h2.astype(MXU_DTYPE)

    hh, hm_, _ = _split3(h2)
    wh, wm_, _ = _split3(wr[...])
    lg = _dot(hh, wh) + _dot(hh, wm_) + _dot(hm_, wh) + br[...]
    lane = lax.broadcasted_iota(jnp.int32, lg.shape, 1)

    def masked_softmax(mask):
        l = jnp.where(mask, lg, -jnp.inf)
        e = jnp.exp(l - jnp.max(l, axis=-1, keepdims=True))
        return e / jnp.sum(e, axis=-1, keepdims=True)

    def top1(p, mask):
        pm = jnp.where(mask, p, -1.0)
        best = jnp.max(pm, axis=-1, keepdims=True)
        idx = jnp.min(jnp.where(jnp.logical_and(mask, pm == best), lane, LANES), axis=-1, keepdims=True)
        return best, idx

    gmask = lane < N_GROUPS
    grp_p, grp = top1(masked_softmax(gmask), gmask)
    e_lo = ROUTE_E0 + grp * EXPERTS_PER_GROUP
    emask = jnp.logical_and(lane >= e_lo, lane < e_lo + EXPERTS_PER_GROUP)
    p_in = masked_softmax(emask)
    p1, i1 = top1(p_in, emask)
    p2, i2 = top1(p_in, jnp.logical_and(emask, lane != i1))
    tot = p1 + p2
    wt_ref[...] = (jnp.where(lane == i1, grp_p * p1 / tot, 0.0)
                   + jnp.where(lane == i2, grp_p * p2 / tot, 0.0)
                   + jnp.where(lane == grp, 1.0, 0.0))


def _merge(hmf, hmb, ogf, ogb, main, x2, mln, gln, wpm, wpg, wo, gt1, g2, sc2, sh2, wr, br, *, tm, rows_per_batch):
    m = x2.shape[0]
    tpb = rows_per_batch // tm
    rowblk = pl.BlockSpec((tm, D_MODEL), lambda i: (i, 0))
    colblk = lambda cb: pl.BlockSpec((tm, D_MODEL), lambda i, cb=cb: (i, cb))
    vec = pl.BlockSpec((1, D_MODEL), lambda i: (0, 0))
    bvec = pl.BlockSpec((None, 1, D_MODEL), lambda i: (i // tpb, 0, 0))
    wmat = pl.BlockSpec((D_MODEL, D_MODEL), lambda i: (0, 0))
    return pl.pallas_call(
        _merge_kernel,
        grid=(m // tm,),
        in_specs=[rowblk, rowblk, rowblk, rowblk, colblk(COL_O_M), colblk(COL_R_G), colblk(COL_MG_M),
                  colblk(COL_MG_G), rowblk, vec, vec, wmat, wmat, wmat, bvec, vec, bvec, bvec,
                  pl.BlockSpec((D_MODEL, LANES), lambda i: (0, 0)), pl.BlockSpec((1, LANES), lambda i: (0, 0))],
        out_specs=[rowblk, rowblk, pl.BlockSpec((tm, LANES), lambda i: (i, 0))],
        out_shape=[jax.ShapeDtypeStruct((m, D_MODEL), F32), jax.ShapeDtypeStruct((m, D_MODEL), MXU_DTYPE),
                   jax.ShapeDtypeStruct((m, LANES), F32)],
        compiler_params=_cparams(("parallel",), VMEM_LIMIT_BIG),
        name="merge_route",
    )(hmf, hmb, ogf, ogb, main, main, main, main, x2, mln, gln, wpm, wpg, wo, gt1, g2, sc2, sh2, wr, br)


MERGE_SPLIT = 2
MOE_BLK = 64
MOE_COMMON_BLKS = (3, 4, 5, 6)
MOE_LOOP_BLKS = 4
MOE_EXPERTS_PER_STEP = 2


def _moe_kernel(h2_ref, wt_ref, wup_ref, wdn_ref, x1_ref, gt2_ref, gf_ref, o_ref,
                xs_ref, ys_ref, ws_ref, dest_ref, blk_ref):
    step = pl.program_id(1)
    tm = h2_ref.shape[0]
    n_rows = xs_ref.shape[0]

    @pl.when(step == 0)
    def _():
        r = wt_ref[...]
        lane = lax.broadcasted_iota(jnp.int32, (tm, LANES), 1)
        lane1 = lax.broadcasted_iota(jnp.int32, (1, LANES), 1)
        gm = jnp.where(lane < N_GROUPS, r, 0.0)
        earlier = (lax.broadcasted_iota(jnp.int32, (tm, tm), 1)
                   < lax.broadcasted_iota(jnp.int32, (tm, tm), 0)).astype(MXU_DTYPE)
        before = _dot(earlier, gm)
        padded = jnp.floor((jnp.sum(gm, axis=0, keepdims=True) + (MOE_BLK - 1)) * (1.0 / MOE_BLK)) * MOE_BLK
        start = jnp.zeros((1, LANES), F32)
        run = jnp.zeros((1, 1), F32)
        for g in range(N_GROUPS):
            size = jnp.sum(jnp.where(lane1 == g, padded, 0.0), axis=-1, keepdims=True)
            start = jnp.where(lane1 == g, run, start)
            blk_ref[g] = (jnp.sum(run) * (1.0 / MOE_BLK)).astype(jnp.int32)
            blk_ref[N_GROUPS + g] = (jnp.sum(size) * (1.0 / MOE_BLK)).astype(jnp.int32)
            run = run + size
        dest = jnp.sum(gm * (start + before), axis=-1, keepdims=True)
        dest_ref[...] = jnp.broadcast_to(dest, (tm, LANES))
        dest_row = dest_ref[...].T[0:1, :].astype(jnp.int32)
        perm = (lax.broadcasted_iota(jnp.int32, (n_rows, tm), 0) == dest_row).astype(MXU_DTYPE)
        xs_ref[...] = _dot(perm, h2_ref[...]).astype(MXU_DTYPE)
        moved = _dot(perm, jnp.concatenate(_split3(r), axis=-1))
        ws_ref[...] = moved[:, :LANES] + moved[:, LANES:2 * LANES] + moved[:, 2 * LANES:]
        ys_ref[...] = jnp.zeros_like(ys_ref)

    group = step // (EXPERTS_PER_GROUP // MOE_EXPERTS_PER_STEP)
    first_blk = blk_ref[group]
    n_blk = blk_ref[N_GROUPS + group]

    def expert_on(r0, rows):
        x = xs_ref[pl.ds(r0, rows), :]
        ws = ws_ref[pl.ds(r0, rows), :]
        lane_b = lax.broadcasted_iota(jnp.int32, (rows, LANES), 1)
        experts = range(MOE_EXPERTS_PER_STEP)
        gus = [_dot(x, wup_ref[k]) for k in experts]
        hiddens = [(_silu(gu[:, :D_EXPERT]) * gu[:, D_EXPERT:]).astype(MXU_DTYPE) for gu in gus]
        w_cols = [jnp.sum(jnp.where(lane_b == ROUTE_E0 + step * MOE_EXPERTS_PER_STEP + k, ws, 0.0),
                          axis=-1, keepdims=True) for k in experts]
        ys = [_dot(hiddens[k], wdn_ref[k]) for k in experts]
        ys_ref[pl.ds(r0, rows), :] += sum(y * w for y, w in zip(ys, w_cols))

    for k in MOE_COMMON_BLKS:
        @pl.when(n_blk == k)
        def _():
            expert_on(pl.multiple_of(first_blk * MOE_BLK, MOE_BLK), k * MOE_BLK)

    @pl.when(functools.reduce(jnp.logical_and, [n_blk != k for k in MOE_COMMON_BLKS]))
    def _():
        n_big = n_blk // MOE_LOOP_BLKS

        def big(j, carry):
            expert_on(pl.multiple_of((first_blk + j * MOE_LOOP_BLKS) * MOE_BLK, MOE_BLK), MOE_LOOP_BLKS * MOE_BLK)
            return carry

        def single(j, carry):
            expert_on(pl.multiple_of((first_blk + j) * MOE_BLK, MOE_BLK), MOE_BLK)
            return carry

        lax.fori_loop(0, n_big, big, 0)
        lax.fori_loop(n_big * MOE_LOOP_BLKS, n_blk, single, 0)

    @pl.when(step == pl.num_programs(1) - 1)
    def _():
        dest = dest_ref[...][:, :1].astype(jnp.int32)
        unperm = (lax.broadcasted_iota(jnp.int32, (tm, n_rows), 1) == dest).astype(MXU_DTYPE)
        y = _dot(unperm, ys_ref[...])
        o_ref[...] = _rms(x1_ref[...] + gt2_ref[...] * y) * gf_ref[...]


def _moe_final(h2, wt, w_up, w_down, x1, gt2, g_final, *, tm, rows_per_batch):
    m = h2.shape[0]
    tpb = rows_per_batch // tm
    n_rows = tm + N_GROUPS * MOE_BLK
    rowblk = pl.BlockSpec((tm, D_MODEL), lambda i, e: (i, 0))
    return pl.pallas_call(
        _moe_kernel,
        grid=(m // tm, N_EXPERTS // MOE_EXPERTS_PER_STEP),
        in_specs=[rowblk,
                  pl.BlockSpec((tm, LANES), lambda i, e: (i, 0)),
                  pl.BlockSpec((MOE_EXPERTS_PER_STEP, D_MODEL, 2 * D_EXPERT), lambda i, e: (e, 0, 0)),
                  pl.BlockSpec((MOE_EXPERTS_PER_STEP, D_EXPERT, D_MODEL), lambda i, e: (e, 0, 0)),
                  rowblk,
                  pl.BlockSpec((None, 1, D_MODEL), lambda i, e: (i // tpb, 0, 0)),
                  pl.BlockSpec((1, D_MODEL), lambda i, e: (0, 0))],
        out_specs=rowblk,
        out_shape=jax.ShapeDtypeStruct((m, D_MODEL), F32),
        scratch_shapes=[pltpu.VMEM((n_rows, D_MODEL), MXU_DTYPE), pltpu.VMEM((n_rows, D_MODEL), F32),
                        pltpu.VMEM((n_rows, LANES), F32), pltpu.VMEM((tm, LANES), F32),
                        pltpu.SMEM((2 * N_GROUPS,), jnp.int32)],
        compiler_params=_cparams(("parallel", "arbitrary"), VMEM_LIMIT_BIG),
        name="moe_final",
    )(h2, wt, w_up, w_down, x1, gt2, g_final)


def _empty_states(batch):
    ml = (jnp.zeros((batch, 2, HEADS, QK, DV + LANES), F32),
          jnp.full((batch, 2, HEADS, 1, LANES), NEG_BIG, F32))
    gla = jnp.zeros((batch, 2, HEADS, DV, QK), F32)
    return ml, gla


def kernel(x, c, ctx, c_ctx, w_mod, b_mod, g_norm1, w_in, ml_conv, ml_conv_b, b_mgate, ml_norm, gla_w2, gla_b2,
           gla_norm, w_proj_m, w_proj_g, w_out, g_norm2, w_grp, b_grp, w_rexp, b_rexp, w_up, w_down, g_final):
    batch, t, d = x.shape
    t_ctx = ctx.shape[1]
    assert d == D_MODEL and w_mod.shape[0] == 1 and w_in.shape[2] == sum(IN_SIZES)
    assert t % GRID_W == 0 and t % CHUNK == 0 and t_ctx % CHUNK == 0
    assert t % IN_PROJ_TM == 0 and t % MERGE_TM == 0 and t % MOE_TM == 0

    off = [0]
    for s in IN_SIZES:
        off.append(off[-1] + s)
    wi = w_in[0].astype(MXU_DTYPE)
    w_main = jnp.concatenate([wi[:, off[0]:off[4]], wi[:, off[5]:off[9]], wi[:, off[10]:off[12]]], axis=1)
    w_small = jnp.concatenate([wi[:, off[9]:off[10]], wi[:, off[4]:off[5]],
                               jnp.zeros((d, LANES - 2 * GLA_RANK - 4 * HEADS), MXU_DTYPE)], axis=1)
    gate_bias = jnp.zeros((1, LANES), F32).at[0, SMALL_GATE0:SMALL_GATE0 + 4 * HEADS].set(b_mgate[0])
    w2p = jnp.zeros((2, LANES, QK_W), F32)
    w2p = w2p.at[0, 0:GLA_RANK].set(gla_w2[0, 0]).at[1, GLA_RANK:2 * GLA_RANK].set(gla_w2[0, 1])
    b2 = gla_b2[0][:, None, :]
    conv_w = ml_conv[0].reshape(9, 2 * QK_W)
    conv_b = ml_conv_b[0][None, :]
    w_route = jnp.concatenate([w_grp[0], w_rexp[0], jnp.zeros((d, LANES - N_GROUPS - N_EXPERTS), F32)], axis=1)
    b_route = jnp.concatenate([b_grp[0], b_rexp[0], jnp.zeros((LANES - N_GROUPS - N_EXPERTS,), F32)])[None, :]

    cc = jnp.concatenate([c, c_ctx[None, :], jnp.zeros((8 - batch - 1, d), F32)], axis=0)
    mod = _modulation(cc, w_mod[0], b_mod[0][None, :])
    sh1, sc1, gt1, sh2, sc2, gt2 = [mod[:batch, i * d:(i + 1) * d][:, None, :] for i in range(6)]
    sh1c, sc1c = [mod[batch:batch + 1, i * d:(i + 1) * d][:, None, :] for i in range(2)]
    g1 = g_norm1[0][None, :]

    main_c, small_c = _in_proj(ctx.reshape(batch * t_ctx, d), g1, sc1c, sh1c, w_main, w_small,
                               tm=batch * t_ctx, rows_per_batch=batch * t_ctx, last_col_block=COL_V_G)
    qk_c = _conv_silu(main_c, conv_w, conv_b, batch=batch, rows=1, cols=t_ctx)
    ml0, gla0 = _empty_states(batch)
    main_c3, small_c3 = main_c.reshape(batch, t_ctx, main_c.shape[1]), small_c.reshape(batch, t_ctx, LANES)
    ml_states = _mlstm_scan(qk_c.reshape(batch, t_ctx, 2 * QK_W), main_c3, small_c3, gate_bias, ml0, with_output=False)
    gla_state = _gla_scan(main_c3, small_c3, w2p, b2, gla0, with_output=False)

    x2 = x.reshape(batch * t, d)
    main, small = _in_proj(x2, g1, sc1, sh1, w_main, w_small, tm=IN_PROJ_TM, rows_per_batch=t)
    qk = _conv_silu(main, conv_w, conv_b, batch=batch, rows=t // GRID_W, cols=GRID_W)
    main3, small3 = main.reshape(batch, t, MAIN_W), small.reshape(batch, t, LANES)
    hm_f, hm_b = [a.reshape(batch * t, V_W) for a in
                  _mlstm_scan(qk.reshape(batch, t, 2 * QK_W), main3, small3, gate_bias, ml_states, with_output=True)]
    n_steps = t // CHUNK
    later_weights = (w_up[0], w_down[0], w_proj_m[0], w_proj_g[0], w_out[0])
    og_f, og_b, *cast_weights = _gla_scan(
        main3, small3, w2p, b2, gla_state, with_output=True,
        cast_along=tuple(w.reshape(n_steps, -1, w.shape[-1]) for w in later_weights))
    og_f, og_b = og_f.reshape(batch * t, V_W), og_b.reshape(batch * t, V_W)
    w_up_c, w_down_c, w_pm_c, w_pg_c, w_out_c = [c.reshape(w.shape) for c, w in zip(cast_weights, later_weights)]

    x1, h2, wt = _merge(hm_f, hm_b, og_f, og_b, main, x2, ml_norm, gla_norm,
                        w_pm_c, w_pg_c, w_out_c,
                        gt1, g_norm2, sc2, sh2, w_route, b_route, tm=MERGE_TM, rows_per_batch=t)
    out = _moe_final(h2, wt, w_up_c, w_down_c, x1, gt2, g_final[None, :],
                     tm=MOE_TM, rows_per_batch=t)
    return out.reshape(batch, t, d)
```

```python
import functools

import jax
import jax.numpy as jnp
from jax import lax
from jax.experimental import pallas as pl
from jax.experimental.pallas import tpu as pltpu

D_MODEL = 1024
GRID_W = 64
CHUNK = 256
EPS = 1e-6
NEG_BIG = -1e30
HEADS = 4
QK = D_MODEL // 8
DV = D_MODEL // 4
QK_W = HEADS * QK
V_W = HEADS * DV
GLA_RANK = 16
GLA_TAU = 16.0
N_GROUPS = 4
EXPERTS_PER_GROUP = 4
N_EXPERTS = N_GROUPS * EXPERTS_PER_GROUP
D_EXPERT = D_MODEL // 2
IN_SIZES = (QK_W, QK_W, V_W, V_W, 4 * HEADS, QK_W, QK_W, V_W, V_W, 2 * GLA_RANK, D_MODEL, D_MODEL)

LANES = 128
CONV_TILE = 256
MXU_DTYPE = jnp.bfloat16
F32 = jnp.float32
MIB = 1024 * 1024
VMEM_LIMIT = 48 * MIB
VMEM_LIMIT_BIG = 56 * MIB
MOD_TN = 2048
IN_PROJ_TM, IN_PROJ_TN = 1024, 4096
MERGE_TM = 512
MOE_TM = 1024

COL_QK_M, COL_V_M, COL_O_M, COL_QK_G, COL_V_G, COL_R_G, COL_MG_M, COL_MG_G = range(8)
MAIN_W = 8 * D_MODEL
SMALL_GATE0 = 2 * GLA_RANK
GLA_SAFE_DECAY = 80.0
ROUTE_E0 = N_GROUPS


def _dot(a, b):
    return jnp.dot(a.astype(MXU_DTYPE), b.astype(MXU_DTYPE), preferred_element_type=F32)


def _dot_nt(a, b):
    return lax.dot_general(a.astype(MXU_DTYPE), b.astype(MXU_DTYPE), (((1,), (1,)), ((), ())),
                           preferred_element_type=F32)


def _transpose_mxu(a):
    m = a.shape[1]
    eye = (lax.broadcasted_iota(jnp.int32, (m, m), 0) == lax.broadcasted_iota(jnp.int32, (m, m), 1))
    return _dot_nt(eye.astype(MXU_DTYPE), a).astype(MXU_DTYPE)


def _dot_tn_xlu(a, b):
    return lax.dot_general(a.astype(MXU_DTYPE), b.astype(MXU_DTYPE), (((0,), (0,)), ((), ())),
                           preferred_element_type=F32)


def _split3(x):
    hi = x.astype(MXU_DTYPE)
    r1 = x - hi.astype(F32)
    mid = r1.astype(MXU_DTYPE)
    lo = (r1 - mid.astype(F32)).astype(MXU_DTYPE)
    return hi, mid, lo


def _dot_exact_lhs(a01, x):
    hi, mid, lo = _split3(x)
    return _dot(a01, hi) + _dot(a01, mid) + _dot(a01, lo)


def _log_sigmoid(x):
    return jnp.minimum(x, 0.0) - jnp.log(1.0 + jnp.exp(-jnp.abs(x)))


def _silu(x):
    return x * jax.nn.sigmoid(x)


def _rms(x):
    return x * lax.rsqrt(jnp.mean(x * x, axis=-1, keepdims=True) + EPS)


def _cparams(sem, vmem_limit=VMEM_LIMIT):
    return pltpu.CompilerParams(dimension_semantics=sem, vmem_limit_bytes=vmem_limit)


def _mod_kernel(c_ref, w_ref, b_ref, o_ref):
    o_ref[...] = _dot(_silu(c_ref[...]), w_ref[...]) + b_ref[...]


def _modulation(cc, w_mod, b_mod):
    n = w_mod.shape[1]
    tn = MOD_TN
    return pl.pallas_call(
        _mod_kernel,
        grid=(n // tn,),
        in_specs=[pl.BlockSpec((8, D_MODEL), lambda j: (0, 0)),
                  pl.BlockSpec((D_MODEL, tn), lambda j: (0, j)),
                  pl.BlockSpec((1, tn), lambda j: (0, j))],
        out_specs=pl.BlockSpec((8, tn), lambda j: (0, j)),
        out_shape=jax.ShapeDtypeStruct((8, n), F32),
        compiler_params=_cparams(("arbitrary",)),
        name="modulation",
    )(cc, w_mod, b_mod)


def _inproj_kernel(x_ref, g_ref, sc_ref, sh_ref, w_ref, ws_ref, o_ref, os_ref, xn_ref):
    @pl.when(pl.program_id(1) == 0)
    def _():
        xn = _rms(x_ref[...]) * g_ref[...] * (1.0 + sc_ref[...]) + sh_ref[...]
        xn_ref[...] = xn.astype(MXU_DTYPE)
        os_ref[...] = _dot(xn_ref[...], ws_ref[...])

    half = o_ref.shape[1] // 2
    for k in range(2):
        o_ref[:, k * half:(k + 1) * half] = _dot(xn_ref[...], w_ref[:, k * half:(k + 1) * half]).astype(o_ref.dtype)


def _in_proj(x2, g, sc, sh, w_main, w_small, *, tm, rows_per_batch, last_col_block=COL_MG_G):
    m = x2.shape[0]
    tn = IN_PROJ_TN
    col_tiles = last_col_block * D_MODEL // tn + 1
    tiles_per_batch = rows_per_batch // tm
    vec = pl.BlockSpec((None, 1, D_MODEL), lambda i, j: (i // tiles_per_batch, 0, 0))
    return pl.pallas_call(
        _inproj_kernel,
        grid=(m // tm, col_tiles),
        in_specs=[pl.BlockSpec((tm, D_MODEL), lambda i, j: (i, 0)),
                  pl.BlockSpec((1, D_MODEL), lambda i, j: (0, 0)),
                  vec, vec,
                  pl.BlockSpec((D_MODEL, tn), lambda i, j: (0, j)),
                  pl.BlockSpec((D_MODEL, LANES), lambda i, j: (0, 0))],
        out_specs=[pl.BlockSpec((tm, tn), lambda i, j: (i, j)),
                   pl.BlockSpec((tm, LANES), lambda i, j: (i, 0))],
        out_shape=[jax.ShapeDtypeStruct((m, col_tiles * tn), MXU_DTYPE), jax.ShapeDtypeStruct((m, LANES), F32)],
        scratch_shapes=[pltpu.VMEM((tm, D_MODEL), MXU_DTYPE)],
        compiler_params=_cparams(("parallel", "arbitrary"), VMEM_LIMIT_BIG),
        name="in_proj",
    )(x2, g, sc, sh, w_main, w_small)


def _conv_kernel(x_ref, w_ref, b_ref, o_ref, *, rows, cols):
    scale = jnp.where(pl.program_id(1) * CONV_TILE >= QK_W, QK ** -0.5, 1.0).astype(F32)
    w = w_ref[...]
    bias = b_ref[...]
    tpos = lax.broadcasted_iota(jnp.int32, (cols, 1), 0)
    has_left = tpos >= 1
    has_right = tpos < cols - 1

    def row_filters(j):
        tile = x_ref[pl.ds(pl.multiple_of(j * cols, cols), cols), :].astype(F32)
        left = jnp.where(has_left, pltpu.roll(tile, 1, axis=0), 0.0)
        right = jnp.where(has_right, pltpu.roll(tile, cols - 1, axis=0), 0.0)
        return [left * w[3 * i:3 * i + 1, :] + tile * w[3 * i + 1:3 * i + 2, :] + right * w[3 * i + 2:3 * i + 3, :]
                for i in range(3)]

    def finish(j, acc):
        o_ref[pl.ds(pl.multiple_of(j * cols, cols), cols), :] = (_silu(acc + bias) * scale).astype(o_ref.dtype)

    first = row_filters(0)

    def body(j, carry):
        acc, below = carry
        h = row_filters(j)
        finish(j - 1, acc + h[2])
        return below + h[1], h[0]

    acc, _ = lax.fori_loop(1, rows, body, (first[1], first[0]))
    finish(rows - 1, acc)


def _conv_silu(main, conv_w, conv_b, *, batch, rows, cols):
    t = rows * cols
    nct = 2 * QK_W // CONV_TILE
    return pl.pallas_call(
        functools.partial(_conv_kernel, rows=rows, cols=cols),
        grid=(batch, nct),
        in_specs=[pl.BlockSpec((t, CONV_TILE), lambda b, c: (b, c)),
                  pl.BlockSpec((9, CONV_TILE), lambda b, c: (0, c)),
                  pl.BlockSpec((1, CONV_TILE), lambda b, c: (0, c))],
        out_specs=pl.BlockSpec((t, CONV_TILE), lambda b, c: (b, c)),
        out_shape=jax.ShapeDtypeStruct((batch * t, 2 * QK_W), MXU_DTYPE),
        compiler_params=_cparams(("parallel", "arbitrary")),
        name="conv_silu",
    )(main, conv_w, conv_b)


def _chunk_masks(direction):
    row = lax.broadcasted_iota(jnp.int32, (CHUNK, CHUNK), 0)
    col = lax.broadcasted_iota(jnp.int32, (CHUNK, CHUNK), 1)
    seen = (row >= col) if direction == 0 else (row <= col)
    return seen, seen.astype(MXU_DTYPE)


def _scan_specs(batch, nc, col_blocks, widths):
    specs = []
    for direction in (0, 1):
        for cb, wd in zip(col_blocks, widths):
            if direction == 0:
                specs.append(pl.BlockSpec((batch, CHUNK, wd), lambda c, cb=cb: (0, c, cb)))
            else:
                specs.append(pl.BlockSpec((batch, CHUNK, wd), lambda c, cb=cb: (0, nc - 1 - c, cb)))
    return specs


def _scan_out_specs(batch, nc):
    return [pl.BlockSpec((batch, CHUNK, V_W), lambda c: (0, c, 0)),
            pl.BlockSpec((batch, CHUNK, V_W), lambda c: (0, nc - 1 - c, 0))]


def _whole(shape):
    nd = len(shape)
    return pl.BlockSpec(tuple(shape), lambda c: (0,) * nd)


def _lane_tile(x, width):
    return jnp.concatenate([x] * (width // LANES), axis=-1)


def _mlstm_kernel(*refs, with_output, batch):
    (qk_f, v_f, sm_f, qk_b, v_b, sm_b, bias_ref, cn0_ref, m0_ref) = refs[:9]
    if with_output:
        hf_ref, hb_ref, cn_s, m_s = refs[9:]
    else:
        cn_out, m_out, cn_s, m_s = refs[9:]
    step = pl.program_id(0)

    @pl.when(step == 0)
    def _():
        cn_s[...] = cn0_ref[...]
        m_s[...] = m0_ref[...]

    lane = lax.broadcasted_iota(jnp.int32, (1, LANES), 1)
    gate_lane = jnp.logical_and(lane >= SMALL_GATE0, lane < SMALL_GATE0 + 4 * HEADS)
    forget_lane = jnp.logical_and(gate_lane, ((lane - SMALL_GATE0) % (2 * HEADS)) >= HEADS)
    eye = (lax.broadcasted_iota(jnp.int32, (LANES, LANES), 0)
           == lax.broadcasted_iota(jnp.int32, (LANES, LANES), 1)).astype(MXU_DTYPE)
    ones_cols = jnp.ones((CHUNK, LANES), MXU_DTYPE)

    chains = []
    for direction, (qk_ref, v_ref, sm_ref) in enumerate(((qk_f, v_f, sm_f), (qk_b, v_b, sm_b))):
        seen, seen01 = _chunk_masks(direction)
        last = CHUNK - 1 if direction == 0 else 0
        for bi in range(batch):
            g = sm_ref[bi] + bias_ref[...]
            gp = jnp.where(forget_lane, _log_sigmoid(g), g)
            bc = _dot_exact_lhs(seen01, gp)
            hi, mid, lo = _split3(gp)
            gp_t = _dot_nt(eye, hi) + _dot_nt(eye, mid) + _dot_nt(eye, lo)
            hi, mid, lo = _split3(bc)
            bc_t = _dot_nt(eye, hi) + _dot_nt(eye, mid) + _dot_nt(eye, lo)
            bend_row = bc[last:last + 1, :]
            for h in range(HEADS):
                chains.append(dict(h=h, bi=bi, direction=direction, seen=seen, gp=gp, bc=bc, gp_t=gp_t, bc_t=bc_t,
                                   bend_row=bend_row, qk_ref=qk_ref, v_ref=v_ref))

    def load(c):
        h, bi, direction = c["h"], c["bi"], c["direction"]
        c["ji"] = SMALL_GATE0 + direction * 2 * HEADS + h
        c["jf"] = c["ji"] + HEADS
        c["q"] = c["qk_ref"][bi, :, h * QK:(h + 1) * QK].astype(F32)
        c["k"] = c["qk_ref"][bi, :, QK_W + h * QK:QK_W + (h + 1) * QK].astype(F32)
        c["v_ext"] = jnp.concatenate([c["v_ref"][bi, :, h * DV:(h + 1) * DV].astype(MXU_DTYPE), ones_cols], axis=-1)
        c["cn_old"] = cn_s[bi, direction, h]
        c["m_old"] = m_s[bi, direction, h]
        if with_output:
            c["qk"] = _dot_nt(c["q"], c["k"])
            c["qc"] = _dot(c["q"], c["cn_old"])

    def gates(c):
        b_end = jnp.broadcast_to(c["bend_row"][:, c["jf"]:c["jf"] + 1], (1, LANES))
        i_col = jnp.broadcast_to(c["gp"][:, c["ji"]:c["ji"] + 1], (CHUNK, LANES))
        c["b_col"] = jnp.broadcast_to(c["bc"][:, c["jf"]:c["jf"] + 1], (CHUNK, LANES))
        log_w = b_end - c["b_col"] + i_col
        c["m_new"] = jnp.maximum(b_end + c["m_old"], jnp.max(log_w, axis=0, keepdims=True))
        c["kw"] = (c["k"] * jnp.exp(log_w - c["m_new"])).astype(MXU_DTYPE)
        c["decay"] = jnp.exp(b_end + c["m_old"] - c["m_new"])
        c["kw_t"] = _transpose_mxu(c["kw"])

    def weights(c):
        if with_output:
            i_row = c["gp_t"][c["ji"]:c["ji"] + 1, :]
            b_row = c["bc_t"][c["jf"]:c["jf"] + 1, :]
            log_d = jnp.where(c["seen"], _lane_tile(c["b_col"], CHUNK) - b_row + i_row, -jnp.inf)
            log_inter = c["b_col"] + c["m_old"]
            c["m_t"] = jnp.maximum(log_inter, jnp.max(log_d, axis=-1, keepdims=True))
            c["s"] = (c["qk"] * jnp.exp(log_d - _lane_tile(c["m_t"], CHUNK))).astype(MXU_DTYPE)
            c["w_inter"] = jnp.exp(log_inter - c["m_t"])

    def apply(c):
        idx = (c["bi"], c["direction"], c["h"])
        if with_output:
            c["sv"] = _dot(c["s"], c["v_ext"])
        cn_s[idx] = _lane_tile(c["decay"], DV + LANES) * c["cn_old"] + _dot(c["kw_t"], c["v_ext"])
        m_s[idx] = c["m_new"]

    def emit(c):
        if with_output:
            sv, qc, w_inter, h = c["sv"], c["qc"], c["w_inter"], c["h"]
            num = sv[:, :DV] + _lane_tile(w_inter, DV) * qc[:, :DV]
            den = jnp.abs(sv[:, DV:] + w_inter * qc[:, DV:])
            (hf_ref if c["direction"] == 0 else hb_ref)[c["bi"], :, h * DV:(h + 1) * DV] = (
                num / _lane_tile(jnp.maximum(den, jnp.exp(-c["m_t"])), DV))

    stages = (load, gates, weights, apply, emit)
    for tick in range(len(chains) + len(stages) - 1):
        for depth, stage in enumerate(stages):
            if 0 <= tick - depth < len(chains):
                stage(chains[tick - depth])

    if not with_output:
        @pl.when(step == pl.num_programs(0) - 1)
        def _():
            cn_out[...] = cn_s[...]
            m_out[...] = m_s[...]


def _mlstm_scan(qk, main, small, gate_bias, states, *, with_output):
    cn0, m0 = states
    batch, t, _ = qk.shape
    nc = t // CHUNK
    in_specs = _scan_specs(batch, nc, (0, COL_V_M, 0), (2 * QK_W, V_W, LANES))
    in_specs += [_whole(gate_bias.shape), _whole(cn0.shape), _whole(m0.shape)]
    if with_output:
        out_specs = _scan_out_specs(batch, nc)
        out_shape = [jax.ShapeDtypeStruct((batch, t, V_W), F32)] * 2
    else:
        out_specs = [_whole(cn0.shape), _whole(m0.shape)]
        out_shape = [jax.ShapeDtypeStruct(s.shape, F32) for s in states]
    return pl.pallas_call(
        functools.partial(_mlstm_kernel, with_output=with_output, batch=batch),
        grid=(nc,),
        in_specs=in_specs,
        out_specs=out_specs,
        out_shape=out_shape,
        scratch_shapes=[pltpu.VMEM(cn0.shape, F32), pltpu.VMEM(m0.shape, F32)],
        compiler_params=_cparams(("arbitrary",)),
        name="mlstm_scan_out" if with_output else "mlstm_scan_state",
    )(qk, main, small, qk, main, small, gate_bias, cn0, m0)


def _gla_exact_intra(q, k, v, b, direction):
    row_id = lax.broadcasted_iota(jnp.int32, (CHUNK, 1), 0)

    def row(t, acc):
        pick = row_id == t
        b_t = jnp.sum(jnp.where(pick, b, 0.0), axis=0, keepdims=True)
        q_t = jnp.sum(jnp.where(pick, q, 0.0), axis=0, keepdims=True)
        ok = (row_id <= t) if direction == 0 else (row_id >= t)
        e = jnp.exp(jnp.where(ok, b_t - b, -jnp.inf))
        sc = jnp.sum(q_t * k * e, axis=-1, keepdims=True)
        o_t = jnp.sum(sc * v, axis=0, keepdims=True)
        return jnp.where(pick, o_t, acc)

    return lax.fori_loop(0, CHUNK, row, jnp.zeros((CHUNK, DV), F32))


def _gla_kernel(*refs, with_output, batch, n_cast=0):
    (qk_f, v_f, sm_f, qk_b, v_b, sm_b, w2_ref, b2_ref, s0_ref) = refs[:9]
    cast_in, rest = refs[9:9 + n_cast], refs[9 + n_cast:]
    if with_output:
        of_ref, ob_ref = rest[:2]
        cast_out = rest[2:2 + n_cast]
        s_s, b_s, inter_s = rest[2 + n_cast:]
    else:
        s_out, s_s = rest
    step = pl.program_id(0)

    for src, dst in zip(cast_in, cast_out if with_output else ()):
        dst[...] = src[...].astype(dst.dtype)

    @pl.when(step == 0)
    def _():
        s_s[...] = s0_ref[...]

    worst_decay = []
    for direction, (qk_ref, v_ref, sm_ref) in enumerate(((qk_f, v_f, sm_f), (qk_b, v_b, sm_b))):
        seen, seen01 = _chunk_masks(direction)
        last = CHUNK - 1 if direction == 0 else 0
        for bi in range(batch):
            z = _dot(sm_ref[bi], w2_ref[direction]) + b2_ref[direction]
            log_a = _log_sigmoid(z) * (1.0 / GLA_TAU)
            b_all = _dot_exact_lhs(seen01, log_a)
            outs, inters = [], []
            for h in range(HEADS):
                q = qk_ref[bi, :, h * QK:(h + 1) * QK].astype(F32) * (QK ** -0.5)
                k = qk_ref[bi, :, QK_W + h * QK:QK_W + (h + 1) * QK].astype(F32)
                v = v_ref[bi, :, h * DV:(h + 1) * DV]
                b = b_all[:, h * QK:(h + 1) * QK]
                b_end = b[last:last + 1, :]
                st_old = s_s[bi, direction, h]
                k_dec = k * jnp.exp(b_end - b)
                s_s[bi, direction, h] = st_old * jnp.exp(b_end) + _dot_tn_xlu(v, k_dec)
                if with_output:
                    q_dec = q * jnp.exp(b)
                    inter = _dot_nt(q_dec, st_old)
                    scores = jnp.where(seen, _dot_nt(q_dec, k * jnp.exp(-b)), 0.0)
                    outs.append(_dot(scores, v) + inter)
                    inters.append(inter)
            if with_output:
                (of_ref if direction == 0 else ob_ref)[bi] = jnp.concatenate(outs, axis=-1)
                b_s[bi, direction] = b_all
                inter_s[bi, direction] = jnp.concatenate(inters, axis=-1)
                worst_decay.append(jnp.max(-b_all[last:last + 1, :]))

    if with_output:
        @pl.when(functools.reduce(jnp.maximum, worst_decay) > GLA_SAFE_DECAY)
        def _():
            for direction, (qk_ref, v_ref, o_ref) in enumerate(((qk_f, v_f, of_ref), (qk_b, v_b, ob_ref))):
                last = CHUNK - 1 if direction == 0 else 0
                for bi in range(batch):
                    for h in range(HEADS):
                        b = b_s[bi, direction, :, h * QK:(h + 1) * QK]

                        @pl.when(jnp.max(-b[last:last + 1, :]) > GLA_SAFE_DECAY)
                        def _():
                            q = qk_ref[bi, :, h * QK:(h + 1) * QK].astype(F32) * (QK ** -0.5)
                            k = qk_ref[bi, :, QK_W + h * QK:QK_W + (h + 1) * QK].astype(F32)
                            v = v_ref[bi, :, h * DV:(h + 1) * DV].astype(F32)
                            o_ref[bi, :, h * DV:(h + 1) * DV] = (
                                inter_s[bi, direction, :, h * DV:(h + 1) * DV]
                                + _gla_exact_intra(q, k, v, b, direction))
    else:
        @pl.when(step == pl.num_programs(0) - 1)
        def _():
            s_out[...] = s_s[...]


def _gla_scan(main, small, w2p, b2, s0, *, with_output, cast_along=()):
    batch, t, _ = main.shape
    nc = t // CHUNK
    in_specs = _scan_specs(batch, nc, (COL_QK_G, COL_V_G, 0), (2 * QK_W, V_W, LANES))
    in_specs += [_whole(w2p.shape), _whole(b2.shape), _whole(s0.shape)]
    slab = lambda a: pl.BlockSpec((None,) + a.shape[1:], lambda c: (c, 0, 0))
    in_specs += [slab(a) for a in cast_along]
    scratch = [pltpu.VMEM(s0.shape, F32)]
    if with_output:
        assert all(a.shape[0] == nc for a in cast_along)
        out_specs = _scan_out_specs(batch, nc) + [slab(a) for a in cast_along]
        out_shape = ([jax.ShapeDtypeStruct((batch, t, V_W), F32)] * 2
                     + [jax.ShapeDtypeStruct(a.shape, MXU_DTYPE) for a in cast_along])
        scratch += [pltpu.VMEM((batch, 2, CHUNK, QK_W), F32), pltpu.VMEM((batch, 2, CHUNK, V_W), F32)]
    else:
        out_specs = _whole(s0.shape)
        out_shape = jax.ShapeDtypeStruct(s0.shape, F32)
    return pl.pallas_call(
        functools.partial(_gla_kernel, with_output=with_output, batch=batch, n_cast=len(cast_along)),
        grid=(nc,),
        in_specs=in_specs,
        out_specs=out_specs,
        out_shape=out_shape,
        scratch_shapes=scratch,
        compiler_params=_cparams(("arbitrary",)),
        name="gla_scan_out" if with_output else "gla_scan_state",
    )(main, main, small, main, main, small, w2p, b2, s0, *cast_along)


def _head_rms(a):
    return jnp.concatenate([_rms(a[:, h * DV:(h + 1) * DV]) for h in range(HEADS)], axis=-1)


def _merge_kernel(hmf, hmb, ogf, ogb, om, rg, mgm, mgg, x_ref, mln, gln, wpm, wpg, wo, gt1, g2, sc2, sh2,
                  wr, br, x1_ref, h2_ref, wt_ref):
    tm = x_ref.shape[0]
    parts = [slice(i * tm // MERGE_SPLIT, (i + 1) * tm // MERGE_SPLIT) for i in range(MERGE_SPLIT)]
    y_m = [_head_rms(hmf[r, :] + hmb[r, :]) * mln[...] * jax.nn.sigmoid(om[r, :].astype(F32)) for r in parts]
    p_m = [_dot(a, wpm[...]) for a in y_m]
    y_g = [_head_rms(ogf[r, :] + ogb[r, :]) * gln[...] * _silu(rg[r, :].astype(F32)) for r in parts]
    p_g = [_dot(a, wpg[...]) for a in y_g]
    y = [jax.nn.sigmoid(mgm[r, :].astype(F32)) * a + jax.nn.sigmoid(mgg[r, :].astype(F32)) * b
         for r, a, b in zip(parts, p_m, p_g)]
    mix = [_dot(a, wo[...]) for a in y]
    h2_parts = []
    for r, a in zip(parts, mix):
        x1 = x_ref[r, :] + gt1[...] * a
        x1_ref[r, :] = x1
        h2_parts.append(_rms(x1) * g2[...] * (1.0 + sc2[...]) + sh2[...])
    h2 = jnp.concatenate(h2_parts, axis=0)
    h2_ref[...] = h2.astype(MXU_DTYPE)

    hh, hm_, _ = _split3(h2)
    wh, wm_, _ = _split3(wr[...])
    lg = _dot(hh, wh) + _dot(hh, wm_) + _dot(hm_, wh) + br[...]
    lane = lax.broadcasted_iota(jnp.int32, lg.shape, 1)

    def masked_softmax(mask):
        l = jnp.where(mask, lg, -jnp.inf)
        e = jnp.exp(l - jnp.max(l, axis=-1, keepdims=True))
        return e / jnp.sum(e, axis=-1, keepdims=True)

    def top1(p, mask):
        pm = jnp.where(mask, p, -1.0)
        best = jnp.max(pm, axis=-1, keepdims=True)
        idx = jnp.min(jnp.where(jnp.logical_and(mask, pm == best), lane, LANES), axis=-1, keepdims=True)
        return best, idx

    gmask = lane < N_GROUPS
    grp_p, grp = top1(masked_softmax(gmask), gmask)
    e_lo = ROUTE_E0 + grp * EXPERTS_PER_GROUP
    emask = jnp.logical_and(lane >= e_lo, lane < e_lo + EXPERTS_PER_GROUP)
    p_in = masked_softmax(emask)
    p1, i1 = top1(p_in, emask)
    p2, i2 = top1(p_in, jnp.logical_and(emask, lane != i1))
    tot = p1 + p2
    wt_ref[...] = (jnp.where(lane == i1, grp_p * p1 / tot, 0.0)
                   + jnp.where(lane == i2, grp_p * p2 / tot, 0.0)
                   + jnp.where(lane == grp, 1.0, 0.0))


def _merge(hmf, hmb, ogf, ogb, main, x2, mln, gln, wpm, wpg, wo, gt1, g2, sc2, sh2, wr, br, *, tm, rows_per_batch):
    m = x2.shape[0]
    tpb = rows_per_batch // tm
    rowblk = pl.BlockSpec((tm, D_MODEL), lambda i: (i, 0))
    colblk = lambda cb: pl.BlockSpec((tm, D_MODEL), lambda i, cb=cb: (i, cb))
    vec = pl.BlockSpec((1, D_MODEL), lambda i: (0, 0))
    bvec = pl.BlockSpec((None, 1, D_MODEL), lambda i: (i // tpb, 0, 0))
    wmat = pl.BlockSpec((D_MODEL, D_MODEL), lambda i: (0, 0))
    return pl.pallas_call(
        _merge_kernel,
        grid=(m // tm,),
        in_specs=[rowblk, rowblk, rowblk, rowblk, colblk(COL_O_M), colblk(COL_R_G), colblk(COL_MG_M),
                  colblk(COL_MG_G), rowblk, vec, vec, wmat, wmat, wmat, bvec, vec, bvec, bvec,
                  pl.BlockSpec((D_MODEL, LANES), lambda i: (0, 0)), pl.BlockSpec((1, LANES), lambda i: (0, 0))],
        out_specs=[rowblk, rowblk, pl.BlockSpec((tm, LANES), lambda i: (i, 0))],
        out_shape=[jax.ShapeDtypeStruct((m, D_MODEL), F32), jax.ShapeDtypeStruct((m, D_MODEL), MXU_DTYPE),
                   jax.ShapeDtypeStruct((m, LANES), F32)],
        compiler_params=_cparams(("parallel",), VMEM_LIMIT_BIG),
        name="merge_route",
    )(hmf, hmb, ogf, ogb, main, main, main, main, x2, mln, gln, wpm, wpg, wo, gt1, g2, sc2, sh2, wr, br)


MERGE_SPLIT = 2
MOE_BLK = 64
MOE_COMMON_BLKS = (3, 4, 5, 6)
MOE_LOOP_BLKS = 4
MOE_EXPERTS_PER_STEP = 2


def _moe_kernel(h2_ref, wt_ref, wup_ref, wdn_ref, x1_ref, gt2_ref, gf_ref, o_ref,
                xs_ref, ys_ref, ws_ref, dest_ref, blk_ref):
    step = pl.program_id(1)
    tm = h2_ref.shape[0]
    n_rows = xs_ref.shape[0]

    @pl.when(step == 0)
    def _():
        r = wt_ref[...]
        lane = lax.broadcasted_iota(jnp.int32, (tm, LANES), 1)
        lane1 = lax.broadcasted_iota(jnp.int32, (1, LANES), 1)
        gm = jnp.where(lane < N_GROUPS, r, 0.0)
        earlier = (lax.broadcasted_iota(jnp.int32, (tm, tm), 1)
                   < lax.broadcasted_iota(jnp.int32, (tm, tm), 0)).astype(MXU_DTYPE)
        before = _dot(earlier, gm)
        padded = jnp.floor((jnp.sum(gm, axis=0, keepdims=True) + (MOE_BLK - 1)) * (1.0 / MOE_BLK)) * MOE_BLK
        start = jnp.zeros((1, LANES), F32)
        run = jnp.zeros((1, 1), F32)
        for g in range(N_GROUPS):
            size = jnp.sum(jnp.where(lane1 == g, padded, 0.0), axis=-1, keepdims=True)
            start = jnp.where(lane1 == g, run, start)
            blk_ref[g] = (jnp.sum(run) * (1.0 / MOE_BLK)).astype(jnp.int32)
            blk_ref[N_GROUPS + g] = (jnp.sum(size) * (1.0 / MOE_BLK)).astype(jnp.int32)
            run = run + size
        dest = jnp.sum(gm * (start + before), axis=-1, keepdims=True)
        dest_ref[...] = jnp.broadcast_to(dest, (tm, LANES))
        dest_row = dest_ref[...].T[0:1, :].astype(jnp.int32)
        perm = (lax.broadcasted_iota(jnp.int32, (n_rows, tm), 0) == dest_row).astype(MXU_DTYPE)
        xs_ref[...] = _dot(perm, h2_ref[...]).astype(MXU_DTYPE)
        moved = _dot(perm, jnp.concatenate(_split3(r), axis=-1))
        ws_ref[...] = moved[:, :LANES] + moved[:, LANES:2 * LANES] + moved[:, 2 * LANES:]
        ys_ref[...] = jnp.zeros_like(ys_ref)

    group = step // (EXPERTS_PER_GROUP // MOE_EXPERTS_PER_STEP)
    first_blk = blk_ref[group]
    n_blk = blk_ref[N_GROUPS + group]

    def expert_on(r0, rows):
        x = xs_ref[pl.ds(r0, rows), :]
        ws = ws_ref[pl.ds(r0, rows), :]
        lane_b = lax.broadcasted_iota(jnp.int32, (rows, LANES), 1)
        experts = range(MOE_EXPERTS_PER_STEP)
        gus = [_dot(x, wup_ref[k]) for k in experts]
        hiddens = [(_silu(gu[:, :D_EXPERT]) * gu[:, D_EXPERT:]).astype(MXU_DTYPE) for gu in gus]
        w_cols = [jnp.sum(jnp.where(lane_b == ROUTE_E0 + step * MOE_EXPERTS_PER_STEP + k, ws, 0.0),
                          axis=-1, keepdims=True) for k in experts]
        ys = [_dot(hiddens[k], wdn_ref[k]) for k in experts]
        ys_ref[pl.ds(r0, rows), :] += sum(y * w for y, w in zip(ys, w_cols))

    for k in MOE_COMMON_BLKS:
        @pl.when(n_blk == k)
        def _():
            expert_on(pl.multiple_of(first_blk * MOE_BLK, MOE_BLK), k * MOE_BLK)

    @pl.when(functools.reduce(jnp.logical_and, [n_blk != k for k in MOE_COMMON_BLKS]))
    def _():
        n_big = n_blk // MOE_LOOP_BLKS

        def big(j, carry):
            expert_on(pl.multiple_of((first_blk + j * MOE_LOOP_BLKS) * MOE_BLK, MOE_BLK), MOE_LOOP_BLKS * MOE_BLK)
            return carry

        def single(j, carry):
            expert_on(pl.multiple_of((first_blk + j) * MOE_BLK, MOE_BLK), MOE_BLK)
            return carry

        lax.fori_loop(0, n_big, big, 0)
        lax.fori_loop(n_big * MOE_LOOP_BLKS, n_blk, single, 0)

    @pl.when(step == pl.num_programs(1) - 1)
    def _():
        dest = dest_ref[...][:, :1].astype(jnp.int32)
        unperm = (lax.broadcasted_iota(jnp.int32, (tm, n_rows), 1) == dest).astype(MXU_DTYPE)
        y = _dot(unperm, ys_ref[...])
        o_ref[...] = _rms(x1_ref[...] + gt2_ref[...] * y) * gf_ref[...]


def _moe_final(h2, wt, w_up, w_down, x1, gt2, g_final, *, tm, rows_per_batch):
    m = h2.shape[0]
    tpb = rows_per_batch // tm
    n_rows = tm + N_GROUPS * MOE_BLK
    rowblk = pl.BlockSpec((tm, D_MODEL), lambda i, e: (i, 0))
    return pl.pallas_call(
        _moe_kernel,
        grid=(m // tm, N_EXPERTS // MOE_EXPERTS_PER_STEP),
        in_specs=[rowblk,
                  pl.BlockSpec((tm, LANES), lambda i, e: (i, 0)),
                  pl.BlockSpec((MOE_EXPERTS_PER_STEP, D_MODEL, 2 * D_EXPERT), lambda i, e: (e, 0, 0)),
                  pl.BlockSpec((MOE_EXPERTS_PER_STEP, D_EXPERT, D_MODEL), lambda i, e: (e, 0, 0)),
                  rowblk,
                  pl.BlockSpec((None, 1, D_MODEL), lambda i, e: (i // tpb, 0, 0)),
                  pl.BlockSpec((1, D_MODEL), lambda i, e: (0, 0))],
        out_specs=rowblk,
        out_shape=jax.ShapeDtypeStruct((m, D_MODEL), F32),
        scratch_shapes=[pltpu.VMEM((n_rows, D_MODEL), MXU_DTYPE), pltpu.VMEM((n_rows, D_MODEL), F32),
                        pltpu.VMEM((n_rows, LANES), F32), pltpu.VMEM((tm, LANES), F32),
                        pltpu.SMEM((2 * N_GROUPS,), jnp.int32)],
        compiler_params=_cparams(("parallel", "arbitrary"), VMEM_LIMIT_BIG),
        name="moe_final",
    )(h2, wt, w_up, w_down, x1, gt2, g_final)


def _empty_states(batch):
    ml = (jnp.zeros((batch, 2, HEADS, QK, DV + LANES), F32),
          jnp.full((batch, 2, HEADS, 1, LANES), NEG_BIG, F32))
    gla = jnp.zeros((batch, 2, HEADS, DV, QK), F32)
    return ml, gla


def kernel(x, c, ctx, c_ctx, w_mod, b_mod, g_norm1, w_in, ml_conv, ml_conv_b, b_mgate, ml_norm, gla_w2, gla_b2,
           gla_norm, w_proj_m, w_proj_g, w_out, g_norm2, w_grp, b_grp, w_rexp, b_rexp, w_up, w_down, g_final):
    batch, t, d = x.shape
    t_ctx = ctx.shape[1]
    assert d == D_MODEL and w_mod.shape[0] == 1 and w_in.shape[2] == sum(IN_SIZES)
    assert t % GRID_W == 0 and t % CHUNK == 0 and t_ctx % CHUNK == 0
    assert t % IN_PROJ_TM == 0 and t % MERGE_TM == 0 and t % MOE_TM == 0

    off = [0]
    for s in IN_SIZES:
        off.append(off[-1] + s)
    wi = w_in[0].astype(MXU_DTYPE)
    w_main = jnp.concatenate([wi[:, off[0]:off[4]], wi[:, off[5]:off[9]], wi[:, off[10]:off[12]]], axis=1)
    w_small = jnp.concatenate([wi[:, off[9]:off[10]], wi[:, off[4]:off[5]],
                               jnp.zeros((d, LANES - 2 * GLA_RANK - 4 * HEADS), MXU_DTYPE)], axis=1)
    gate_bias = jnp.zeros((1, LANES), F32).at[0, SMALL_GATE0:SMALL_GATE0 + 4 * HEADS].set(b_mgate[0])
    w2p = jnp.zeros((2, LANES, QK_W), F32)
    w2p = w2p.at[0, 0:GLA_RANK].set(gla_w2[0, 0]).at[1, GLA_RANK:2 * GLA_RANK].set(gla_w2[0, 1])
    b2 = gla_b2[0][:, None, :]
    conv_w = ml_conv[0].reshape(9, 2 * QK_W)
    conv_b = ml_conv_b[0][None, :]
    w_route = jnp.concatenate([w_grp[0], w_rexp[0], jnp.zeros((d, LANES - N_GROUPS - N_EXPERTS), F32)], axis=1)
    b_route = jnp.concatenate([b_grp[0], b_rexp[0], jnp.zeros((LANES - N_GROUPS - N_EXPERTS,), F32)])[None, :]

    cc = jnp.concatenate([c, c_ctx[None, :], jnp.zeros((8 - batch - 1, d), F32)], axis=0)
    mod = _modulation(cc, w_mod[0], b_mod[0][None, :])
    sh1, sc1, gt1, sh2, sc2, gt2 = [mod[:batch, i * d:(i + 1) * d][:, None, :] for i in range(6)]
    sh1c, sc1c = [mod[batch:batch + 1, i * d:(i + 1) * d][:, None, :] for i in range(2)]
    g1 = g_norm1[0][None, :]

    main_c, small_c = _in_proj(ctx.reshape(batch * t_ctx, d), g1, sc1c, sh1c, w_main, w_small,
                               tm=batch * t_ctx, rows_per_batch=batch * t_ctx, last_col_block=COL_V_G)
    qk_c = _conv_silu(main_c, conv_w, conv_b, batch=batch, rows=1, cols=t_ctx)
    ml0, gla0 = _empty_states(batch)
    main_c3, small_c3 = main_c.reshape(batch, t_ctx, main_c.shape[1]), small_c.reshape(batch, t_ctx, LANES)
    ml_states = _mlstm_scan(qk_c.reshape(batch, t_ctx, 2 * QK_W), main_c3, small_c3, gate_bias, ml0, with_output=False)
    gla_state = _gla_scan(main_c3, small_c3, w2p, b2, gla0, with_output=False)

    x2 = x.reshape(batch * t, d)
    main, small = _in_proj(x2, g1, sc1, sh1, w_main, w_small, tm=IN_PROJ_TM, rows_per_batch=t)
    qk = _conv_silu(main, conv_w, conv_b, batch=batch, rows=t // GRID_W, cols=GRID_W)
    main3, small3 = main.reshape(batch, t, MAIN_W), small.reshape(batch, t, LANES)
    hm_f, hm_b = [a.reshape(batch * t, V_W) for a in
                  _mlstm_scan(qk.reshape(batch, t, 2 * QK_W), main3, small3, gate_bias, ml_states, with_output=True)]
    n_steps = t // CHUNK
    later_weights = (w_up[0], w_down[0], w_proj_m[0], w_proj_g[0], w_out[0])
    og_f, og_b, *cast_weights = _gla_scan(
        main3, small3, w2p, b2, gla_state, with_output=True,
        cast_along=tuple(w.reshape(n_steps, -1, w.shape[-1]) for w in later_weights))
    og_f, og_b = og_f.reshape(batch * t, V_W), og_b.reshape(batch * t, V_W)
    w_up_c, w_down_c, w_pm_c, w_pg_c, w_out_c = [c.reshape(w.shape) for c, w in zip(cast_weights, later_weights)]

    x1, h2, wt = _merge(hm_f, hm_b, og_f, og_b, main, x2, ml_norm, gla_norm,
                        w_pm_c, w_pg_c, w_out_c,
                        gt1, g_norm2, sc2, sh2, w_route, b_route, tm=MERGE_TM, rows_per_batch=t)
    out = _moe_final(h2, wt, w_up_c, w_down_c, x1, gt2, g_final[None, :],
                     tm=MOE_TM, rows_per_batch=t)
    return out.reshape(batch, t, d)
```

```python
import functools

import jax
import jax.numpy as jnp
from jax import lax
from jax.experimental import pallas as pl
from jax.experimental.pallas import tpu as pltpu

D_MODEL = 1024
GRID_W = 64
CHUNK = 256
EPS = 1e-6
NEG_BIG = -1e30
HEADS = 4
QK = D_MODEL // 8
DV = D_MODEL // 4
QK_W = HEADS * QK
V_W = HEADS * DV
GLA_RANK = 16
GLA_TAU = 16.0
N_GROUPS = 4
EXPERTS_PER_GROUP = 4
N_EXPERTS = N_GROUPS * EXPERTS_PER_GROUP
D_EXPERT = D_MODEL // 2
IN_SIZES = (QK_W, QK_W, V_W, V_W, 4 * HEADS, QK_W, QK_W, V_W, V_W, 2 * GLA_RANK, D_MODEL, D_MODEL)

LANES = 128
CONV_TILE = 256
MXU_DTYPE = jnp.bfloat16
F32 = jnp.float32
MIB = 1024 * 1024
VMEM_LIMIT = 48 * MIB
VMEM_LIMIT_BIG = 56 * MIB
MOD_TN = 2048
IN_PROJ_TM, IN_PROJ_TN = 1024, 4096
MERGE_TM = 512
MOE_TM = 1024

COL_QK_M, COL_V_M, COL_O_M, COL_QK_G, COL_V_G, COL_R_G, COL_MG_M, COL_MG_G = range(8)
MAIN_W = 8 * D_MODEL
SMALL_GATE0 = 2 * GLA_RANK
GLA_SAFE_DECAY = 80.0
ROUTE_E0 = N_GROUPS


def _dot(a, b):
    return jnp.dot(a.astype(MXU_DTYPE), b.astype(MXU_DTYPE), preferred_element_type=F32)


def _dot_nt(a, b):
    return lax.dot_general(a.astype(MXU_DTYPE), b.astype(MXU_DTYPE), (((1,), (1,)), ((), ())),
                           preferred_element_type=F32)


def _transpose_mxu(a):
    m = a.shape[1]
    eye = (lax.broadcasted_iota(jnp.int32, (m, m), 0) == lax.broadcasted_iota(jnp.int32, (m, m), 1))
    return _dot_nt(eye.astype(MXU_DTYPE), a).astype(MXU_DTYPE)


def _dot_tn_xlu(a, b):
    return lax.dot_general(a.astype(MXU_DTYPE), b.astype(MXU_DTYPE), (((0,), (0,)), ((), ())),
                           preferred_element_type=F32)


def _split3(x):
    hi = x.astype(MXU_DTYPE)
    r1 = x - hi.astype(F32)
    mid = r1.astype(MXU_DTYPE)
    lo = (r1 - mid.astype(F32)).astype(MXU_DTYPE)
    return hi, mid, lo


def _dot_exact_lhs(a01, x):
    hi, mid, lo = _split3(x)
    return _dot(a01, hi) + _dot(a01, mid) + _dot(a01, lo)


def _log_sigmoid(x):
    return jnp.minimum(x, 0.0) - jnp.log(1.0 + jnp.exp(-jnp.abs(x)))


def _silu(x):
    return x * jax.nn.sigmoid(x)


def _rms(x):
    return x * lax.rsqrt(jnp.mean(x * x, axis=-1, keepdims=True) + EPS)


def _cparams(sem, vmem_limit=VMEM_LIMIT):
    return pltpu.CompilerParams(dimension_semantics=sem, vmem_limit_bytes=vmem_limit)


def _mod_kernel(c_ref, w_ref, b_ref, o_ref):
    o_ref[...] = _dot(_silu(c_ref[...]), w_ref[...]) + b_ref[...]


def _modulation(cc, w_mod, b_mod):
    n = w_mod.shape[1]
    tn = MOD_TN
    return pl.pallas_call(
        _mod_kernel,
        grid=(n // tn,),
        in_specs=[pl.BlockSpec((8, D_MODEL), lambda j: (0, 0)),
                  pl.BlockSpec((D_MODEL, tn), lambda j: (0, j)),
                  pl.BlockSpec((1, tn), lambda j: (0, j))],
        out_specs=pl.BlockSpec((8, tn), lambda j: (0, j)),
        out_shape=jax.ShapeDtypeStruct((8, n), F32),
        compiler_params=_cparams(("arbitrary",)),
        name="modulation",
    )(cc, w_mod, b_mod)


def _inproj_kernel(x_ref, g_ref, sc_ref, sh_ref, w_ref, ws_ref, o_ref, os_ref, xn_ref):
    @pl.when(pl.program_id(1) == 0)
    def _():
        xn = _rms(x_ref[...]) * g_ref[...] * (1.0 + sc_ref[...]) + sh_ref[...]
        xn_ref[...] = xn.astype(MXU_DTYPE)
        os_ref[...] = _dot(xn_ref[...], ws_ref[...])

    half = o_ref.shape[1] // 2
    for k in range(2):
        o_ref[:, k * half:(k + 1) * half] = _dot(xn_ref[...], w_ref[:, k * half:(k + 1) * half]).astype(o_ref.dtype)


def _in_proj(x2, g, sc, sh, w_main, w_small, *, tm, rows_per_batch, last_col_block=COL_MG_G):
    m = x2.shape[0]
    tn = IN_PROJ_TN
    col_tiles = last_col_block * D_MODEL // tn + 1
    tiles_per_batch = rows_per_batch // tm
    vec = pl.BlockSpec((None, 1, D_MODEL), lambda i, j: (i // tiles_per_batch, 0, 0))
    return pl.pallas_call(
        _inproj_kernel,
        grid=(m // tm, col_tiles),
        in_specs=[pl.BlockSpec((tm, D_MODEL), lambda i, j: (i, 0)),
                  pl.BlockSpec((1, D_MODEL), lambda i, j: (0, 0)),
                  vec, vec,
                  pl.BlockSpec((D_MODEL, tn), lambda i, j: (0, j)),
                  pl.BlockSpec((D_MODEL, LANES), lambda i, j: (0, 0))],
        out_specs=[pl.BlockSpec((tm, tn), lambda i, j: (i, j)),
                   pl.BlockSpec((tm, LANES), lambda i, j: (i, 0))],
        out_shape=[jax.ShapeDtypeStruct((m, col_tiles * tn), MXU_DTYPE), jax.ShapeDtypeStruct((m, LANES), F32)],
        scratch_shapes=[pltpu.VMEM((tm, D_MODEL), MXU_DTYPE)],
        compiler_params=_cparams(("parallel", "arbitrary"), VMEM_LIMIT_BIG),
        name="in_proj",
    )(x2, g, sc, sh, w_main, w_small)


def _conv_kernel(x_ref, w_ref, b_ref, o_ref, *, rows, cols):
    scale = jnp.where(pl.program_id(1) * CONV_TILE >= QK_W, QK ** -0.5, 1.0).astype(F32)
    w = w_ref[...]
    bias = b_ref[...]
    tpos = lax.broadcasted_iota(jnp.int32, (cols, 1), 0)
    has_left = tpos >= 1
    has_right = tpos < cols - 1

    def row_filters(j):
        tile = x_ref[pl.ds(pl.multiple_of(j * cols, cols), cols), :].astype(F32)
        left = jnp.where(has_left, pltpu.roll(tile, 1, axis=0), 0.0)
        right = jnp.where(has_right, pltpu.roll(tile, cols - 1, axis=0), 0.0)
        return [left * w[3 * i:3 * i + 1, :] + tile * w[3 * i + 1:3 * i + 2, :] + right * w[3 * i + 2:3 * i + 3, :]
                for i in range(3)]

    def finish(j, acc):
        o_ref[pl.ds(pl.multiple_of(j * cols, cols), cols), :] = (_silu(acc + bias) * scale).astype(o_ref.dtype)

    first = row_filters(0)

    def body(j, carry):
        acc, below = carry
        h = row_filters(j)
        finish(j - 1, acc + h[2])
        return below + h[1], h[0]

    acc, _ = lax.fori_loop(1, rows, body, (first[1], first[0]))
    finish(rows - 1, acc)


def _conv_silu(main, conv_w, conv_b, *, batch, rows, cols):
    t = rows * cols
    nct = 2 * QK_W // CONV_TILE
    return pl.pallas_call(
        functools.partial(_conv_kernel, rows=rows, cols=cols),
        grid=(batch, nct),
        in_specs=[pl.BlockSpec((t, CONV_TILE), lambda b, c: (b, c)),
                  pl.BlockSpec((9, CONV_TILE), lambda b, c: (0, c)),
                  pl.BlockSpec((1, CONV_TILE), lambda b, c: (0, c))],
        out_specs=pl.BlockSpec((t, CONV_TILE), lambda b, c: (b, c)),
        out_shape=jax.ShapeDtypeStruct((batch * t, 2 * QK_W), MXU_DTYPE),
        compiler_params=_cparams(("parallel", "arbitrary")),
        name="conv_silu",
    )(main, conv_w, conv_b)


def _chunk_masks(direction):
    row = lax.broadcasted_iota(jnp.int32, (CHUNK, CHUNK), 0)
    col = lax.broadcasted_iota(jnp.int32, (CHUNK, CHUNK), 1)
    seen = (row >= col) if direction == 0 else (row <= col)
    return seen, seen.astype(MXU_DTYPE)


def _scan_specs(batch, nc, col_blocks, widths):
    specs = []
    for direction in (0, 1):
        for cb, wd in zip(col_blocks, widths):
            if direction == 0:
                specs.append(pl.BlockSpec((batch, CHUNK, wd), lambda c, cb=cb: (0, c, cb)))
            else:
                specs.append(pl.BlockSpec((batch, CHUNK, wd), lambda c, cb=cb: (0, nc - 1 - c, cb)))
    return specs


def _scan_out_specs(batch, nc):
    return [pl.BlockSpec((batch, CHUNK, V_W), lambda c: (0, c, 0)),
            pl.BlockSpec((batch, CHUNK, V_W), lambda c: (0, nc - 1 - c, 0))]


def _whole(shape):
    nd = len(shape)
    return pl.BlockSpec(tuple(shape), lambda c: (0,) * nd)


def _lane_tile(x, width):
    return jnp.concatenate([x] * (width // LANES), axis=-1)


def _mlstm_kernel(*refs, with_output, batch):
    (qk_f, v_f, sm_f, qk_b, v_b, sm_b, bias_ref, cn0_ref, m0_ref) = refs[:9]
    if with_output:
        hf_ref, hb_ref, cn_s, m_s = refs[9:]
    else:
        cn_out, m_out, cn_s, m_s = refs[9:]
    step = pl.program_id(0)

    @pl.when(step == 0)
    def _():
        cn_s[...] = cn0_ref[...]
        m_s[...] = m0_ref[...]

    lane = lax.broadcasted_iota(jnp.int32, (1, LANES), 1)
    gate_lane = jnp.logical_and(lane >= SMALL_GATE0, lane < SMALL_GATE0 + 4 * HEADS)
    forget_lane = jnp.logical_and(gate_lane, ((lane - SMALL_GATE0) % (2 * HEADS)) >= HEADS)
    eye = (lax.broadcasted_iota(jnp.int32, (LANES, LANES), 0)
           == lax.broadcasted_iota(jnp.int32, (LANES, LANES), 1)).astype(MXU_DTYPE)
    ones_cols = jnp.ones((CHUNK, LANES), MXU_DTYPE)

    chains = []
    for direction, (qk_ref, v_ref, sm_ref) in enumerate(((qk_f, v_f, sm_f), (qk_b, v_b, sm_b))):
        seen, seen01 = _chunk_masks(direction)
        last = CHUNK - 1 if direction == 0 else 0
        for bi in range(batch):
            g = sm_ref[bi] + bias_ref[...]
            gp = jnp.where(forget_lane, _log_sigmoid(g), g)
            bc = _dot_exact_lhs(seen01, gp)
            hi, mid, lo = _split3(gp)
            gp_t = _dot_nt(eye, hi) + _dot_nt(eye, mid) + _dot_nt(eye, lo)
            hi, mid, lo = _split3(bc)
            bc_t = _dot_nt(eye, hi) + _dot_nt(eye, mid) + _dot_nt(eye, lo)
            bend_row = bc[last:last + 1, :]
            for h in range(HEADS):
                chains.append(dict(h=h, bi=bi, direction=direction, seen=seen, gp=gp, bc=bc, gp_t=gp_t, bc_t=bc_t,
                                   bend_row=bend_row, qk_ref=qk_ref, v_ref=v_ref))

    def load(c):
        h, bi, direction = c["h"], c["bi"], c["direction"]
        c["ji"] = SMALL_GATE0 + direction * 2 * HEADS + h
        c["jf"] = c["ji"] + HEADS
        c["q"] = c["qk_ref"][bi, :, h * QK:(h + 1) * QK].astype(F32)
        c["k"] = c["qk_ref"][bi, :, QK_W + h * QK:QK_W + (h + 1) * QK].astype(F32)
        c["v_ext"] = jnp.concatenate([c["v_ref"][bi, :, h * DV:(h + 1) * DV].astype(MXU_DTYPE), ones_cols], axis=-1)
        c["cn_old"] = cn_s[bi, direction, h]
        c["m_old"] = m_s[bi, direction, h]
        if with_output:
            c["qk"] = _dot_nt(c["q"], c["k"])
            c["qc"] = _dot(c["q"], c["cn_old"])

    def gates(c):
        b_end = jnp.broadcast_to(c["bend_row"][:, c["jf"]:c["jf"] + 1], (1, LANES))
        i_col = jnp.broadcast_to(c["gp"][:, c["ji"]:c["ji"] + 1], (CHUNK, LANES))
        c["b_col"] = jnp.broadcast_to(c["bc"][:, c["jf"]:c["jf"] + 1], (CHUNK, LANES))
        log_w = b_end - c["b_col"] + i_col
        c["m_new"] = jnp.maximum(b_end + c["m_old"], jnp.max(log_w, axis=0, keepdims=True))
        c["kw"] = (c["k"] * jnp.exp(log_w - c["m_new"])).astype(MXU_DTYPE)
        c["decay"] = jnp.exp(b_end + c["m_old"] - c["m_new"])
        c["kw_t"] = _transpose_mxu(c["kw"])

    def weights(c):
        if with_output:
            i_row = c["gp_t"][c["ji"]:c["ji"] + 1, :]
            b_row = c["bc_t"][c["jf"]:c["jf"] + 1, :]
            log_d = jnp.where(c["seen"], _lane_tile(c["b_col"], CHUNK) - b_row + i_row, -jnp.inf)
            log_inter = c["b_col"] + c["m_old"]
            c["m_t"] = jnp.maximum(log_inter, jnp.max(log_d, axis=-1, keepdims=True))
            c["s"] = (c["qk"] * jnp.exp(log_d - _lane_tile(c["m_t"], CHUNK))).astype(MXU_DTYPE)
            c["w_inter"] = jnp.exp(log_inter - c["m_t"])

    def apply(c):
        idx = (c["bi"], c["direction"], c["h"])
        if with_output:
            c["sv"] = _dot(c["s"], c["v_ext"])
        cn_s[idx] = _lane_tile(c["decay"], DV + LANES) * c["cn_old"] + _dot(c["kw_t"], c["v_ext"])
        m_s[idx] = c["m_new"]

    def emit(c):
        if with_output:
            sv, qc, w_inter, h = c["sv"], c["qc"], c["w_inter"], c["h"]
            num = sv[:, :DV] + _lane_tile(w_inter, DV) * qc[:, :DV]
            den = jnp.abs(sv[:, DV:] + w_inter * qc[:, DV:])
            (hf_ref if c["direction"] == 0 else hb_ref)[c["bi"], :, h * DV:(h + 1) * DV] = (
                num / _lane_tile(jnp.maximum(den, jnp.exp(-c["m_t"])), DV))

    stages = (load, gates, weights, apply, emit)
    for tick in range(len(chains) + len(stages) - 1):
        for depth, stage in enumerate(stages):
            if 0 <= tick - depth < len(chains):
                stage(chains[tick - depth])

    if not with_output:
        @pl.when(step == pl.num_programs(0) - 1)
        def _():
            cn_out[...] = cn_s[...]
            m_out[...] = m_s[...]


def _mlstm_scan(qk, main, small, gate_bias, states, *, with_output):
    cn0, m0 = states
    batch, t, _ = qk.shape
    nc = t // CHUNK
    in_specs = _scan_specs(batch, nc, (0, COL_V_M, 0), (2 * QK_W, V_W, LANES))
    in_specs += [_whole(gate_bias.shape), _whole(cn0.shape), _whole(m0.shape)]
    if with_output:
        out_specs = _scan_out_specs(batch, nc)
        out_shape = [jax.ShapeDtypeStruct((batch, t, V_W), F32)] * 2
    else:
        out_specs = [_whole(cn0.shape), _whole(m0.shape)]
        out_shape = [jax.ShapeDtypeStruct(s.shape, F32) for s in states]
    return pl.pallas_call(
        functools.partial(_mlstm_kernel, with_output=with_output, batch=batch),
        grid=(nc,),
        in_specs=in_specs,
        out_specs=out_specs,
        out_shape=out_shape,
        scratch_shapes=[pltpu.VMEM(cn0.shape, F32), pltpu.VMEM(m0.shape, F32)],
        compiler_params=_cparams(("arbitrary",)),
        name="mlstm_scan_out" if with_output else "mlstm_scan_state",
    )(qk, main, small, qk, main, small, gate_bias, cn0, m0)


def _gla_exact_intra(q, k, v, b, direction):
    row_id = lax.broadcasted_iota(jnp.int32, (CHUNK, 1), 0)

    def row(t, acc):
        pick = row_id == t
        b_t = jnp.sum(jnp.where(pick, b, 0.0), axis=0, keepdims=True)
        q_t = jnp.sum(jnp.where(pick, q, 0.0), axis=0, keepdims=True)
        ok = (row_id <= t) if direction == 0 else (row_id >= t)
        e = jnp.exp(jnp.where(ok, b_t - b, -jnp.inf))
        sc = jnp.sum(q_t * k * e, axis=-1, keepdims=True)
        o_t = jnp.sum(sc * v, axis=0, keepdims=True)
        return jnp.where(pick, o_t, acc)

    return lax.fori_loop(0, CHUNK, row, jnp.zeros((CHUNK, DV), F32))


def _gla_kernel(*refs, with_output, batch, n_cast=0):
    (qk_f, v_f, sm_f, qk_b, v_b, sm_b, w2_ref, b2_ref, s0_ref) = refs[:9]
    cast_in, rest = refs[9:9 + n_cast], refs[9 + n_cast:]
    if with_output:
        of_ref, ob_ref = rest[:2]
        cast_out = rest[2:2 + n_cast]
        s_s, b_s, inter_s = rest[2 + n_cast:]
    else:
        s_out, s_s = rest
    step = pl.program_id(0)

    for src, dst in zip(cast_in, cast_out if with_output else ()):
        dst[...] = src[...].astype(dst.dtype)

    @pl.when(step == 0)
    def _():
        s_s[...] = s0_ref[...]

    worst_decay = []
    for direction, (qk_ref, v_ref, sm_ref) in enumerate(((qk_f, v_f, sm_f), (qk_b, v_b, sm_b))):
        seen, seen01 = _chunk_masks(direction)
        last = CHUNK - 1 if direction == 0 else 0
        for bi in range(batch):
            z = _dot(sm_ref[bi], w2_ref[direction]) + b2_ref[direction]
            log_a = _log_sigmoid(z) * (1.0 / GLA_TAU)
            b_all = _dot_exact_lhs(seen01, log_a)
            outs, inters = [], []
            for h in range(HEADS):
                q = qk_ref[bi, :, h * QK:(h + 1) * QK].astype(F32) * (QK ** -0.5)
                k = qk_ref[bi, :, QK_W + h * QK:QK_W + (h + 1) * QK].astype(F32)
                v = v_ref[bi, :, h * DV:(h + 1) * DV]
                b = b_all[:, h * QK:(h + 1) * QK]
                b_end = b[last:last + 1, :]
                st_old = s_s[bi, direction, h]
                k_dec = k * jnp.exp(b_end - b)
                s_s[bi, direction, h] = st_old * jnp.exp(b_end) + _dot_tn_xlu(v, k_dec)
                if with_output:
                    q_dec = q * jnp.exp(b)
                    inter = _dot_nt(q_dec, st_old)
                    scores = jnp.where(seen, _dot_nt(q_dec, k * jnp.exp(-b)), 0.0)
                    outs.append(_dot(scores, v) + inter)
                    inters.append(inter)
            if with_output:
                (of_ref if direction == 0 else ob_ref)[bi] = jnp.concatenate(outs, axis=-1)
                b_s[bi, direction] = b_all
                inter_s[bi, direction] = jnp.concatenate(inters, axis=-1)
                worst_decay.append(jnp.max(-b_all[last:last + 1, :]))

    if with_output:
        @pl.when(functools.reduce(jnp.maximum, worst_decay) > GLA_SAFE_DECAY)
        def _():
            for direction, (qk_ref, v_ref, o_ref) in enumerate(((qk_f, v_f, of_ref), (qk_b, v_b, ob_ref))):
                last = CHUNK - 1 if direction == 0 else 0
                for bi in range(batch):
                    for h in range(HEADS):
                        b = b_s[bi, direction, :, h * QK:(h + 1) * QK]

                        @pl.when(jnp.max(-b[last:last + 1, :]) > GLA_SAFE_DECAY)
                        def _():
                            q = qk_ref[bi, :, h * QK:(h + 1) * QK].astype(F32) * (QK ** -0.5)
                            k = qk_ref[bi, :, QK_W + h * QK:QK_W + (h + 1) * QK].astype(F32)
                            v = v_ref[bi, :, h * DV:(h + 1) * DV].astype(F32)
                            o_ref[bi, :, h * DV:(h + 1) * DV] = (
                                inter_s[bi, direction, :, h * DV:(h + 1) * DV]
                                + _gla_exact_intra(q, k, v, b, direction))
    else:
        @pl.when(step == pl.num_programs(0) - 1)
        def _():
            s_out[...] = s_s[...]


def _gla_scan(main, small, w2p, b2, s0, *, with_output, cast_along=()):
    batch, t, _ = main.shape
    nc = t // CHUNK
    in_specs = _scan_specs(batch, nc, (COL_QK_G, COL_V_G, 0), (2 * QK_W, V_W, LANES))
    in_specs += [_whole(w2p.shape), _whole(b2.shape), _whole(s0.shape)]
    slab = lambda a: pl.BlockSpec((None,) + a.shape[1:], lambda c: (c, 0, 0))
    in_specs += [slab(a) for a in cast_along]
    scratch = [pltpu.VMEM(s0.shape, F32)]
    if with_output:
        assert all(a.shape[0] == nc for a in cast_along)
        out_specs = _scan_out_specs(batch, nc) + [slab(a) for a in cast_along]
        out_shape = ([jax.ShapeDtypeStruct((batch, t, V_W), F32)] * 2
                     + [jax.ShapeDtypeStruct(a.shape, MXU_DTYPE) for a in cast_along])
        scratch += [pltpu.VMEM((batch, 2, CHUNK, QK_W), F32), pltpu.VMEM((batch, 2, CHUNK, V_W), F32)]
    else:
        out_specs = _whole(s0.shape)
        out_shape = jax.ShapeDtypeStruct(s0.shape, F32)
    return pl.pallas_call(
        functools.partial(_gla_kernel, with_output=with_output, batch=batch, n_cast=len(cast_along)),
        grid=(nc,),
        in_specs=in_specs,
        out_specs=out_specs,
        out_shape=out_shape,
        scratch_shapes=scratch,
        compiler_params=_cparams(("arbitrary",)),
        name="gla_scan_out" if with_output else "gla_scan_state",
    )(main, main, small, main, main, small, w2p, b2, s0, *cast_along)


def _head_rms(a):
    return jnp.concatenate([_rms(a[:, h * DV:(h + 1) * DV]) for h in range(HEADS)], axis=-1)


def _merge_kernel(hmf, hmb, ogf, ogb, om, rg, mgm, mgg, x_ref, mln, gln, wpm, wpg, wo, gt1, g2, sc2, sh2,
                  wr, br, x1_ref, h2_ref, wt_ref):
    tm = x_ref.shape[0]
    parts = [slice(i * tm // MERGE_SPLIT, (i + 1) * tm // MERGE_SPLIT) for i in range(MERGE_SPLIT)]
    y_m = [_head_rms(hmf[r, :] + hmb[r, :]) * mln[...] * jax.nn.sigmoid(om[r, :].astype(F32)) for r in parts]
    p_m = [_dot(a, wpm[...]) for a in y_m]
    y_g = [_head_rms(ogf[r, :] + ogb[r, :]) * gln[...] * _silu(rg[r, :].astype(F32)) for r in parts]
    p_g = [_dot(a, wpg[...]) for a in y_g]
    y = [jax.nn.sigmoid(mgm[r, :].astype(F32)) * a + jax.nn.sigmoid(mgg[r, :].astype(F32)) * b
         for r, a, b in zip(parts, p_m, p_g)]
    mix = [_dot(a, wo[...]) for a in y]
    h2_parts = []
    for r, a in zip(parts, mix):
        x1 = x_ref[r, :] + gt1[...] * a
        x1_ref[r, :] = x1
        h2_parts.append(_rms(x1) * g2[...] * (1.0 + sc2[...]) + sh2[...])
    h2 = jnp.concatenate(h2_parts, axis=0)
    h2_ref[...] = h2.astype(MXU_DTYPE)

    hh, hm_, _ = _split3(h2)
    wh, wm_, _ = _split3(wr[...])
    lg = _dot(hh, wh) + _dot(hh, wm_) + _dot(hm_, wh) + br[...]
    lane = lax.broadcasted_iota(jnp.int32, lg.shape, 1)

    def masked_softmax(mask):
        l = jnp.where(mask, lg, -jnp.inf)
        e = jnp.exp(l - jnp.max(l, axis=-1, keepdims=True))
        return e / jnp.sum(e, axis=-1, keepdims=True)

    def top1(p, mask):
        pm = jnp.where(mask, p, -1.0)
        best = jnp.max(pm, axis=-1, keepdims=True)
        idx = jnp.min(jnp.where(jnp.logical_and(mask, pm == best), lane, LANES), axis=-1, keepdims=True)
        return best, idx

    gmask = lane < N_GROUPS
    grp_p, grp = top1(masked_softmax(gmask), gmask)
    e_lo = ROUTE_E0 + grp * EXPERTS_PER_GROUP
    emask = jnp.logical_and(lane >= e_lo, lane < e_lo + EXPERTS_PER_GROUP)
    p_in = masked_softmax(emask)
    p1, i1 = top1(p_in, emask)
    p2, i2 = top1(p_in, jnp.logical_and(emask, lane != i1))
    tot = p1 + p2
    wt_ref[...] = (jnp.where(lane == i1, grp_p * p1 / tot, 0.0)
                   + jnp.where(lane == i2, grp_p * p2 / tot, 0.0)
                   + jnp.where(lane == grp, 1.0, 0.0))


def _merge(hmf, hmb, ogf, ogb, main, x2, mln, gln, wpm, wpg, wo, gt1, g2, sc2, sh2, wr, br, *, tm, rows_per_batch):
    m = x2.shape[0]
    tpb = rows_per_batch // tm
    rowblk = pl.BlockSpec((tm, D_MODEL), lambda i: (i, 0))
    colblk = lambda cb: pl.BlockSpec((tm, D_MODEL), lambda i, cb=cb: (i, cb))
    vec = pl.BlockSpec((1, D_MODEL), lambda i: (0, 0))
    bvec = pl.BlockSpec((None, 1, D_MODEL), lambda i: (i // tpb, 0, 0))
    wmat = pl.BlockSpec((D_MODEL, D_MODEL), lambda i: (0, 0))
    return pl.pallas_call(
        _merge_kernel,
        grid=(m // tm,),
        in_specs=[rowblk, rowblk, rowblk, rowblk, colblk(COL_O_M), colblk(COL_R_G), colblk(COL_MG_M),
                  colblk(COL_MG_G), rowblk, vec, vec, wmat, wmat, wmat, bvec, vec, bvec, bvec,
                  pl.BlockSpec((D_MODEL, LANES), lambda i: (0, 0)), pl.BlockSpec((1, LANES), lambda i: (0, 0))],
        out_specs=[rowblk, rowblk, pl.BlockSpec((tm, LANES), lambda i: (i, 0))],
        out_shape=[jax.ShapeDtypeStruct((m, D_MODEL), F32), jax.ShapeDtypeStruct((m, D_MODEL), MXU_DTYPE),
                   jax.ShapeDtypeStruct((m, LANES), F32)],
        compiler_params=_cparams(("parallel",), VMEM_LIMIT_BIG),
        name="merge_route",
    )(hmf, hmb, ogf, ogb, main, main, main, main, x2, mln, gln, wpm, wpg, wo, gt1, g2, sc2, sh2, wr, br)


MERGE_SPLIT = 2
MOE_BLK = 64
MOE_COMMON_BLKS = (3, 4, 5, 6)
MOE_LOOP_BLKS = 4
MOE_EXPERTS_PER_STEP = 2
MOE_W_SLOTS = 3


def _moe_kernel(h2_ref, wt_ref, wup_hbm, wdn_hbm, x1_hbm, gt2_ref, gf_ref, o_ref,
                xs_ref, ys_ref, ws_ref, dest_ref, blk_ref, wup_buf, wdn_buf, x1_buf, w_sem, x_sem):
    step = pl.program_id(1)
    n_steps = pl.num_programs(1)
    tm = h2_ref.shape[0]
    n_rows = xs_ref.shape[0]
    g_step = pl.program_id(0) * n_steps + step
    g_total = pl.num_programs(0) * n_steps
    ahead = MOE_W_SLOTS - 1

    def weight_copies(g):
        first = lax.rem(g, n_steps) * MOE_EXPERTS_PER_STEP
        slot = lax.rem(g, MOE_W_SLOTS)
        return (pltpu.make_async_copy(wup_hbm.at[pl.ds(first, MOE_EXPERTS_PER_STEP)], wup_buf.at[slot], w_sem.at[slot, 0]),
                pltpu.make_async_copy(wdn_hbm.at[pl.ds(first, MOE_EXPERTS_PER_STEP)], wdn_buf.at[slot], w_sem.at[slot, 1]))

    x1_copy = pltpu.make_async_copy(x1_hbm.at[pl.ds(pl.multiple_of(pl.program_id(0) * tm, tm), tm)], x1_buf, x_sem.at[0])

    @pl.when(g_step == 0)
    def _():
        for g in range(ahead):
            for cp in weight_copies(jnp.int32(g)):
                cp.start()

    @pl.when(g_step + ahead < g_total)
    def _():
        for cp in weight_copies(g_step + ahead):
            cp.start()

    @pl.when(step == 0)
    def _():
        x1_copy.start()

    for cp in weight_copies(g_step):
        cp.wait()
    slot = lax.rem(g_step, MOE_W_SLOTS)

    @pl.when(step == 0)
    def _():
        r = wt_ref[...]
        lane = lax.broadcasted_iota(jnp.int32, (tm, LANES), 1)
        lane1 = lax.broadcasted_iota(jnp.int32, (1, LANES), 1)
        gm = jnp.where(lane < N_GROUPS, r, 0.0)
        earlier = (lax.broadcasted_iota(jnp.int32, (tm, tm), 1)
                   < lax.broadcasted_iota(jnp.int32, (tm, tm), 0)).astype(MXU_DTYPE)
        before = _dot(earlier, gm)
        padded = jnp.floor((jnp.sum(gm, axis=0, keepdims=True) + (MOE_BLK - 1)) * (1.0 / MOE_BLK)) * MOE_BLK
        start = jnp.zeros((1, LANES), F32)
        run = jnp.zeros((1, 1), F32)
        for g in range(N_GROUPS):
            size = jnp.sum(jnp.where(lane1 == g, padded, 0.0), axis=-1, keepdims=True)
            start = jnp.where(lane1 == g, run, start)
            blk_ref[g] = (jnp.sum(run) * (1.0 / MOE_BLK)).astype(jnp.int32)
            blk_ref[N_GROUPS + g] = (jnp.sum(size) * (1.0 / MOE_BLK)).astype(jnp.int32)
            run = run + size
        dest = jnp.sum(gm * (start + before), axis=-1, keepdims=True)
        dest_ref[...] = jnp.broadcast_to(dest, (tm, LANES))
        dest_row = dest_ref[...].T[0:1, :].astype(jnp.int32)
        perm = (lax.broadcasted_iota(jnp.int32, (n_rows, tm), 0) == dest_row).astype(MXU_DTYPE)
        xs_ref[...] = _dot(perm, h2_ref[...]).astype(MXU_DTYPE)
        moved = _dot(perm, jnp.concatenate(_split3(r), axis=-1))
        ws_ref[...] = moved[:, :LANES] + moved[:, LANES:2 * LANES] + moved[:, 2 * LANES:]
        ys_ref[...] = jnp.zeros_like(ys_ref)

    group = step // (EXPERTS_PER_GROUP // MOE_EXPERTS_PER_STEP)
    first_blk = blk_ref[group]
    n_blk = blk_ref[N_GROUPS + group]

    def expert_on(r0, rows):
        x = xs_ref[pl.ds(r0, rows), :]
        ws = ws_ref[pl.ds(r0, rows), :]
        lane_b = lax.broadcasted_iota(jnp.int32, (rows, LANES), 1)
        experts = range(MOE_EXPERTS_PER_STEP)
        gus = [_dot(x, wup_buf[slot, k]) for k in experts]
        hiddens = [(_silu(gu[:, :D_EXPERT]) * gu[:, D_EXPERT:]).astype(MXU_DTYPE) for gu in gus]
        w_cols = [jnp.sum(jnp.where(lane_b == ROUTE_E0 + step * MOE_EXPERTS_PER_STEP + k, ws, 0.0),
                          axis=-1, keepdims=True) for k in experts]
        ys = [_dot(hiddens[k], wdn_buf[slot, k]) for k in experts]
        ys_ref[pl.ds(r0, rows), :] += sum(y * w for y, w in zip(ys, w_cols))

    for k in MOE_COMMON_BLKS:
        @pl.when(n_blk == k)
        def _():
            expert_on(pl.multiple_of(first_blk * MOE_BLK, MOE_BLK), k * MOE_BLK)

    @pl.when(functools.reduce(jnp.logical_and, [n_blk != k for k in MOE_COMMON_BLKS]))
    def _():
        n_big = n_blk // MOE_LOOP_BLKS

        def big(j, carry):
            expert_on(pl.multiple_of((first_blk + j * MOE_LOOP_BLKS) * MOE_BLK, MOE_BLK), MOE_LOOP_BLKS * MOE_BLK)
            return carry

        def single(j, carry):
            expert_on(pl.multiple_of((first_blk + j) * MOE_BLK, MOE_BLK), MOE_BLK)
            return carry

        lax.fori_loop(0, n_big, big, 0)
        lax.fori_loop(n_big * MOE_LOOP_BLKS, n_blk, single, 0)

    @pl.when(step == pl.num_programs(1) - 1)
    def _():
        dest = dest_ref[...][:, :1].astype(jnp.int32)
        unperm = (lax.broadcasted_iota(jnp.int32, (tm, n_rows), 1) == dest).astype(MXU_DTYPE)
        y = _dot(unperm, ys_ref[...])
        x1_copy.wait()
        o_ref[...] = _rms(x1_buf[...] + gt2_ref[...] * y) * gf_ref[...]


def _moe_final(h2, wt, w_up, w_down, x1, gt2, g_final, *, tm, rows_per_batch):
    m = h2.shape[0]
    tpb = rows_per_batch // tm
    n_rows = tm + N_GROUPS * MOE_BLK
    rowblk = pl.BlockSpec((tm, D_MODEL), lambda i, e: (i, 0))
    return pl.pallas_call(
        _moe_kernel,
        grid=(m // tm, N_EXPERTS // MOE_EXPERTS_PER_STEP),
        in_specs=[rowblk,
                  pl.BlockSpec((tm, LANES), lambda i, e: (i, 0)),
                  pl.BlockSpec(memory_space=pl.ANY),
                  pl.BlockSpec(memory_space=pl.ANY),
                  pl.BlockSpec(memory_space=pl.ANY),
                  pl.BlockSpec((None, 1, D_MODEL), lambda i, e: (i // tpb, 0, 0)),
                  pl.BlockSpec((1, D_MODEL), lambda i, e: (0, 0))],
        out_specs=rowblk,
        out_shape=jax.ShapeDtypeStruct((m, D_MODEL), F32),
        scratch_shapes=[pltpu.VMEM((n_rows, D_MODEL), MXU_DTYPE), pltpu.VMEM((n_rows, D_MODEL), F32),
                        pltpu.VMEM((n_rows, LANES), F32), pltpu.VMEM((tm, LANES), F32),
                        pltpu.SMEM((2 * N_GROUPS,), jnp.int32),
                        pltpu.VMEM((MOE_W_SLOTS, MOE_EXPERTS_PER_STEP, D_MODEL, 2 * D_EXPERT), MXU_DTYPE),
                        pltpu.VMEM((MOE_W_SLOTS, MOE_EXPERTS_PER_STEP, D_EXPERT, D_MODEL), MXU_DTYPE),
                        pltpu.VMEM((tm, D_MODEL), F32),
                        pltpu.SemaphoreType.DMA((MOE_W_SLOTS, 2)), pltpu.SemaphoreType.DMA((1,))],
        compiler_params=_cparams(("arbitrary", "arbitrary"), VMEM_LIMIT_BIG),
        name="moe_final",
    )(h2, wt, w_up, w_down, x1, gt2, g_final)


def _empty_states(batch):
    ml = (jnp.zeros((batch, 2, HEADS, QK, DV + LANES), F32),
          jnp.full((batch, 2, HEADS, 1, LANES), NEG_BIG, F32))
    gla = jnp.zeros((batch, 2, HEADS, DV, QK), F32)
    return ml, gla


def kernel(x, c, ctx, c_ctx, w_mod, b_mod, g_norm1, w_in, ml_conv, ml_conv_b, b_mgate, ml_norm, gla_w2, gla_b2,
           gla_norm, w_proj_m, w_proj_g, w_out, g_norm2, w_grp, b_grp, w_rexp, b_rexp, w_up, w_down, g_final):
    batch, t, d = x.shape
    t_ctx = ctx.shape[1]
    assert d == D_MODEL and w_mod.shape[0] == 1 and w_in.shape[2] == sum(IN_SIZES)
    assert t % GRID_W == 0 and t % CHUNK == 0 and t_ctx % CHUNK == 0
    assert t % IN_PROJ_TM == 0 and t % MERGE_TM == 0 and t % MOE_TM == 0

    off = [0]
    for s in IN_SIZES:
        off.append(off[-1] + s)
    wi = w_in[0].astype(MXU_DTYPE)
    w_main = jnp.concatenate([wi[:, off[0]:off[4]], wi[:, off[5]:off[9]], wi[:, off[10]:off[12]]], axis=1)
    w_small = jnp.concatenate([wi[:, off[9]:off[10]], wi[:, off[4]:off[5]],
                               jnp.zeros((d, LANES - 2 * GLA_RANK - 4 * HEADS), MXU_DTYPE)], axis=1)
    gate_bias = jnp.zeros((1, LANES), F32).at[0, SMALL_GATE0:SMALL_GATE0 + 4 * HEADS].set(b_mgate[0])
    w2p = jnp.zeros((2, LANES, QK_W), F32)
    w2p = w2p.at[0, 0:GLA_RANK].set(gla_w2[0, 0]).at[1, GLA_RANK:2 * GLA_RANK].set(gla_w2[0, 1])
    b2 = gla_b2[0][:, None, :]
    conv_w = ml_conv[0].reshape(9, 2 * QK_W)
    conv_b = ml_conv_b[0][None, :]
    w_route = jnp.concatenate([w_grp[0], w_rexp[0], jnp.zeros((d, LANES - N_GROUPS - N_EXPERTS), F32)], axis=1)
    b_route = jnp.concatenate([b_grp[0], b_rexp[0], jnp.zeros((LANES - N_GROUPS - N_EXPERTS,), F32)])[None, :]

    cc = jnp.concatenate([c, c_ctx[None, :], jnp.zeros((8 - batch - 1, d), F32)], axis=0)
    mod = _modulation(cc, w_mod[0], b_mod[0][None, :])
    sh1, sc1, gt1, sh2, sc2, gt2 = [mod[:batch, i * d:(i + 1) * d][:, None, :] for i in range(6)]
    sh1c, sc1c = [mod[batch:batch + 1, i * d:(i + 1) * d][:, None, :] for i in range(2)]
    g1 = g_norm1[0][None, :]

    main_c, small_c = _in_proj(ctx.reshape(batch * t_ctx, d), g1, sc1c, sh1c, w_main, w_small,
                               tm=batch * t_ctx, rows_per_batch=batch * t_ctx, last_col_block=COL_V_G)
    qk_c = _conv_silu(main_c, conv_w, conv_b, batch=batch, rows=1, cols=t_ctx)
    ml0, gla0 = _empty_states(batch)
    main_c3, small_c3 = main_c.reshape(batch, t_ctx, main_c.shape[1]), small_c.reshape(batch, t_ctx, LANES)
    ml_states = _mlstm_scan(qk_c.reshape(batch, t_ctx, 2 * QK_W), main_c3, small_c3, gate_bias, ml0, with_output=False)
    gla_state = _gla_scan(main_c3, small_c3, w2p, b2, gla0, with_output=False)

    x2 = x.reshape(batch * t, d)
    main, small = _in_proj(x2, g1, sc1, sh1, w_main, w_small, tm=IN_PROJ_TM, rows_per_batch=t)
    qk = _conv_silu(main, conv_w, conv_b, batch=batch, rows=t // GRID_W, cols=GRID_W)
    main3, small3 = main.reshape(batch, t, MAIN_W), small.reshape(batch, t, LANES)
    hm_f, hm_b = [a.reshape(batch * t, V_W) for a in
                  _mlstm_scan(qk.reshape(batch, t, 2 * QK_W), main3, small3, gate_bias, ml_states, with_output=True)]
    n_steps = t // CHUNK
    later_weights = (w_up[0], w_down[0], w_proj_m[0], w_proj_g[0], w_out[0])
    og_f, og_b, *cast_weights = _gla_scan(
        main3, small3, w2p, b2, gla_state, with_output=True,
        cast_along=tuple(w.reshape(n_steps, -1, w.shape[-1]) for w in later_weights))
    og_f, og_b = og_f.reshape(batch * t, V_W), og_b.reshape(batch * t, V_W)
    w_up_c, w_down_c, w_pm_c, w_pg_c, w_out_c = [c.reshape(w.shape) for c, w in zip(cast_weights, later_weights)]

    x1, h2, wt = _merge(hm_f, hm_b, og_f, og_b, main, x2, ml_norm, gla_norm,
                        w_pm_c, w_pg_c, w_out_c,
                        gt1, g_norm2, sc2, sh2, w_route, b_route, tm=MERGE_TM, rows_per_batch=t)
    out = _moe_final(h2, wt, w_up_c, w_down_c, x1, gt2, g_final[None, :],
                     tm=MOE_TM, rows_per_batch=t)
    return out.reshape(batch, t, d)
```

```python
import functools

import jax
import jax.numpy as jnp
from jax import lax
from jax.experimental import pallas as pl
from jax.experimental.pallas import tpu as pltpu

D_MODEL = 1024
GRID_W = 64
CHUNK = 256
EPS = 1e-6
NEG_BIG = -1e30
HEADS = 4
QK = D_MODEL // 8
DV = D_MODEL // 4
QK_W = HEADS * QK
V_W = HEADS * DV
GLA_RANK = 16
GLA_TAU = 16.0
N_GROUPS = 4
EXPERTS_PER_GROUP = 4
N_EXPERTS = N_GROUPS * EXPERTS_PER_GROUP
D_EXPERT = D_MODEL // 2
IN_SIZES = (QK_W, QK_W, V_W, V_W, 4 * HEADS, QK_W, QK_W, V_W, V_W, 2 * GLA_RANK, D_MODEL, D_MODEL)

LANES = 128
CONV_TILE = 256
MXU_DTYPE = jnp.bfloat16
F32 = jnp.float32
MIB = 1024 * 1024
VMEM_LIMIT = 48 * MIB
VMEM_LIMIT_BIG = 56 * MIB
MOD_TN = 2048
IN_PROJ_TM, IN_PROJ_TN = 1024, 4096
MERGE_TM = 512
MOE_TM = 1024

COL_QK_M, COL_V_M, COL_O_M, COL_QK_G, COL_V_G, COL_R_G, COL_MG_M, COL_MG_G = range(8)
MAIN_W = 8 * D_MODEL
SMALL_GATE0 = 2 * GLA_RANK
GLA_SAFE_DECAY = 80.0
ROUTE_E0 = N_GROUPS


def _dot(a, b):
    return jnp.dot(a.astype(MXU_DTYPE), b.astype(MXU_DTYPE), preferred_element_type=F32)


def _dot_nt(a, b):
    return lax.dot_general(a.astype(MXU_DTYPE), b.astype(MXU_DTYPE), (((1,), (1,)), ((), ())),
                           preferred_element_type=F32)


def _transpose_mxu(a):
    m = a.shape[1]
    eye = (lax.broadcasted_iota(jnp.int32, (m, m), 0) == lax.broadcasted_iota(jnp.int32, (m, m), 1))
    return _dot_nt(eye.astype(MXU_DTYPE), a).astype(MXU_DTYPE)


def _dot_tn_xlu(a, b):
    return lax.dot_general(a.astype(MXU_DTYPE), b.astype(MXU_DTYPE), (((0,), (0,)), ((), ())),
                           preferred_element_type=F32)


def _split3(x):
    hi = x.astype(MXU_DTYPE)
    r1 = x - hi.astype(F32)
    mid = r1.astype(MXU_DTYPE)
    lo = (r1 - mid.astype(F32)).astype(MXU_DTYPE)
    return hi, mid, lo


def _dot_exact_lhs(a01, x):
    hi, mid, lo = _split3(x)
    return _dot(a01, hi) + _dot(a01, mid) + _dot(a01, lo)


def _log_sigmoid(x):
    return jnp.minimum(x, 0.0) - jnp.log(1.0 + jnp.exp(-jnp.abs(x)))


def _silu(x):
    return x * jax.nn.sigmoid(x)


def _rms(x):
    return x * lax.rsqrt(jnp.mean(x * x, axis=-1, keepdims=True) + EPS)


def _cparams(sem, vmem_limit=VMEM_LIMIT):
    return pltpu.CompilerParams(dimension_semantics=sem, vmem_limit_bytes=vmem_limit)


def _mod_kernel(c_ref, w_ref, b_ref, o_ref):
    o_ref[...] = _dot(_silu(c_ref[...]), w_ref[...]) + b_ref[...]


def _modulation(cc, w_mod, b_mod):
    n = w_mod.shape[1]
    tn = MOD_TN
    return pl.pallas_call(
        _mod_kernel,
        grid=(n // tn,),
        in_specs=[pl.BlockSpec((8, D_MODEL), lambda j: (0, 0)),
                  pl.BlockSpec((D_MODEL, tn), lambda j: (0, j)),
                  pl.BlockSpec((1, tn), lambda j: (0, j))],
        out_specs=pl.BlockSpec((8, tn), lambda j: (0, j)),
        out_shape=jax.ShapeDtypeStruct((8, n), F32),
        compiler_params=_cparams(("arbitrary",)),
        name="modulation",
    )(cc, w_mod, b_mod)


def _inproj_kernel(x_ref, g_ref, sc_ref, sh_ref, w_ref, ws_ref, o_ref, os_ref, xn_ref):
    @pl.when(pl.program_id(1) == 0)
    def _():
        xn = _rms(x_ref[...]) * g_ref[...] * (1.0 + sc_ref[...]) + sh_ref[...]
        xn_ref[...] = xn.astype(MXU_DTYPE)
        os_ref[...] = _dot(xn_ref[...], ws_ref[...])

    half = o_ref.shape[1] // 2
    for k in range(2):
        o_ref[:, k * half:(k + 1) * half] = _dot(xn_ref[...], w_ref[:, k * half:(k + 1) * half]).astype(o_ref.dtype)


def _in_proj(x2, g, sc, sh, w_main, w_small, *, tm, rows_per_batch, last_col_block=COL_MG_G):
    m = x2.shape[0]
    tn = IN_PROJ_TN
    col_tiles = last_col_block * D_MODEL // tn + 1
    tiles_per_batch = rows_per_batch // tm
    vec = pl.BlockSpec((None, 1, D_MODEL), lambda i, j: (i // tiles_per_batch, 0, 0))
    return pl.pallas_call(
        _inproj_kernel,
        grid=(m // tm, col_tiles),
        in_specs=[pl.BlockSpec((tm, D_MODEL), lambda i, j: (i, 0)),
                  pl.BlockSpec((1, D_MODEL), lambda i, j: (0, 0)),
                  vec, vec,
                  pl.BlockSpec((D_MODEL, tn), lambda i, j: (0, j)),
                  pl.BlockSpec((D_MODEL, LANES), lambda i, j: (0, 0))],
        out_specs=[pl.BlockSpec((tm, tn), lambda i, j: (i, j)),
                   pl.BlockSpec((tm, LANES), lambda i, j: (i, 0))],
        out_shape=[jax.ShapeDtypeStruct((m, col_tiles * tn), MXU_DTYPE), jax.ShapeDtypeStruct((m, LANES), F32)],
        scratch_shapes=[pltpu.VMEM((tm, D_MODEL), MXU_DTYPE)],
        compiler_params=_cparams(("parallel", "arbitrary"), VMEM_LIMIT_BIG),
        name="in_proj",
    )(x2, g, sc, sh, w_main, w_small)


def _conv_kernel(x_ref, w_ref, b_ref, o_ref, *, rows, cols):
    scale = jnp.where(pl.program_id(1) * CONV_TILE >= QK_W, QK ** -0.5, 1.0).astype(F32)
    w = w_ref[...]
    bias = b_ref[...]
    tpos = lax.broadcasted_iota(jnp.int32, (cols, 1), 0)
    has_left = tpos >= 1
    has_right = tpos < cols - 1

    def row_filters(j):
        tile = x_ref[pl.ds(pl.multiple_of(j * cols, cols), cols), :].astype(F32)
        left = jnp.where(has_left, pltpu.roll(tile, 1, axis=0), 0.0)
        right = jnp.where(has_right, pltpu.roll(tile, cols - 1, axis=0), 0.0)
        return [left * w[3 * i:3 * i + 1, :] + tile * w[3 * i + 1:3 * i + 2, :] + right * w[3 * i + 2:3 * i + 3, :]
                for i in range(3)]

    def finish(j, acc):
        o_ref[pl.ds(pl.multiple_of(j * cols, cols), cols), :] = (_silu(acc + bias) * scale).astype(o_ref.dtype)

    first = row_filters(0)

    def body(j, carry):
        acc, below = carry
        h = row_filters(j)
        finish(j - 1, acc + h[2])
        return below + h[1], h[0]

    acc, _ = lax.fori_loop(1, rows, body, (first[1], first[0]))
    finish(rows - 1, acc)


def _conv_silu(main, conv_w, conv_b, *, batch, rows, cols):
    t = rows * cols
    nct = 2 * QK_W // CONV_TILE
    return pl.pallas_call(
        functools.partial(_conv_kernel, rows=rows, cols=cols),
        grid=(batch, nct),
        in_specs=[pl.BlockSpec((t, CONV_TILE), lambda b, c: (b, c)),
                  pl.BlockSpec((9, CONV_TILE), lambda b, c: (0, c)),
                  pl.BlockSpec((1, CONV_TILE), lambda b, c: (0, c))],
        out_specs=pl.BlockSpec((t, CONV_TILE), lambda b, c: (b, c)),
        out_shape=jax.ShapeDtypeStruct((batch * t, 2 * QK_W), MXU_DTYPE),
        compiler_params=_cparams(("parallel", "arbitrary")),
        name="conv_silu",
    )(main, conv_w, conv_b)


def _chunk_masks(direction):
    row = lax.broadcasted_iota(jnp.int32, (CHUNK, CHUNK), 0)
    col = lax.broadcasted_iota(jnp.int32, (CHUNK, CHUNK), 1)
    seen = (row >= col) if direction == 0 else (row <= col)
    return seen, seen.astype(MXU_DTYPE)


def _scan_specs(batch, nc, col_blocks, widths):
    specs = []
    for direction in (0, 1):
        for cb, wd in zip(col_blocks, widths):
            if direction == 0:
                specs.append(pl.BlockSpec((batch, CHUNK, wd), lambda c, cb=cb: (0, c, cb)))
            else:
                specs.append(pl.BlockSpec((batch, CHUNK, wd), lambda c, cb=cb: (0, nc - 1 - c, cb)))
    return specs


def _scan_out_specs(batch, nc):
    return [pl.BlockSpec((batch, CHUNK, V_W), lambda c: (0, c, 0)),
            pl.BlockSpec((batch, CHUNK, V_W), lambda c: (0, nc - 1 - c, 0))]


def _whole(shape):
    nd = len(shape)
    return pl.BlockSpec(tuple(shape), lambda c: (0,) * nd)


def _lane_tile(x, width):
    return jnp.concatenate([x] * (width // LANES), axis=-1)


def _mlstm_kernel(*refs, with_output, batch):
    (qk_f, v_f, sm_f, qk_b, v_b, sm_b, bias_ref, cn0_ref, m0_ref) = refs[:9]
    if with_output:
        hf_ref, hb_ref, cn_s, m_s = refs[9:]
    else:
        cn_out, m_out, cn_s, m_s = refs[9:]
    step = pl.program_id(0)

    @pl.when(step == 0)
    def _():
        cn_s[...] = cn0_ref[...]
        m_s[...] = m0_ref[...]

    lane = lax.broadcasted_iota(jnp.int32, (1, LANES), 1)
    gate_lane = jnp.logical_and(lane >= SMALL_GATE0, lane < SMALL_GATE0 + 4 * HEADS)
    forget_lane = jnp.logical_and(gate_lane, ((lane - SMALL_GATE0) % (2 * HEADS)) >= HEADS)
    eye = (lax.broadcasted_iota(jnp.int32, (LANES, LANES), 0)
           == lax.broadcasted_iota(jnp.int32, (LANES, LANES), 1)).astype(MXU_DTYPE)
    ones_cols = jnp.ones((CHUNK, LANES), MXU_DTYPE)

    chains = []
    for direction, (qk_ref, v_ref, sm_ref) in enumerate(((qk_f, v_f, sm_f), (qk_b, v_b, sm_b))):
        seen, seen01 = _chunk_masks(direction)
        last = CHUNK - 1 if direction == 0 else 0
        for bi in range(batch):
            g = sm_ref[bi] + bias_ref[...]
            gp = jnp.where(forget_lane, _log_sigmoid(g), g)
            bc = _dot_exact_lhs(seen01, gp)
            hi, mid, lo = _split3(gp)
            gp_t = _dot_nt(eye, hi) + _dot_nt(eye, mid) + _dot_nt(eye, lo)
            hi, mid, lo = _split3(bc)
            bc_t = _dot_nt(eye, hi) + _dot_nt(eye, mid) + _dot_nt(eye, lo)
            bend_row = bc[last:last + 1, :]
            for h in range(HEADS):
                chains.append(dict(h=h, bi=bi, direction=direction, seen=seen, gp=gp, bc=bc, gp_t=gp_t, bc_t=bc_t,
                                   bend_row=bend_row, qk_ref=qk_ref, v_ref=v_ref))

    def load(c):
        h, bi, direction = c["h"], c["bi"], c["direction"]
        c["ji"] = SMALL_GATE0 + direction * 2 * HEADS + h
        c["jf"] = c["ji"] + HEADS
        c["q"] = c["qk_ref"][bi, :, h * QK:(h + 1) * QK].astype(F32)
        c["k"] = c["qk_ref"][bi, :, QK_W + h * QK:QK_W + (h + 1) * QK].astype(F32)
        c["v_ext"] = jnp.concatenate([c["v_ref"][bi, :, h * DV:(h + 1) * DV].astype(MXU_DTYPE), ones_cols], axis=-1)
        c["cn_old"] = cn_s[bi, direction, h]
        c["m_old"] = m_s[bi, direction, h]
        if with_output:
            c["qk"] = _dot_nt(c["q"], c["k"])
            c["qc"] = _dot(c["q"], c["cn_old"])

    def gates(c):
        b_end = jnp.broadcast_to(c["bend_row"][:, c["jf"]:c["jf"] + 1], (1, LANES))
        i_col = jnp.broadcast_to(c["gp"][:, c["ji"]:c["ji"] + 1], (CHUNK, LANES))
        c["b_col"] = jnp.broadcast_to(c["bc"][:, c["jf"]:c["jf"] + 1], (CHUNK, LANES))
        log_w = b_end - c["b_col"] + i_col
        c["m_new"] = jnp.maximum(b_end + c["m_old"], jnp.max(log_w, axis=0, keepdims=True))
        c["kw"] = (c["k"] * jnp.exp(log_w - c["m_new"])).astype(MXU_DTYPE)
        c["decay"] = jnp.exp(b_end + c["m_old"] - c["m_new"])
        c["kw_t"] = _transpose_mxu(c["kw"])

    def weights(c):
        if with_output:
            i_row = c["gp_t"][c["ji"]:c["ji"] + 1, :]
            b_row = c["bc_t"][c["jf"]:c["jf"] + 1, :]
            log_d = jnp.where(c["seen"], _lane_tile(c["b_col"], CHUNK) - b_row + i_row, -jnp.inf)
            log_inter = c["b_col"] + c["m_old"]
            c["m_t"] = jnp.maximum(log_inter, jnp.max(log_d, axis=-1, keepdims=True))
            c["s"] = (c["qk"] * jnp.exp(log_d - _lane_tile(c["m_t"], CHUNK))).astype(MXU_DTYPE)
            c["w_inter"] = jnp.exp(log_inter - c["m_t"])

    def apply(c):
        idx = (c["bi"], c["direction"], c["h"])
        if with_output:
            c["sv"] = _dot(c["s"], c["v_ext"])
        cn_s[idx] = _lane_tile(c["decay"], DV + LANES) * c["cn_old"] + _dot(c["kw_t"], c["v_ext"])
        m_s[idx] = c["m_new"]

    def emit(c):
        if with_output:
            sv, qc, w_inter, h = c["sv"], c["qc"], c["w_inter"], c["h"]
            num = sv[:, :DV] + _lane_tile(w_inter, DV) * qc[:, :DV]
            den = jnp.abs(sv[:, DV:] + w_inter * qc[:, DV:])
            (hf_ref if c["direction"] == 0 else hb_ref)[c["bi"], :, h * DV:(h + 1) * DV] = (
                num / _lane_tile(jnp.maximum(den, jnp.exp(-c["m_t"])), DV))

    stages = (load, gates, weights, apply, emit)
    for tick in range(len(chains) + len(stages) - 1):
        for depth, stage in enumerate(stages):
            if 0 <= tick - depth < len(chains):
                stage(chains[tick - depth])

    if not with_output:
        @pl.when(step == pl.num_programs(0) - 1)
        def _():
            cn_out[...] = cn_s[...]
            m_out[...] = m_s[...]


def _mlstm_scan(qk, main, small, gate_bias, states, *, with_output):
    cn0, m0 = states
    batch, t, _ = qk.shape
    nc = t // CHUNK
    in_specs = _scan_specs(batch, nc, (0, COL_V_M, 0), (2 * QK_W, V_W, LANES))
    in_specs += [_whole(gate_bias.shape), _whole(cn0.shape), _whole(m0.shape)]
    if with_output:
        out_specs = _scan_out_specs(batch, nc)
        out_shape = [jax.ShapeDtypeStruct((batch, t, V_W), F32)] * 2
    else:
        out_specs = [_whole(cn0.shape), _whole(m0.shape)]
        out_shape = [jax.ShapeDtypeStruct(s.shape, F32) for s in states]
    return pl.pallas_call(
        functools.partial(_mlstm_kernel, with_output=with_output, batch=batch),
        grid=(nc,),
        in_specs=in_specs,
        out_specs=out_specs,
        out_shape=out_shape,
        scratch_shapes=[pltpu.VMEM(cn0.shape, F32), pltpu.VMEM(m0.shape, F32)],
        compiler_params=_cparams(("arbitrary",)),
        name="mlstm_scan_out" if with_output else "mlstm_scan_state",
    )(qk, main, small, qk, main, small, gate_bias, cn0, m0)


def _gla_exact_intra(q, k, v, b, direction):
    row_id = lax.broadcasted_iota(jnp.int32, (CHUNK, 1), 0)

    def row(t, acc):
        pick = row_id == t
        b_t = jnp.sum(jnp.where(pick, b, 0.0), axis=0, keepdims=True)
        q_t = jnp.sum(jnp.where(pick, q, 0.0), axis=0, keepdims=True)
        ok = (row_id <= t) if direction == 0 else (row_id >= t)
        e = jnp.exp(jnp.where(ok, b_t - b, -jnp.inf))
        sc = jnp.sum(q_t * k * e, axis=-1, keepdims=True)
        o_t = jnp.sum(sc * v, axis=0, keepdims=True)
        return jnp.where(pick, o_t, acc)

    return lax.fori_loop(0, CHUNK, row, jnp.zeros((CHUNK, DV), F32))


def _gla_kernel(*refs, with_output, batch, n_cast=0):
    (qk_f, v_f, sm_f, qk_b, v_b, sm_b, w2_ref, b2_ref, s0_ref) = refs[:9]
    cast_in, rest = refs[9:9 + n_cast], refs[9 + n_cast:]
    if with_output:
        of_ref, ob_ref = rest[:2]
        cast_out = rest[2:2 + n_cast]
        s_s, b_s, inter_s = rest[2 + n_cast:]
    else:
        s_out, s_s = rest
    step = pl.program_id(0)

    for src, dst in zip(cast_in, cast_out if with_output else ()):
        dst[...] = src[...].astype(dst.dtype)

    @pl.when(step == 0)
    def _():
        s_s[...] = s0_ref[...]

    worst_decay = []
    for direction, (qk_ref, v_ref, sm_ref) in enumerate(((qk_f, v_f, sm_f), (qk_b, v_b, sm_b))):
        seen, seen01 = _chunk_masks(direction)
        last = CHUNK - 1 if direction == 0 else 0
        for bi in range(batch):
            z = _dot(sm_ref[bi], w2_ref[direction]) + b2_ref[direction]
            log_a = _log_sigmoid(z) * (1.0 / GLA_TAU)
            b_all = _dot_exact_lhs(seen01, log_a)
            outs, inters = [], []
            for h in range(HEADS):
                q = qk_ref[bi, :, h * QK:(h + 1) * QK].astype(F32) * (QK ** -0.5)
                k = qk_ref[bi, :, QK_W + h * QK:QK_W + (h + 1) * QK].astype(F32)
                v = v_ref[bi, :, h * DV:(h + 1) * DV]
                b = b_all[:, h * QK:(h + 1) * QK]
                b_end = b[last:last + 1, :]
                st_old = s_s[bi, direction, h]
                k_dec = k * jnp.exp(b_end - b)
                s_s[bi, direction, h] = st_old * jnp.exp(b_end) + _dot_tn_xlu(v, k_dec)
                if with_output:
                    q_dec = q * jnp.exp(b)
                    inter = _dot_nt(q_dec, st_old)
                    scores = jnp.where(seen, _dot_nt(q_dec, k * jnp.exp(-b)), 0.0)
                    outs.append(_dot(scores, v) + inter)
                    inters.append(inter)
            if with_output:
                (of_ref if direction == 0 else ob_ref)[bi] = jnp.concatenate(outs, axis=-1)
                b_s[bi, direction] = b_all
                inter_s[bi, direction] = jnp.concatenate(inters, axis=-1)
                worst_decay.append(jnp.max(-b_all[last:last + 1, :]))

    if with_output:
        @pl.when(functools.reduce(jnp.maximum, worst_decay) > GLA_SAFE_DECAY)
        def _():
            for direction, (qk_ref, v_ref, o_ref) in enumerate(((qk_f, v_f, of_ref), (qk_b, v_b, ob_ref))):
                last = CHUNK - 1 if direction == 0 else 0
                for bi in range(batch):
                    for h in range(HEADS):
                        b = b_s[bi, direction, :, h * QK:(h + 1) * QK]

                        @pl.when(jnp.max(-b[last:last + 1, :]) > GLA_SAFE_DECAY)
                        def _():
                            q = qk_ref[bi, :, h * QK:(h + 1) * QK].astype(F32) * (QK ** -0.5)
                            k = qk_ref[bi, :, QK_W + h * QK:QK_W + (h + 1) * QK].astype(F32)
                            v = v_ref[bi, :, h * DV:(h + 1) * DV].astype(F32)
                            o_ref[bi, :, h * DV:(h + 1) * DV] = (
                                inter_s[bi, direction, :, h * DV:(h + 1) * DV]
                                + _gla_exact_intra(q, k, v, b, direction))
    else:
        @pl.when(step == pl.num_programs(0) - 1)
        def _():
            s_out[...] = s_s[...]


def _gla_scan(main, small, w2p, b2, s0, *, with_output, cast_along=()):
    batch, t, _ = main.shape
    nc = t // CHUNK
    in_specs = _scan_specs(batch, nc, (COL_QK_G, COL_V_G, 0), (2 * QK_W, V_W, LANES))
    in_specs += [_whole(w2p.shape), _whole(b2.shape), _whole(s0.shape)]
    slab = lambda a: pl.BlockSpec((None,) + a.shape[1:], lambda c: (c, 0, 0))
    in_specs += [slab(a) for a in cast_along]
    scratch = [pltpu.VMEM(s0.shape, F32)]
    if with_output:
        assert all(a.shape[0] == nc for a in cast_along)
        out_specs = _scan_out_specs(batch, nc) + [slab(a) for a in cast_along]
        out_shape = ([jax.ShapeDtypeStruct((batch, t, V_W), F32)] * 2
                     + [jax.ShapeDtypeStruct(a.shape, MXU_DTYPE) for a in cast_along])
        scratch += [pltpu.VMEM((batch, 2, CHUNK, QK_W), F32), pltpu.VMEM((batch, 2, CHUNK, V_W), F32)]
    else:
        out_specs = _whole(s0.shape)
        out_shape = jax.ShapeDtypeStruct(s0.shape, F32)
    return pl.pallas_call(
        functools.partial(_gla_kernel, with_output=with_output, batch=batch, n_cast=len(cast_along)),
        grid=(nc,),
        in_specs=in_specs,
        out_specs=out_specs,
        out_shape=out_shape,
        scratch_shapes=scratch,
        compiler_params=_cparams(("arbitrary",)),
        name="gla_scan_out" if with_output else "gla_scan_state",
    )(main, main, small, main, main, small, w2p, b2, s0, *cast_along)


def _head_rms(a):
    return jnp.concatenate([_rms(a[:, h * DV:(h + 1) * DV]) for h in range(HEADS)], axis=-1)


def _merge_kernel(hmf, hmb, ogf, ogb, om, rg, mgm, mgg, x_ref, mln, gln, wpm, wpg, wo, gt1, g2, sc2, sh2,
                  wr, br, x1_ref, h2_ref, wt_ref):
    tm = x_ref.shape[0]
    parts = [slice(i * tm // MERGE_SPLIT, (i + 1) * tm // MERGE_SPLIT) for i in range(MERGE_SPLIT)]
    y_m = [_head_rms(hmf[r, :] + hmb[r, :]) * mln[...] * jax.nn.sigmoid(om[r, :].astype(F32)) for r in parts]
    p_m = [_dot(a, wpm[...]) for a in y_m]
    y_g = [_head_rms(ogf[r, :] + ogb[r, :]) * gln[...] * _silu(rg[r, :].astype(F32)) for r in parts]
    p_g = [_dot(a, wpg[...]) for a in y_g]
    y = [jax.nn.sigmoid(mgm[r, :].astype(F32)) * a + jax.nn.sigmoid(mgg[r, :].astype(F32)) * b
         for r, a, b in zip(parts, p_m, p_g)]
    mix = [_dot(a, wo[...]) for a in y]
    h2_parts = []
    for r, a in zip(parts, mix):
        x1 = x_ref[r, :] + gt1[...] * a
        x1_ref[r, :] = x1
        h2_parts.append(_rms(x1) * g2[...] * (1.0 + sc2[...]) + sh2[...])
    h2 = jnp.concatenate(h2_parts, axis=0)
    h2_ref[...] = h2.astype(MXU_DTYPE)

    hh, hm_, _ = _split3(h2)
    wh, wm_, _ = _split3(wr[...])
    lg = _dot(hh, wh) + _dot(hh, wm_) + _dot(hm_, wh) + br[...]
    lane = lax.broadcasted_iota(jnp.int32, lg.shape, 1)

    def masked_softmax(mask):
        l = jnp.where(mask, lg, -jnp.inf)
        e = jnp.exp(l - jnp.max(l, axis=-1, keepdims=True))
        return e / jnp.sum(e, axis=-1, keepdims=True)

    def top1(p, mask):
        pm = jnp.where(mask, p, -1.0)
        best = jnp.max(pm, axis=-1, keepdims=True)
        idx = jnp.min(jnp.where(jnp.logical_and(mask, pm == best), lane, LANES), axis=-1, keepdims=True)
        return best, idx

    gmask = lane < N_GROUPS
    grp_p, grp = top1(masked_softmax(gmask), gmask)
    e_lo = ROUTE_E0 + grp * EXPERTS_PER_GROUP
    emask = jnp.logical_and(lane >= e_lo, lane < e_lo + EXPERTS_PER_GROUP)
    p_in = masked_softmax(emask)
    p1, i1 = top1(p_in, emask)
    p2, i2 = top1(p_in, jnp.logical_and(emask, lane != i1))
    tot = p1 + p2
    wt_ref[...] = (jnp.where(lane == i1, grp_p * p1 / tot, 0.0)
                   + jnp.where(lane == i2, grp_p * p2 / tot, 0.0)
                   + jnp.where(lane == grp, 1.0, 0.0))


def _merge(hmf, hmb, ogf, ogb, main, x2, mln, gln, wpm, wpg, wo, gt1, g2, sc2, sh2, wr, br, *, tm, rows_per_batch):
    m = x2.shape[0]
    tpb = rows_per_batch // tm
    rowblk = pl.BlockSpec((tm, D_MODEL), lambda i: (i, 0))
    colblk = lambda cb: pl.BlockSpec((tm, D_MODEL), lambda i, cb=cb: (i, cb))
    vec = pl.BlockSpec((1, D_MODEL), lambda i: (0, 0))
    bvec = pl.BlockSpec((None, 1, D_MODEL), lambda i: (i // tpb, 0, 0))
    wmat = pl.BlockSpec((D_MODEL, D_MODEL), lambda i: (0, 0))
    return pl.pallas_call(
        _merge_kernel,
        grid=(m // tm,),
        in_specs=[rowblk, rowblk, rowblk, rowblk, colblk(COL_O_M), colblk(COL_R_G), colblk(COL_MG_M),
                  colblk(COL_MG_G), rowblk, vec, vec, wmat, wmat, wmat, bvec, vec, bvec, bvec,
                  pl.BlockSpec((D_MODEL, LANES), lambda i: (0, 0)), pl.BlockSpec((1, LANES), lambda i: (0, 0))],
        out_specs=[rowblk, rowblk, pl.BlockSpec((tm, LANES), lambda i: (i, 0))],
        out_shape=[jax.ShapeDtypeStruct((m, D_MODEL), F32), jax.ShapeDtypeStruct((m, D_MODEL), MXU_DTYPE),
                   jax.ShapeDtypeStruct((m, LANES), F32)],
        compiler_params=_cparams(("parallel",), VMEM_LIMIT_BIG),
        name="merge_route",
    )(hmf, hmb, ogf, ogb, main, main, main, main, x2, mln, gln, wpm, wpg, wo, gt1, g2, sc2, sh2, wr, br)


MERGE_SPLIT = 2
MOE_BLK = 64
MOE_COMMON_BLKS = (3, 4, 5, 6)
MOE_LOOP_BLKS = 4
MOE_EXPERTS_PER_STEP = 2
MOE_W_SLOTS = 4


def _moe_kernel(h2_ref, wt_ref, wup_hbm, wdn_hbm, x1_hbm, gt2_ref, gf_ref, o_ref,
                xs_ref, ys_ref, ws_ref, dest_ref, blk_ref, wup_buf, wdn_buf, x1_buf, w_sem, x_sem):
    step = pl.program_id(1)
    n_steps = pl.num_programs(1)
    tm = h2_ref.shape[0]
    n_rows = xs_ref.shape[0]
    g_step = pl.program_id(0) * n_steps + step
    g_total = pl.num_programs(0) * n_steps
    ahead = MOE_W_SLOTS - 1

    def weight_copies(g):
        first = lax.rem(g, n_steps) * MOE_EXPERTS_PER_STEP
        slot = lax.rem(g, MOE_W_SLOTS)
        return (pltpu.make_async_copy(wup_hbm.at[pl.ds(first, MOE_EXPERTS_PER_STEP)], wup_buf.at[slot], w_sem.at[slot, 0]),
                pltpu.make_async_copy(wdn_hbm.at[pl.ds(first, MOE_EXPERTS_PER_STEP)], wdn_buf.at[slot], w_sem.at[slot, 1]))

    x1_copy = pltpu.make_async_copy(x1_hbm.at[pl.ds(pl.multiple_of(pl.program_id(0) * tm, tm), tm)], x1_buf, x_sem.at[0])

    @pl.when(g_step == 0)
    def _():
        for g in range(ahead):
            for cp in weight_copies(jnp.int32(g)):
                cp.start()

    @pl.when(g_step + ahead < g_total)
    def _():
        for cp in weight_copies(g_step + ahead):
            cp.start()

    @pl.when(step == 0)
    def _():
        x1_copy.start()

    for cp in weight_copies(g_step):
        cp.wait()
    slot = lax.rem(g_step, MOE_W_SLOTS)

    @pl.when(step == 0)
    def _():
        r = wt_ref[...]
        lane = lax.broadcasted_iota(jnp.int32, (tm, LANES), 1)
        lane1 = lax.broadcasted_iota(jnp.int32, (1, LANES), 1)
        gm = jnp.where(lane < N_GROUPS, r, 0.0)
        earlier = (lax.broadcasted_iota(jnp.int32, (tm, tm), 1)
                   < lax.broadcasted_iota(jnp.int32, (tm, tm), 0)).astype(MXU_DTYPE)
        before = _dot(earlier, gm)
        padded = jnp.floor((jnp.sum(gm, axis=0, keepdims=True) + (MOE_BLK - 1)) * (1.0 / MOE_BLK)) * MOE_BLK
        start = jnp.zeros((1, LANES), F32)
        run = jnp.zeros((1, 1), F32)
        for g in range(N_GROUPS):
            size = jnp.sum(jnp.where(lane1 == g, padded, 0.0), axis=-1, keepdims=True)
            start = jnp.where(lane1 == g, run, start)
            blk_ref[g] = (jnp.sum(run) * (1.0 / MOE_BLK)).astype(jnp.int32)
            blk_ref[N_GROUPS + g] = (jnp.sum(size) * (1.0 / MOE_BLK)).astype(jnp.int32)
            run = run + size
        dest = jnp.sum(gm * (start + before), axis=-1, keepdims=True)
        dest_ref[...] = jnp.broadcast_to(dest, (tm, LANES))
        dest_row = dest_ref[...].T[0:1, :].astype(jnp.int32)
        perm = (lax.broadcasted_iota(jnp.int32, (n_rows, tm), 0) == dest_row).astype(MXU_DTYPE)
        xs_ref[...] = _dot(perm, h2_ref[...]).astype(MXU_DTYPE)
        moved = _dot(perm, jnp.concatenate(_split3(r), axis=-1))
        ws_ref[...] = moved[:, :LANES] + moved[:, LANES:2 * LANES] + moved[:, 2 * LANES:]
        ys_ref[...] = jnp.zeros_like(ys_ref)

    group = step // (EXPERTS_PER_GROUP // MOE_EXPERTS_PER_STEP)
    first_blk = blk_ref[group]
    n_blk = blk_ref[N_GROUPS + group]

    def expert_on(r0, rows):
        x = xs_ref[pl.ds(r0, rows), :]
        ws = ws_ref[pl.ds(r0, rows), :]
        lane_b = lax.broadcasted_iota(jnp.int32, (rows, LANES), 1)
        experts = range(MOE_EXPERTS_PER_STEP)
        gus = [_dot(x, wup_buf[slot, k]) for k in experts]
        hiddens = [(_silu(gu[:, :D_EXPERT]) * gu[:, D_EXPERT:]).astype(MXU_DTYPE) for gu in gus]
        w_cols = [jnp.sum(jnp.where(lane_b == ROUTE_E0 + step * MOE_EXPERTS_PER_STEP + k, ws, 0.0),
                          axis=-1, keepdims=True) for k in experts]
        ys = [_dot(hiddens[k], wdn_buf[slot, k]) for k in experts]
        ys_ref[pl.ds(r0, rows), :] += sum(y * w for y, w in zip(ys, w_cols))

    for k in MOE_COMMON_BLKS:
        @pl.when(n_blk == k)
        def _():
            expert_on(pl.multiple_of(first_blk * MOE_BLK, MOE_BLK), k * MOE_BLK)

    @pl.when(functools.reduce(jnp.logical_and, [n_blk != k for k in MOE_COMMON_BLKS]))
    def _():
        n_big = n_blk // MOE_LOOP_BLKS

        def big(j, carry):
            expert_on(pl.multiple_of((first_blk + j * MOE_LOOP_BLKS) * MOE_BLK, MOE_BLK), MOE_LOOP_BLKS * MOE_BLK)
            return carry

        def single(j, carry):
            expert_on(pl.multiple_of((first_blk + j) * MOE_BLK, MOE_BLK), MOE_BLK)
            return carry

        lax.fori_loop(0, n_big, big, 0)
        lax.fori_loop(n_big * MOE_LOOP_BLKS, n_blk, single, 0)

    @pl.when(step == pl.num_programs(1) - 1)
    def _():
        dest = dest_ref[...][:, :1].astype(jnp.int32)
        unperm = (lax.broadcasted_iota(jnp.int32, (tm, n_rows), 1) == dest).astype(MXU_DTYPE)
        y = _dot(unperm, ys_ref[...])
        x1_copy.wait()
        o_ref[...] = _rms(x1_buf[...] + gt2_ref[...] * y) * gf_ref[...]


def _moe_final(h2, wt, w_up, w_down, x1, gt2, g_final, *, tm, rows_per_batch):
    m = h2.shape[0]
    tpb = rows_per_batch // tm
    n_rows = tm + N_GROUPS * MOE_BLK
    rowblk = pl.BlockSpec((tm, D_MODEL), lambda i, e: (i, 0))
    return pl.pallas_call(
        _moe_kernel,
        grid=(m // tm, N_EXPERTS // MOE_EXPERTS_PER_STEP),
        in_specs=[rowblk,
                  pl.BlockSpec((tm, LANES), lambda i, e: (i, 0)),
                  pl.BlockSpec(memory_space=pl.ANY),
                  pl.BlockSpec(memory_space=pl.ANY),
                  pl.BlockSpec(memory_space=pl.ANY),
                  pl.BlockSpec((None, 1, D_MODEL), lambda i, e: (i // tpb, 0, 0)),
                  pl.BlockSpec((1, D_MODEL), lambda i, e: (0, 0))],
        out_specs=rowblk,
        out_shape=jax.ShapeDtypeStruct((m, D_MODEL), F32),
        scratch_shapes=[pltpu.VMEM((n_rows, D_MODEL), MXU_DTYPE), pltpu.VMEM((n_rows, D_MODEL), F32),
                        pltpu.VMEM((n_rows, LANES), F32), pltpu.VMEM((tm, LANES), F32),
                        pltpu.SMEM((2 * N_GROUPS,), jnp.int32),
                        pltpu.VMEM((MOE_W_SLOTS, MOE_EXPERTS_PER_STEP, D_MODEL, 2 * D_EXPERT), MXU_DTYPE),
                        pltpu.VMEM((MOE_W_SLOTS, MOE_EXPERTS_PER_STEP, D_EXPERT, D_MODEL), MXU_DTYPE),
                        pltpu.VMEM((tm, D_MODEL), F32),
                        pltpu.SemaphoreType.DMA((MOE_W_SLOTS, 2)), pltpu.SemaphoreType.DMA((1,))],
        compiler_params=_cparams(("arbitrary", "arbitrary"), VMEM_LIMIT_BIG),
        name="moe_final",
    )(h2, wt, w_up, w_down, x1, gt2, g_final)


def _empty_states(batch):
    ml = (jnp.zeros((batch, 2, HEADS, QK, DV + LANES), F32),
          jnp.full((batch, 2, HEADS, 1, LANES), NEG_BIG, F32))
    gla = jnp.zeros((batch, 2, HEADS, DV, QK), F32)
    return ml, gla


def kernel(x, c, ctx, c_ctx, w_mod, b_mod, g_norm1, w_in, ml_conv, ml_conv_b, b_mgate, ml_norm, gla_w2, gla_b2,
           gla_norm, w_proj_m, w_proj_g, w_out, g_norm2, w_grp, b_grp, w_rexp, b_rexp, w_up, w_down, g_final):
    batch, t, d = x.shape
    t_ctx = ctx.shape[1]
    assert d == D_MODEL and w_mod.shape[0] == 1 and w_in.shape[2] == sum(IN_SIZES)
    assert t % GRID_W == 0 and t % CHUNK == 0 and t_ctx % CHUNK == 0
    assert t % IN_PROJ_TM == 0 and t % MERGE_TM == 0 and t % MOE_TM == 0

    off = [0]
    for s in IN_SIZES:
        off.append(off[-1] + s)
    wi = w_in[0].astype(MXU_DTYPE)
    w_main = jnp.concatenate([wi[:, off[0]:off[4]], wi[:, off[5]:off[9]], wi[:, off[10]:off[12]]], axis=1)
    w_small = jnp.concatenate([wi[:, off[9]:off[10]], wi[:, off[4]:off[5]],
                               jnp.zeros((d, LANES - 2 * GLA_RANK - 4 * HEADS), MXU_DTYPE)], axis=1)
    gate_bias = jnp.zeros((1, LANES), F32).at[0, SMALL_GATE0:SMALL_GATE0 + 4 * HEADS].set(b_mgate[0])
    w2p = jnp.zeros((2, LANES, QK_W), F32)
    w2p = w2p.at[0, 0:GLA_RANK].set(gla_w2[0, 0]).at[1, GLA_RANK:2 * GLA_RANK].set(gla_w2[0, 1])
    b2 = gla_b2[0][:, None, :]
    conv_w = ml_conv[0].reshape(9, 2 * QK_W)
    conv_b = ml_conv_b[0][None, :]
    w_route = jnp.concatenate([w_grp[0], w_rexp[0], jnp.zeros((d, LANES - N_GROUPS - N_EXPERTS), F32)], axis=1)
    b_route = jnp.concatenate([b_grp[0], b_rexp[0], jnp.zeros((LANES - N_GROUPS - N_EXPERTS,), F32)])[None, :]

    cc = jnp.concatenate([c, c_ctx[None, :], jnp.zeros((8 - batch - 1, d), F32)], axis=0)
    mod = _modulation(cc, w_mod[0], b_mod[0][None, :])
    sh1, sc1, gt1, sh2, sc2, gt2 = [mod[:batch, i * d:(i + 1) * d][:, None, :] for i in range(6)]
    sh1c, sc1c = [mod[batch:batch + 1, i * d:(i + 1) * d][:, None, :] for i in range(2)]
    g1 = g_norm1[0][None, :]

    main_c, small_c = _in_proj(ctx.reshape(batch * t_ctx, d), g1, sc1c, sh1c, w_main, w_small,
                               tm=batch * t_ctx, rows_per_batch=batch * t_ctx, last_col_block=COL_V_G)
    qk_c = _conv_silu(main_c, conv_w, conv_b, batch=batch, rows=1, cols=t_ctx)
    ml0, gla0 = _empty_states(batch)
    main_c3, small_c3 = main_c.reshape(batch, t_ctx, main_c.shape[1]), small_c.reshape(batch, t_ctx, LANES)
    ml_states = _mlstm_scan(qk_c.reshape(batch, t_ctx, 2 * QK_W), main_c3, small_c3, gate_bias, ml0, with_output=False)
    gla_state = _gla_scan(main_c3, small_c3, w2p, b2, gla0, with_output=False)

    x2 = x.reshape(batch * t, d)
    main, small = _in_proj(x2, g1, sc1, sh1, w_main, w_small, tm=IN_PROJ_TM, rows_per_batch=t)
    qk = _conv_silu(main, conv_w, conv_b, batch=batch, rows=t // GRID_W, cols=GRID_W)
    main3, small3 = main.reshape(batch, t, MAIN_W), small.reshape(batch, t, LANES)
    hm_f, hm_b = [a.reshape(batch * t, V_W) for a in
                  _mlstm_scan(qk.reshape(batch, t, 2 * QK_W), main3, small3, gate_bias, ml_states, with_output=True)]
    n_steps = t // CHUNK
    later_weights = (w_up[0], w_down[0], w_proj_m[0], w_proj_g[0], w_out[0])
    og_f, og_b, *cast_weights = _gla_scan(
        main3, small3, w2p, b2, gla_state, with_output=True,
        cast_along=tuple(w.reshape(n_steps, -1, w.shape[-1]) for w in later_weights))
    og_f, og_b = og_f.reshape(batch * t, V_W), og_b.reshape(batch * t, V_W)
    w_up_c, w_down_c, w_pm_c, w_pg_c, w_out_c = [c.reshape(w.shape) for c, w in zip(cast_weights, later_weights)]

    x1, h2, wt = _merge(hm_f, hm_b, og_f, og_b, main, x2, ml_norm, gla_norm,
                        w_pm_c, w_pg_c, w_out_c,
                        gt1, g_norm2, sc2, sh2, w_route, b_route, tm=MERGE_TM, rows_per_batch=t)
    out = _moe_final(h2, wt, w_up_c, w_down_c, x1, gt2, g_final[None, :],
                     tm=MOE_TM, rows_per_batch=t)
    return out.reshape(batch, t, d)
```

```python
import functools

import jax
import jax.numpy as jnp
from jax import lax
from jax.experimental import pallas as pl
from jax.experimental.pallas import tpu as pltpu

D_MODEL = 1024
GRID_W = 64
CHUNK = 256
EPS = 1e-6
NEG_BIG = -1e30
HEADS = 4
QK = D_MODEL // 8
DV = D_MODEL // 4
QK_W = HEADS * QK
V_W = HEADS * DV
GLA_RANK = 16
GLA_TAU = 16.0
N_GROUPS = 4
EXPERTS_PER_GROUP = 4
N_EXPERTS = N_GROUPS * EXPERTS_PER_GROUP
D_EXPERT = D_MODEL // 2
IN_SIZES = (QK_W, QK_W, V_W, V_W, 4 * HEADS, QK_W, QK_W, V_W, V_W, 2 * GLA_RANK, D_MODEL, D_MODEL)

LANES = 128
CONV_TILE = 256
MXU_DTYPE = jnp.bfloat16
F32 = jnp.float32
MIB = 1024 * 1024
VMEM_LIMIT = 48 * MIB
VMEM_LIMIT_BIG = 56 * MIB
MOD_TN = 2048
IN_PROJ_TM, IN_PROJ_TN = 1024, 4096
MERGE_TM = 512
MOE_TM = 1024

COL_QK_M, COL_V_M, COL_O_M, COL_QK_G, COL_V_G, COL_R_G, COL_MG_M, COL_MG_G = range(8)
MAIN_W = 8 * D_MODEL
SMALL_GATE0 = 2 * GLA_RANK
GLA_SAFE_DECAY = 80.0
ROUTE_E0 = N_GROUPS


def _dot(a, b):
    return jnp.dot(a.astype(MXU_DTYPE), b.astype(MXU_DTYPE), preferred_element_type=F32)


def _dot_nt(a, b):
    return lax.dot_general(a.astype(MXU_DTYPE), b.astype(MXU_DTYPE), (((1,), (1,)), ((), ())),
                           preferred_element_type=F32)


def _transpose_mxu(a):
    m = a.shape[1]
    eye = (lax.broadcasted_iota(jnp.int32, (m, m), 0) == lax.broadcasted_iota(jnp.int32, (m, m), 1))
    return _dot_nt(eye.astype(MXU_DTYPE), a).astype(MXU_DTYPE)


def _dot_tn_xlu(a, b):
    return lax.dot_general(a.astype(MXU_DTYPE), b.astype(MXU_DTYPE), (((0,), (0,)), ((), ())),
                           preferred_element_type=F32)


def _split3(x):
    hi = x.astype(MXU_DTYPE)
    r1 = x - hi.astype(F32)
    mid = r1.astype(MXU_DTYPE)
    lo = (r1 - mid.astype(F32)).astype(MXU_DTYPE)
    return hi, mid, lo


def _dot_exact_lhs(a01, x):
    hi, mid, lo = _split3(x)
    return _dot(a01, hi) + _dot(a01, mid) + _dot(a01, lo)


def _log_sigmoid(x):
    return jnp.minimum(x, 0.0) - jnp.log(1.0 + jnp.exp(-jnp.abs(x)))


def _silu(x):
    return x * jax.nn.sigmoid(x)


def _rms(x):
    return x * lax.rsqrt(jnp.mean(x * x, axis=-1, keepdims=True) + EPS)


def _cparams(sem, vmem_limit=VMEM_LIMIT):
    return pltpu.CompilerParams(dimension_semantics=sem, vmem_limit_bytes=vmem_limit)


def _mod_kernel(c_ref, w_ref, b_ref, o_ref):
    o_ref[...] = _dot(_silu(c_ref[...]), w_ref[...]) + b_ref[...]


def _modulation(cc, w_mod, b_mod):
    n = w_mod.shape[1]
    tn = MOD_TN
    return pl.pallas_call(
        _mod_kernel,
        grid=(n // tn,),
        in_specs=[pl.BlockSpec((8, D_MODEL), lambda j: (0, 0)),
                  pl.BlockSpec((D_MODEL, tn), lambda j: (0, j)),
                  pl.BlockSpec((1, tn), lambda j: (0, j))],
        out_specs=pl.BlockSpec((8, tn), lambda j: (0, j)),
        out_shape=jax.ShapeDtypeStruct((8, n), F32),
        compiler_params=_cparams(("arbitrary",)),
        name="modulation",
    )(cc, w_mod, b_mod)


def _inproj_kernel(x_ref, g_ref, sc_ref, sh_ref, w_ref, ws_ref, o_ref, os_ref, xn_ref):
    @pl.when(pl.program_id(1) == 0)
    def _():
        xn = _rms(x_ref[...]) * g_ref[...] * (1.0 + sc_ref[...]) + sh_ref[...]
        xn_ref[...] = xn.astype(MXU_DTYPE)
        os_ref[...] = _dot(xn_ref[...], ws_ref[...])

    half = o_ref.shape[1] // 2
    for k in range(2):
        o_ref[:, k * half:(k + 1) * half] = _dot(xn_ref[...], w_ref[:, k * half:(k + 1) * half]).astype(o_ref.dtype)


def _in_proj(x2, g, sc, sh, w_main, w_small, *, tm, rows_per_batch, last_col_block=COL_MG_G):
    m = x2.shape[0]
    tn = IN_PROJ_TN
    col_tiles = last_col_block * D_MODEL // tn + 1
    tiles_per_batch = rows_per_batch // tm
    vec = pl.BlockSpec((None, 1, D_MODEL), lambda i, j: (i // tiles_per_batch, 0, 0))
    return pl.pallas_call(
        _inproj_kernel,
        grid=(m // tm, col_tiles),
        in_specs=[pl.BlockSpec((tm, D_MODEL), lambda i, j: (i, 0)),
                  pl.BlockSpec((1, D_MODEL), lambda i, j: (0, 0)),
                  vec, vec,
                  pl.BlockSpec((D_MODEL, tn), lambda i, j: (0, j)),
                  pl.BlockSpec((D_MODEL, LANES), lambda i, j: (0, 0))],
        out_specs=[pl.BlockSpec((tm, tn), lambda i, j: (i, j)),
                   pl.BlockSpec((tm, LANES), lambda i, j: (i, 0))],
        out_shape=[jax.ShapeDtypeStruct((m, col_tiles * tn), MXU_DTYPE), jax.ShapeDtypeStruct((m, LANES), F32)],
        scratch_shapes=[pltpu.VMEM((tm, D_MODEL), MXU_DTYPE)],
        compiler_params=_cparams(("parallel", "arbitrary"), VMEM_LIMIT_BIG),
        name="in_proj",
    )(x2, g, sc, sh, w_main, w_small)


def _conv_kernel(x_ref, w_ref, b_ref, o_ref, *, rows, cols):
    scale = jnp.where(pl.program_id(1) * CONV_TILE >= QK_W, QK ** -0.5, 1.0).astype(F32)
    w = w_ref[...]
    bias = b_ref[...]
    tpos = lax.broadcasted_iota(jnp.int32, (cols, 1), 0)
    has_left = tpos >= 1
    has_right = tpos < cols - 1

    def row_filters(j):
        tile = x_ref[pl.ds(pl.multiple_of(j * cols, cols), cols), :].astype(F32)
        left = jnp.where(has_left, pltpu.roll(tile, 1, axis=0), 0.0)
        right = jnp.where(has_right, pltpu.roll(tile, cols - 1, axis=0), 0.0)
        return [left * w[3 * i:3 * i + 1, :] + tile * w[3 * i + 1:3 * i + 2, :] + right * w[3 * i + 2:3 * i + 3, :]
                for i in range(3)]

    def finish(j, acc):
        o_ref[pl.ds(pl.multiple_of(j * cols, cols), cols), :] = (_silu(acc + bias) * scale).astype(o_ref.dtype)

    first = row_filters(0)

    def body(j, carry):
        acc, below = carry
        h = row_filters(j)
        finish(j - 1, acc + h[2])
        return below + h[1], h[0]

    acc, _ = lax.fori_loop(1, rows, body, (first[1], first[0]))
    finish(rows - 1, acc)


def _conv_silu(main, conv_w, conv_b, *, batch, rows, cols):
    t = rows * cols
    nct = 2 * QK_W // CONV_TILE
    return pl.pallas_call(
        functools.partial(_conv_kernel, rows=rows, cols=cols),
        grid=(batch, nct),
        in_specs=[pl.BlockSpec((t, CONV_TILE), lambda b, c: (b, c)),
                  pl.BlockSpec((9, CONV_TILE), lambda b, c: (0, c)),
                  pl.BlockSpec((1, CONV_TILE), lambda b, c: (0, c))],
        out_specs=pl.BlockSpec((t, CONV_TILE), lambda b, c: (b, c)),
        out_shape=jax.ShapeDtypeStruct((batch * t, 2 * QK_W), MXU_DTYPE),
        compiler_params=_cparams(("parallel", "arbitrary")),
        name="conv_silu",
    )(main, conv_w, conv_b)


def _chunk_masks(direction):
    row = lax.broadcasted_iota(jnp.int32, (CHUNK, CHUNK), 0)
    col = lax.broadcasted_iota(jnp.int32, (CHUNK, CHUNK), 1)
    seen = (row >= col) if direction == 0 else (row <= col)
    return seen, seen.astype(MXU_DTYPE)


def _scan_specs(batch, nc, col_blocks, widths):
    specs = []
    for direction in (0, 1):
        for cb, wd in zip(col_blocks, widths):
            if direction == 0:
                specs.append(pl.BlockSpec((batch, CHUNK, wd), lambda c, cb=cb: (0, c, cb)))
            else:
                specs.append(pl.BlockSpec((batch, CHUNK, wd), lambda c, cb=cb: (0, nc - 1 - c, cb)))
    return specs


def _scan_out_specs(batch, nc):
    return [pl.BlockSpec((batch, CHUNK, V_W), lambda c: (0, c, 0)),
            pl.BlockSpec((batch, CHUNK, V_W), lambda c: (0, nc - 1 - c, 0))]


def _whole(shape):
    nd = len(shape)
    return pl.BlockSpec(tuple(shape), lambda c: (0,) * nd)


def _lane_tile(x, width):
    return jnp.concatenate([x] * (width // LANES), axis=-1)


def _mlstm_kernel(*refs, with_output, batch):
    (qk_f, v_f, sm_f, qk_b, v_b, sm_b, bias_ref, cn0_ref, m0_ref) = refs[:9]
    if with_output:
        hf_ref, hb_ref, cn_s, m_s = refs[9:]
    else:
        cn_out, m_out, cn_s, m_s = refs[9:]
    step = pl.program_id(0)

    @pl.when(step == 0)
    def _():
        cn_s[...] = cn0_ref[...]
        m_s[...] = m0_ref[...]

    lane = lax.broadcasted_iota(jnp.int32, (1, LANES), 1)
    gate_lane = jnp.logical_and(lane >= SMALL_GATE0, lane < SMALL_GATE0 + 4 * HEADS)
    forget_lane = jnp.logical_and(gate_lane, ((lane - SMALL_GATE0) % (2 * HEADS)) >= HEADS)
    eye = (lax.broadcasted_iota(jnp.int32, (LANES, LANES), 0)
           == lax.broadcasted_iota(jnp.int32, (LANES, LANES), 1)).astype(MXU_DTYPE)
    ones_cols = jnp.ones((CHUNK, LANES), MXU_DTYPE)

    chains = []
    for direction, (qk_ref, v_ref, sm_ref) in enumerate(((qk_f, v_f, sm_f), (qk_b, v_b, sm_b))):
        seen, seen01 = _chunk_masks(direction)
        last = CHUNK - 1 if direction == 0 else 0
        for bi in range(batch):
            g = sm_ref[bi] + bias_ref[...]
            gp = jnp.where(forget_lane, _log_sigmoid(g), g)
            bc = _dot_exact_lhs(seen01, gp)
            hi, mid, lo = _split3(gp)
            gp_t = _dot_nt(eye, hi) + _dot_nt(eye, mid) + _dot_nt(eye, lo)
            hi, mid, lo = _split3(bc)
            bc_t = _dot_nt(eye, hi) + _dot_nt(eye, mid) + _dot_nt(eye, lo)
            bend_row = bc[last:last + 1, :]
            for h in range(HEADS):
                chains.append(dict(h=h, bi=bi, direction=direction, seen=seen, gp=gp, bc=bc, gp_t=gp_t, bc_t=bc_t,
                                   bend_row=bend_row, qk_ref=qk_ref, v_ref=v_ref))

    def load(c):
        h, bi, direction = c["h"], c["bi"], c["direction"]
        c["ji"] = SMALL_GATE0 + direction * 2 * HEADS + h
        c["jf"] = c["ji"] + HEADS
        c["q"] = c["qk_ref"][bi, :, h * QK:(h + 1) * QK].astype(F32)
        c["k"] = c["qk_ref"][bi, :, QK_W + h * QK:QK_W + (h + 1) * QK].astype(F32)
        c["v_ext"] = jnp.concatenate([c["v_ref"][bi, :, h * DV:(h + 1) * DV].astype(MXU_DTYPE), ones_cols], axis=-1)
        c["cn_old"] = cn_s[bi, direction, h]
        c["m_old"] = m_s[bi, direction, h]
        if with_output:
            c["qk"] = _dot_nt(c["q"], c["k"])
            c["qc"] = _dot(c["q"], c["cn_old"])

    def gates(c):
        b_end = jnp.broadcast_to(c["bend_row"][:, c["jf"]:c["jf"] + 1], (1, LANES))
        i_col = jnp.broadcast_to(c["gp"][:, c["ji"]:c["ji"] + 1], (CHUNK, LANES))
        c["b_col"] = jnp.broadcast_to(c["bc"][:, c["jf"]:c["jf"] + 1], (CHUNK, LANES))
        log_w = b_end - c["b_col"] + i_col
        c["m_new"] = jnp.maximum(b_end + c["m_old"], jnp.max(log_w, axis=0, keepdims=True))
        c["kw"] = (c["k"] * jnp.exp(log_w - c["m_new"])).astype(MXU_DTYPE)
        c["decay"] = jnp.exp(b_end + c["m_old"] - c["m_new"])
        c["kw_t"] = _transpose_mxu(c["kw"])

    def weights(c):
        if with_output:
            i_row = c["gp_t"][c["ji"]:c["ji"] + 1, :]
            b_row = c["bc_t"][c["jf"]:c["jf"] + 1, :]
            log_d = jnp.where(c["seen"], _lane_tile(c["b_col"], CHUNK) - b_row + i_row, -jnp.inf)
            log_inter = c["b_col"] + c["m_old"]
            c["m_t"] = jnp.maximum(log_inter, jnp.max(log_d, axis=-1, keepdims=True))
            c["s"] = (c["qk"] * jnp.exp(log_d - _lane_tile(c["m_t"], CHUNK))).astype(MXU_DTYPE)
            c["w_inter"] = jnp.exp(log_inter - c["m_t"])

    def apply(c):
        idx = (c["bi"], c["direction"], c["h"])
        if with_output:
            c["sv"] = _dot(c["s"], c["v_ext"])
        cn_s[idx] = _lane_tile(c["decay"], DV + LANES) * c["cn_old"] + _dot(c["kw_t"], c["v_ext"])
        m_s[idx] = c["m_new"]

    def emit(c):
        if with_output:
            sv, qc, w_inter, h = c["sv"], c["qc"], c["w_inter"], c["h"]
            num = sv[:, :DV] + _lane_tile(w_inter, DV) * qc[:, :DV]
            den = jnp.abs(sv[:, DV:] + w_inter * qc[:, DV:])
            (hf_ref if c["direction"] == 0 else hb_ref)[c["bi"], :, h * DV:(h + 1) * DV] = (
                num / _lane_tile(jnp.maximum(den, jnp.exp(-c["m_t"])), DV))

    stages = (load, gates, weights, apply, emit)
    for tick in range(len(chains) + len(stages) - 1):
        for depth, stage in enumerate(stages):
            if 0 <= tick - depth < len(chains):
                stage(chains[tick - depth])

    if not with_output:
        @pl.when(step == pl.num_programs(0) - 1)
        def _():
            cn_out[...] = cn_s[...]
            m_out[...] = m_s[...]


def _mlstm_scan(qk, main, small, gate_bias, states, *, with_output):
    cn0, m0 = states
    batch, t, _ = qk.shape
    nc = t // CHUNK
    in_specs = _scan_specs(batch, nc, (0, COL_V_M, 0), (2 * QK_W, V_W, LANES))
    in_specs += [_whole(gate_bias.shape), _whole(cn0.shape), _whole(m0.shape)]
    if with_output:
        out_specs = _scan_out_specs(batch, nc)
        out_shape = [jax.ShapeDtypeStruct((batch, t, V_W), F32)] * 2
    else:
        out_specs = [_whole(cn0.shape), _whole(m0.shape)]
        out_shape = [jax.ShapeDtypeStruct(s.shape, F32) for s in states]
    return pl.pallas_call(
        functools.partial(_mlstm_kernel, with_output=with_output, batch=batch),
        grid=(nc,),
        in_specs=in_specs,
        out_specs=out_specs,
        out_shape=out_shape,
        scratch_shapes=[pltpu.VMEM(cn0.shape, F32), pltpu.VMEM(m0.shape, F32)],
        compiler_params=_cparams(("arbitrary",)),
        name="mlstm_scan_out" if with_output else "mlstm_scan_state",
    )(qk, main, small, qk, main, small, gate_bias, cn0, m0)


def _gla_exact_intra(q, k, v, b, direction):
    row_id = lax.broadcasted_iota(jnp.int32, (CHUNK, 1), 0)

    def row(t, acc):
        pick = row_id == t
        b_t = jnp.sum(jnp.where(pick, b, 0.0), axis=0, keepdims=True)
        q_t = jnp.sum(jnp.where(pick, q, 0.0), axis=0, keepdims=True)
        ok = (row_id <= t) if direction == 0 else (row_id >= t)
        e = jnp.exp(jnp.where(ok, b_t - b, -jnp.inf))
        sc = jnp.sum(q_t * k * e, axis=-1, keepdims=True)
        o_t = jnp.sum(sc * v, axis=0, keepdims=True)
        return jnp.where(pick, o_t, acc)

    return lax.fori_loop(0, CHUNK, row, jnp.zeros((CHUNK, DV), F32))


def _gla_kernel(*refs, with_output, batch, n_cast=0):
    (qk_f, v_f, sm_f, qk_b, v_b, sm_b, w2_ref, b2_ref, s0_ref) = refs[:9]
    cast_in, rest = refs[9:9 + n_cast], refs[9 + n_cast:]
    if with_output:
        of_ref, ob_ref = rest[:2]
        cast_out = rest[2:2 + n_cast]
        s_s, b_s, inter_s = rest[2 + n_cast:]
    else:
        s_out, s_s = rest
    step = pl.program_id(0)

    for src, dst in zip(cast_in, cast_out if with_output else ()):
        dst[...] = src[...].astype(dst.dtype)

    @pl.when(step == 0)
    def _():
        s_s[...] = s0_ref[...]

    worst_decay = []
    for direction, (qk_ref, v_ref, sm_ref) in enumerate(((qk_f, v_f, sm_f), (qk_b, v_b, sm_b))):
        seen, seen01 = _chunk_masks(direction)
        last = CHUNK - 1 if direction == 0 else 0
        for bi in range(batch):
            z = _dot(sm_ref[bi], w2_ref[direction]) + b2_ref[direction]
            log_a = _log_sigmoid(z) * (1.0 / GLA_TAU)
            b_all = _dot_exact_lhs(seen01, log_a)
            outs, inters = [], []
            for h in range(HEADS):
                q = qk_ref[bi, :, h * QK:(h + 1) * QK].astype(F32) * (QK ** -0.5)
                k = qk_ref[bi, :, QK_W + h * QK:QK_W + (h + 1) * QK].astype(F32)
                v = v_ref[bi, :, h * DV:(h + 1) * DV]
                b = b_all[:, h * QK:(h + 1) * QK]
                b_end = b[last:last + 1, :]
                st_old = s_s[bi, direction, h]
                k_dec = k * jnp.exp(b_end - b)
                s_s[bi, direction, h] = st_old * jnp.exp(b_end) + _dot_tn_xlu(v, k_dec)
                if with_output:
                    q_dec = q * jnp.exp(b)
                    inter = _dot_nt(q_dec, st_old)
                    scores = jnp.where(seen, _dot_nt(q_dec, k * jnp.exp(-b)), 0.0)
                    outs.append(_dot(scores, v) + inter)
                    inters.append(inter)
            if with_output:
                (of_ref if direction == 0 else ob_ref)[bi] = jnp.concatenate(outs, axis=-1)
                b_s[bi, direction] = b_all
                inter_s[bi, direction] = jnp.concatenate(inters, axis=-1)
                worst_decay.append(jnp.max(-b_all[last:last + 1, :]))

    if with_output:
        @pl.when(functools.reduce(jnp.maximum, worst_decay) > GLA_SAFE_DECAY)
        def _():
            for direction, (qk_ref, v_ref, o_ref) in enumerate(((qk_f, v_f, of_ref), (qk_b, v_b, ob_ref))):
                last = CHUNK - 1 if direction == 0 else 0
                for bi in range(batch):
                    for h in range(HEADS):
                        b = b_s[bi, direction, :, h * QK:(h + 1) * QK]

                        @pl.when(jnp.max(-b[last:last + 1, :]) > GLA_SAFE_DECAY)
                        def _():
                            q = qk_ref[bi, :, h * QK:(h + 1) * QK].astype(F32) * (QK ** -0.5)
                            k = qk_ref[bi, :, QK_W + h * QK:QK_W + (h + 1) * QK].astype(F32)
                            v = v_ref[bi, :, h * DV:(h + 1) * DV].astype(F32)
                            o_ref[bi, :, h * DV:(h + 1) * DV] = (
                                inter_s[bi, direction, :, h * DV:(h + 1) * DV]
                                + _gla_exact_intra(q, k, v, b, direction))
    else:
        @pl.when(step == pl.num_programs(0) - 1)
        def _():
            s_out[...] = s_s[...]


def _gla_scan(main, small, w2p, b2, s0, *, with_output, cast_along=()):
    batch, t, _ = main.shape
    nc = t // CHUNK
    in_specs = _scan_specs(batch, nc, (COL_QK_G, COL_V_G, 0), (2 * QK_W, V_W, LANES))
    in_specs += [_whole(w2p.shape), _whole(b2.shape), _whole(s0.shape)]
    slab = lambda a: pl.BlockSpec((None,) + a.shape[1:], lambda c: (c, 0, 0))
    in_specs += [slab(a) for a in cast_along]
    scratch = [pltpu.VMEM(s0.shape, F32)]
    if with_output:
        assert all(a.shape[0] == nc for a in cast_along)
        out_specs = _scan_out_specs(batch, nc) + [slab(a) for a in cast_along]
        out_shape = ([jax.ShapeDtypeStruct((batch, t, V_W), F32)] * 2
                     + [jax.ShapeDtypeStruct(a.shape, MXU_DTYPE) for a in cast_along])
        scratch += [pltpu.VMEM((batch, 2, CHUNK, QK_W), F32), pltpu.VMEM((batch, 2, CHUNK, V_W), F32)]
    else:
        out_specs = _whole(s0.shape)
        out_shape = jax.ShapeDtypeStruct(s0.shape, F32)
    return pl.pallas_call(
        functools.partial(_gla_kernel, with_output=with_output, batch=batch, n_cast=len(cast_along)),
        grid=(nc,),
        in_specs=in_specs,
        out_specs=out_specs,
        out_shape=out_shape,
        scratch_shapes=scratch,
        compiler_params=_cparams(("arbitrary",)),
        name="gla_scan_out" if with_output else "gla_scan_state",
    )(main, main, small, main, main, small, w2p, b2, s0, *cast_along)


def _head_rms(a):
    return jnp.concatenate([_rms(a[:, h * DV:(h + 1) * DV]) for h in range(HEADS)], axis=-1)


def _merge_kernel(hmf, hmb, ogf, ogb, om, rg, mgm, mgg, x_ref, mln, gln, wpm, wpg, wo, gt1, g2, sc2, sh2,
                  wr, br, x1_ref, h2_ref, wt_ref):
    tm = x_ref.shape[0]
    parts = [slice(i * tm // MERGE_SPLIT, (i + 1) * tm // MERGE_SPLIT) for i in range(MERGE_SPLIT)]
    y_m = [_head_rms(hmf[r, :] + hmb[r, :]) * mln[...] * jax.nn.sigmoid(om[r, :].astype(F32)) for r in parts]
    p_m = [_dot(a, wpm[...]) for a in y_m]
    y_g = [_head_rms(ogf[r, :] + ogb[r, :]) * gln[...] * _silu(rg[r, :].astype(F32)) for r in parts]
    p_g = [_dot(a, wpg[...]) for a in y_g]
    y = [jax.nn.sigmoid(mgm[r, :].astype(F32)) * a + jax.nn.sigmoid(mgg[r, :].astype(F32)) * b
         for r, a, b in zip(parts, p_m, p_g)]
    mix = [_dot(a, wo[...]) for a in y]
    h2_parts = []
    for r, a in zip(parts, mix):
        x1 = x_ref[r, :] + gt1[...] * a
        x1_ref[r, :] = x1
        h2_parts.append(_rms(x1) * g2[...] * (1.0 + sc2[...]) + sh2[...])
    h2 = jnp.concatenate(h2_parts, axis=0)
    h2_ref[...] = h2.astype(MXU_DTYPE)

    hh, hm_, _ = _split3(h2)
    wh, wm_, _ = _split3(wr[...])
    lg = _dot(hh, wh) + _dot(hh, wm_) + _dot(hm_, wh) + br[...]
    lane = lax.broadcasted_iota(jnp.int32, lg.shape, 1)

    def masked_softmax(mask):
        l = jnp.where(mask, lg, -jnp.inf)
        e = jnp.exp(l - jnp.max(l, axis=-1, keepdims=True))
        return e / jnp.sum(e, axis=-1, keepdims=True)

    def top1(p, mask):
        pm = jnp.where(mask, p, -1.0)
        best = jnp.max(pm, axis=-1, keepdims=True)
        idx = jnp.min(jnp.where(jnp.logical_and(mask, pm == best), lane, LANES), axis=-1, keepdims=True)
        return best, idx

    gmask = lane < N_GROUPS
    grp_p, grp = top1(masked_softmax(gmask), gmask)
    e_lo = ROUTE_E0 + grp * EXPERTS_PER_GROUP
    emask = jnp.logical_and(lane >= e_lo, lane < e_lo + EXPERTS_PER_GROUP)
    p_in = masked_softmax(emask)
    p1, i1 = top1(p_in, emask)
    p2, i2 = top1(p_in, jnp.logical_and(emask, lane != i1))
    tot = p1 + p2
    wt_ref[...] = (jnp.where(lane == i1, grp_p * p1 / tot, 0.0)
                   + jnp.where(lane == i2, grp_p * p2 / tot, 0.0)
                   + jnp.where(lane == grp, 1.0, 0.0))


def _merge(hmf, hmb, ogf, ogb, main, x2, mln, gln, wpm, wpg, wo, gt1, g2, sc2, sh2, wr, br, *, tm, rows_per_batch):
    m = x2.shape[0]
    tpb = rows_per_batch // tm
    rowblk = pl.BlockSpec((tm, D_MODEL), lambda i: (i, 0))
    colblk = lambda cb: pl.BlockSpec((tm, D_MODEL), lambda i, cb=cb: (i, cb))
    vec = pl.BlockSpec((1, D_MODEL), lambda i: (0, 0))
    bvec = pl.BlockSpec((None, 1, D_MODEL), lambda i: (i // tpb, 0, 0))
    wmat = pl.BlockSpec((D_MODEL, D_MODEL), lambda i: (0, 0))
    return pl.pallas_call(
        _merge_kernel,
        grid=(m // tm,),
        in_specs=[rowblk, rowblk, rowblk, rowblk, colblk(COL_O_M), colblk(COL_R_G), colblk(COL_MG_M),
                  colblk(COL_MG_G), rowblk, vec, vec, wmat, wmat, wmat, bvec, vec, bvec, bvec,
                  pl.BlockSpec((D_MODEL, LANES), lambda i: (0, 0)), pl.BlockSpec((1, LANES), lambda i: (0, 0))],
        out_specs=[rowblk, rowblk, pl.BlockSpec((tm, LANES), lambda i: (i, 0))],
        out_shape=[jax.ShapeDtypeStruct((m, D_MODEL), F32), jax.ShapeDtypeStruct((m, D_MODEL), MXU_DTYPE),
                   jax.ShapeDtypeStruct((m, LANES), F32)],
        compiler_params=_cparams(("parallel",), VMEM_LIMIT_BIG),
        name="merge_route",
    )(hmf, hmb, ogf, ogb, main, main, main, main, x2, mln, gln, wpm, wpg, wo, gt1, g2, sc2, sh2, wr, br)


MERGE_SPLIT = 2
MOE_BLK = 64
MOE_COMMON_BLKS = (3, 4, 5, 6)
MOE_LOOP_BLKS = 4
MOE_EXPERTS_PER_STEP = 4
MOE_W_SLOTS = 2


def _moe_kernel(h2_ref, wt_ref, wup_hbm, wdn_hbm, x1_hbm, gt2_ref, gf_ref, o_ref,
                xs_ref, ys_ref, ws_ref, dest_ref, blk_ref, wup_buf, wdn_buf, x1_buf, w_sem, x_sem):
    step = pl.program_id(1)
    n_steps = pl.num_programs(1)
    tm = h2_ref.shape[0]
    n_rows = xs_ref.shape[0]
    g_step = pl.program_id(0) * n_steps + step
    g_total = pl.num_programs(0) * n_steps
    ahead = MOE_W_SLOTS - 1

    def weight_copies(g):
        first = lax.rem(g, n_steps) * MOE_EXPERTS_PER_STEP
        slot = lax.rem(g, MOE_W_SLOTS)
        return (pltpu.make_async_copy(wup_hbm.at[pl.ds(first, MOE_EXPERTS_PER_STEP)], wup_buf.at[slot], w_sem.at[slot, 0]),
                pltpu.make_async_copy(wdn_hbm.at[pl.ds(first, MOE_EXPERTS_PER_STEP)], wdn_buf.at[slot], w_sem.at[slot, 1]))

    x1_copy = pltpu.make_async_copy(x1_hbm.at[pl.ds(pl.multiple_of(pl.program_id(0) * tm, tm), tm)], x1_buf, x_sem.at[0])

    @pl.when(g_step == 0)
    def _():
        for g in range(ahead):
            for cp in weight_copies(jnp.int32(g)):
                cp.start()

    @pl.when(g_step + ahead < g_total)
    def _():
        for cp in weight_copies(g_step + ahead):
            cp.start()

    @pl.when(step == 0)
    def _():
        x1_copy.start()

    for cp in weight_copies(g_step):
        cp.wait()
    slot = lax.rem(g_step, MOE_W_SLOTS)

    @pl.when(step == 0)
    def _():
        r = wt_ref[...]
        lane = lax.broadcasted_iota(jnp.int32, (tm, LANES), 1)
        lane1 = lax.broadcasted_iota(jnp.int32, (1, LANES), 1)
        gm = jnp.where(lane < N_GROUPS, r, 0.0)
        earlier = (lax.broadcasted_iota(jnp.int32, (tm, tm), 1)
                   < lax.broadcasted_iota(jnp.int32, (tm, tm), 0)).astype(MXU_DTYPE)
        before = _dot(earlier, gm)
        padded = jnp.floor((jnp.sum(gm, axis=0, keepdims=True) + (MOE_BLK - 1)) * (1.0 / MOE_BLK)) * MOE_BLK
        start = jnp.zeros((1, LANES), F32)
        run = jnp.zeros((1, 1), F32)
        for g in range(N_GROUPS):
            size = jnp.sum(jnp.where(lane1 == g, padded, 0.0), axis=-1, keepdims=True)
            start = jnp.where(lane1 == g, run, start)
            blk_ref[g] = (jnp.sum(run) * (1.0 / MOE_BLK)).astype(jnp.int32)
            blk_ref[N_GROUPS + g] = (jnp.sum(size) * (1.0 / MOE_BLK)).astype(jnp.int32)
            run = run + size
        dest = jnp.sum(gm * (start + before), axis=-1, keepdims=True)
        dest_ref[...] = jnp.broadcast_to(dest, (tm, LANES))
        dest_row = dest_ref[...].T[0:1, :].astype(jnp.int32)
        perm = (lax.broadcasted_iota(jnp.int32, (n_rows, tm), 0) == dest_row).astype(MXU_DTYPE)
        xs_ref[...] = _dot(perm, h2_ref[...]).astype(MXU_DTYPE)
        moved = _dot(perm, jnp.concatenate(_split3(r), axis=-1))
        ws_ref[...] = moved[:, :LANES] + moved[:, LANES:2 * LANES] + moved[:, 2 * LANES:]
        ys_ref[...] = jnp.zeros_like(ys_ref)

    group = step // (EXPERTS_PER_GROUP // MOE_EXPERTS_PER_STEP)
    first_blk = blk_ref[group]
    n_blk = blk_ref[N_GROUPS + group]

    def expert_on(r0, rows):
        x = xs_ref[pl.ds(r0, rows), :]
        ws = ws_ref[pl.ds(r0, rows), :]
        lane_b = lax.broadcasted_iota(jnp.int32, (rows, LANES), 1)
        experts = range(MOE_EXPERTS_PER_STEP)
        gus = [_dot(x, wup_buf[slot, k]) for k in experts]
        hiddens = [(_silu(gu[:, :D_EXPERT]) * gu[:, D_EXPERT:]).astype(MXU_DTYPE) for gu in gus]
        w_cols = [jnp.sum(jnp.where(lane_b == ROUTE_E0 + step * MOE_EXPERTS_PER_STEP + k, ws, 0.0),
                          axis=-1, keepdims=True) for k in experts]
        ys = [_dot(hiddens[k], wdn_buf[slot, k]) for k in experts]
        ys_ref[pl.ds(r0, rows), :] += sum(y * w for y, w in zip(ys, w_cols))

    for k in MOE_COMMON_BLKS:
        @pl.when(n_blk == k)
        def _():
            expert_on(pl.multiple_of(first_blk * MOE_BLK, MOE_BLK), k * MOE_BLK)

    @pl.when(functools.reduce(jnp.logical_and, [n_blk != k for k in MOE_COMMON_BLKS]))
    def _():
        n_big = n_blk // MOE_LOOP_BLKS

        def big(j, carry):
            expert_on(pl.multiple_of((first_blk + j * MOE_LOOP_BLKS) * MOE_BLK, MOE_BLK), MOE_LOOP_BLKS * MOE_BLK)
            return carry

        def single(j, carry):
            expert_on(pl.multiple_of((first_blk + j) * MOE_BLK, MOE_BLK), MOE_BLK)
            return carry

        lax.fori_loop(0, n_big, big, 0)
        lax.fori_loop(n_big * MOE_LOOP_BLKS, n_blk, single, 0)

    @pl.when(step == pl.num_programs(1) - 1)
    def _():
        dest = dest_ref[...][:, :1].astype(jnp.int32)
        unperm = (lax.broadcasted_iota(jnp.int32, (tm, n_rows), 1) == dest).astype(MXU_DTYPE)
        y = _dot(unperm, ys_ref[...])
        x1_copy.wait()
        o_ref[...] = _rms(x1_buf[...] + gt2_ref[...] * y) * gf_ref[...]


def _moe_final(h2, wt, w_up, w_down, x1, gt2, g_final, *, tm, rows_per_batch):
    m = h2.shape[0]
    tpb = rows_per_batch // tm
    n_rows = tm + N_GROUPS * MOE_BLK
    rowblk = pl.BlockSpec((tm, D_MODEL), lambda i, e: (i, 0))
    return pl.pallas_call(
        _moe_kernel,
        grid=(m // tm, N_EXPERTS // MOE_EXPERTS_PER_STEP),
        in_specs=[rowblk,
                  pl.BlockSpec((tm, LANES), lambda i, e: (i, 0)),
                  pl.BlockSpec(memory_space=pl.ANY),
                  pl.BlockSpec(memory_space=pl.ANY),
                  pl.BlockSpec(memory_space=pl.ANY),
                  pl.BlockSpec((None, 1, D_MODEL), lambda i, e: (i // tpb, 0, 0)),
                  pl.BlockSpec((1, D_MODEL), lambda i, e: (0, 0))],
        out_specs=rowblk,
        out_shape=jax.ShapeDtypeStruct((m, D_MODEL), F32),
        scratch_shapes=[pltpu.VMEM((n_rows, D_MODEL), MXU_DTYPE), pltpu.VMEM((n_rows, D_MODEL), F32),
                        pltpu.VMEM((n_rows, LANES), F32), pltpu.VMEM((tm, LANES), F32),
                        pltpu.SMEM((2 * N_GROUPS,), jnp.int32),
                        pltpu.VMEM((MOE_W_SLOTS, MOE_EXPERTS_PER_STEP, D_MODEL, 2 * D_EXPERT), MXU_DTYPE),
                        pltpu.VMEM((MOE_W_SLOTS, MOE_EXPERTS_PER_STEP, D_EXPERT, D_MODEL), MXU_DTYPE),
                        pltpu.VMEM((tm, D_MODEL), F32),
                        pltpu.SemaphoreType.DMA((MOE_W_SLOTS, 2)), pltpu.SemaphoreType.DMA((1,))],
        compiler_params=_cparams(("arbitrary", "arbitrary"), VMEM_LIMIT_BIG),
        name="moe_final",
    )(h2, wt, w_up, w_down, x1, gt2, g_final)


def _empty_states(batch):
    ml = (jnp.zeros((batch, 2, HEADS, QK, DV + LANES), F32),
          jnp.full((batch, 2, HEADS, 1, LANES), NEG_BIG, F32))
    gla = jnp.zeros((batch, 2, HEADS, DV, QK), F32)
    return ml, gla


def kernel(x, c, ctx, c_ctx, w_mod, b_mod, g_norm1, w_in, ml_conv, ml_conv_b, b_mgate, ml_norm, gla_w2, gla_b2,
           gla_norm, w_proj_m, w_proj_g, w_out, g_norm2, w_grp, b_grp, w_rexp, b_rexp, w_up, w_down, g_final):
    batch, t, d = x.shape
    t_ctx = ctx.shape[1]
    assert d == D_MODEL and w_mod.shape[0] == 1 and w_in.shape[2] == sum(IN_SIZES)
    assert t % GRID_W == 0 and t % CHUNK == 0 and t_ctx % CHUNK == 0
    assert t % IN_PROJ_TM == 0 and t % MERGE_TM == 0 and t % MOE_TM == 0

    off = [0]
    for s in IN_SIZES:
        off.append(off[-1] + s)
    wi = w_in[0].astype(MXU_DTYPE)
    w_main = jnp.concatenate([wi[:, off[0]:off[4]], wi[:, off[5]:off[9]], wi[:, off[10]:off[12]]], axis=1)
    w_small = jnp.concatenate([wi[:, off[9]:off[10]], wi[:, off[4]:off[5]],
                               jnp.zeros((d, LANES - 2 * GLA_RANK - 4 * HEADS), MXU_DTYPE)], axis=1)
    gate_bias = jnp.zeros((1, LANES), F32).at[0, SMALL_GATE0:SMALL_GATE0 + 4 * HEADS].set(b_mgate[0])
    w2p = jnp.zeros((2, LANES, QK_W), F32)
    w2p = w2p.at[0, 0:GLA_RANK].set(gla_w2[0, 0]).at[1, GLA_RANK:2 * GLA_RANK].set(gla_w2[0, 1])
    b2 = gla_b2[0][:, None, :]
    conv_w = ml_conv[0].reshape(9, 2 * QK_W)
    conv_b = ml_conv_b[0][None, :]
    w_route = jnp.concatenate([w_grp[0], w_rexp[0], jnp.zeros((d, LANES - N_GROUPS - N_EXPERTS), F32)], axis=1)
    b_route = jnp.concatenate([b_grp[0], b_rexp[0], jnp.zeros((LANES - N_GROUPS - N_EXPERTS,), F32)])[None, :]

    cc = jnp.concatenate([c, c_ctx[None, :], jnp.zeros((8 - batch - 1, d), F32)], axis=0)
    mod = _modulation(cc, w_mod[0], b_mod[0][None, :])
    sh1, sc1, gt1, sh2, sc2, gt2 = [mod[:batch, i * d:(i + 1) * d][:, None, :] for i in range(6)]
    sh1c, sc1c = [mod[batch:batch + 1, i * d:(i + 1) * d][:, None, :] for i in range(2)]
    g1 = g_norm1[0][None, :]

    main_c, small_c = _in_proj(ctx.reshape(batch * t_ctx, d), g1, sc1c, sh1c, w_main, w_small,
                               tm=batch * t_ctx, rows_per_batch=batch * t_ctx, last_col_block=COL_V_G)
    qk_c = _conv_silu(main_c, conv_w, conv_b, batch=batch, rows=1, cols=t_ctx)
    ml0, gla0 = _empty_states(batch)
    main_c3, small_c3 = main_c.reshape(batch, t_ctx, main_c.shape[1]), small_c.reshape(batch, t_ctx, LANES)
    ml_states = _mlstm_scan(qk_c.reshape(batch, t_ctx, 2 * QK_W), main_c3, small_c3, gate_bias, ml0, with_output=False)
    gla_state = _gla_scan(main_c3, small_c3, w2p, b2, gla0, with_output=False)

    x2 = x.reshape(batch * t, d)
    main, small = _in_proj(x2, g1, sc1, sh1, w_main, w_small, tm=IN_PROJ_TM, rows_per_batch=t)
    qk = _conv_silu(main, conv_w, conv_b, batch=batch, rows=t // GRID_W, cols=GRID_W)
    main3, small3 = main.reshape(batch, t, MAIN_W), small.reshape(batch, t, LANES)
    hm_f, hm_b = [a.reshape(batch * t, V_W) for a in
                  _mlstm_scan(qk.reshape(batch, t, 2 * QK_W), main3, small3, gate_bias, ml_states, with_output=True)]
    n_steps = t // CHUNK
    later_weights = (w_up[0], w_down[0], w_proj_m[0], w_proj_g[0], w_out[0])
    og_f, og_b, *cast_weights = _gla_scan(
        main3, small3, w2p, b2, gla_state, with_output=True,
        cast_along=tuple(w.reshape(n_steps, -1, w.shape[-1]) for w in later_weights))
    og_f, og_b = og_f.reshape(batch * t, V_W), og_b.reshape(batch * t, V_W)
    w_up_c, w_down_c, w_pm_c, w_pg_c, w_out_c = [c.reshape(w.shape) for c, w in zip(cast_weights, later_weights)]

    x1, h2, wt = _merge(hm_f, hm_b, og_f, og_b, main, x2, ml_norm, gla_norm,
                        w_pm_c, w_pg_c, w_out_c,
                        gt1, g_norm2, sc2, sh2, w_route, b_route, tm=MERGE_TM, rows_per_batch=t)
    out = _moe_final(h2, wt, w_up_c, w_down_c, x1, gt2, g_final[None, :],
                     tm=MOE_TM, rows_per_batch=t)
    return out.reshape(batch, t, d)
```
